```python
import math, functools
import jax, jax.numpy as jnp
from jax import lax
import numpy as np

D_MODEL = 1024
BATCH = 8
SEQ = 4096
DEPTH = 1

GRID_W = 64
CTX_LEN = 256
D_MIX = D_MODEL
D_SSM = D_MIX // 2
SSM_HEADDIM = 64
SSM_HEADS = D_SSM // SSM_HEADDIM
SSM_STATE = 128
SSM_GROUPS = 2
SSM_CONV = 5
SSD_CHUNK = 128
XBC_DIM = D_SSM + 2 * SSM_GROUPS * SSM_STATE
D_CONV = D_MIX - D_SSM
CONV_WIDTH = 31
IN_COLS = D_SSM + XBC_DIM + 2 * SSM_HEADS + 2 * D_CONV
PEER_HEADS = 8
PEER_KEY_DIM = 256
N_KEYS = 128
N_EXPERTS = N_KEYS * N_KEYS
PEER_TOPK = 16
PEER_BLOCK = 128
EPS = 1e-6

kernel_name = "hymba_ssd_conformer_peer_dit_layer"


def rms_norm(h, w):
    hf = h.astype(jnp.float32)
    hf = hf * lax.rsqrt(jnp.mean(hf * hf, axis=-1, keepdims=True) + EPS)
    return (hf * w.astype(jnp.float32)).astype(h.dtype)


def layer_norm(h, w, b):
    hf = h.astype(jnp.float32)
    mu = jnp.mean(hf, axis=-1, keepdims=True)
    var = jnp.mean(jnp.square(hf - mu), axis=-1, keepdims=True)
    out = (hf - mu) * lax.rsqrt(var + EPS) * w.astype(jnp.float32) + b.astype(jnp.float32)
    return out.astype(h.dtype)


def gated_rms_norm(y, z, w):
    bsz, length, ch = y.shape
    g = (y * jax.nn.silu(z)).astype(jnp.float32).reshape(bsz, length, SSM_GROUPS, ch // SSM_GROUPS)
    g = g * lax.rsqrt(jnp.mean(g * g, axis=-1, keepdims=True) + EPS)
    return (g.reshape(bsz, length, ch) * w.astype(jnp.float32)).astype(y.dtype)


def modulate(h, shift, scale):
    return h * (1 + scale[:, None, :]) + shift[:, None, :]


def dwconv_seq(h, w, b):
    k, ch = w.shape
    pad = (k - 1) // 2
    out = lax.conv_general_dilated(h, w[:, None, :], window_strides=(1,), padding=[(pad, pad)],
                                   dimension_numbers=("NWC", "WIO", "NWC"), feature_group_count=ch)
    return out + b


def dwconv_grid_axial(h, w, b, rows):
    bsz, length, ch = h.shape
    k = w.shape[0]
    pad = (k - 1) // 2
    half = ch // 2
    g = h.reshape(bsz, rows, GRID_W, ch)
    dn = ("NHWC", "HWIO", "NHWC")
    horiz = lax.conv_general_dilated(g[..., :half], w[:, :half].reshape(1, k, 1, half), (1, 1),
                                     [(0, 0), (pad, pad)], dimension_numbers=dn,
                                     feature_group_count=half)
    vert = lax.conv_general_dilated(g[..., half:], w[:, half:].reshape(k, 1, 1, ch - half), (1, 1),
                                    [(pad, pad), (0, 0)], dimension_numbers=dn,
                                    feature_group_count=ch - half)
    return jnp.concatenate([horiz, vert], axis=-1).reshape(bsz, length, ch) + b


def segsum_exp(a):
    t = a.shape[-1]
    acum = jnp.cumsum(a, axis=-1)
    diff = acum[..., :, None] - acum[..., None, :]
    mask = jnp.tril(jnp.ones((t, t), dtype=bool))
    return jnp.where(mask, jnp.exp(jnp.where(mask, diff, 0.0)), 0.0)


def ssd(xs, dt, a, bm, cm, h0):
    bsz, length, nh, hp = xs.shape
    nc = length // SSD_CHUNK
    rep = nh // bm.shape[2]
    f32 = jnp.float32
    xdt = (xs.astype(f32) * dt[..., None]).reshape(bsz, nc, SSD_CHUNK, nh, hp)
    bh = jnp.repeat(bm.astype(f32), rep, axis=2).reshape(bsz, nc, SSD_CHUNK, nh, SSM_STATE)
    ch = jnp.repeat(cm.astype(f32), rep, axis=2).reshape(bsz, nc, SSD_CHUNK, nh, SSM_STATE)
    ac = (dt * a).reshape(bsz, nc, SSD_CHUNK, nh).transpose(0, 3, 1, 2)
    acum = jnp.cumsum(ac, axis=-1)
    scores = jnp.einsum("bclhn,bcshn->bhcls", ch, bh) * segsum_exp(ac)
    y_diag = jnp.einsum("bhcls,bcshp->bclhp", scores, xdt)
    decay_to_end = jnp.exp(acum[..., -1:] - acum)
    states = jnp.einsum("bclhn,bhcl,bclhp->bchpn", bh, decay_to_end, xdt)
    states = jnp.concatenate([h0[:, None], states], axis=1)
    chunk_decay = segsum_exp(jnp.pad(acum[..., -1], ((0, 0), (0, 0), (1, 0))))
    states = jnp.einsum("bhzc,bchpn->bzhpn", chunk_decay, states)
    y_off = jnp.einsum("bclhn,bchpn,bhcl->bclhp", ch, states[:, :-1], jnp.exp(acum))
    y = (y_diag + y_off).reshape(bsz, length, nh, hp).astype(xs.dtype)
    return y, states[:, -1]


def ssd_final_state(xs, dt, a, bm):
    rep = xs.shape[2] // bm.shape[2]
    acum = jnp.cumsum(dt * a, axis=1)
    w = jnp.exp(acum[:, -1:] - acum) * dt
    bh = jnp.repeat(bm.astype(jnp.float32), rep, axis=2)
    return jnp.einsum("blhn,blh,blhp->bhpn", bh, w, xs.astype(jnp.float32))


def split_in(p):
    return jnp.split(p, [D_SSM, D_SSM + XBC_DIM, D_SSM + XBC_DIM + 2 * SSM_HEADS], axis=-1)


def ssm_prep(xbc, dt_raw, conv_w, conv_b, dt_bias):
    bsz, length, _ = xbc.shape
    xbc = jax.nn.silu(dwconv_seq(xbc, conv_w, conv_b))
    xs, bm, cm = jnp.split(xbc, [D_SSM, D_SSM + SSM_GROUPS * SSM_STATE], axis=-1)
    dt = jax.nn.softplus((dt_raw.reshape(bsz, length, 2, SSM_HEADS) + dt_bias).astype(jnp.float32))
    return (xs.reshape(bsz, length, SSM_HEADS, SSM_HEADDIM),
            bm.reshape(bsz, length, SSM_GROUPS, SSM_STATE),
            cm.reshape(bsz, length, SSM_GROUPS, SSM_STATE),
            dt)


def conformer_branch(glu, conv_fn, conv_w, conv_b, ln_w, ln_b):
    u = glu[..., :D_CONV] * jax.nn.sigmoid(glu[..., D_CONV:])
    return jax.nn.silu(layer_norm(conv_fn(u, conv_w, conv_b), ln_w, ln_b))


def peer(h, wq, sub_keys, u_tab, v_tab):
    bsz, length, d = h.shape
    t = bsz * length
    hf = h.reshape(t, d)
    q = (hf @ wq).reshape(t, PEER_HEADS, 2, PEER_KEY_DIM // 2)
    s = jnp.einsum("thpd,hpkd->thpk", q, sub_keys).astype(jnp.float32)
    sv, si = lax.top_k(s, PEER_TOPK)
    cand = sv[:, :, 0, :, None] + sv[:, :, 1, None, :]
    best, flat = lax.top_k(cand.reshape(t, PEER_HEADS, PEER_TOPK * PEER_TOPK), PEER_TOPK)
    i1 = jnp.take_along_axis(si[:, :, 0], flat // PEER_TOPK, axis=-1)
    i2 = jnp.take_along_axis(si[:, :, 1], flat % PEER_TOPK, axis=-1)
    experts = (i1 * N_KEYS + i2).reshape(t, PEER_HEADS * PEER_TOPK)
    gates = jax.nn.softmax(best, axis=-1).reshape(t, PEER_HEADS * PEER_TOPK).astype(h.dtype)
    nb = t // PEER_BLOCK

    def block(args):
        xb, eb, gb = args
        act = jax.nn.gelu(jnp.einsum("pd,ped->pe", xb, u_tab[eb]), approximate=False)
        return jnp.einsum("pe,ped->pd", gb * act, v_tab[eb])

    out = lax.map(block, (hf.reshape(nb, PEER_BLOCK, d),
                          experts.reshape(nb, PEER_BLOCK, PEER_HEADS * PEER_TOPK),
                          gates.reshape(nb, PEER_BLOCK, PEER_HEADS * PEER_TOPK)))
    return out.reshape(bsz, length, d)


def mixer_sublayer(x, ctx, mod, mod_ctx, rows, norm_w, w_in, ssm_conv_w, ssm_conv_b, ssm_dt_bias,
                   ssm_a_log, ssm_d, ssm_norm_w, cfm_conv_w, cfm_conv_b, cfm_ln_w, cfm_ln_b, w_out,
                   update_ctx):
    a = -jnp.exp(ssm_a_log.astype(jnp.float32))
    flip = lambda t: t[:, ::-1]
    z_l, xbc_l, dt_l, glu_l = split_in(modulate(rms_norm(x, norm_w), mod[:, 0], mod[:, 1]) @ w_in)
    h_c = modulate(rms_norm(ctx, norm_w), mod_ctx[:, 0], mod_ctx[:, 1])
    if update_ctx:
        z_c, xbc_c, dt_c, glu_c = split_in(h_c @ w_in)
    else:
        xbc_c, dt_c = jnp.split(h_c @ w_in[:, D_SSM:D_SSM + XBC_DIM + 2 * SSM_HEADS], [XBC_DIM], axis=-1)
    xs_l, b_l, c_l, dts_l = ssm_prep(xbc_l, dt_l, ssm_conv_w, ssm_conv_b, ssm_dt_bias)
    xs_c, b_c, c_c, dts_c = ssm_prep(xbc_c, dt_c, ssm_conv_w, ssm_conv_b, ssm_dt_bias)

    if update_ctx:
        h0 = jnp.zeros((ctx.shape[0], SSM_HEADS, SSM_HEADDIM, SSM_STATE), jnp.float32)
        yc_f, hc_f = ssd(xs_c, dts_c[:, :, 0], a[0], b_c, c_c, h0)
        yc_b, hc_b = ssd(flip(xs_c), flip(dts_c[:, :, 1]), a[1], flip(b_c), flip(c_c), h0)
    else:
        hc_f = ssd_final_state(xs_c, dts_c[:, :, 0], a[0], b_c)
        hc_b = ssd_final_state(flip(xs_c), flip(dts_c[:, :, 1]), a[1], flip(b_c))
    y_f, _ = ssd(xs_l, dts_l[:, :, 0], a[0], b_l, c_l, hc_f)
    y_b, _ = ssd(flip(xs_l), flip(dts_l[:, :, 1]), a[1], flip(b_l), flip(c_l), hc_b)
    y_l = (y_f + flip(y_b) + ssm_d[:, None] * xs_l).reshape(x.shape[0], x.shape[1], D_SSM)
    ssm_l = gated_rms_norm(y_l, z_l, ssm_norm_w)
    cfm_l = conformer_branch(glu_l, functools.partial(dwconv_grid_axial, rows=rows),
                             cfm_conv_w, cfm_conv_b, cfm_ln_w, cfm_ln_b)
    x_out = x + mod[:, 2, None, :] * (jnp.concatenate([ssm_l, cfm_l], axis=-1) @ w_out)
    if not update_ctx:
        return x_out, ctx
    y_c = (yc_f + flip(yc_b) + ssm_d[:, None] * xs_c).reshape(ctx.shape[0], ctx.shape[1], D_SSM)
    ssm_c = gated_rms_norm(y_c, z_c, ssm_norm_w)
    cfm_c = conformer_branch(glu_c, dwconv_seq, cfm_conv_w, cfm_conv_b, cfm_ln_w, cfm_ln_b)
    ctx_out = ctx + mod_ctx[:, 2, None, :] * (jnp.concatenate([ssm_c, cfm_c], axis=-1) @ w_out)
    return x_out, ctx_out


def setup_inputs(seed: int = 0) -> dict:
    key = jax.random.key(seed)
    ks = jax.random.split(key, 26)
    D = D_MODEL

    def nrm(k, shape, scale):
        return jax.random.normal(k, shape, jnp.float32) * scale

    dt0 = jnp.exp(jax.random.uniform(ks[9], (DEPTH, 2, SSM_HEADS), jnp.float32,
                                     minval=math.log(1e-3), maxval=math.log(1e-1)))
    return {
        "x": nrm(ks[0], (BATCH, SEQ, D), 1.0),
        "c": nrm(ks[1], (BATCH, D), 1.0),
        "ctx": nrm(ks[2], (BATCH, CTX_LEN, D), 1.0),
        "c_ctx": nrm(ks[3], (D,), 1.0),
        "ada_w": nrm(ks[4], (DEPTH, D, 6 * D), 0.5 * D ** -0.5),
        "ada_b": nrm(ks[5], (DEPTH, 6 * D), 0.02),
        "norm1_w": 1.0 + nrm(ks[6], (DEPTH, D), 0.02),
        "norm2_w": 1.0 + nrm(ks[7], (DEPTH, D), 0.02),
        "w_in": nrm(ks[8], (DEPTH, D, IN_COLS), D ** -0.5),
        "ssm_conv_w": nrm(ks[10], (DEPTH, SSM_CONV, XBC_DIM), SSM_CONV ** -0.5),
        "ssm_conv_b": nrm(ks[11], (DEPTH, XBC_DIM), 0.02),
        "ssm_dt_bias": dt0 + jnp.log(-jnp.expm1(-dt0)),
        "ssm_a_log": jnp.log(jax.random.uniform(ks[12], (DEPTH, 2, SSM_HEADS), jnp.float32,
                                                 minval=1.0, maxval=16.0)),
        "ssm_d": 1.0 + nrm(ks[13], (DEPTH, SSM_HEADS), 0.1),
        "ssm_norm_w": 1.0 + nrm(ks[14], (DEPTH, D_SSM), 0.02),
        "cfm_conv_w": nrm(ks[15], (DEPTH, CONV_WIDTH, D_CONV), CONV_WIDTH ** -0.5),
        "cfm_conv_b": nrm(ks[16], (DEPTH, D_CONV), 0.02),
        "cfm_ln_w": 1.0 + nrm(ks[17], (DEPTH, D_CONV), 0.02),
        "cfm_ln_b": nrm(ks[18], (DEPTH, D_CONV), 0.02),
        "w_out": nrm(ks[19], (DEPTH, D_MIX, D), D_MIX ** -0.5),
        "peer_wq": nrm(ks[20], (DEPTH, D, PEER_HEADS * PEER_KEY_DIM), D ** -0.5),
        "peer_subkeys": nrm(ks[21], (DEPTH, PEER_HEADS, 2, N_KEYS, PEER_KEY_DIM // 2),
                            (PEER_KEY_DIM // 2) ** -0.5),
        "peer_u": nrm(ks[22], (DEPTH, N_EXPERTS, D), D ** -0.5),
        "peer_v": nrm(ks[23], (DEPTH, N_EXPERTS, D), PEER_HEADS ** -0.5),
        "final_norm_w": 1.0 + nrm(ks[24], (D,), 0.02),
    }


def reference(x, c, ctx, c_ctx, ada_w, ada_b, norm1_w, norm2_w, w_in, ssm_conv_w, ssm_conv_b,
              ssm_dt_bias, ssm_a_log, ssm_d, ssm_norm_w, cfm_conv_w, cfm_conv_b, cfm_ln_w, cfm_ln_b,
              w_out, peer_wq, peer_subkeys, peer_u, peer_v, final_norm_w):
    rows = x.shape[1] // GRID_W
    sc = jax.nn.silu(c)
    sc_ctx = jax.nn.silu(c_ctx)[None]
    for i in range(DEPTH):
        update_ctx = i < DEPTH - 1
        mod = (sc @ ada_w[i] + ada_b[i]).reshape(-1, 6, D_MODEL)
        mod_ctx = (sc_ctx @ ada_w[i] + ada_b[i]).reshape(1, 6, D_MODEL)
        x, ctx = mixer_sublayer(x, ctx, mod, mod_ctx, rows, norm1_w[i], w_in[i], ssm_conv_w[i],
                                ssm_conv_b[i], ssm_dt_bias[i], ssm_a_log[i], ssm_d[i], ssm_norm_w[i],
                                cfm_conv_w[i], cfm_conv_b[i], cfm_ln_w[i], cfm_ln_b[i], w_out[i],
                                update_ctx)
        x = x + mod[:, 5, None, :] * peer(modulate(rms_norm(x, norm2_w[i]), mod[:, 3], mod[:, 4]),
                                          peer_wq[i], peer_subkeys[i], peer_u[i], peer_v[i])
        if update_ctx:
            ctx = ctx + mod_ctx[:, 5, None, :] * peer(
                modulate(rms_norm(ctx, norm2_w[i]), mod_ctx[:, 3], mod_ctx[:, 4]),
                peer_wq[i], peer_subkeys[i], peer_u[i], peer_v[i])
    return rms_norm(x, final_norm_w)
```

```python
import functools

import jax
import jax.numpy as jnp
from jax import lax
from jax.experimental import pallas as pl
from jax.experimental.pallas import tpu as pltpu

F32 = jnp.float32
BF16 = jnp.bfloat16

D_MODEL = 1024
GRID_W = 64
D_SSM = 512
SSM_HEADDIM = 64
SSM_HEADS = 8
SSM_STATE = 128
SSM_GROUPS = 2
SSM_CONV = 5
SSD_CHUNK = 128
XBC_DIM = D_SSM + 2 * SSM_GROUPS * SSM_STATE
D_CONV = 512
CONV_WIDTH = 31
PEER_HEADS = 8
PEER_KEY_DIM = 256
N_KEYS = 128
PEER_TOPK = 16
EPS = 1e-6

LANES = 128
SUBLANES = 8
VMEM_LIMIT = 56 * 1024 * 1024

IN_TILE = 512
CFM_ROWS = 16
OUT_TILE = 512
SCORE_TILE = 512
PEER_TILE = 512
PEER_EBLK = 1024
PEER_SUB = 256


def _params(sem):
    return pltpu.CompilerParams(dimension_semantics=sem, vmem_limit_bytes=VMEM_LIMIT)


def _silu(v):
    return v * jax.nn.sigmoid(v)


def _split3(v):
    hi = v.astype(BF16)
    r1 = v - hi.astype(F32)
    mid = r1.astype(BF16)
    lo = (r1 - mid.astype(F32)).astype(BF16)
    return hi, mid, lo


def _dot01_right(v, m01):
    return sum(jnp.dot(p, m01, preferred_element_type=F32) for p in _split3(v))


def _dot01_left(m01, v):
    return sum(jnp.dot(m01, p, preferred_element_type=F32) for p in _split3(v))


def _ada_kernel(c_ref, w_ref, b_ref, o_ref):
    sc = _silu(c_ref[...])
    o_ref[...] = jnp.dot(sc.astype(BF16), w_ref[...].astype(BF16),
                         preferred_element_type=F32) + b_ref[...]


def _ada(c_rows, ada_w, ada_b):
    n = ada_w.shape[1]
    tn = 768
    return pl.pallas_call(
        _ada_kernel,
        grid=(n // tn,),
        in_specs=[pl.BlockSpec(c_rows.shape, lambda j: (0, 0)),
                  pl.BlockSpec((D_MODEL, tn), lambda j: (0, j)),
                  pl.BlockSpec((1, tn), lambda j: (0, j))],
        out_specs=pl.BlockSpec((c_rows.shape[0], tn), lambda j: (0, j)),
        out_shape=jax.ShapeDtypeStruct((c_rows.shape[0], n), F32),
        compiler_params=_params(("parallel",)),
        name="ada",
    )(c_rows, ada_w, ada_b.reshape(1, n))


def _inproj_kernel(x_ref, mod_ref, nw_ref, w_ref, z_ref, xbc_ref, u_ref, dt_ref):
    xv = x_ref[0]
    h = xv * lax.rsqrt(jnp.mean(xv * xv, axis=-1, keepdims=True) + EPS) * nw_ref[...]
    h = h * (1.0 + mod_ref[0, 1:2, :]) + mod_ref[0, 0:1, :]
    hb = h.astype(BF16)

    def proj(lo, hi):
        return jnp.dot(hb, w_ref[:, lo:hi], preferred_element_type=F32)

    z_ref[0] = proj(0, D_SSM)
    xbc_ref[0] = proj(D_SSM, D_SSM + XBC_DIM)
    o = D_SSM + XBC_DIM
    u_ref[0] = proj(o, o + D_CONV) * jax.nn.sigmoid(proj(o + D_CONV, o + 2 * D_CONV))
    dt_ref[0] = proj(o + 2 * D_CONV, o + 2 * D_CONV + LANES)


def _inproj(x, mod, per_batch_mod, norm_w, w_r):
    b, l, d = x.shape
    tm = min(IN_TILE, l)
    mod_idx = (lambda i, j: (i, 0, 0)) if per_batch_mod else (lambda i, j: (0, 0, 0))
    tok = lambda w: pl.BlockSpec((1, tm, w), lambda i, j: (i, j, 0))
    return pl.pallas_call(
        _inproj_kernel,
        grid=(b, l // tm),
        in_specs=[tok(d),
                  pl.BlockSpec((1, 6, d), mod_idx),
                  pl.BlockSpec((1, d), lambda i, j: (0, 0)),
                  pl.BlockSpec(w_r.shape, lambda i, j: (0, 0))],
        out_specs=[tok(D_SSM), tok(XBC_DIM), tok(D_CONV), tok(LANES)],
        out_shape=[jax.ShapeDtypeStruct((b, l, w), F32) for w in (D_SSM, XBC_DIM, D_CONV, LANES)],
        compiler_params=_params(("parallel", "parallel")),
        name="inproj",
    )(x, mod, norm_w.reshape(1, d), w_r)


def _ssd_kernel(cur_ref, prev_ref, next_ref, dt_ref, h0_ref, cw_ref, cb_ref, dtb_ref, alog_ref,
                dskip_ref, expand_ref, y_ref, hfin_ref, ext_ref, state_ref, *, reverse, nc, add_skip):
    q = SSD_CHUNK
    halo = SUBLANES
    pad = (SSM_CONV - 1) // 2
    gw = D_SSM // SSM_GROUPS
    hpg = SSM_HEADS // SSM_GROUPS
    col0 = SSM_HEADS if reverse else 0
    c = pl.program_id(1)
    cc = (nc - 1 - c) if reverse else c

    @pl.when(c == 0)
    def _():
        state_ref[...] = h0_ref[0]

    ext_ref[0:halo, :] = jnp.where(cc == 0, 0.0, prev_ref[0])
    ext_ref[halo:halo + q, :] = cur_ref[0]
    ext_ref[halo + q:2 * halo + q, :] = jnp.where(cc == nc - 1, 0.0, next_ref[0])
    conv = cb_ref[...] + cw_ref[0:1, :] * ext_ref[halo - pad:halo - pad + q, :]
    for k in range(1, SSM_CONV):
        conv = conv + cw_ref[k:k + 1, :] * ext_ref[halo - pad + k:halo - pad + k + q, :]
    act = _silu(conv)

    dtv = dt_ref[0] + dtb_ref[...]
    dt = jnp.maximum(dtv, 0.0) + jnp.log(1.0 + jnp.exp(-jnp.abs(dtv)))
    dta = dt * (-jnp.exp(alog_ref[...]))
    ri = lax.broadcasted_iota(jnp.int32, (q, q), 0)
    ci = lax.broadcasted_iota(jnp.int32, (q, q), 1)
    causal = (ci >= ri) if reverse else (ci <= ri)
    acum = _dot01_left(jnp.where(causal, 1.0, 0.0).astype(BF16), dta)
    acum_t = acum.T
    tot = acum[0:1, :] if reverse else acum[q - 1:q, :]
    stacked = jnp.concatenate([dt, jnp.exp(tot - acum), jnp.exp(acum),
                               jnp.broadcast_to(jnp.exp(tot), (SUBLANES, LANES))], axis=0)
    spread = _dot01_right(stacked, expand_ref[...])
    dt_e, dte_e, ea_e, etot_e = spread[0:q], spread[q:2 * q], spread[2 * q:3 * q], spread[3 * q:3 * q + 1]

    xs = act[:, :D_SSM]
    xdt = xs * dt_e
    xdt_b = xdt.astype(BF16)
    w_b = (xdt * dte_e).astype(BF16)
    for g in range(SSM_GROUPS):
        gs = slice(g * gw, (g + 1) * gw)
        b_g = act[:, D_SSM + g * SSM_STATE:D_SSM + (g + 1) * SSM_STATE]
        c_g = act[:, D_SSM + (SSM_GROUPS + g) * SSM_STATE:D_SSM + (SSM_GROUPS + g + 1) * SSM_STATE]
        c_b = c_g.astype(BF16)
        bt_b = b_g.T.astype(BF16)
        cb = jnp.dot(c_b, bt_b, preferred_element_type=F32)
        st = state_ref[:, gs]
        y_g = jnp.dot(c_b, st.astype(BF16), preferred_element_type=F32) * ea_e[:, gs]
        y_heads = []
        for hh in range(hpg):
            h = g * hpg + hh
            col = col0 + h
            diff = acum[:, col:col + 1] - acum_t[col:col + 1, :]
            decay = jnp.where(causal, jnp.exp(jnp.minimum(diff, 0.0)), 0.0)
            y_heads.append(jnp.dot((cb * decay).astype(BF16),
                                   xdt_b[:, h * SSM_HEADDIM:(h + 1) * SSM_HEADDIM],
                                   preferred_element_type=F32))
        y_g = y_g + jnp.concatenate(y_heads, axis=-1)
        if add_skip:
            y_g = y_g + dskip_ref[:, gs] * xs[:, gs]
        y_ref[0, :, gs] = y_g
        state_ref[:, gs] = st * etot_e[:, gs] + jnp.dot(bt_b, w_b[:, gs], preferred_element_type=F32)
    hfin_ref[0] = state_ref[...]


def _ssd(xbc, dtp, h0, conv_w, conv_b, dt_bias, a_log, d_skip, expand, reverse, add_skip):
    b, l, _ = xbc.shape
    q = SSD_CHUNK
    nc = l // q
    hb = q // SUBLANES
    pos = (lambda c: nc - 1 - c) if reverse else (lambda c: c)
    const = lambda shape: pl.BlockSpec(shape, lambda i, c: (0,) * len(shape))
    kern = functools.partial(_ssd_kernel, reverse=reverse, nc=nc, add_skip=add_skip)
    return pl.pallas_call(
        kern,
        grid=(b, nc),
        in_specs=[pl.BlockSpec((1, q, XBC_DIM), lambda i, c: (i, pos(c), 0)),
                  pl.BlockSpec((1, SUBLANES, XBC_DIM), lambda i, c: (i, jnp.maximum(pos(c) * hb - 1, 0), 0)),
                  pl.BlockSpec((1, SUBLANES, XBC_DIM),
                               lambda i, c: (i, jnp.minimum((pos(c) + 1) * hb, l // SUBLANES - 1), 0)),
                  pl.BlockSpec((1, q, LANES), lambda i, c: (i, pos(c), 0)),
                  pl.BlockSpec((1, SSM_STATE, D_SSM), lambda i, c: (i, 0, 0)),
                  const(conv_w.shape), const(conv_b.shape), const(dt_bias.shape), const(a_log.shape),
                  const(d_skip.shape), const(expand.shape)],
        out_specs=[pl.BlockSpec((1, q, D_SSM), lambda i, c: (i, pos(c), 0)),
                   pl.BlockSpec((1, SSM_STATE, D_SSM), lambda i, c: (i, 0, 0))],
        out_shape=[jax.ShapeDtypeStruct((b, l, D_SSM), F32),
                   jax.ShapeDtypeStruct((b, SSM_STATE, D_SSM), F32)],
        scratch_shapes=[pltpu.VMEM((q + 2 * SUBLANES, XBC_DIM), F32),
                        pltpu.VMEM((SSM_STATE, D_SSM), F32)],
        compiler_params=_params(("parallel", "arbitrary")),
        name="ssd_bwd" if reverse else "ssd_fwd",
    )(xbc, xbc, xbc, dtp, h0, conv_w, conv_b, dt_bias, a_log, d_skip, expand)


def _cfm_kernel(cur_ref, prev_ref, next_ref, w_ref, b_ref, lnw_ref, lnb_ref, o_ref, hpad_ref, vbuf_ref,
                *, nb):
    rows = CFM_ROWS
    band = rows * GRID_W
    half = D_CONV // 2
    pad = (CONV_WIDTH - 1) // 2
    lead = 2 * SUBLANES
    r = pl.program_id(1)

    zeros = jnp.zeros((rows, lead, half), F32)
    hpad_ref[:, 0:lead, :] = zeros
    hpad_ref[:, lead + GRID_W:2 * lead + GRID_W, :] = zeros
    hpad_ref[:, lead:lead + GRID_W, :] = cur_ref[0, :, 0:half].reshape(rows, GRID_W, half)
    vbuf_ref[0:band, :] = jnp.where(r == 0, 0.0, prev_ref[0])
    vbuf_ref[band:2 * band, :] = cur_ref[0, :, half:D_CONV]
    vbuf_ref[2 * band:3 * band, :] = jnp.where(r == nb - 1, 0.0, next_ref[0])

    def row_body(i, carry):
        acc_h = jnp.zeros((GRID_W, half), F32)
        acc_v = jnp.zeros((GRID_W, half), F32)
        for k in range(CONV_WIDTH):
            acc_h = acc_h + w_ref[k:k + 1, 0:half] * hpad_ref[i, lead - pad + k:lead - pad + k + GRID_W, :]
            start = pl.multiple_of(band + (i + k - pad) * GRID_W, GRID_W)
            acc_v = acc_v + w_ref[k:k + 1, half:D_CONV] * vbuf_ref[pl.ds(start, GRID_W), :]
        conv = jnp.concatenate([acc_h, acc_v], axis=-1) + b_ref[...]
        mu = jnp.mean(conv, axis=-1, keepdims=True)
        cen = conv - mu
        var = jnp.mean(cen * cen, axis=-1, keepdims=True)
        o_ref[0, pl.ds(pl.multiple_of(i * GRID_W, GRID_W), GRID_W), :] = _silu(
            cen * lax.rsqrt(var + EPS) * lnw_ref[...] + lnb_ref[...])
        return carry

    lax.fori_loop(0, rows, row_body, 0)


def _cfm(u, conv_w, conv_b, ln_w, ln_b):
    b, l, _ = u.shape
    band = CFM_ROWS * GRID_W
    nb = l // band
    half = D_CONV // 2
    const = lambda shape: pl.BlockSpec(shape, lambda i, r: (0,) * len(shape))
    return pl.pallas_call(
        functools.partial(_cfm_kernel, nb=nb),
        grid=(b, nb),
        in_specs=[pl.BlockSpec((1, band, D_CONV), lambda i, r: (i, r, 0)),
                  pl.BlockSpec((1, band, half), lambda i, r: (i, jnp.maximum(r - 1, 0), 1)),
                  pl.BlockSpec((1, band, half), lambda i, r: (i, jnp.minimum(r + 1, nb - 1), 1)),
                  const(conv_w.shape), const(conv_b.shape), const(ln_w.shape), const(ln_b.shape)],
        out_specs=pl.BlockSpec((1, band, D_CONV), lambda i, r: (i, r, 0)),
        out_shape=jax.ShapeDtypeStruct((b, l, D_CONV), F32),
        scratch_shapes=[pltpu.VMEM((CFM_ROWS, GRID_W + 4 * SUBLANES, half), F32),
                        pltpu.VMEM((3 * band, half), F32)],
        compiler_params=_params(("parallel", "parallel")),
        name="cfm",
    )(u, u, u, conv_w, conv_b, ln_w, ln_b)


def _outproj_kernel(yf_ref, yb_ref, z_ref, cfm_ref, x_ref, mod_ref, gnw_ref, w_ref, n2w_ref,
                    x1_ref, h2_ref):
    gw = D_SSM // SSM_GROUPS
    g = (yf_ref[0] + yb_ref[0]) * _silu(z_ref[0])
    parts = []
    for k in range(SSM_GROUPS):
        gk = g[:, k * gw:(k + 1) * gw]
        parts.append(gk * lax.rsqrt(jnp.mean(gk * gk, axis=-1, keepdims=True) + EPS))
    ssm = jnp.concatenate(parts, axis=-1) * gnw_ref[...]
    mixed = (jnp.dot(ssm.astype(BF16), w_ref[0:D_SSM, :], preferred_element_type=F32)
             + jnp.dot(cfm_ref[0].astype(BF16), w_ref[D_SSM:, :], preferred_element_type=F32))
    x1 = x_ref[0] + mod_ref[0, 2:3, :] * mixed
    x1_ref[0] = x1
    h = x1 * lax.rsqrt(jnp.mean(x1 * x1, axis=-1, keepdims=True) + EPS) * n2w_ref[...]
    h2_ref[0] = (h * (1.0 + mod_ref[0, 4:5, :]) + mod_ref[0, 3:4, :]).astype(BF16)


def _outproj(yf, yb, z, cfm, x, mod, gn_w, w_out_b, n2_w):
    b, l, d = x.shape
    tm = min(OUT_TILE, l)
    tok = lambda w: pl.BlockSpec((1, tm, w), lambda i, j: (i, j, 0))
    const = lambda shape: pl.BlockSpec(shape, lambda i, j: (0,) * len(shape))
    return pl.pallas_call(
        _outproj_kernel,
        grid=(b, l // tm),
        in_specs=[tok(D_SSM), tok(D_SSM), tok(D_SSM), tok(D_CONV), tok(d),
                  pl.BlockSpec((1, 6, d), lambda i, j: (i, 0, 0)),
                  const(gn_w.shape), const(w_out_b.shape), const(n2_w.shape)],
        out_specs=[tok(d), tok(d)],
        out_shape=[jax.ShapeDtypeStruct((b, l, d), F32), jax.ShapeDtypeStruct((b, l, d), BF16)],
        compiler_params=_params(("parallel", "parallel")),
        name="outproj",
    )(yf, yb, z, cfm, x, mod, gn_w, w_out_b, n2_w)


def _top_values(s, k):
    vals = []
    for _ in range(k):
        m = jnp.max(s, axis=0, keepdims=True)
        vals.append(m)
        s = jnp.where(s == m, -jnp.inf, s)
    return jnp.concatenate(vals, axis=0)


def _score_kernel(h_ref, wq_ref, keys_ref, s1_ref, e1_ref, s2_ref, e2_ref, tau_ref, q_ref):
    tm = h_ref.shape[0]
    nsub = tm // LANES
    kd = PEER_KEY_DIM // 2
    q_ref[...] = jnp.dot(h_ref[...], wq_ref[...], preferred_element_type=F32).astype(BF16)

    def body(it, carry):
        h = it // nsub
        tok = pl.multiple_of((it % nsub) * LANES, LANES)
        nt = (((1,), (1,)), ((), ()))

        def scores(p):
            qs = q_ref[pl.ds(tok, LANES), pl.ds(pl.multiple_of((2 * h + p) * kd, kd), kd)]
            return lax.dot_general(keys_ref[2 * h + p], qs, nt, preferred_element_type=F32)

        s1 = scores(0)
        s2 = scores(1)
        a = _top_values(s1, PEER_TOPK)
        b = _top_values(s2, PEER_TOPK)
        cand = jnp.concatenate([a[i:i + 1, :] + b for i in range(PEER_TOPK)], axis=0)
        tau = _top_values(cand, PEER_TOPK)[PEER_TOPK - 1:PEER_TOPK, :]
        best = a[0:1, :] + b[0:1, :]
        zsum = jnp.sum(jnp.where(cand >= tau, jnp.exp(cand - best), 0.0), axis=0, keepdims=True)
        s1_ref[h, :, pl.ds(tok, LANES)] = s1
        s2_ref[h, :, pl.ds(tok, LANES)] = s2
        e1_ref[h, :, pl.ds(tok, LANES)] = jnp.exp(s1 - a[0:1, :]) / zsum
        e2_ref[h, :, pl.ds(tok, LANES)] = jnp.exp(s2 - b[0:1, :])
        tau_ref[h, :, pl.ds(tok, LANES)] = tau
        return carry

    lax.fori_loop(0, PEER_HEADS * nsub, body, 0)


def _score(h2, wq_b, keys_b):
    t, d = h2.shape
    tm = min(SCORE_TILE, t)
    big = lambda: pl.BlockSpec((PEER_HEADS, N_KEYS, tm), lambda j: (0, 0, j))
    big_shape = jax.ShapeDtypeStruct((PEER_HEADS, N_KEYS, t), F32)
    return pl.pallas_call(
        _score_kernel,
        grid=(t // tm,),
        in_specs=[pl.BlockSpec((tm, d), lambda j: (j, 0)),
                  pl.BlockSpec(wq_b.shape, lambda j: (0, 0)),
                  pl.BlockSpec(keys_b.shape, lambda j: (0, 0, 0))],
        out_specs=[big(), big(), big(), big(), pl.BlockSpec((PEER_HEADS, 1, tm), lambda j: (0, 0, j))],
        out_shape=[big_shape, big_shape, big_shape, big_shape,
                   jax.ShapeDtypeStruct((PEER_HEADS, 1, t), F32)],
        scratch_shapes=[pltpu.VMEM((tm, PEER_HEADS * PEER_KEY_DIM), BF16)],
        compiler_params=_params(("parallel",)),
        name="peer_score",
    )(h2, wq_b, keys_b)


def _peer_kernel(h_ref, u_ref, vt_ref, s1_ref, e1_ref, s2_ref, e2_ref, tau_ref, x1_ref, mod_ref, fnw_ref,
                 o_ref, acc_ref):
    eb = pl.program_id(1)
    nsub = PEER_EBLK // PEER_SUB
    keys_per_sub = PEER_SUB // N_KEYS
    nt = (((1,), (1,)), ((), ()))

    @pl.when(eb == 0)
    def _():
        acc_ref[...] = jnp.zeros_like(acc_ref)

    def sub_body(sub, carry):
        e0 = pl.multiple_of(sub * PEER_SUB, PEER_SUB)
        pre = lax.dot_general(u_ref[pl.ds(e0, PEER_SUB), :], h_ref[...], nt,
                              preferred_element_type=F32)
        act = 0.5 * pre * (1.0 + lax.erf(pre * (2.0 ** -0.5)))
        gates = []
        for kk in range(keys_per_sub):
            i1 = (eb * nsub + sub) * keys_per_sub + kk
            g = jnp.zeros((N_KEYS, h_ref.shape[0]), F32)
            for h in range(PEER_HEADS):
                total = s2_ref[h] + s1_ref[h, pl.ds(i1, 1), :]
                g = g + jnp.where(total >= tau_ref[h], e2_ref[h] * e1_ref[h, pl.ds(i1, 1), :], 0.0)
            gates.append(g)
        w = (act * jnp.concatenate(gates, axis=0)).astype(BF16)
        acc_ref[...] += jnp.dot(vt_ref[:, pl.ds(e0, PEER_SUB)], w, preferred_element_type=F32)
        return carry

    lax.fori_loop(0, nsub, sub_body, 0)

    @pl.when(eb == pl.num_programs(1) - 1)
    def _():
        x2 = x1_ref[...] + mod_ref[0, 5:6, :] * acc_ref[...].T
        o_ref[...] = x2 * lax.rsqrt(jnp.mean(x2 * x2, axis=-1, keepdims=True) + EPS) * fnw_ref[...]


def _peer(h2, u_b, vt_b, s1, e1, s2, e2, tau, x1, mod, fn_w, seq):
    t, d = h2.shape
    tm = min(PEER_TILE, seq)
    n_exp = u_b.shape[0]
    big = lambda: pl.BlockSpec((PEER_HEADS, N_KEYS, tm), lambda j, e: (0, 0, j))
    return pl.pallas_call(
        _peer_kernel,
        grid=(t // tm, n_exp // PEER_EBLK),
        in_specs=[pl.BlockSpec((tm, d), lambda j, e: (j, 0)),
                  pl.BlockSpec((PEER_EBLK, d), lambda j, e: (e, 0)),
                  pl.BlockSpec((d, PEER_EBLK), lambda j, e: (0, e)),
                  big(), big(), big(), big(),
                  pl.BlockSpec((PEER_HEADS, 1, tm), lambda j, e: (0, 0, j)),
                  pl.BlockSpec((tm, d), lambda j, e: (j, 0)),
                  pl.BlockSpec((1, 6, d), lambda j, e: ((j * tm) // seq, 0, 0)),
                  pl.BlockSpec((1, d), lambda j, e: (0, 0))],
        out_specs=pl.BlockSpec((tm, d), lambda j, e: (j, 0)),
        out_shape=jax.ShapeDtypeStruct((t, d), F32),
        scratch_shapes=[pltpu.VMEM((d, tm), F32)],
        compiler_params=_params(("parallel", "arbitrary")),
        name="peer_dense",
    )(h2, u_b, vt_b, s1, e1, s2, e2, tau, x1, mod, fn_w)


def _pad_lanes(v, width=LANES):
    flat = v.reshape(1, -1)
    return jnp.pad(flat, ((0, 0), (0, width - flat.shape[1])))


def _head_expand(col0):
    rows = lax.broadcasted_iota(jnp.int32, (LANES, D_SSM), 0)
    heads = lax.broadcasted_iota(jnp.int32, (LANES, D_SSM), 1) // SSM_HEADDIM
    return (rows == heads + col0).astype(BF16)


def kernel(x, c, ctx, c_ctx, ada_w, ada_b, norm1_w, norm2_w, w_in, ssm_conv_w, ssm_conv_b, ssm_dt_bias,
           ssm_a_log, ssm_d, ssm_norm_w, cfm_conv_w, cfm_conv_b, cfm_ln_w, cfm_ln_b, w_out, peer_wq,
           peer_subkeys, peer_u, peer_v, final_norm_w):
    depth = ada_w.shape[0]
    assert depth == 1, "single-layer configuration"
    b, l, d = x.shape
    i = 0

    c_rows = jnp.concatenate([c, c_ctx[None], jnp.zeros((2 * SUBLANES - b - 1, d), F32)], axis=0)
    mod_all = _ada(c_rows, ada_w[i], ada_b[i])
    mod = mod_all[:b].reshape(b, 6, d)
    mod_ctx = mod_all[b:b + 1].reshape(1, 6, d)

    wi = w_in[i]
    o_dt = D_SSM + XBC_DIM
    w_r = jnp.concatenate([wi[:, :o_dt], wi[:, o_dt + 2 * SSM_HEADS:], wi[:, o_dt:o_dt + 2 * SSM_HEADS],
                           jnp.zeros((d, LANES - 2 * SSM_HEADS), F32)], axis=1).astype(BF16)
    z_l, xbc_l, u_l, dt_l = _inproj(x, mod, True, norm1_w[i], w_r)
    _, xbc_c, _, dt_c = _inproj(ctx, mod_ctx, False, norm1_w[i], w_r)

    conv_w = jnp.pad(ssm_conv_w[i], ((0, SUBLANES - SSM_CONV), (0, 0)))
    conv_b = ssm_conv_b[i].reshape(1, XBC_DIM)
    dt_bias = _pad_lanes(ssm_dt_bias[i])
    a_log = _pad_lanes(ssm_a_log[i])
    d_skip = jnp.repeat(ssm_d[i], SSM_HEADDIM).reshape(1, D_SSM)
    zero_state = jnp.zeros((b, SSM_STATE, D_SSM), F32)
    y_dirs = []
    for reverse in (False, True):
        expand = _head_expand(SSM_HEADS if reverse else 0)
        args = (conv_w, conv_b, dt_bias, a_log, d_skip, expand, reverse)
        _, h_ctx = _ssd(xbc_c, dt_c, zero_state, *args, add_skip=False)
        y_dir, _ = _ssd(xbc_l, dt_l, h_ctx, *args, add_skip=not reverse)
        y_dirs.append(y_dir)

    cfm = _cfm(u_l, jnp.pad(cfm_conv_w[i], ((0, 1), (0, 0))), cfm_conv_b[i].reshape(1, D_CONV),
               cfm_ln_w[i].reshape(1, D_CONV), cfm_ln_b[i].reshape(1, D_CONV))

    x1, h2 = _outproj(y_dirs[0], y_dirs[1], z_l, cfm, x, mod, ssm_norm_w[i].reshape(1, D_SSM),
                      w_out[i].astype(BF16), norm2_w[i].reshape(1, d))

    t = b * l
    h2 = h2.reshape(t, d)
    keys_b = peer_subkeys[i].reshape(PEER_HEADS * 2, N_KEYS, PEER_KEY_DIM // 2).astype(BF16)
    s1, e1, s2, e2, tau = _score(h2, peer_wq[i].astype(BF16), keys_b)
    out = _peer(h2, peer_u[i].astype(BF16), peer_v[i].T.astype(BF16), s1, e1, s2, e2, tau,
                x1.reshape(t, d), mod, final_norm_w.reshape(1, d), l)
    return out.reshape(b, l, d)
```

```python
import functools

import jax
import jax.numpy as jnp
from jax import lax
from jax.experimental import pallas as pl
from jax.experimental.pallas import tpu as pltpu

F32 = jnp.float32
BF16 = jnp.bfloat16

D_MODEL = 1024
GRID_W = 64
D_SSM = 512
SSM_HEADDIM = 64
SSM_HEADS = 8
SSM_STATE = 128
SSM_GROUPS = 2
SSM_CONV = 5
SSD_CHUNK = 128
XBC_DIM = D_SSM + 2 * SSM_GROUPS * SSM_STATE
D_CONV = 512
CONV_WIDTH = 31
PEER_HEADS = 8
PEER_KEY_DIM = 256
N_KEYS = 128
PEER_TOPK = 16
EPS = 1e-6

LANES = 128
SUBLANES = 8
VMEM_LIMIT = 56 * 1024 * 1024

IN_TILE = 512
CFM_ROWS = 16
OUT_TILE = 512
SCORE_TILE = 512
PEER_TILE = 512
PEER_EBLK = 1024
PEER_SUB = 256


def _params(sem):
    return pltpu.CompilerParams(dimension_semantics=sem, vmem_limit_bytes=VMEM_LIMIT)


def _silu(v):
    return v * jax.nn.sigmoid(v)


def _split3(v):
    hi = v.astype(BF16)
    r1 = v - hi.astype(F32)
    mid = r1.astype(BF16)
    lo = (r1 - mid.astype(F32)).astype(BF16)
    return hi, mid, lo


def _dot01_right(v, m01):
    return sum(jnp.dot(p, m01, preferred_element_type=F32) for p in _split3(v))


def _dot01_left(m01, v):
    return sum(jnp.dot(m01, p, preferred_element_type=F32) for p in _split3(v))


def _ada_kernel(c_ref, w_ref, b_ref, o_ref):
    sc = _silu(c_ref[...])
    o_ref[...] = jnp.dot(sc.astype(BF16), w_ref[...].astype(BF16),
                         preferred_element_type=F32) + b_ref[...]


def _ada(c_rows, ada_w, ada_b):
    n = ada_w.shape[1]
    tn = 768
    return pl.pallas_call(
        _ada_kernel,
        grid=(n // tn,),
        in_specs=[pl.BlockSpec(c_rows.shape, lambda j: (0, 0)),
                  pl.BlockSpec((D_MODEL, tn), lambda j: (0, j)),
                  pl.BlockSpec((1, tn), lambda j: (0, j))],
        out_specs=pl.BlockSpec((c_rows.shape[0], tn), lambda j: (0, j)),
        out_shape=jax.ShapeDtypeStruct((c_rows.shape[0], n), F32),
        compiler_params=_params(("parallel",)),
        name="ada",
    )(c_rows, ada_w, ada_b.reshape(1, n))


def _inproj_kernel(x_ref, mod_ref, nw_ref, w_ref, z_ref, xbc_ref, u_ref, dt_ref):
    xv = x_ref[0]
    h = xv * lax.rsqrt(jnp.mean(xv * xv, axis=-1, keepdims=True) + EPS) * nw_ref[...]
    h = h * (1.0 + mod_ref[0, 1:2, :]) + mod_ref[0, 0:1, :]
    hb = h.astype(BF16)

    def proj(lo, hi):
        return jnp.dot(hb, w_ref[:, lo:hi], preferred_element_type=F32)

    z_ref[0] = proj(0, D_SSM)
    xbc_ref[0] = proj(D_SSM, D_SSM + XBC_DIM)
    o = D_SSM + XBC_DIM
    u_ref[0] = proj(o, o + D_CONV) * jax.nn.sigmoid(proj(o + D_CONV, o + 2 * D_CONV))
    dt_ref[0] = proj(o + 2 * D_CONV, o + 2 * D_CONV + LANES)


def _inproj(x, mod, per_batch_mod, norm_w, w_r):
    b, l, d = x.shape
    tm = min(IN_TILE, l)
    mod_idx = (lambda i, j: (i, 0, 0)) if per_batch_mod else (lambda i, j: (0, 0, 0))
    tok = lambda w: pl.BlockSpec((1, tm, w), lambda i, j: (i, j, 0))
    return pl.pallas_call(
        _inproj_kernel,
        grid=(b, l // tm),
        in_specs=[tok(d),
                  pl.BlockSpec((1, 6, d), mod_idx),
                  pl.BlockSpec((1, d), lambda i, j: (0, 0)),
                  pl.BlockSpec(w_r.shape, lambda i, j: (0, 0))],
        out_specs=[tok(D_SSM), tok(XBC_DIM), tok(D_CONV), tok(LANES)],
        out_shape=[jax.ShapeDtypeStruct((b, l, w), F32) for w in (D_SSM, XBC_DIM, D_CONV, LANES)],
        compiler_params=_params(("parallel", "parallel")),
        name="inproj",
    )(x, mod, norm_w.reshape(1, d), w_r)


def _ssd_kernel(cur_ref, prev_ref, next_ref, dt_ref, h0_ref, cw_ref, cb_ref, dtb_ref, alog_ref,
                dskip_ref, expand_ref, y_ref, hfin_ref, ext_ref, state_ref, *, reverse, nc, add_skip):
    q = SSD_CHUNK
    halo = SUBLANES
    pad = (SSM_CONV - 1) // 2
    gw = D_SSM // SSM_GROUPS
    hpg = SSM_HEADS // SSM_GROUPS
    col0 = SSM_HEADS if reverse else 0
    c = pl.program_id(1)
    cc = (nc - 1 - c) if reverse else c

    @pl.when(c == 0)
    def _():
        state_ref[...] = h0_ref[0]

    ext_ref[0:halo, :] = jnp.where(cc == 0, 0.0, prev_ref[0])
    ext_ref[halo:halo + q, :] = cur_ref[0]
    ext_ref[halo + q:2 * halo + q, :] = jnp.where(cc == nc - 1, 0.0, next_ref[0])
    conv = cb_ref[...] + cw_ref[0:1, :] * ext_ref[halo - pad:halo - pad + q, :]
    for k in range(1, SSM_CONV):
        conv = conv + cw_ref[k:k + 1, :] * ext_ref[halo - pad + k:halo - pad + k + q, :]
    act = _silu(conv)

    dtv = dt_ref[0] + dtb_ref[...]
    dt = jnp.maximum(dtv, 0.0) + jnp.log(1.0 + jnp.exp(-jnp.abs(dtv)))
    dta = dt * (-jnp.exp(alog_ref[...]))
    ri = lax.broadcasted_iota(jnp.int32, (q, q), 0)
    ci = lax.broadcasted_iota(jnp.int32, (q, q), 1)
    causal = (ci >= ri) if reverse else (ci <= ri)
    acum = _dot01_left(jnp.where(causal, 1.0, 0.0).astype(BF16), dta)
    acum_t = acum.T
    tot = acum[0:1, :] if reverse else acum[q - 1:q, :]
    stacked = jnp.concatenate([dt, jnp.exp(tot - acum), jnp.exp(acum),
                               jnp.broadcast_to(jnp.exp(tot), (SUBLANES, LANES))], axis=0)
    spread = _dot01_right(stacked, expand_ref[...])
    dt_e, dte_e, ea_e, etot_e = spread[0:q], spread[q:2 * q], spread[2 * q:3 * q], spread[3 * q:3 * q + 1]

    xs = act[:, :D_SSM]
    xdt = xs * dt_e
    xdt_b = xdt.astype(BF16)
    w_b = (xdt * dte_e).astype(BF16)
    for g in range(SSM_GROUPS):
        gs = slice(g * gw, (g + 1) * gw)
        b_g = act[:, D_SSM + g * SSM_STATE:D_SSM + (g + 1) * SSM_STATE]
        c_g = act[:, D_SSM + (SSM_GROUPS + g) * SSM_STATE:D_SSM + (SSM_GROUPS + g + 1) * SSM_STATE]
        c_b = c_g.astype(BF16)
        bt_b = b_g.T.astype(BF16)
        cb = jnp.dot(c_b, bt_b, preferred_element_type=F32)
        st = state_ref[:, gs]
        y_g = jnp.dot(c_b, st.astype(BF16), preferred_element_type=F32) * ea_e[:, gs]
        y_heads = []
        for hh in range(hpg):
            h = g * hpg + hh
            col = col0 + h
            diff = acum[:, col:col + 1] - acum_t[col:col + 1, :]
            decay = jnp.where(causal, jnp.exp(jnp.minimum(diff, 0.0)), 0.0)
            y_heads.append(jnp.dot((cb * decay).astype(BF16),
                                   xdt_b[:, h * SSM_HEADDIM:(h + 1) * SSM_HEADDIM],
                                   preferred_element_type=F32))
        y_g = y_g + jnp.concatenate(y_heads, axis=-1)
        if add_skip:
            y_g = y_g + dskip_ref[:, gs] * xs[:, gs]
        y_ref[0, :, gs] = y_g
        state_ref[:, gs] = st * etot_e[:, gs] + jnp.dot(bt_b, w_b[:, gs], preferred_element_type=F32)
    hfin_ref[0] = state_ref[...]


def _ssd(xbc, dtp, h0, conv_w, conv_b, dt_bias, a_log, d_skip, expand, reverse, add_skip):
    b, l, _ = xbc.shape
    q = SSD_CHUNK
    nc = l // q
    hb = q // SUBLANES
    pos = (lambda c: nc - 1 - c) if reverse else (lambda c: c)
    const = lambda shape: pl.BlockSpec(shape, lambda i, c: (0,) * len(shape))
    kern = functools.partial(_ssd_kernel, reverse=reverse, nc=nc, add_skip=add_skip)
    return pl.pallas_call(
        kern,
        grid=(b, nc),
        in_specs=[pl.BlockSpec((1, q, XBC_DIM), lambda i, c: (i, pos(c), 0)),
                  pl.BlockSpec((1, SUBLANES, XBC_DIM), lambda i, c: (i, jnp.maximum(pos(c) * hb - 1, 0), 0)),
                  pl.BlockSpec((1, SUBLANES, XBC_DIM),
                               lambda i, c: (i, jnp.minimum((pos(c) + 1) * hb, l // SUBLANES - 1), 0)),
                  pl.BlockSpec((1, q, LANES), lambda i, c: (i, pos(c), 0)),
                  pl.BlockSpec((1, SSM_STATE, D_SSM), lambda i, c: (i, 0, 0)),
                  const(conv_w.shape), const(conv_b.shape), const(dt_bias.shape), const(a_log.shape),
                  const(d_skip.shape), const(expand.shape)],
        out_specs=[pl.BlockSpec((1, q, D_SSM), lambda i, c: (i, pos(c), 0)),
                   pl.BlockSpec((1, SSM_STATE, D_SSM), lambda i, c: (i, 0, 0))],
        out_shape=[jax.ShapeDtypeStruct((b, l, D_SSM), F32),
                   jax.ShapeDtypeStruct((b, SSM_STATE, D_SSM), F32)],
        scratch_shapes=[pltpu.VMEM((q + 2 * SUBLANES, XBC_DIM), F32),
                        pltpu.VMEM((SSM_STATE, D_SSM), F32)],
        compiler_params=_params(("parallel", "arbitrary")),
        name="ssd_bwd" if reverse else "ssd_fwd",
    )(xbc, xbc, xbc, dtp, h0, conv_w, conv_b, dt_bias, a_log, d_skip, expand)


def _cfm_kernel(cur_ref, prev_ref, next_ref, w_ref, b_ref, lnw_ref, lnb_ref, o_ref, hpad_ref, vbuf_ref,
                *, nb):
    rows = CFM_ROWS
    band = rows * GRID_W
    half = D_CONV // 2
    pad = (CONV_WIDTH - 1) // 2
    lead = 2 * SUBLANES
    r = pl.program_id(1)

    zeros = jnp.zeros((rows, lead, half), F32)
    hpad_ref[:, 0:lead, :] = zeros
    hpad_ref[:, lead + GRID_W:2 * lead + GRID_W, :] = zeros
    hpad_ref[:, lead:lead + GRID_W, :] = cur_ref[0, :, 0:half].reshape(rows, GRID_W, half)
    vbuf_ref[0:band, :] = jnp.where(r == 0, 0.0, prev_ref[0])
    vbuf_ref[band:2 * band, :] = cur_ref[0, :, half:D_CONV]
    vbuf_ref[2 * band:3 * band, :] = jnp.where(r == nb - 1, 0.0, next_ref[0])

    def row_body(i, carry):
        acc_h = jnp.zeros((GRID_W, half), F32)
        acc_v = jnp.zeros((GRID_W, half), F32)
        for k in range(CONV_WIDTH):
            acc_h = acc_h + w_ref[k:k + 1, 0:half] * hpad_ref[i, lead - pad + k:lead - pad + k + GRID_W, :]
            start = pl.multiple_of(band + (i + k - pad) * GRID_W, GRID_W)
            acc_v = acc_v + w_ref[k:k + 1, half:D_CONV] * vbuf_ref[pl.ds(start, GRID_W), :]
        conv = jnp.concatenate([acc_h, acc_v], axis=-1) + b_ref[...]
        mu = jnp.mean(conv, axis=-1, keepdims=True)
        cen = conv - mu
        var = jnp.mean(cen * cen, axis=-1, keepdims=True)
        o_ref[0, pl.ds(pl.multiple_of(i * GRID_W, GRID_W), GRID_W), :] = _silu(
            cen * lax.rsqrt(var + EPS) * lnw_ref[...] + lnb_ref[...])
        return carry

    lax.fori_loop(0, rows, row_body, 0)


def _cfm(u, conv_w, conv_b, ln_w, ln_b):
    b, l, _ = u.shape
    band = CFM_ROWS * GRID_W
    nb = l // band
    half = D_CONV // 2
    const = lambda shape: pl.BlockSpec(shape, lambda i, r: (0,) * len(shape))
    return pl.pallas_call(
        functools.partial(_cfm_kernel, nb=nb),
        grid=(b, nb),
        in_specs=[pl.BlockSpec((1, band, D_CONV), lambda i, r: (i, r, 0)),
                  pl.BlockSpec((1, band, half), lambda i, r: (i, jnp.maximum(r - 1, 0), 1)),
                  pl.BlockSpec((1, band, half), lambda i, r: (i, jnp.minimum(r + 1, nb - 1), 1)),
                  const(conv_w.shape), const(conv_b.shape), const(ln_w.shape), const(ln_b.shape)],
        out_specs=pl.BlockSpec((1, band, D_CONV), lambda i, r: (i, r, 0)),
        out_shape=jax.ShapeDtypeStruct((b, l, D_CONV), F32),
        scratch_shapes=[pltpu.VMEM((CFM_ROWS, GRID_W + 4 * SUBLANES, half), F32),
                        pltpu.VMEM((3 * band, half), F32)],
        compiler_params=_params(("parallel", "parallel")),
        name="cfm",
    )(u, u, u, conv_w, conv_b, ln_w, ln_b)


def _outproj_kernel(yf_ref, yb_ref, z_ref, cfm_ref, x_ref, mod_ref, gnw_ref, w_ref, n2w_ref,
                    x1_ref, h2_ref):
    gw = D_SSM // SSM_GROUPS
    g = (yf_ref[0] + yb_ref[0]) * _silu(z_ref[0])
    parts = []
    for k in range(SSM_GROUPS):
        gk = g[:, k * gw:(k + 1) * gw]
        parts.append(gk * lax.rsqrt(jnp.mean(gk * gk, axis=-1, keepdims=True) + EPS))
    ssm = jnp.concatenate(parts, axis=-1) * gnw_ref[...]
    mixed = (jnp.dot(ssm.astype(BF16), w_ref[0:D_SSM, :], preferred_element_type=F32)
             + jnp.dot(cfm_ref[0].astype(BF16), w_ref[D_SSM:, :], preferred_element_type=F32))
    x1 = x_ref[0] + mod_ref[0, 2:3, :] * mixed
    x1_ref[0] = x1
    h = x1 * lax.rsqrt(jnp.mean(x1 * x1, axis=-1, keepdims=True) + EPS) * n2w_ref[...]
    h2_ref[0] = (h * (1.0 + mod_ref[0, 4:5, :]) + mod_ref[0, 3:4, :]).astype(BF16)


def _outproj(yf, yb, z, cfm, x, mod, gn_w, w_out_b, n2_w):
    b, l, d = x.shape
    tm = min(OUT_TILE, l)
    tok = lambda w: pl.BlockSpec((1, tm, w), lambda i, j: (i, j, 0))
    const = lambda shape: pl.BlockSpec(shape, lambda i, j: (0,) * len(shape))
    return pl.pallas_call(
        _outproj_kernel,
        grid=(b, l // tm),
        in_specs=[tok(D_SSM), tok(D_SSM), tok(D_SSM), tok(D_CONV), tok(d),
                  pl.BlockSpec((1, 6, d), lambda i, j: (i, 0, 0)),
                  const(gn_w.shape), const(w_out_b.shape), const(n2_w.shape)],
        out_specs=[tok(d), tok(d)],
        out_shape=[jax.ShapeDtypeStruct((b, l, d), F32), jax.ShapeDtypeStruct((b, l, d), BF16)],
        compiler_params=_params(("parallel", "parallel")),
        name="outproj",
    )(yf, yb, z, cfm, x, mod, gn_w, w_out_b, n2_w)


def _top_values(s, k):
    vals = []
    for _ in range(k):
        m = jnp.max(s, axis=0, keepdims=True)
        vals.append(m)
        s = jnp.where(s == m, -jnp.inf, s)
    return jnp.concatenate(vals, axis=0)


def _top_ranked(s, k):
    rank = jnp.full(s.shape, float(k), F32)
    vals = []
    for j in range(k):
        m = jnp.max(s, axis=0, keepdims=True)
        vals.append(m)
        hit = s == m
        rank = jnp.minimum(rank, jnp.where(hit, float(j), float(k)))
        s = jnp.where(hit, -jnp.inf, s)
    return vals, rank


def _score_kernel(h_ref, wq_ref, keys_ref, c1_ref, e1_ref, r2_ref, e2_ref, q_ref):
    tm = h_ref.shape[0]
    nsub = tm // LANES
    kd = PEER_KEY_DIM // 2
    q_ref[...] = jnp.dot(h_ref[...], wq_ref[...], preferred_element_type=F32).astype(BF16)

    def body(it, carry):
        h = it // nsub
        tok = pl.multiple_of((it % nsub) * LANES, LANES)
        nt = (((1,), (1,)), ((), ()))

        def scores(p):
            qs = q_ref[pl.ds(tok, LANES), pl.ds(pl.multiple_of((2 * h + p) * kd, kd), kd)]
            return lax.dot_general(keys_ref[2 * h + p], qs, nt, preferred_element_type=F32)

        s1 = scores(0)
        s2 = scores(1)
        a, r1 = _top_ranked(s1, PEER_TOPK)
        b, r2 = _top_ranked(s2, PEER_TOPK)
        b_all = jnp.concatenate(b, axis=0)
        cand = [a[i] + b_all for i in range(PEER_TOPK)]
        tau = _top_values(jnp.concatenate(cand, axis=0), PEER_TOPK)[PEER_TOPK - 1:PEER_TOPK, :]
        best = a[0] + b[0]
        zsum = jnp.zeros_like(best)
        c1 = jnp.zeros_like(r1)
        for i in range(PEER_TOPK):
            ok = cand[i] >= tau
            zsum = zsum + jnp.sum(jnp.where(ok, jnp.exp(cand[i] - best), 0.0), axis=0, keepdims=True)
            count = jnp.sum(jnp.where(ok, 1.0, 0.0), axis=0, keepdims=True)
            c1 = jnp.where(r1 == float(i), count, c1)
        c1_ref[h, :, pl.ds(tok, LANES)] = c1
        e1_ref[h, :, pl.ds(tok, LANES)] = jnp.exp(s1 - a[0]) / zsum
        r2_ref[h, :, pl.ds(tok, LANES)] = r2.astype(BF16)
        e2_ref[h, :, pl.ds(tok, LANES)] = jnp.exp(s2 - b[0]).astype(BF16)
        return carry

    lax.fori_loop(0, PEER_HEADS * nsub, body, 0)


def _score(h2, wq_b, keys_b):
    t, d = h2.shape
    tm = min(SCORE_TILE, t)
    big = lambda: pl.BlockSpec((PEER_HEADS, N_KEYS, tm), lambda j: (0, 0, j))
    big_shape = lambda dt: jax.ShapeDtypeStruct((PEER_HEADS, N_KEYS, t), dt)
    return pl.pallas_call(
        _score_kernel,
        grid=(t // tm,),
        in_specs=[pl.BlockSpec((tm, d), lambda j: (j, 0)),
                  pl.BlockSpec(wq_b.shape, lambda j: (0, 0)),
                  pl.BlockSpec(keys_b.shape, lambda j: (0, 0, 0))],
        out_specs=[big(), big(), big(), big()],
        out_shape=[big_shape(F32), big_shape(F32), big_shape(BF16), big_shape(BF16)],
        scratch_shapes=[pltpu.VMEM((tm, PEER_HEADS * PEER_KEY_DIM), BF16)],
        compiler_params=_params(("parallel",)),
        name="peer_score",
    )(h2, wq_b, keys_b)


def _peer_kernel(h_ref, u_ref, vt_ref, c1_ref, e1_ref, r2_ref, e2_ref, x1_ref, mod_ref, fnw_ref,
                 o_ref, acc_ref):
    eb = pl.program_id(1)
    nsub = PEER_EBLK // PEER_SUB
    keys_per_sub = PEER_SUB // N_KEYS
    nt = (((1,), (1,)), ((), ()))

    @pl.when(eb == 0)
    def _():
        acc_ref[...] = jnp.zeros_like(acc_ref)

    hb = h_ref[...]
    weighted = []
    for sub in range(nsub):
        pre = lax.dot_general(u_ref[sub * PEER_SUB:(sub + 1) * PEER_SUB, :], hb, nt,
                              preferred_element_type=F32)
        act = 0.5 * pre * (1.0 + lax.erf(pre * (2.0 ** -0.5)))
        gates = []
        for kk in range(keys_per_sub):
            i1 = (eb * nsub + sub) * keys_per_sub + kk
            g = jnp.zeros((N_KEYS, h_ref.shape[0]), BF16)
            for h in range(PEER_HEADS):
                c1 = c1_ref[h, pl.ds(i1, 1), :].astype(BF16)
                e1 = e1_ref[h, pl.ds(i1, 1), :].astype(BF16)
                g = g + jnp.where(r2_ref[h] < c1, e2_ref[h], jnp.zeros((), BF16)) * e1
            gates.append(g)
        weighted.append(act.astype(BF16) * jnp.concatenate(gates, axis=0))
    acc_ref[...] += jnp.dot(vt_ref[...], jnp.concatenate(weighted, axis=0), preferred_element_type=F32)

    @pl.when(eb == pl.num_programs(1) - 1)
    def _():
        x2 = x1_ref[...] + mod_ref[0, 5:6, :] * acc_ref[...].T
        o_ref[...] = x2 * lax.rsqrt(jnp.mean(x2 * x2, axis=-1, keepdims=True) + EPS) * fnw_ref[...]


def _peer(h2, u_b, vt_b, c1, e1, r2, e2, x1, mod, fn_w, seq):
    t, d = h2.shape
    tm = min(PEER_TILE, seq)
    n_exp = u_b.shape[0]
    big = lambda: pl.BlockSpec((PEER_HEADS, N_KEYS, tm), lambda j, e: (0, 0, j))
    return pl.pallas_call(
        _peer_kernel,
        grid=(t // tm, n_exp // PEER_EBLK),
        in_specs=[pl.BlockSpec((tm, d), lambda j, e: (j, 0)),
                  pl.BlockSpec((PEER_EBLK, d), lambda j, e: (e, 0)),
                  pl.BlockSpec((d, PEER_EBLK), lambda j, e: (0, e)),
                  big(), big(), big(), big(),
                  pl.BlockSpec((tm, d), lambda j, e: (j, 0)),
                  pl.BlockSpec((1, 6, d), lambda j, e: ((j * tm) // seq, 0, 0)),
                  pl.BlockSpec((1, d), lambda j, e: (0, 0))],
        out_specs=pl.BlockSpec((tm, d), lambda j, e: (j, 0)),
        out_shape=jax.ShapeDtypeStruct((t, d), F32),
        scratch_shapes=[pltpu.VMEM((d, tm), F32)],
        compiler_params=_params(("parallel", "arbitrary")),
        name="peer_dense",
    )(h2, u_b, vt_b, c1, e1, r2, e2, x1, mod, fn_w)


def _pad_lanes(v, width=LANES):
    flat = v.reshape(1, -1)
    return jnp.pad(flat, ((0, 0), (0, width - flat.shape[1])))


def _head_expand(col0):
    rows = lax.broadcasted_iota(jnp.int32, (LANES, D_SSM), 0)
    heads = lax.broadcasted_iota(jnp.int32, (LANES, D_SSM), 1) // SSM_HEADDIM
    return (rows == heads + col0).astype(BF16)


def kernel(x, c, ctx, c_ctx, ada_w, ada_b, norm1_w, norm2_w, w_in, ssm_conv_w, ssm_conv_b, ssm_dt_bias,
           ssm_a_log, ssm_d, ssm_norm_w, cfm_conv_w, cfm_conv_b, cfm_ln_w, cfm_ln_b, w_out, peer_wq,
           peer_subkeys, peer_u, peer_v, final_norm_w):
    depth = ada_w.shape[0]
    assert depth == 1, "single-layer configuration"
    b, l, d = x.shape
    i = 0

    c_rows = jnp.concatenate([c, c_ctx[None], jnp.zeros((2 * SUBLANES - b - 1, d), F32)], axis=0)
    mod_all = _ada(c_rows, ada_w[i], ada_b[i])
    mod = mod_all[:b].reshape(b, 6, d)
    mod_ctx = mod_all[b:b + 1].reshape(1, 6, d)

    wi = w_in[i]
    o_dt = D_SSM + XBC_DIM
    w_r = jnp.concatenate([wi[:, :o_dt], wi[:, o_dt + 2 * SSM_HEADS:], wi[:, o_dt:o_dt + 2 * SSM_HEADS],
                           jnp.zeros((d, LANES - 2 * SSM_HEADS), F32)], axis=1).astype(BF16)
    z_l, xbc_l, u_l, dt_l = _inproj(x, mod, True, norm1_w[i], w_r)
    _, xbc_c, _, dt_c = _inproj(ctx, mod_ctx, False, norm1_w[i], w_r)

    conv_w = jnp.pad(ssm_conv_w[i], ((0, SUBLANES - SSM_CONV), (0, 0)))
    conv_b = ssm_conv_b[i].reshape(1, XBC_DIM)
    dt_bias = _pad_lanes(ssm_dt_bias[i])
    a_log = _pad_lanes(ssm_a_log[i])
    d_skip = jnp.repeat(ssm_d[i], SSM_HEADDIM).reshape(1, D_SSM)
    zero_state = jnp.zeros((b, SSM_STATE, D_SSM), F32)
    y_dirs = []
    for reverse in (False, True):
        expand = _head_expand(SSM_HEADS if reverse else 0)
        args = (conv_w, conv_b, dt_bias, a_log, d_skip, expand, reverse)
        _, h_ctx = _ssd(xbc_c, dt_c, zero_state, *args, add_skip=False)
        y_dir, _ = _ssd(xbc_l, dt_l, h_ctx, *args, add_skip=not reverse)
        y_dirs.append(y_dir)

    cfm = _cfm(u_l, jnp.pad(cfm_conv_w[i], ((0, 1), (0, 0))), cfm_conv_b[i].reshape(1, D_CONV),
               cfm_ln_w[i].reshape(1, D_CONV), cfm_ln_b[i].reshape(1, D_CONV))

    x1, h2 = _outproj(y_dirs[0], y_dirs[1], z_l, cfm, x, mod, ssm_norm_w[i].reshape(1, D_SSM),
                      w_out[i].astype(BF16), norm2_w[i].reshape(1, d))

    t = b * l
    h2 = h2.reshape(t, d)
    keys_b = peer_subkeys[i].reshape(PEER_HEADS * 2, N_KEYS, PEER_KEY_DIM // 2).astype(BF16)
    c1, e1, r2, e2 = _score(h2, peer_wq[i].astype(BF16), keys_b)
    out = _peer(h2, peer_u[i].astype(BF16), peer_v[i].T.astype(BF16), c1, e1, r2, e2,
                x1.reshape(t, d), mod, final_norm_w.reshape(1, d), l)
    return out.reshape(b, l, d)
```

```python
import functools

import jax
import jax.numpy as jnp
from jax import lax
from jax.experimental import pallas as pl
from jax.experimental.pallas import tpu as pltpu

F32 = jnp.float32
BF16 = jnp.bfloat16

D_MODEL = 1024
GRID_W = 64
D_SSM = 512
SSM_HEADDIM = 64
SSM_HEADS = 8
SSM_STATE = 128
SSM_GROUPS = 2
SSM_CONV = 5
SSD_CHUNK = 128
XBC_DIM = D_SSM + 2 * SSM_GROUPS * SSM_STATE
D_CONV = 512
CONV_WIDTH = 31
PEER_HEADS = 8
PEER_KEY_DIM = 256
N_KEYS = 128
PEER_TOPK = 16
EPS = 1e-6

LANES = 128
SUBLANES = 8
VMEM_LIMIT = 56 * 1024 * 1024

IN_TILE = 512
CFM_ROWS = 16
OUT_TILE = 512
SCORE_TILE = 512
PEER_TILE = 512
PEER_EBLK = 1024
PEER_SUB = 256


def _params(sem):
    return pltpu.CompilerParams(dimension_semantics=sem, vmem_limit_bytes=VMEM_LIMIT)


def _silu(v):
    return v * jax.nn.sigmoid(v)


def _split3(v):
    hi = v.astype(BF16)
    r1 = v - hi.astype(F32)
    mid = r1.astype(BF16)
    lo = (r1 - mid.astype(F32)).astype(BF16)
    return hi, mid, lo


def _dot01_right(v, m01):
    return sum(jnp.dot(p, m01, preferred_element_type=F32) for p in _split3(v))


def _dot01_left(m01, v):
    return sum(jnp.dot(m01, p, preferred_element_type=F32) for p in _split3(v))


def _ada_kernel(c_ref, w_ref, b_ref, o_ref):
    sc = _silu(c_ref[...])
    o_ref[...] = jnp.dot(sc.astype(BF16), w_ref[...].astype(BF16),
                         preferred_element_type=F32) + b_ref[...]


def _ada(c_rows, ada_w, ada_b):
    n = ada_w.shape[1]
    tn = 768
    return pl.pallas_call(
        _ada_kernel,
        grid=(n // tn,),
        in_specs=[pl.BlockSpec(c_rows.shape, lambda j: (0, 0)),
                  pl.BlockSpec((D_MODEL, tn), lambda j: (0, j)),
                  pl.BlockSpec((1, tn), lambda j: (0, j))],
        out_specs=pl.BlockSpec((c_rows.shape[0], tn), lambda j: (0, j)),
        out_shape=jax.ShapeDtypeStruct((c_rows.shape[0], n), F32),
        compiler_params=_params(("parallel",)),
        name="ada",
    )(c_rows, ada_w, ada_b.reshape(1, n))


def _inproj_kernel(x_ref, mod_ref, nw_ref, w_ref, z_ref, xbc_ref, u_ref, dt_ref):
    xv = x_ref[0]
    h = xv * lax.rsqrt(jnp.mean(xv * xv, axis=-1, keepdims=True) + EPS) * nw_ref[...]
    h = h * (1.0 + mod_ref[0, 1:2, :]) + mod_ref[0, 0:1, :]
    hb = h.astype(BF16)

    def proj(lo, hi):
        return jnp.dot(hb, w_ref[:, lo:hi], preferred_element_type=F32)

    z_ref[0] = proj(0, D_SSM)
    xbc_ref[0] = proj(D_SSM, D_SSM + XBC_DIM)
    o = D_SSM + XBC_DIM
    u_ref[0] = proj(o, o + D_CONV) * jax.nn.sigmoid(proj(o + D_CONV, o + 2 * D_CONV))
    dt_ref[0] = proj(o + 2 * D_CONV, o + 2 * D_CONV + LANES)


def _inproj(x, mod, per_batch_mod, norm_w, w_r):
    b, l, d = x.shape
    tm = min(IN_TILE, l)
    mod_idx = (lambda i, j: (i, 0, 0)) if per_batch_mod else (lambda i, j: (0, 0, 0))
    tok = lambda w: pl.BlockSpec((1, tm, w), lambda i, j: (i, j, 0))
    return pl.pallas_call(
        _inproj_kernel,
        grid=(b, l // tm),
        in_specs=[tok(d),
                  pl.BlockSpec((1, 6, d), mod_idx),
                  pl.BlockSpec((1, d), lambda i, j: (0, 0)),
                  pl.BlockSpec(w_r.shape, lambda i, j: (0, 0))],
        out_specs=[tok(D_SSM), tok(XBC_DIM), tok(D_CONV), tok(LANES)],
        out_shape=[jax.ShapeDtypeStruct((b, l, w), F32) for w in (D_SSM, XBC_DIM, D_CONV, LANES)],
        compiler_params=_params(("parallel", "parallel")),
        name="inproj",
    )(x, mod, norm_w.reshape(1, d), w_r)


def _ssd_kernel(cur_ref, prev_ref, next_ref, dt_ref, h0_ref, cw_ref, cb_ref, dtb_ref, alog_ref,
                dskip_ref, expand_ref, y_ref, hfin_ref, ext_ref, state_ref, *, reverse, nc, add_skip):
    q = SSD_CHUNK
    halo = SUBLANES
    pad = (SSM_CONV - 1) // 2
    gw = D_SSM // SSM_GROUPS
    hpg = SSM_HEADS // SSM_GROUPS
    col0 = SSM_HEADS if reverse else 0
    c = pl.program_id(1)
    cc = (nc - 1 - c) if reverse else c

    @pl.when(c == 0)
    def _():
        state_ref[...] = h0_ref[0]

    ext_ref[0:halo, :] = jnp.where(cc == 0, 0.0, prev_ref[0])
    ext_ref[halo:halo + q, :] = cur_ref[0]
    ext_ref[halo + q:2 * halo + q, :] = jnp.where(cc == nc - 1, 0.0, next_ref[0])
    conv = cb_ref[...] + cw_ref[0:1, :] * ext_ref[halo - pad:halo - pad + q, :]
    for k in range(1, SSM_CONV):
        conv = conv + cw_ref[k:k + 1, :] * ext_ref[halo - pad + k:halo - pad + k + q, :]
    act = _silu(conv)

    dtv = dt_ref[0] + dtb_ref[...]
    dt = jnp.maximum(dtv, 0.0) + jnp.log(1.0 + jnp.exp(-jnp.abs(dtv)))
    dta = dt * (-jnp.exp(alog_ref[...]))
    ri = lax.broadcasted_iota(jnp.int32, (q, q), 0)
    ci = lax.broadcasted_iota(jnp.int32, (q, q), 1)
    causal = (ci >= ri) if reverse else (ci <= ri)
    acum = _dot01_left(jnp.where(causal, 1.0, 0.0).astype(BF16), dta)
    acum_t = acum.T
    tot = acum[0:1, :] if reverse else acum[q - 1:q, :]
    stacked = jnp.concatenate([dt, jnp.exp(tot - acum), jnp.exp(acum),
                               jnp.broadcast_to(jnp.exp(tot), (SUBLANES, LANES))], axis=0)
    spread = _dot01_right(stacked, expand_ref[...])
    dt_e, dte_e, ea_e, etot_e = spread[0:q], spread[q:2 * q], spread[2 * q:3 * q], spread[3 * q:3 * q + 1]

    xs = act[:, :D_SSM]
    xdt = xs * dt_e
    xdt_b = xdt.astype(BF16)
    w_b = (xdt * dte_e).astype(BF16)
    for g in range(SSM_GROUPS):
        gs = slice(g * gw, (g + 1) * gw)
        b_g = act[:, D_SSM + g * SSM_STATE:D_SSM + (g + 1) * SSM_STATE]
        c_g = act[:, D_SSM + (SSM_GROUPS + g) * SSM_STATE:D_SSM + (SSM_GROUPS + g + 1) * SSM_STATE]
        c_b = c_g.astype(BF16)
        bt_b = b_g.T.astype(BF16)
        cb = jnp.dot(c_b, bt_b, preferred_element_type=F32)
        st = state_ref[:, gs]
        y_g = jnp.dot(c_b, st.astype(BF16), preferred_element_type=F32) * ea_e[:, gs]
        y_heads = []
        for hh in range(hpg):
            h = g * hpg + hh
            col = col0 + h
            diff = acum[:, col:col + 1] - acum_t[col:col + 1, :]
            decay = jnp.where(causal, jnp.exp(jnp.minimum(diff, 0.0)), 0.0)
            y_heads.append(jnp.dot((cb * decay).astype(BF16),
                                   xdt_b[:, h * SSM_HEADDIM:(h + 1) * SSM_HEADDIM],
                                   preferred_element_type=F32))
        y_g = y_g + jnp.concatenate(y_heads, axis=-1)
        if add_skip:
            y_g = y_g + dskip_ref[:, gs] * xs[:, gs]
        y_ref[0, :, gs] = y_g
        state_ref[:, gs] = st * etot_e[:, gs] + jnp.dot(bt_b, w_b[:, gs], preferred_element_type=F32)
    hfin_ref[0] = state_ref[...]


def _ssd(xbc, dtp, h0, conv_w, conv_b, dt_bias, a_log, d_skip, expand, reverse, add_skip):
    b, l, _ = xbc.shape
    q = SSD_CHUNK
    nc = l // q
    hb = q // SUBLANES
    pos = (lambda c: nc - 1 - c) if reverse else (lambda c: c)
    const = lambda shape: pl.BlockSpec(shape, lambda i, c: (0,) * len(shape))
    kern = functools.partial(_ssd_kernel, reverse=reverse, nc=nc, add_skip=add_skip)
    return pl.pallas_call(
        kern,
        grid=(b, nc),
        in_specs=[pl.BlockSpec((1, q, XBC_DIM), lambda i, c: (i, pos(c), 0)),
                  pl.BlockSpec((1, SUBLANES, XBC_DIM), lambda i, c: (i, jnp.maximum(pos(c) * hb - 1, 0), 0)),
                  pl.BlockSpec((1, SUBLANES, XBC_DIM),
                               lambda i, c: (i, jnp.minimum((pos(c) + 1) * hb, l // SUBLANES - 1), 0)),
                  pl.BlockSpec((1, q, LANES), lambda i, c: (i, pos(c), 0)),
                  pl.BlockSpec((1, SSM_STATE, D_SSM), lambda i, c: (i, 0, 0)),
                  const(conv_w.shape), const(conv_b.shape), const(dt_bias.shape), const(a_log.shape),
                  const(d_skip.shape), const(expand.shape)],
        out_specs=[pl.BlockSpec((1, q, D_SSM), lambda i, c: (i, pos(c), 0)),
                   pl.BlockSpec((1, SSM_STATE, D_SSM), lambda i, c: (i, 0, 0))],
        out_shape=[jax.ShapeDtypeStruct((b, l, D_SSM), F32),
                   jax.ShapeDtypeStruct((b, SSM_STATE, D_SSM), F32)],
        scratch_shapes=[pltpu.VMEM((q + 2 * SUBLANES, XBC_DIM), F32),
                        pltpu.VMEM((SSM_STATE, D_SSM), F32)],
        compiler_params=_params(("parallel", "arbitrary")),
        name="ssd_bwd" if reverse else "ssd_fwd",
    )(xbc, xbc, xbc, dtp, h0, conv_w, conv_b, dt_bias, a_log, d_skip, expand)


def _cfm_kernel(cur_ref, prev_ref, next_ref, w_ref, b_ref, lnw_ref, lnb_ref, o_ref, hpad_ref, vbuf_ref,
                *, nb):
    rows = CFM_ROWS
    band = rows * GRID_W
    half = D_CONV // 2
    pad = (CONV_WIDTH - 1) // 2
    lead = 2 * SUBLANES
    r = pl.program_id(1)

    zeros = jnp.zeros((rows, lead, half), F32)
    hpad_ref[:, 0:lead, :] = zeros
    hpad_ref[:, lead + GRID_W:2 * lead + GRID_W, :] = zeros
    hpad_ref[:, lead:lead + GRID_W, :] = cur_ref[0, :, 0:half].reshape(rows, GRID_W, half)
    vbuf_ref[0:band, :] = jnp.where(r == 0, 0.0, prev_ref[0])
    vbuf_ref[band:2 * band, :] = cur_ref[0, :, half:D_CONV]
    vbuf_ref[2 * band:3 * band, :] = jnp.where(r == nb - 1, 0.0, next_ref[0])

    def row_body(i, carry):
        acc_h = jnp.zeros((GRID_W, half), F32)
        acc_v = jnp.zeros((GRID_W, half), F32)
        for k in range(CONV_WIDTH):
            acc_h = acc_h + w_ref[k:k + 1, 0:half] * hpad_ref[i, lead - pad + k:lead - pad + k + GRID_W, :]
            start = pl.multiple_of(band + (i + k - pad) * GRID_W, GRID_W)
            acc_v = acc_v + w_ref[k:k + 1, half:D_CONV] * vbuf_ref[pl.ds(start, GRID_W), :]
        conv = jnp.concatenate([acc_h, acc_v], axis=-1) + b_ref[...]
        mu = jnp.mean(conv, axis=-1, keepdims=True)
        cen = conv - mu
        var = jnp.mean(cen * cen, axis=-1, keepdims=True)
        o_ref[0, pl.ds(pl.multiple_of(i * GRID_W, GRID_W), GRID_W), :] = _silu(
            cen * lax.rsqrt(var + EPS) * lnw_ref[...] + lnb_ref[...])
        return carry

    lax.fori_loop(0, rows, row_body, 0)


def _cfm(u, conv_w, conv_b, ln_w, ln_b):
    b, l, _ = u.shape
    band = CFM_ROWS * GRID_W
    nb = l // band
    half = D_CONV // 2
    const = lambda shape: pl.BlockSpec(shape, lambda i, r: (0,) * len(shape))
    return pl.pallas_call(
        functools.partial(_cfm_kernel, nb=nb),
        grid=(b, nb),
        in_specs=[pl.BlockSpec((1, band, D_CONV), lambda i, r: (i, r, 0)),
                  pl.BlockSpec((1, band, half), lambda i, r: (i, jnp.maximum(r - 1, 0), 1)),
                  pl.BlockSpec((1, band, half), lambda i, r: (i, jnp.minimum(r + 1, nb - 1), 1)),
                  const(conv_w.shape), const(conv_b.shape), const(ln_w.shape), const(ln_b.shape)],
        out_specs=pl.BlockSpec((1, band, D_CONV), lambda i, r: (i, r, 0)),
        out_shape=jax.ShapeDtypeStruct((b, l, D_CONV), F32),
        scratch_shapes=[pltpu.VMEM((CFM_ROWS, GRID_W + 4 * SUBLANES, half), F32),
                        pltpu.VMEM((3 * band, half), F32)],
        compiler_params=_params(("parallel", "parallel")),
        name="cfm",
    )(u, u, u, conv_w, conv_b, ln_w, ln_b)


def _outproj_kernel(yf_ref, yb_ref, z_ref, cfm_ref, x_ref, mod_ref, gnw_ref, w_ref, n2w_ref,
                    x1_ref, h2_ref):
    gw = D_SSM // SSM_GROUPS
    g = (yf_ref[0] + yb_ref[0]) * _silu(z_ref[0])
    parts = []
    for k in range(SSM_GROUPS):
        gk = g[:, k * gw:(k + 1) * gw]
        parts.append(gk * lax.rsqrt(jnp.mean(gk * gk, axis=-1, keepdims=True) + EPS))
    ssm = jnp.concatenate(parts, axis=-1) * gnw_ref[...]
    mixed = (jnp.dot(ssm.astype(BF16), w_ref[0:D_SSM, :], preferred_element_type=F32)
             + jnp.dot(cfm_ref[0].astype(BF16), w_ref[D_SSM:, :], preferred_element_type=F32))
    x1 = x_ref[0] + mod_ref[0, 2:3, :] * mixed
    x1_ref[0] = x1
    h = x1 * lax.rsqrt(jnp.mean(x1 * x1, axis=-1, keepdims=True) + EPS) * n2w_ref[...]
    h2_ref[0] = (h * (1.0 + mod_ref[0, 4:5, :]) + mod_ref[0, 3:4, :]).astype(BF16)


def _outproj(yf, yb, z, cfm, x, mod, gn_w, w_out_b, n2_w):
    b, l, d = x.shape
    tm = min(OUT_TILE, l)
    tok = lambda w: pl.BlockSpec((1, tm, w), lambda i, j: (i, j, 0))
    const = lambda shape: pl.BlockSpec(shape, lambda i, j: (0,) * len(shape))
    return pl.pallas_call(
        _outproj_kernel,
        grid=(b, l // tm),
        in_specs=[tok(D_SSM), tok(D_SSM), tok(D_SSM), tok(D_CONV), tok(d),
                  pl.BlockSpec((1, 6, d), lambda i, j: (i, 0, 0)),
                  const(gn_w.shape), const(w_out_b.shape), const(n2_w.shape)],
        out_specs=[tok(d), tok(d)],
        out_shape=[jax.ShapeDtypeStruct((b, l, d), F32), jax.ShapeDtypeStruct((b, l, d), BF16)],
        compiler_params=_params(("parallel", "parallel")),
        name="outproj",
    )(yf, yb, z, cfm, x, mod, gn_w, w_out_b, n2_w)


def _oddeven_merge(lo, hi, r):
    step = r * 2
    if step < hi - lo:
        yield from _oddeven_merge(lo, hi, step)
        yield from _oddeven_merge(lo + r, hi, step)
        yield from [(i, i + r) for i in range(lo + r, hi - r, step)]
    else:
        yield (lo, lo + r)


def _oddeven_sort(lo, hi):
    if hi > lo:
        mid = lo + (hi - lo) // 2
        yield from _oddeven_sort(lo, mid)
        yield from _oddeven_sort(mid + 1, hi)
        yield from _oddeven_merge(lo, hi, 1)


SORT_NET = tuple(_oddeven_sort(0, PEER_TOPK - 1))
PACKED = jnp.uint32


def _cmp_exchange(v, i, j):
    v[i], v[j] = jnp.maximum(v[i], v[j]), jnp.minimum(v[i], v[j])


def _bitonic_sort(v):
    d = len(v) // 2
    while d:
        for i in range(len(v)):
            if not i & d:
                _cmp_exchange(v, i, i + d)
        d //= 2


def _merge_top(v, w):
    n = len(v)
    return [jnp.maximum(v[i], w[n - 1 - i]) for i in range(n)]


def _sublane_rolls():
    shift = SUBLANES // 2
    while shift:
        yield shift
        shift //= 2


def _top_sorted(v):
    for shift in _sublane_rolls():
        v = _merge_top(v, [pltpu.roll(x, shift, 0) for x in v])
        _bitonic_sort(v)
    return v


def _double_bf16(x):
    bits = pltpu.bitcast(x.astype(BF16).astype(F32), jnp.uint32)
    return bits | (bits >> 16)


def _score_kernel(h_ref, wq_ref, keys_ref, neg_ref, c1_ref, e1_ref, r2_ref, e2_ref, q_ref):
    tm = h_ref.shape[0]
    nsub = tm // LANES
    kd = PEER_KEY_DIM // 2
    k = PEER_TOPK
    ntile = N_KEYS // SUBLANES
    q_ref[...] = jnp.dot(h_ref[...], wq_ref[...], preferred_element_type=F32).astype(BF16)

    def body(it, carry):
        h = it // nsub
        tok = pl.multiple_of((it % nsub) * LANES, LANES)
        nt = (((1,), (1,)), ((), ()))

        def score_tiles(p):
            qs = q_ref[pl.ds(tok, LANES), pl.ds(pl.multiple_of((2 * h + p) * kd, kd), kd)]
            s = lax.dot_general(keys_ref[2 * h + p], qs, nt, preferred_element_type=F32)
            return [s[i * SUBLANES:(i + 1) * SUBLANES, :] for i in range(ntile)]

        def top(tiles):
            v = list(tiles)
            for i, j in SORT_NET:
                _cmp_exchange(v, i, j)
            return _top_sorted(v)

        s1 = score_tiles(0)
        s2 = score_tiles(1)
        a = top(s1)
        b = top(s2)

        sub = lax.broadcasted_iota(jnp.int32, (SUBLANES, LANES), 0)
        pick = lambda lo, x, y: jnp.where(sub < lo, x, y)
        base = pick(4, pick(2, pick(1, a[0], a[1]), pick(3, a[2], a[3])), pick(5, b[0], pick(6, b[1], b[2])))
        cand = []
        for m in range(k):
            other = pick(4, b[m], a[4 + m]) if 4 + m < k else b[m]
            cand.append((other + base) + neg_ref[m])
        lists = list(cand)
        rolls = list(_sublane_rolls())
        for shift in rolls[:-1]:
            lists = _merge_top(lists, [pltpu.roll(x, shift, 0) for x in lists])
            _bitonic_sort(lists)
        lists = _merge_top(lists, [pltpu.roll(x, rolls[-1], 0) for x in lists])
        tau = functools.reduce(jnp.minimum, lists)

        best = a[0] + b[0]
        zsum = sum(jnp.where(c >= tau, jnp.exp(c - best), 0.0) for c in cand)
        for shift in rolls:
            zsum = zsum + pltpu.roll(zsum, shift, 0)
        scale = 0.5 / zsum
        top_extra = sum(jnp.where(a[0] + b[j] >= tau, 1.0, 0.0) for j in range(k // 2, k))

        c1_t, e1_t, r2_t, e2_t = [], [], [], []
        for t1, t2 in zip(s1, s2):
            c1 = jnp.zeros_like(t1)
            for j in range(k // 2):
                c1 = jnp.where(t1 + b[j] >= tau, float(j + 1), c1)
            c1_t.append(_double_bf16(jnp.where(t1 == a[0], c1 + top_extra, c1)))
            e1_t.append(_double_bf16(jnp.exp(t1 - a[0]) * scale))
            r2 = jnp.zeros_like(t2)
            for j in range(k):
                r2 = jnp.where(t2 < b[j], float(j + 1), r2)
            r2_t.append(r2)
            e2_t.append(jnp.exp(t2 - b[0]))
        c1_ref[h, :, pl.ds(tok, LANES)] = jnp.concatenate(c1_t, axis=0)
        e1_ref[h, :, pl.ds(tok, LANES)] = jnp.concatenate(e1_t, axis=0)
        r2_ref[h, :, pl.ds(tok, LANES)] = jnp.concatenate(r2_t, axis=0).astype(BF16)
        e2_ref[h, :, pl.ds(tok, LANES)] = jnp.concatenate(e2_t, axis=0).astype(BF16)
        return carry

    lax.fori_loop(0, PEER_HEADS * nsub, body, 0)


def _candidate_mask():
    k = PEER_TOPK
    m = lax.broadcasted_iota(jnp.int32, (k, SUBLANES, LANES), 0)
    s = lax.broadcasted_iota(jnp.int32, (k, SUBLANES, LANES), 1)
    row_ok = (s < 4) & ((s + 1) * (m + 1) <= k)
    col_ok = (s >= 4) & (s < 7) & (4 + m < k) & ((5 + m) * (s - 3) <= k)
    return jnp.where(row_ok | col_ok, 0.0, -jnp.inf).astype(F32)


def _score(h2, wq_b, keys_b):
    t, d = h2.shape
    tm = min(SCORE_TILE, t)
    neg = _candidate_mask()
    big = lambda: pl.BlockSpec((PEER_HEADS, N_KEYS, tm), lambda j: (0, 0, j))
    big_shape = lambda dt: jax.ShapeDtypeStruct((PEER_HEADS, N_KEYS, t), dt)
    return pl.pallas_call(
        _score_kernel,
        grid=(t // tm,),
        in_specs=[pl.BlockSpec((tm, d), lambda j: (j, 0)),
                  pl.BlockSpec(wq_b.shape, lambda j: (0, 0)),
                  pl.BlockSpec(keys_b.shape, lambda j: (0, 0, 0)),
                  pl.BlockSpec(neg.shape, lambda j: (0, 0, 0))],
        out_specs=[big(), big(), big(), big()],
        out_shape=[big_shape(PACKED), big_shape(PACKED), big_shape(BF16), big_shape(BF16)],
        scratch_shapes=[pltpu.VMEM((tm, PEER_HEADS * PEER_KEY_DIM), BF16)],
        compiler_params=_params(("parallel",)),
        name="peer_score",
    )(h2, wq_b, keys_b, neg)


def _key_row_bf16(ref, h, i1):
    word_rows = jnp.broadcast_to(ref[h, pl.ds(i1, 1), :], (SUBLANES, ref.shape[-1]))
    rows = pltpu.bitcast(word_rows, BF16)
    return pltpu.repeat(rows, N_KEYS // rows.shape[0], axis=0)


def _peer_kernel(h_ref, u_ref, vt_ref, c1_ref, e1_ref, r2_ref, e2_ref, x1_ref, mod_ref, fnw_ref,
                 o_ref, acc_ref):
    eb = pl.program_id(1)
    nsub = PEER_EBLK // PEER_SUB
    keys_per_sub = PEER_SUB // N_KEYS
    nt = (((1,), (1,)), ((), ()))

    @pl.when(eb == 0)
    def _():
        acc_ref[...] = jnp.zeros_like(acc_ref)

    hb = h_ref[...]

    def first_matmul(sub):
        return lax.dot_general(u_ref[sub * PEER_SUB:(sub + 1) * PEER_SUB, :], hb, nt,
                               preferred_element_type=F32)

    def gate_weighted(sub, pre):
        act = pre * (1.0 + lax.erf(pre * (2.0 ** -0.5)))
        gates = []
        for kk in range(keys_per_sub):
            i1 = (eb * nsub + sub) * keys_per_sub + kk
            g = jnp.zeros((N_KEYS, h_ref.shape[0]), BF16)
            for h in range(PEER_HEADS):
                c1 = _key_row_bf16(c1_ref, h, i1)
                e1 = _key_row_bf16(e1_ref, h, i1)
                g = g + jnp.where(r2_ref[h] < c1, e2_ref[h], jnp.zeros((), BF16)) * e1
            gates.append(g)
        return act.astype(BF16) * jnp.concatenate(gates, axis=0)

    acc = acc_ref[...]
    pre = first_matmul(0)
    for sub in range(nsub):
        nxt = first_matmul(sub + 1) if sub + 1 < nsub else None
        weighted = gate_weighted(sub, pre)
        acc = jnp.dot(vt_ref[:, sub * PEER_SUB:(sub + 1) * PEER_SUB], weighted,
                      preferred_element_type=F32) + acc
        pre = nxt
    acc_ref[...] = acc

    @pl.when(eb == pl.num_programs(1) - 1)
    def _():
        x2 = x1_ref[...] + mod_ref[0, 5:6, :] * acc_ref[...].T
        o_ref[...] = x2 * lax.rsqrt(jnp.mean(x2 * x2, axis=-1, keepdims=True) + EPS) * fnw_ref[...]


def _peer(h2, u_b, vt_b, c1, e1, r2, e2, x1, mod, fn_w, seq):
    t, d = h2.shape
    tm = min(PEER_TILE, seq)
    n_exp = u_b.shape[0]
    big = lambda: pl.BlockSpec((PEER_HEADS, N_KEYS, tm), lambda j, e: (0, 0, j))
    return pl.pallas_call(
        _peer_kernel,
        grid=(t // tm, n_exp // PEER_EBLK),
        in_specs=[pl.BlockSpec((tm, d), lambda j, e: (j, 0)),
                  pl.BlockSpec((PEER_EBLK, d), lambda j, e: (e, 0)),
                  pl.BlockSpec((d, PEER_EBLK), lambda j, e: (0, e)),
                  big(), big(), big(), big(),
                  pl.BlockSpec((tm, d), lambda j, e: (j, 0)),
                  pl.BlockSpec((1, 6, d), lambda j, e: ((j * tm) // seq, 0, 0)),
                  pl.BlockSpec((1, d), lambda j, e: (0, 0))],
        out_specs=pl.BlockSpec((tm, d), lambda j, e: (j, 0)),
        out_shape=jax.ShapeDtypeStruct((t, d), F32),
        scratch_shapes=[pltpu.VMEM((d, tm), F32)],
        compiler_params=_params(("parallel", "arbitrary")),
        name="peer_dense",
    )(h2, u_b, vt_b, c1, e1, r2, e2, x1, mod, fn_w)


def _pad_lanes(v, width=LANES):
    flat = v.reshape(1, -1)
    return jnp.pad(flat, ((0, 0), (0, width - flat.shape[1])))


def _head_expand(col0):
    rows = lax.broadcasted_iota(jnp.int32, (LANES, D_SSM), 0)
    heads = lax.broadcasted_iota(jnp.int32, (LANES, D_SSM), 1) // SSM_HEADDIM
    return (rows == heads + col0).astype(BF16)


def kernel(x, c, ctx, c_ctx, ada_w, ada_b, norm1_w, norm2_w, w_in, ssm_conv_w, ssm_conv_b, ssm_dt_bias,
           ssm_a_log, ssm_d, ssm_norm_w, cfm_conv_w, cfm_conv_b, cfm_ln_w, cfm_ln_b, w_out, peer_wq,
           peer_subkeys, peer_u, peer_v, final_norm_w):
    depth = ada_w.shape[0]
    assert depth == 1, "single-layer configuration"
    b, l, d = x.shape
    i = 0

    c_rows = jnp.concatenate([c, c_ctx[None], jnp.zeros((2 * SUBLANES - b - 1, d), F32)], axis=0)
    mod_all = _ada(c_rows, ada_w[i], ada_b[i])
    mod = mod_all[:b].reshape(b, 6, d)
    mod_ctx = mod_all[b:b + 1].reshape(1, 6, d)

    wi = w_in[i]
    o_dt = D_SSM + XBC_DIM
    w_r = jnp.concatenate([wi[:, :o_dt], wi[:, o_dt + 2 * SSM_HEADS:], wi[:, o_dt:o_dt + 2 * SSM_HEADS],
                           jnp.zeros((d, LANES - 2 * SSM_HEADS), F32)], axis=1).astype(BF16)
    z_l, xbc_l, u_l, dt_l = _inproj(x, mod, True, norm1_w[i], w_r)
    _, xbc_c, _, dt_c = _inproj(ctx, mod_ctx, False, norm1_w[i], w_r)

    conv_w = jnp.pad(ssm_conv_w[i], ((0, SUBLANES - SSM_CONV), (0, 0)))
    conv_b = ssm_conv_b[i].reshape(1, XBC_DIM)
    dt_bias = _pad_lanes(ssm_dt_bias[i])
    a_log = _pad_lanes(ssm_a_log[i])
    d_skip = jnp.repeat(ssm_d[i], SSM_HEADDIM).reshape(1, D_SSM)
    zero_state = jnp.zeros((b, SSM_STATE, D_SSM), F32)
    y_dirs = []
    for reverse in (False, True):
        expand = _head_expand(SSM_HEADS if reverse else 0)
        args = (conv_w, conv_b, dt_bias, a_log, d_skip, expand, reverse)
        _, h_ctx = _ssd(xbc_c, dt_c, zero_state, *args, add_skip=False)
        y_dir, _ = _ssd(xbc_l, dt_l, h_ctx, *args, add_skip=not reverse)
        y_dirs.append(y_dir)

    cfm = _cfm(u_l, jnp.pad(cfm_conv_w[i], ((0, 1), (0, 0))), cfm_conv_b[i].reshape(1, D_CONV),
               cfm_ln_w[i].reshape(1, D_CONV), cfm_ln_b[i].reshape(1, D_CONV))

    x1, h2 = _outproj(y_dirs[0], y_dirs[1], z_l, cfm, x, mod, ssm_norm_w[i].reshape(1, D_SSM),
                      w_out[i].astype(BF16), norm2_w[i].reshape(1, d))

    t = b * l
    h2 = h2.reshape(t, d)
    keys_b = peer_subkeys[i].reshape(PEER_HEADS * 2, N_KEYS, PEER_KEY_DIM // 2).astype(BF16)
    c1, e1, r2, e2 = _score(h2, peer_wq[i].astype(BF16), keys_b)
    out = _peer(h2, peer_u[i].astype(BF16), peer_v[i].T.astype(BF16), c1, e1, r2, e2,
                x1.reshape(t, d), mod, final_norm_w.reshape(1, d), l)
    return out.reshape(b, l, d)
```

```python
import functools

import jax
import jax.numpy as jnp
from jax import lax
from jax.experimental import pallas as pl
from jax.experimental.pallas import tpu as pltpu

F32 = jnp.float32
BF16 = jnp.bfloat16

D_MODEL = 1024
GRID_W = 64
D_SSM = 512
SSM_HEADDIM = 64
SSM_HEADS = 8
SSM_STATE = 128
SSM_GROUPS = 2
SSM_CONV = 5
SSD_CHUNK = 128
XBC_DIM = D_SSM + 2 * SSM_GROUPS * SSM_STATE
D_CONV = 512
CONV_WIDTH = 31
PEER_HEADS = 8
PEER_KEY_DIM = 256
N_KEYS = 128
PEER_TOPK = 16
EPS = 1e-6

LANES = 128
SUBLANES = 8
VMEM_LIMIT = 56 * 1024 * 1024

IN_TILE = 512
SSD_ROWS = 2
CFM_ROWS = 16
OUT_TILE = 512
SCORE_TILE = 512
PEER_TILE = 512
PEER_EBLK = 2048
PEER_SUB = 256


def _params(sem):
    return pltpu.CompilerParams(dimension_semantics=sem, vmem_limit_bytes=VMEM_LIMIT)


def _silu(v):
    return v * jax.nn.sigmoid(v)


def _split3(v):
    hi = v.astype(BF16)
    r1 = v - hi.astype(F32)
    mid = r1.astype(BF16)
    lo = (r1 - mid.astype(F32)).astype(BF16)
    return hi, mid, lo


def _dot01_right(v, m01):
    return sum(jnp.dot(p, m01, preferred_element_type=F32) for p in _split3(v))


def _dot01_left(m01, v):
    return sum(jnp.dot(m01, p, preferred_element_type=F32) for p in _split3(v))


def _ada_kernel(c_ref, w_ref, b_ref, o_ref):
    sc = _silu(c_ref[...])
    o_ref[...] = jnp.dot(sc.astype(BF16), w_ref[...].astype(BF16),
                         preferred_element_type=F32) + b_ref[...]


def _ada(c_rows, ada_w, ada_b):
    n = ada_w.shape[1]
    tn = 768
    return pl.pallas_call(
        _ada_kernel,
        grid=(n // tn,),
        in_specs=[pl.BlockSpec(c_rows.shape, lambda j: (0, 0)),
                  pl.BlockSpec((D_MODEL, tn), lambda j: (0, j)),
                  pl.BlockSpec((1, tn), lambda j: (0, j))],
        out_specs=pl.BlockSpec((c_rows.shape[0], tn), lambda j: (0, j)),
        out_shape=jax.ShapeDtypeStruct((c_rows.shape[0], n), F32),
        compiler_params=_params(("parallel",)),
        name="ada",
    )(c_rows, ada_w, ada_b.reshape(1, n))


def _inproj_kernel(x_ref, mod_ref, nw_ref, w_ref, z_ref, xbc_ref, u_ref, dt_ref):
    xv = x_ref[0]
    h = xv * lax.rsqrt(jnp.mean(xv * xv, axis=-1, keepdims=True) + EPS) * nw_ref[...]
    h = h * (1.0 + mod_ref[0, 1:2, :]) + mod_ref[0, 0:1, :]
    hb = h.astype(BF16)

    def proj(lo, hi):
        return jnp.dot(hb, w_ref[:, lo:hi], preferred_element_type=F32)

    z_ref[0] = proj(0, D_SSM)
    xbc_ref[0] = proj(D_SSM, D_SSM + XBC_DIM)
    o = D_SSM + XBC_DIM
    u_ref[0] = proj(o, o + D_CONV) * jax.nn.sigmoid(proj(o + D_CONV, o + 2 * D_CONV))
    dt_ref[0] = proj(o + 2 * D_CONV, o + 2 * D_CONV + LANES)


def _inproj(x, mod, per_batch_mod, norm_w, w_r):
    b, l, d = x.shape
    tm = min(IN_TILE, l)
    mod_idx = (lambda i, j: (i, 0, 0)) if per_batch_mod else (lambda i, j: (0, 0, 0))
    tok = lambda w: pl.BlockSpec((1, tm, w), lambda i, j: (i, j, 0))
    return pl.pallas_call(
        _inproj_kernel,
        grid=(b, l // tm),
        in_specs=[tok(d),
                  pl.BlockSpec((1, 6, d), mod_idx),
                  pl.BlockSpec((1, d), lambda i, j: (0, 0)),
                  pl.BlockSpec(w_r.shape, lambda i, j: (0, 0))],
        out_specs=[tok(D_SSM), tok(XBC_DIM), tok(D_CONV), tok(LANES)],
        out_shape=[jax.ShapeDtypeStruct((b, l, w), F32) for w in (D_SSM, XBC_DIM, D_CONV, LANES)],
        compiler_params=_params(("parallel", "parallel")),
        name="inproj",
    )(x, mod, norm_w.reshape(1, d), w_r)


def _ssd_kernel(cur_ref, prev_ref, next_ref, dt_ref, h0_ref, cw_ref, cb_ref, dtb_ref, alog_ref,
                dskip_ref, expand_ref, y_ref, hfin_ref, *rest, reverse, nc, add_skip, activated):
    act_ref, ext_ref, state_ref = (None, *rest) if activated else rest
    q = SSD_CHUNK
    halo = SUBLANES
    pad = (SSM_CONV - 1) // 2
    gw = D_SSM // SSM_GROUPS
    hpg = SSM_HEADS // SSM_GROUPS
    col0 = SSM_HEADS if reverse else 0
    c = pl.program_id(1)
    cc = (nc - 1 - c) if reverse else c

    @pl.when(c == 0)
    def _():
        state_ref[...] = h0_ref[...]

    ri = lax.broadcasted_iota(jnp.int32, (q, q), 0)
    ci = lax.broadcasted_iota(jnp.int32, (q, q), 1)
    causal = (ci >= ri) if reverse else (ci <= ri)
    causal01 = jnp.where(causal, 1.0, 0.0).astype(BF16)
    a_row = -jnp.exp(alog_ref[...])

    for r in range(cur_ref.shape[0]):
        if activated:
            act = cur_ref[r]
        else:
            ext_ref[r, 0:halo, :] = jnp.where(cc == 0, 0.0, prev_ref[r])
            ext_ref[r, halo:halo + q, :] = cur_ref[r]
            ext_ref[r, halo + q:2 * halo + q, :] = jnp.where(cc == nc - 1, 0.0, next_ref[r])
            conv = cb_ref[...] + cw_ref[0:1, :] * ext_ref[r, halo - pad:halo - pad + q, :]
            for k in range(1, SSM_CONV):
                conv = conv + cw_ref[k:k + 1, :] * ext_ref[r, halo - pad + k:halo - pad + k + q, :]
            act = _silu(conv)
            act_ref[r] = act

        dtv = dt_ref[r] + dtb_ref[...]
        dt = jnp.maximum(dtv, 0.0) + jnp.log(1.0 + jnp.exp(-jnp.abs(dtv)))
        acum = _dot01_left(causal01, dt * a_row)
        acum_t = acum.T
        tot = acum[0:1, :] if reverse else acum[q - 1:q, :]
        stacked = jnp.concatenate([dt, jnp.exp(tot - acum), jnp.exp(acum),
                                   jnp.broadcast_to(jnp.exp(tot), (SUBLANES, LANES))], axis=0)
        spread = _dot01_right(stacked, expand_ref[...])
        dt_e, dte_e, ea_e = spread[0:q], spread[q:2 * q], spread[2 * q:3 * q]
        etot_e = spread[3 * q:3 * q + 1]

        xs = act[:, :D_SSM]
        xdt = xs * dt_e
        xdt_b = xdt.astype(BF16)
        w_b = (xdt * dte_e).astype(BF16)
        for g in range(SSM_GROUPS):
            gs = slice(g * gw, (g + 1) * gw)
            b_g = act[:, D_SSM + g * SSM_STATE:D_SSM + (g + 1) * SSM_STATE]
            c_g = act[:, D_SSM + (SSM_GROUPS + g) * SSM_STATE:D_SSM + (SSM_GROUPS + g + 1) * SSM_STATE]
            c_b = c_g.astype(BF16)
            bt_b = b_g.T.astype(BF16)
            cb = jnp.dot(c_b, bt_b, preferred_element_type=F32)
            st = state_ref[r, :, gs]
            y_g = jnp.dot(c_b, st.astype(BF16), preferred_element_type=F32) * ea_e[:, gs]
            y_heads = []
            for hh in range(hpg):
                h = g * hpg + hh
                col = col0 + h
                diff = acum[:, col:col + 1] - acum_t[col:col + 1, :]
                decay = jnp.where(causal, jnp.exp(jnp.minimum(diff, 0.0)), 0.0)
                y_heads.append(jnp.dot((cb * decay).astype(BF16),
                                       xdt_b[:, h * SSM_HEADDIM:(h + 1) * SSM_HEADDIM],
                                       preferred_element_type=F32))
            y_g = y_g + jnp.concatenate(y_heads, axis=-1)
            if add_skip:
                y_g = y_g + dskip_ref[:, gs] * xs[:, gs]
            y_ref[r, :, gs] = y_g
            state_ref[r, :, gs] = st * etot_e[:, gs] + jnp.dot(bt_b, w_b[:, gs], preferred_element_type=F32)
    hfin_ref[...] = state_ref[...]


def _ssd(xbc, dtp, h0, conv_w, conv_b, dt_bias, a_log, d_skip, expand, reverse, add_skip, activated):
    b, l, _ = xbc.shape
    q = SSD_CHUNK
    nc = l // q
    nr = SSD_ROWS if b % SSD_ROWS == 0 else 1
    hb = q // SUBLANES
    pos = (lambda c: nc - 1 - c) if reverse else (lambda c: c)
    const = lambda shape: pl.BlockSpec(shape, lambda i, c: (0,) * len(shape))
    kern = functools.partial(_ssd_kernel, reverse=reverse, nc=nc, add_skip=add_skip, activated=activated)
    chunk = lambda w: pl.BlockSpec((nr, q, w), lambda i, c: (i, pos(c), 0))
    act_spec = [] if activated else [chunk(XBC_DIM)]
    act_shape = [] if activated else [jax.ShapeDtypeStruct((b, l, XBC_DIM), F32)]
    return pl.pallas_call(
        kern,
        grid=(b // nr, nc),
        in_specs=[pl.BlockSpec((nr, q, XBC_DIM), lambda i, c: (i, pos(c), 0)),
                  pl.BlockSpec((nr, SUBLANES, XBC_DIM), lambda i, c: (i, jnp.maximum(pos(c) * hb - 1, 0), 0)),
                  pl.BlockSpec((nr, SUBLANES, XBC_DIM),
                               lambda i, c: (i, jnp.minimum((pos(c) + 1) * hb, l // SUBLANES - 1), 0)),
                  pl.BlockSpec((nr, q, LANES), lambda i, c: (i, pos(c), 0)),
                  pl.BlockSpec((nr, SSM_STATE, D_SSM), lambda i, c: (i, 0, 0)),
                  const(conv_w.shape), const(conv_b.shape), const(dt_bias.shape), const(a_log.shape),
                  const(d_skip.shape), const(expand.shape)],
        out_specs=[chunk(D_SSM), pl.BlockSpec((nr, SSM_STATE, D_SSM), lambda i, c: (i, 0, 0))] + act_spec,
        out_shape=[jax.ShapeDtypeStruct((b, l, D_SSM), F32),
                   jax.ShapeDtypeStruct((b, SSM_STATE, D_SSM), F32)] + act_shape,
        scratch_shapes=[pltpu.VMEM((nr, q + 2 * SUBLANES, XBC_DIM), F32),
                        pltpu.VMEM((nr, SSM_STATE, D_SSM), F32)],
        compiler_params=_params(("parallel", "arbitrary")),
        name="ssd_bwd" if reverse else "ssd_fwd",
    )(xbc, xbc, xbc, dtp, h0, conv_w, conv_b, dt_bias, a_log, d_skip, expand)


def _cfm_kernel(cur_ref, prev_ref, next_ref, w_ref, b_ref, lnw_ref, lnb_ref, o_ref, hpad_ref, vbuf_ref,
                shift_ref, *, nb):
    rows = CFM_ROWS
    band = rows * GRID_W
    half = D_CONV // 2
    pad = (CONV_WIDTH - 1) // 2
    lead = 2 * SUBLANES
    r = pl.program_id(1)

    zeros = jnp.zeros((rows, lead, half), F32)
    hpad_ref[:, 0:lead, :] = zeros
    hpad_ref[:, lead + GRID_W:2 * lead + GRID_W, :] = zeros
    hpad_ref[:, lead:lead + GRID_W, :] = cur_ref[0, :, 0:half].reshape(rows, GRID_W, half)
    vbuf_ref[0:band, :] = jnp.where(r == 0, 0.0, prev_ref[0])
    vbuf_ref[band:2 * band, :] = cur_ref[0, :, half:D_CONV]
    vbuf_ref[2 * band:3 * band, :] = jnp.where(r == nb - 1, 0.0, next_ref[0])

    span = shift_ref.shape[1]

    def row_body(i, carry):
        for s in range(SUBLANES):
            shift_ref[s] = hpad_ref[i, s:s + span, :]
        acc_h = jnp.zeros((GRID_W, half), F32)
        acc_v = jnp.zeros((GRID_W, half), F32)
        for k in range(CONV_WIDTH):
            first = lead - pad + k
            tile0 = first // SUBLANES * SUBLANES
            acc_h = acc_h + w_ref[k:k + 1, 0:half] * shift_ref[first % SUBLANES, tile0:tile0 + GRID_W, :]
            start = pl.multiple_of(band + (i + k - pad) * GRID_W, GRID_W)
            acc_v = acc_v + w_ref[k:k + 1, half:D_CONV] * vbuf_ref[pl.ds(start, GRID_W), :]
        conv = jnp.concatenate([acc_h, acc_v], axis=-1) + b_ref[...]
        mu = jnp.mean(conv, axis=-1, keepdims=True)
        cen = conv - mu
        var = jnp.mean(cen * cen, axis=-1, keepdims=True)
        o_ref[0, pl.ds(pl.multiple_of(i * GRID_W, GRID_W), GRID_W), :] = _silu(
            cen * lax.rsqrt(var + EPS) * lnw_ref[...] + lnb_ref[...])
        return carry

    lax.fori_loop(0, rows, row_body, 0)


def _cfm(u, conv_w, conv_b, ln_w, ln_b):
    b, l, _ = u.shape
    band = CFM_ROWS * GRID_W
    nb = l // band
    half = D_CONV // 2
    const = lambda shape: pl.BlockSpec(shape, lambda i, r: (0,) * len(shape))
    return pl.pallas_call(
        functools.partial(_cfm_kernel, nb=nb),
        grid=(b, nb),
        in_specs=[pl.BlockSpec((1, band, D_CONV), lambda i, r: (i, r, 0)),
                  pl.BlockSpec((1, band, half), lambda i, r: (i, jnp.maximum(r - 1, 0), 1)),
                  pl.BlockSpec((1, band, half), lambda i, r: (i, jnp.minimum(r + 1, nb - 1), 1)),
                  const(conv_w.shape), const(conv_b.shape), const(ln_w.shape), const(ln_b.shape)],
        out_specs=pl.BlockSpec((1, band, D_CONV), lambda i, r: (i, r, 0)),
        out_shape=jax.ShapeDtypeStruct((b, l, D_CONV), F32),
        scratch_shapes=[pltpu.VMEM((CFM_ROWS, GRID_W + 4 * SUBLANES, half), F32),
                        pltpu.VMEM((3 * band, half), F32),
                        pltpu.VMEM((SUBLANES, GRID_W + 3 * SUBLANES, half), F32)],
        compiler_params=_params(("parallel", "parallel")),
        name="cfm",
    )(u, u, u, conv_w, conv_b, ln_w, ln_b)


def _outproj_kernel(yf_ref, yb_ref, z_ref, cfm_ref, x_ref, mod_ref, gnw_ref, w_ref, n2w_ref,
                    x1_ref, h2_ref, h2t_ref):
    gw = D_SSM // SSM_GROUPS
    g = (yf_ref[0] + yb_ref[0]) * _silu(z_ref[0])
    parts = []
    for k in range(SSM_GROUPS):
        gk = g[:, k * gw:(k + 1) * gw]
        parts.append(gk * lax.rsqrt(jnp.mean(gk * gk, axis=-1, keepdims=True) + EPS))
    ssm = jnp.concatenate(parts, axis=-1) * gnw_ref[...]
    mixed = (jnp.dot(ssm.astype(BF16), w_ref[0:D_SSM, :], preferred_element_type=F32)
             + jnp.dot(cfm_ref[0].astype(BF16), w_ref[D_SSM:, :], preferred_element_type=F32))
    x1 = x_ref[0] + mod_ref[0, 2:3, :] * mixed
    x1_ref[0] = x1
    h = x1 * lax.rsqrt(jnp.mean(x1 * x1, axis=-1, keepdims=True) + EPS) * n2w_ref[...]
    h2 = h * (1.0 + mod_ref[0, 4:5, :]) + mod_ref[0, 3:4, :]
    h2_ref[0] = h2.astype(BF16)
    h2t_ref[...] = h2.T.astype(BF16)


def _outproj(yf, yb, z, cfm, x, mod, gn_w, w_out_b, n2_w):
    b, l, d = x.shape
    tm = min(OUT_TILE, l)
    tok = lambda w: pl.BlockSpec((1, tm, w), lambda i, j: (i, j, 0))
    const = lambda shape: pl.BlockSpec(shape, lambda i, j: (0,) * len(shape))
    return pl.pallas_call(
        _outproj_kernel,
        grid=(b, l // tm),
        in_specs=[tok(D_SSM), tok(D_SSM), tok(D_SSM), tok(D_CONV), tok(d),
                  pl.BlockSpec((1, 6, d), lambda i, j: (i, 0, 0)),
                  const(gn_w.shape), const(w_out_b.shape), const(n2_w.shape)],
        out_specs=[tok(d), tok(d), pl.BlockSpec((d, tm), lambda i, j: (0, i * (l // tm) + j))],
        out_shape=[jax.ShapeDtypeStruct((b, l, d), F32), jax.ShapeDtypeStruct((b, l, d), BF16),
                   jax.ShapeDtypeStruct((d, b * l), BF16)],
        compiler_params=_params(("parallel", "parallel")),
        name="outproj",
    )(yf, yb, z, cfm, x, mod, gn_w, w_out_b, n2_w)


def _oddeven_merge(lo, hi, r):
    step = r * 2
    if step < hi - lo:
        yield from _oddeven_merge(lo, hi, step)
        yield from _oddeven_merge(lo + r, hi, step)
        yield from [(i, i + r) for i in range(lo + r, hi - r, step)]
    else:
        yield (lo, lo + r)


def _oddeven_sort(lo, hi):
    if hi > lo:
        mid = lo + (hi - lo) // 2
        yield from _oddeven_sort(lo, mid)
        yield from _oddeven_sort(mid + 1, hi)
        yield from _oddeven_merge(lo, hi, 1)


SORT_NET = tuple(_oddeven_sort(0, PEER_TOPK - 1))
PACKED = jnp.uint32


def _cmp_exchange(v, i, j):
    v[i], v[j] = jnp.maximum(v[i], v[j]), jnp.minimum(v[i], v[j])


def _bitonic_sort(v):
    d = len(v) // 2
    while d:
        for i in range(len(v)):
            if not i & d:
                _cmp_exchange(v, i, i + d)
        d //= 2


def _merge_top(v, w):
    n = len(v)
    return [jnp.maximum(v[i], w[n - 1 - i]) for i in range(n)]


def _sublane_rolls():
    shift = SUBLANES // 2
    while shift:
        yield shift
        shift //= 2


def _top_sorted(v):
    for shift in _sublane_rolls():
        v = _merge_top(v, [pltpu.roll(x, shift, 0) for x in v])
        _bitonic_sort(v)
    return v


def _double_bf16(x):
    bits = pltpu.bitcast(x.astype(BF16).astype(F32), jnp.uint32)
    return bits | (bits >> 16)


def _score_kernel(h_ref, wq_ref, keys_ref, neg_ref, c1_ref, e1_ref, r2_ref, e2_ref, q_ref):
    tm = h_ref.shape[0]
    nsub = tm // LANES
    kd = PEER_KEY_DIM // 2
    k = PEER_TOPK
    ntile = N_KEYS // SUBLANES
    q_ref[...] = jnp.dot(h_ref[...], wq_ref[...], preferred_element_type=F32).astype(BF16)

    def body(it, carry):
        h = it // nsub
        tok = pl.multiple_of((it % nsub) * LANES, LANES)
        nt = (((1,), (1,)), ((), ()))

        def score_tiles(p):
            qs = q_ref[pl.ds(tok, LANES), pl.ds(pl.multiple_of((2 * h + p) * kd, kd), kd)]
            s = lax.dot_general(keys_ref[2 * h + p], qs, nt, preferred_element_type=F32)
            return [s[i * SUBLANES:(i + 1) * SUBLANES, :] for i in range(ntile)]

        def top(tiles):
            v = list(tiles)
            for i, j in SORT_NET:
                _cmp_exchange(v, i, j)
            return _top_sorted(v)

        s1 = score_tiles(0)
        s2 = score_tiles(1)
        a = top(s1)
        b = top(s2)

        sub = lax.broadcasted_iota(jnp.int32, (SUBLANES, LANES), 0)
        pick = lambda lo, x, y: jnp.where(sub < lo, x, y)
        base = pick(4, pick(2, pick(1, a[0], a[1]), pick(3, a[2], a[3])), pick(5, b[0], pick(6, b[1], b[2])))
        cand = []
        for m in range(k):
            other = pick(4, b[m], a[4 + m]) if 4 + m < k else b[m]
            cand.append((other + base) + neg_ref[m])
        lists = list(cand)
        rolls = list(_sublane_rolls())
        for shift in rolls[:-1]:
            lists = _merge_top(lists, [pltpu.roll(x, shift, 0) for x in lists])
            _bitonic_sort(lists)
        lists = _merge_top(lists, [pltpu.roll(x, rolls[-1], 0) for x in lists])
        tau = functools.reduce(jnp.minimum, lists)

        best = a[0] + b[0]
        zsum = sum(jnp.where(c >= tau, jnp.exp(c - best), 0.0) for c in cand)
        for shift in rolls:
            zsum = zsum + pltpu.roll(zsum, shift, 0)
        scale = 0.5 / zsum
        top_extra = sum(jnp.where(a[0] + b[j] >= tau, 1.0, 0.0) for j in range(k // 2, k))

        c1_t, e1_t, r2_t, e2_t = [], [], [], []
        for t1, t2 in zip(s1, s2):
            c1 = jnp.zeros_like(t1)
            for j in range(k // 2):
                c1 = jnp.where(t1 + b[j] >= tau, float(j + 1), c1)
            c1_t.append(_double_bf16(jnp.where(t1 == a[0], c1 + top_extra, c1)))
            e1_t.append(_double_bf16(jnp.exp(t1 - a[0]) * scale))
            r2 = jnp.zeros_like(t2)
            for j in range(k):
                r2 = jnp.where(t2 < b[j], float(j + 1), r2)
            r2_t.append(r2)
            e2_t.append(jnp.exp(t2 - b[0]))
        c1_ref[h, :, pl.ds(tok, LANES)] = jnp.concatenate(c1_t, axis=0)
        e1_ref[h, :, pl.ds(tok, LANES)] = jnp.concatenate(e1_t, axis=0)
        r2_ref[h, :, pl.ds(tok, LANES)] = jnp.concatenate(r2_t, axis=0).astype(BF16)
        e2_ref[h, :, pl.ds(tok, LANES)] = jnp.concatenate(e2_t, axis=0).astype(BF16)
        return carry

    lax.fori_loop(0, PEER_HEADS * nsub, body, 0, unroll=2)


def _candidate_mask():
    k = PEER_TOPK
    m = lax.broadcasted_iota(jnp.int32, (k, SUBLANES, LANES), 0)
    s = lax.broadcasted_iota(jnp.int32, (k, SUBLANES, LANES), 1)
    row_ok = (s < 4) & ((s + 1) * (m + 1) <= k)
    col_ok = (s >= 4) & (s < 7) & (4 + m < k) & ((5 + m) * (s - 3) <= k)
    return jnp.where(row_ok | col_ok, 0.0, -jnp.inf).astype(F32)


def _score(h2, wq_b, keys_b):
    t, d = h2.shape
    tm = min(SCORE_TILE, t)
    neg = _candidate_mask()
    big = lambda: pl.BlockSpec((PEER_HEADS, N_KEYS, tm), lambda j: (0, 0, j))
    big_shape = lambda dt: jax.ShapeDtypeStruct((PEER_HEADS, N_KEYS, t), dt)
    return pl.pallas_call(
        _score_kernel,
        grid=(t // tm,),
        in_specs=[pl.BlockSpec((tm, d), lambda j: (j, 0)),
                  pl.BlockSpec(wq_b.shape, lambda j: (0, 0)),
                  pl.BlockSpec(keys_b.shape, lambda j: (0, 0, 0)),
                  pl.BlockSpec(neg.shape, lambda j: (0, 0, 0))],
        out_specs=[big(), big(), big(), big()],
        out_shape=[big_shape(PACKED), big_shape(PACKED), big_shape(BF16), big_shape(BF16)],
        scratch_shapes=[pltpu.VMEM((tm, PEER_HEADS * PEER_KEY_DIM), BF16)],
        compiler_params=_params(("parallel",)),
        name="peer_score",
    )(h2, wq_b, keys_b, neg)


def _key_row_bf16(ref, h, i1):
    word_rows = jnp.broadcast_to(ref[h, pl.ds(i1, 1), :], (SUBLANES, ref.shape[-1]))
    rows = pltpu.bitcast(word_rows, BF16)
    return pltpu.repeat(rows, N_KEYS // rows.shape[0], axis=0)


def _peer_kernel(ht_ref, u_ref, vt_ref, c1_ref, e1_ref, r2_ref, e2_ref, x1_ref, mod_ref, fnw_ref,
                 o_ref, acc_ref):
    eb = pl.program_id(1)
    nsub = PEER_EBLK // PEER_SUB
    keys_per_sub = PEER_SUB // N_KEYS
    keys_per_step = PEER_EBLK // N_KEYS
    key0 = pl.multiple_of(eb * keys_per_step, keys_per_step)

    @pl.when(eb == 0)
    def _():
        acc_ref[...] = jnp.zeros_like(acc_ref)

    def first_matmul(sub):
        return jnp.dot(u_ref[sub * PEER_SUB:(sub + 1) * PEER_SUB, :], ht_ref[...],
                       preferred_element_type=F32)

    def gate_weighted(sub, pre):
        act = pre * (1.0 + lax.erf(pre * (2.0 ** -0.5)))
        gates = []
        for kk in range(keys_per_sub):
            i1 = key0 + (sub * keys_per_sub + kk)
            g = jnp.zeros((N_KEYS, ht_ref.shape[1]), BF16)
            for h in range(PEER_HEADS):
                c1 = _key_row_bf16(c1_ref, h, i1)
                e1 = _key_row_bf16(e1_ref, h, i1)
                g = g + jnp.where(r2_ref[h] < c1, e2_ref[h], jnp.zeros((), BF16)) * e1
            gates.append(g)
        return act.astype(BF16) * jnp.concatenate(gates, axis=0)

    acc = acc_ref[...]
    pre = first_matmul(0)
    for sub in range(nsub):
        nxt = first_matmul(sub + 1) if sub + 1 < nsub else None
        weighted = gate_weighted(sub, pre)
        acc = jnp.dot(vt_ref[:, sub * PEER_SUB:(sub + 1) * PEER_SUB], weighted,
                      preferred_element_type=F32) + acc
        pre = nxt
    acc_ref[...] = acc

    @pl.when(eb == pl.num_programs(1) - 1)
    def _():
        x2 = x1_ref[...] + mod_ref[0, 5:6, :] * acc_ref[...].T
        o_ref[...] = x2 * lax.rsqrt(jnp.mean(x2 * x2, axis=-1, keepdims=True) + EPS) * fnw_ref[...]


def _peer(h2t, u_b, vt_b, c1, e1, r2, e2, x1, mod, fn_w, seq):
    d, t = h2t.shape
    tm = min(PEER_TILE, seq)
    n_exp = u_b.shape[0]
    big = lambda: pl.BlockSpec((PEER_HEADS, N_KEYS, tm), lambda j, e: (0, 0, j))
    return pl.pallas_call(
        _peer_kernel,
        grid=(t // tm, n_exp // PEER_EBLK),
        in_specs=[pl.BlockSpec((d, tm), lambda j, e: (0, j)),
                  pl.BlockSpec((PEER_EBLK, d), lambda j, e: (e, 0)),
                  pl.BlockSpec((None, d, PEER_EBLK), lambda j, e: (e, 0, 0)),
                  big(), big(), big(), big(),
                  pl.BlockSpec((tm, d), lambda j, e: (j, 0)),
                  pl.BlockSpec((1, 6, d), lambda j, e: ((j * tm) // seq, 0, 0)),
                  pl.BlockSpec((1, d), lambda j, e: (0, 0))],
        out_specs=pl.BlockSpec((tm, d), lambda j, e: (j, 0)),
        out_shape=jax.ShapeDtypeStruct((t, d), F32),
        scratch_shapes=[pltpu.VMEM((d, tm), F32)],
        compiler_params=_params(("parallel", "arbitrary")),
        name="peer_dense",
    )(h2t, u_b, vt_b, c1, e1, r2, e2, x1, mod, fn_w)


def _pad_lanes(v, width=LANES):
    flat = v.reshape(1, -1)
    return jnp.pad(flat, ((0, 0), (0, width - flat.shape[1])))


def _head_expand(col0):
    rows = lax.broadcasted_iota(jnp.int32, (LANES, D_SSM), 0)
    heads = lax.broadcasted_iota(jnp.int32, (LANES, D_SSM), 1) // SSM_HEADDIM
    return (rows == heads + col0).astype(BF16)


def kernel(x, c, ctx, c_ctx, ada_w, ada_b, norm1_w, norm2_w, w_in, ssm_conv_w, ssm_conv_b, ssm_dt_bias,
           ssm_a_log, ssm_d, ssm_norm_w, cfm_conv_w, cfm_conv_b, cfm_ln_w, cfm_ln_b, w_out, peer_wq,
           peer_subkeys, peer_u, peer_v, final_norm_w):
    depth = ada_w.shape[0]
    assert depth == 1, "single-layer configuration"
    b, l, d = x.shape
    i = 0

    c_rows = jnp.concatenate([c, c_ctx[None], jnp.zeros((2 * SUBLANES - b - 1, d), F32)], axis=0)
    mod_all = _ada(c_rows, ada_w[i], ada_b[i])
    mod = mod_all[:b].reshape(b, 6, d)
    mod_ctx = mod_all[b:b + 1].reshape(1, 6, d)

    wi = w_in[i]
    o_dt = D_SSM + XBC_DIM
    w_r = jnp.concatenate([wi[:, :o_dt], wi[:, o_dt + 2 * SSM_HEADS:], wi[:, o_dt:o_dt + 2 * SSM_HEADS],
                           jnp.zeros((d, LANES - 2 * SSM_HEADS), F32)], axis=1).astype(BF16)
    z_l, xbc_l, u_l, dt_l = _inproj(x, mod, True, norm1_w[i], w_r)
    _, xbc_c, _, dt_c = _inproj(ctx, mod_ctx, False, norm1_w[i], w_r)

    conv_w = jnp.pad(ssm_conv_w[i], ((0, SUBLANES - SSM_CONV), (0, 0)))
    conv_b = ssm_conv_b[i].reshape(1, XBC_DIM)
    dt_bias = _pad_lanes(ssm_dt_bias[i])
    a_log = _pad_lanes(ssm_a_log[i])
    d_skip = jnp.repeat(ssm_d[i], SSM_HEADDIM).reshape(1, D_SSM)
    zero_state = jnp.zeros((b, SSM_STATE, D_SSM), F32)
    y_dirs = []
    act_c = act_l = None
    for reverse in (False, True):
        expand = _head_expand(SSM_HEADS if reverse else 0)
        args = (conv_w, conv_b, dt_bias, a_log, d_skip, expand, reverse)
        if not reverse:
            _, h_ctx, act_c = _ssd(xbc_c, dt_c, zero_state, *args, add_skip=False, activated=False)
            y_dir, _, act_l = _ssd(xbc_l, dt_l, h_ctx, *args, add_skip=True, activated=False)
        else:
            _, h_ctx = _ssd(act_c, dt_c, zero_state, *args, add_skip=False, activated=True)
            y_dir, _ = _ssd(act_l, dt_l, h_ctx, *args, add_skip=False, activated=True)
        y_dirs.append(y_dir)

    cfm = _cfm(u_l, jnp.pad(cfm_conv_w[i], ((0, 1), (0, 0))), cfm_conv_b[i].reshape(1, D_CONV),
               cfm_ln_w[i].reshape(1, D_CONV), cfm_ln_b[i].reshape(1, D_CONV))

    x1, h2, h2t = _outproj(y_dirs[0], y_dirs[1], z_l, cfm, x, mod, ssm_norm_w[i].reshape(1, D_SSM),
                      w_out[i].astype(BF16), norm2_w[i].reshape(1, d))

    t = b * l
    h2 = h2.reshape(t, d)
    keys_b = peer_subkeys[i].reshape(PEER_HEADS * 2, N_KEYS, PEER_KEY_DIM // 2).astype(BF16)
    c1, e1, r2, e2 = _score(h2, peer_wq[i].astype(BF16), keys_b)
    vt_b = peer_v[i].reshape(-1, PEER_EBLK, d).transpose(0, 2, 1).astype(BF16)
    out = _peer(h2t, peer_u[i].astype(BF16), vt_b, c1, e1, r2, e2,
                x1.reshape(t, d), mod, final_norm_w.reshape(1, d), l)
    return out.reshape(b, l, d)
```

```python
import functools

import jax
import jax.numpy as jnp
from jax import lax
from jax.experimental import pallas as pl
from jax.experimental.pallas import tpu as pltpu

F32 = jnp.float32
BF16 = jnp.bfloat16

D_MODEL = 1024
GRID_W = 64
D_SSM = 512
SSM_HEADDIM = 64
SSM_HEADS = 8
SSM_STATE = 128
SSM_GROUPS = 2
SSM_CONV = 5
SSD_CHUNK = 128
XBC_DIM = D_SSM + 2 * SSM_GROUPS * SSM_STATE
D_CONV = 512
CONV_WIDTH = 31
PEER_HEADS = 8
PEER_KEY_DIM = 256
N_KEYS = 128
PEER_TOPK = 16
EPS = 1e-6

LANES = 128
SUBLANES = 8
VMEM_LIMIT = 56 * 1024 * 1024

IN_TILE = 512
SSD_ROWS = 2
CFM_ROWS = 16
OUT_TILE = 512
SCORE_TILE = 512
PEER_TILE = 512
PEER_EBLK = 2048
PEER_SUB = 256


def _params(sem):
    return pltpu.CompilerParams(dimension_semantics=sem, vmem_limit_bytes=VMEM_LIMIT)


def _silu(v):
    return v * jax.nn.sigmoid(v)


def _split3(v):
    hi = v.astype(BF16)
    r1 = v - hi.astype(F32)
    mid = r1.astype(BF16)
    lo = (r1 - mid.astype(F32)).astype(BF16)
    return hi, mid, lo


def _dot01_right(v, m01):
    return sum(jnp.dot(p, m01, preferred_element_type=F32) for p in _split3(v))


def _dot01_left(m01, v):
    return sum(jnp.dot(m01, p, preferred_element_type=F32) for p in _split3(v))


def _ada_kernel(c_ref, w_ref, b_ref, o_ref):
    sc = _silu(c_ref[...])
    o_ref[...] = jnp.dot(sc.astype(BF16), w_ref[...].astype(BF16),
                         preferred_element_type=F32) + b_ref[...]


def _ada(c_rows, ada_w, ada_b):
    n = ada_w.shape[1]
    tn = 768
    return pl.pallas_call(
        _ada_kernel,
        grid=(n // tn,),
        in_specs=[pl.BlockSpec(c_rows.shape, lambda j: (0, 0)),
                  pl.BlockSpec((D_MODEL, tn), lambda j: (0, j)),
                  pl.BlockSpec((1, tn), lambda j: (0, j))],
        out_specs=pl.BlockSpec((c_rows.shape[0], tn), lambda j: (0, j)),
        out_shape=jax.ShapeDtypeStruct((c_rows.shape[0], n), F32),
        compiler_params=_params(("parallel",)),
        name="ada",
    )(c_rows, ada_w, ada_b.reshape(1, n))


def _inproj_kernel(x_ref, mod_ref, nw_ref, w_ref, z_ref, xbc_ref, u_ref, dt_ref):
    xv = x_ref[0]
    h = xv * lax.rsqrt(jnp.mean(xv * xv, axis=-1, keepdims=True) + EPS) * nw_ref[...]
    h = h * (1.0 + mod_ref[0, 1:2, :]) + mod_ref[0, 0:1, :]
    hb = h.astype(BF16)

    def proj(lo, hi):
        return jnp.dot(hb, w_ref[:, lo:hi], preferred_element_type=F32)

    z_ref[0] = proj(0, D_SSM)
    xbc_ref[0] = proj(D_SSM, D_SSM + XBC_DIM)
    o = D_SSM + XBC_DIM
    u_ref[0] = proj(o, o + D_CONV) * jax.nn.sigmoid(proj(o + D_CONV, o + 2 * D_CONV))
    dt_ref[0] = proj(o + 2 * D_CONV, o + 2 * D_CONV + LANES)


def _inproj(x, mod, per_batch_mod, norm_w, w_r):
    b, l, d = x.shape
    tm = min(IN_TILE, l)
    mod_idx = (lambda i, j: (i, 0, 0)) if per_batch_mod else (lambda i, j: (0, 0, 0))
    tok = lambda w: pl.BlockSpec((1, tm, w), lambda i, j: (i, j, 0))
    return pl.pallas_call(
        _inproj_kernel,
        grid=(b, l // tm),
        in_specs=[tok(d),
                  pl.BlockSpec((1, 6, d), mod_idx),
                  pl.BlockSpec((1, d), lambda i, j: (0, 0)),
                  pl.BlockSpec(w_r.shape, lambda i, j: (0, 0))],
        out_specs=[tok(D_SSM), tok(XBC_DIM), tok(D_CONV), tok(LANES)],
        out_shape=[jax.ShapeDtypeStruct((b, l, w), F32) for w in (D_SSM, XBC_DIM, D_CONV, LANES)],
        compiler_params=_params(("parallel", "parallel")),
        name="inproj",
    )(x, mod, norm_w.reshape(1, d), w_r)


def _ssd_kernel(cur_ref, prev_ref, next_ref, dt_ref, h0_ref, cw_ref, cb_ref, dtb_ref, alog_ref,
                dskip_ref, expand_ref, y_ref, hfin_ref, *rest, reverse, nc, add_skip, activated):
    act_ref, ext_ref, state_ref = (None, *rest) if activated else rest
    q = SSD_CHUNK
    halo = SUBLANES
    pad = (SSM_CONV - 1) // 2
    gw = D_SSM // SSM_GROUPS
    hpg = SSM_HEADS // SSM_GROUPS
    col0 = SSM_HEADS if reverse else 0
    c = pl.program_id(1)
    cc = (nc - 1 - c) if reverse else c

    @pl.when(c == 0)
    def _():
        state_ref[...] = h0_ref[...]

    ri = lax.broadcasted_iota(jnp.int32, (q, q), 0)
    ci = lax.broadcasted_iota(jnp.int32, (q, q), 1)
    causal = (ci >= ri) if reverse else (ci <= ri)
    causal01 = jnp.where(causal, 1.0, 0.0).astype(BF16)
    a_row = -jnp.exp(alog_ref[...])

    for r in range(cur_ref.shape[0]):
        if activated:
            act = cur_ref[r]
        else:
            ext_ref[r, 0:halo, :] = jnp.where(cc == 0, 0.0, prev_ref[r])
            ext_ref[r, halo:halo + q, :] = cur_ref[r]
            ext_ref[r, halo + q:2 * halo + q, :] = jnp.where(cc == nc - 1, 0.0, next_ref[r])
            conv = cb_ref[...] + cw_ref[0:1, :] * ext_ref[r, halo - pad:halo - pad + q, :]
            for k in range(1, SSM_CONV):
                conv = conv + cw_ref[k:k + 1, :] * ext_ref[r, halo - pad + k:halo - pad + k + q, :]
            act = _silu(conv)
            act_ref[r] = act

        dtv = dt_ref[r] + dtb_ref[...]
        dt = jnp.maximum(dtv, 0.0) + jnp.log(1.0 + jnp.exp(-jnp.abs(dtv)))
        acum = _dot01_left(causal01, dt * a_row)
        acum_t = acum.T
        tot = acum[0:1, :] if reverse else acum[q - 1:q, :]
        stacked = jnp.concatenate([dt, jnp.exp(tot - acum), jnp.exp(acum),
                                   jnp.broadcast_to(jnp.exp(tot), (SUBLANES, LANES))], axis=0)
        spread = _dot01_right(stacked, expand_ref[...])
        dt_e, dte_e, ea_e = spread[0:q], spread[q:2 * q], spread[2 * q:3 * q]
        etot_e = spread[3 * q:3 * q + 1]

        xs = act[:, :D_SSM]
        xdt = xs * dt_e
        xdt_b = xdt.astype(BF16)
        w_b = (xdt * dte_e).astype(BF16)
        for g in range(SSM_GROUPS):
            gs = slice(g * gw, (g + 1) * gw)
            b_g = act[:, D_SSM + g * SSM_STATE:D_SSM + (g + 1) * SSM_STATE]
            c_g = act[:, D_SSM + (SSM_GROUPS + g) * SSM_STATE:D_SSM + (SSM_GROUPS + g + 1) * SSM_STATE]
            c_b = c_g.astype(BF16)
            bt_b = b_g.T.astype(BF16)
            cb = jnp.dot(c_b, bt_b, preferred_element_type=F32)
            st = state_ref[r, :, gs]
            y_g = jnp.dot(c_b, st.astype(BF16), preferred_element_type=F32) * ea_e[:, gs]
            y_heads = []
            for hh in range(hpg):
                h = g * hpg + hh
                col = col0 + h
                diff = acum[:, col:col + 1] - acum_t[col:col + 1, :]
                decay = jnp.where(causal, jnp.exp(jnp.minimum(diff, 0.0)), 0.0)
                y_heads.append(jnp.dot((cb * decay).astype(BF16),
                                       xdt_b[:, h * SSM_HEADDIM:(h + 1) * SSM_HEADDIM],
                                       preferred_element_type=F32))
            y_g = y_g + jnp.concatenate(y_heads, axis=-1)
            if add_skip:
                y_g = y_g + dskip_ref[:, gs] * xs[:, gs]
            y_ref[r, :, gs] = y_g
            state_ref[r, :, gs] = st * etot_e[:, gs] + jnp.dot(bt_b, w_b[:, gs], preferred_element_type=F32)
    hfin_ref[...] = state_ref[...]


def _ssd(xbc, dtp, h0, conv_w, conv_b, dt_bias, a_log, d_skip, expand, reverse, add_skip, activated):
    b, l, _ = xbc.shape
    q = SSD_CHUNK
    nc = l // q
    nr = SSD_ROWS if b % SSD_ROWS == 0 else 1
    hb = q // SUBLANES
    pos = (lambda c: nc - 1 - c) if reverse else (lambda c: c)
    const = lambda shape: pl.BlockSpec(shape, lambda i, c: (0,) * len(shape))
    kern = functools.partial(_ssd_kernel, reverse=reverse, nc=nc, add_skip=add_skip, activated=activated)
    chunk = lambda w: pl.BlockSpec((nr, q, w), lambda i, c: (i, pos(c), 0))
    act_spec = [] if activated else [chunk(XBC_DIM)]
    act_shape = [] if activated else [jax.ShapeDtypeStruct((b, l, XBC_DIM), F32)]
    return pl.pallas_call(
        kern,
        grid=(b // nr, nc),
        in_specs=[pl.BlockSpec((nr, q, XBC_DIM), lambda i, c: (i, pos(c), 0)),
                  pl.BlockSpec((nr, SUBLANES, XBC_DIM), lambda i, c: (i, jnp.maximum(pos(c) * hb - 1, 0), 0)),
                  pl.BlockSpec((nr, SUBLANES, XBC_DIM),
                               lambda i, c: (i, jnp.minimum((pos(c) + 1) * hb, l // SUBLANES - 1), 0)),
                  pl.BlockSpec((nr, q, LANES), lambda i, c: (i, pos(c), 0)),
                  pl.BlockSpec((nr, SSM_STATE, D_SSM), lambda i, c: (i, 0, 0)),
                  const(conv_w.shape), const(conv_b.shape), const(dt_bias.shape), const(a_log.shape),
                  const(d_skip.shape), const(expand.shape)],
        out_specs=[chunk(D_SSM), pl.BlockSpec((nr, SSM_STATE, D_SSM), lambda i, c: (i, 0, 0))] + act_spec,
        out_shape=[jax.ShapeDtypeStruct((b, l, D_SSM), F32),
                   jax.ShapeDtypeStruct((b, SSM_STATE, D_SSM), F32)] + act_shape,
        scratch_shapes=[pltpu.VMEM((nr, q + 2 * SUBLANES, XBC_DIM), F32),
                        pltpu.VMEM((nr, SSM_STATE, D_SSM), F32)],
        compiler_params=_params(("parallel", "arbitrary")),
        name="ssd_bwd" if reverse else "ssd_fwd",
    )(xbc, xbc, xbc, dtp, h0, conv_w, conv_b, dt_bias, a_log, d_skip, expand)


def _cfm_kernel(cur_ref, prev_ref, next_ref, w_ref, b_ref, lnw_ref, lnb_ref, o_ref, hpad_ref, vbuf_ref,
                shift_ref, *, nb):
    rows = CFM_ROWS
    band = rows * GRID_W
    half = D_CONV // 2
    pad = (CONV_WIDTH - 1) // 2
    lead = 2 * SUBLANES
    r = pl.program_id(1)

    zeros = jnp.zeros((rows, lead, half), F32)
    hpad_ref[:, 0:lead, :] = zeros
    hpad_ref[:, lead + GRID_W:2 * lead + GRID_W, :] = zeros
    hpad_ref[:, lead:lead + GRID_W, :] = cur_ref[0, :, 0:half].reshape(rows, GRID_W, half)
    vbuf_ref[0:band, :] = jnp.where(r == 0, 0.0, prev_ref[0])
    vbuf_ref[band:2 * band, :] = cur_ref[0, :, half:D_CONV]
    vbuf_ref[2 * band:3 * band, :] = jnp.where(r == nb - 1, 0.0, next_ref[0])

    span = shift_ref.shape[1]

    def row_body(i, carry):
        for s in range(SUBLANES):
            shift_ref[s] = hpad_ref[i, s:s + span, :]
        acc_h = jnp.zeros((GRID_W, half), F32)
        acc_v = jnp.zeros((GRID_W, half), F32)
        for k in range(CONV_WIDTH):
            first = lead - pad + k
            tile0 = first // SUBLANES * SUBLANES
            acc_h = acc_h + w_ref[k:k + 1, 0:half] * shift_ref[first % SUBLANES, tile0:tile0 + GRID_W, :]
            start = pl.multiple_of(band + (i + k - pad) * GRID_W, GRID_W)
            acc_v = acc_v + w_ref[k:k + 1, half:D_CONV] * vbuf_ref[pl.ds(start, GRID_W), :]
        conv = jnp.concatenate([acc_h, acc_v], axis=-1) + b_ref[...]
        mu = jnp.mean(conv, axis=-1, keepdims=True)
        cen = conv - mu
        var = jnp.mean(cen * cen, axis=-1, keepdims=True)
        o_ref[0, pl.ds(pl.multiple_of(i * GRID_W, GRID_W), GRID_W), :] = _silu(
            cen * lax.rsqrt(var + EPS) * lnw_ref[...] + lnb_ref[...])
        return carry

    lax.fori_loop(0, rows, row_body, 0)


def _cfm(u, conv_w, conv_b, ln_w, ln_b):
    b, l, _ = u.shape
    band = CFM_ROWS * GRID_W
    nb = l // band
    half = D_CONV // 2
    const = lambda shape: pl.BlockSpec(shape, lambda i, r: (0,) * len(shape))
    return pl.pallas_call(
        functools.partial(_cfm_kernel, nb=nb),
        grid=(b, nb),
        in_specs=[pl.BlockSpec((1, band, D_CONV), lambda i, r: (i, r, 0)),
                  pl.BlockSpec((1, band, half), lambda i, r: (i, jnp.maximum(r - 1, 0), 1)),
                  pl.BlockSpec((1, band, half), lambda i, r: (i, jnp.minimum(r + 1, nb - 1), 1)),
                  const(conv_w.shape), const(conv_b.shape), const(ln_w.shape), const(ln_b.shape)],
        out_specs=pl.BlockSpec((1, band, D_CONV), lambda i, r: (i, r, 0)),
        out_shape=jax.ShapeDtypeStruct((b, l, D_CONV), F32),
        scratch_shapes=[pltpu.VMEM((CFM_ROWS, GRID_W + 4 * SUBLANES, half), F32),
                        pltpu.VMEM((3 * band, half), F32),
                        pltpu.VMEM((SUBLANES, GRID_W + 3 * SUBLANES, half), F32)],
        compiler_params=_params(("parallel", "parallel")),
        name="cfm",
    )(u, u, u, conv_w, conv_b, ln_w, ln_b)


def _outproj_kernel(yf_ref, yb_ref, z_ref, cfm_ref, x_ref, mod_ref, gnw_ref, w_ref, n2w_ref,
                    x1_ref, h2_ref, h2t_ref):
    gw = D_SSM // SSM_GROUPS
    g = (yf_ref[0] + yb_ref[0]) * _silu(z_ref[0])
    parts = []
    for k in range(SSM_GROUPS):
        gk = g[:, k * gw:(k + 1) * gw]
        parts.append(gk * lax.rsqrt(jnp.mean(gk * gk, axis=-1, keepdims=True) + EPS))
    ssm = jnp.concatenate(parts, axis=-1) * gnw_ref[...]
    mixed = (jnp.dot(ssm.astype(BF16), w_ref[0:D_SSM, :], preferred_element_type=F32)
             + jnp.dot(cfm_ref[0].astype(BF16), w_ref[D_SSM:, :], preferred_element_type=F32))
    x1 = x_ref[0] + mod_ref[0, 2:3, :] * mixed
    x1_ref[0] = x1
    h = x1 * lax.rsqrt(jnp.mean(x1 * x1, axis=-1, keepdims=True) + EPS) * n2w_ref[...]
    h2 = h * (1.0 + mod_ref[0, 4:5, :]) + mod_ref[0, 3:4, :]
    h2_ref[0] = h2.astype(BF16)
    h2t_ref[...] = h2.T.astype(BF16)


def _outproj(yf, yb, z, cfm, x, mod, gn_w, w_out_b, n2_w):
    b, l, d = x.shape
    tm = min(OUT_TILE, l)
    tok = lambda w: pl.BlockSpec((1, tm, w), lambda i, j: (i, j, 0))
    const = lambda shape: pl.BlockSpec(shape, lambda i, j: (0,) * len(shape))
    return pl.pallas_call(
        _outproj_kernel,
        grid=(b, l // tm),
        in_specs=[tok(D_SSM), tok(D_SSM), tok(D_SSM), tok(D_CONV), tok(d),
                  pl.BlockSpec((1, 6, d), lambda i, j: (i, 0, 0)),
                  const(gn_w.shape), const(w_out_b.shape), const(n2_w.shape)],
        out_specs=[tok(d), tok(d), pl.BlockSpec((d, tm), lambda i, j: (0, i * (l // tm) + j))],
        out_shape=[jax.ShapeDtypeStruct((b, l, d), F32), jax.ShapeDtypeStruct((b, l, d), BF16),
                   jax.ShapeDtypeStruct((d, b * l), BF16)],
        compiler_params=_params(("parallel", "parallel")),
        name="outproj",
    )(yf, yb, z, cfm, x, mod, gn_w, w_out_b, n2_w)


def _oddeven_merge(lo, hi, r):
    step = r * 2
    if step < hi - lo:
        yield from _oddeven_merge(lo, hi, step)
        yield from _oddeven_merge(lo + r, hi, step)
        yield from [(i, i + r) for i in range(lo + r, hi - r, step)]
    else:
        yield (lo, lo + r)


def _oddeven_sort(lo, hi):
    if hi > lo:
        mid = lo + (hi - lo) // 2
        yield from _oddeven_sort(lo, mid)
        yield from _oddeven_sort(mid + 1, hi)
        yield from _oddeven_merge(lo, hi, 1)


SORT_NET = tuple(_oddeven_sort(0, PEER_TOPK - 1))


def _cmp_exchange(v, i, j):
    v[i], v[j] = jnp.maximum(v[i], v[j]), jnp.minimum(v[i], v[j])


def _bitonic_sort(v):
    d = len(v) // 2
    while d:
        for i in range(len(v)):
            if not i & d:
                _cmp_exchange(v, i, i + d)
        d //= 2


def _merge_top(v, w):
    n = len(v)
    return [jnp.maximum(v[i], w[n - 1 - i]) for i in range(n)]


def _sublane_rolls():
    shift = SUBLANES // 2
    while shift:
        yield shift
        shift //= 2


def _top_sorted(v):
    for shift in _sublane_rolls():
        v = _merge_top(v, [pltpu.roll(x, shift, 0) for x in v])
        _bitonic_sort(v)
    return v


def _score_kernel(h_ref, wq_ref, keys_ref, neg_ref, c1_ref, e1_ref, r2_ref, e2_ref, q_ref):
    tm = h_ref.shape[0]
    nsub = tm // LANES
    kd = PEER_KEY_DIM // 2
    k = PEER_TOPK
    ntile = N_KEYS // SUBLANES
    q_ref[...] = jnp.dot(h_ref[...], wq_ref[...], preferred_element_type=F32).astype(BF16)

    def body(it, carry):
        h = it // nsub
        tok = pl.multiple_of((it % nsub) * LANES, LANES)
        nt = (((1,), (1,)), ((), ()))

        def score_tiles(p):
            qs = q_ref[pl.ds(tok, LANES), pl.ds(pl.multiple_of((2 * h + p) * kd, kd), kd)]
            s = lax.dot_general(keys_ref[2 * h + p], qs, nt, preferred_element_type=F32)
            return [s[i * SUBLANES:(i + 1) * SUBLANES, :] for i in range(ntile)]

        def top(tiles):
            v = list(tiles)
            for i, j in SORT_NET:
                _cmp_exchange(v, i, j)
            return _top_sorted(v)

        s1 = score_tiles(0)
        s2 = score_tiles(1)
        a = top(s1)
        b = top(s2)

        sub = lax.broadcasted_iota(jnp.int32, (SUBLANES, LANES), 0)
        pick = lambda lo, x, y: jnp.where(sub < lo, x, y)
        base = pick(4, pick(2, pick(1, a[0], a[1]), pick(3, a[2], a[3])), pick(5, b[0], pick(6, b[1], b[2])))
        cand = []
        for m in range(k):
            other = pick(4, b[m], a[4 + m]) if 4 + m < k else b[m]
            cand.append((other + base) + neg_ref[m])
        lists = list(cand)
        rolls = list(_sublane_rolls())
        for shift in rolls[:-1]:
            lists = _merge_top(lists, [pltpu.roll(x, shift, 0) for x in lists])
            _bitonic_sort(lists)
        lists = _merge_top(lists, [pltpu.roll(x, rolls[-1], 0) for x in lists])
        tau = functools.reduce(jnp.minimum, lists)

        best = a[0] + b[0]
        zsum = sum(jnp.where(c >= tau, jnp.exp(c - best), 0.0) for c in cand)
        for shift in rolls:
            zsum = zsum + pltpu.roll(zsum, shift, 0)
        scale = 0.5 / zsum
        top_extra = sum(jnp.where(a[0] + b[j] >= tau, 1.0, 0.0) for j in range(k // 2, k))

        c1_t, e1_t, r2_t, e2_t = [], [], [], []
        for t1, t2 in zip(s1, s2):
            c1 = jnp.zeros_like(t1)
            for j in range(k // 2):
                c1 = jnp.where(t1 + b[j] >= tau, float(j + 1), c1)
            c1_t.append(jnp.where(t1 == a[0], c1 + top_extra, c1))
            e1_t.append(jnp.exp(t1 - a[0]) * scale)
            r2 = jnp.zeros_like(t2)
            for j in range(k):
                r2 = jnp.where(t2 < b[j], float(j + 1), r2)
            r2_t.append(r2)
            e2_t.append(jnp.exp(t2 - b[0]))
        c1_ref[h, :, pl.ds(tok, LANES)] = jnp.concatenate(c1_t, axis=0)
        e1_ref[h, :, pl.ds(tok, LANES)] = jnp.concatenate(e1_t, axis=0)
        r2_ref[h, :, pl.ds(tok, LANES)] = jnp.concatenate(r2_t, axis=0).astype(BF16)
        e2_ref[h, :, pl.ds(tok, LANES)] = jnp.concatenate(e2_t, axis=0).astype(BF16)
        return carry

    lax.fori_loop(0, PEER_HEADS * nsub, body, 0, unroll=2)


def _candidate_mask():
    k = PEER_TOPK
    m = lax.broadcasted_iota(jnp.int32, (k, SUBLANES, LANES), 0)
    s = lax.broadcasted_iota(jnp.int32, (k, SUBLANES, LANES), 1)
    row_ok = (s < 4) & ((s + 1) * (m + 1) <= k)
    col_ok = (s >= 4) & (s < 7) & (4 + m < k) & ((5 + m) * (s - 3) <= k)
    return jnp.where(row_ok | col_ok, 0.0, -jnp.inf).astype(F32)


def _score(h2, wq_b, keys_b):
    t, d = h2.shape
    tm = min(SCORE_TILE, t)
    neg = _candidate_mask()
    big = lambda: pl.BlockSpec((PEER_HEADS, N_KEYS, tm), lambda j: (0, 0, j))
    big_shape = lambda dt: jax.ShapeDtypeStruct((PEER_HEADS, N_KEYS, t), dt)
    return pl.pallas_call(
        _score_kernel,
        grid=(t // tm,),
        in_specs=[pl.BlockSpec((tm, d), lambda j: (j, 0)),
                  pl.BlockSpec(wq_b.shape, lambda j: (0, 0)),
                  pl.BlockSpec(keys_b.shape, lambda j: (0, 0, 0)),
                  pl.BlockSpec(neg.shape, lambda j: (0, 0, 0))],
        out_specs=[big(), big(), big(), big()],
        out_shape=[big_shape(F32), big_shape(F32), big_shape(BF16), big_shape(BF16)],
        scratch_shapes=[pltpu.VMEM((tm, PEER_HEADS * PEER_KEY_DIM), BF16)],
        compiler_params=_params(("parallel",)),
        name="peer_score",
    )(h2, wq_b, keys_b, neg)


def _key_row_bf16(ref, h, i1):
    rows = jnp.broadcast_to(ref[h, pl.ds(i1, 1), :], (2 * SUBLANES, ref.shape[-1])).astype(BF16)
    return pltpu.repeat(rows, N_KEYS // rows.shape[0], axis=0)


def _peer_kernel(ht_ref, u_ref, vt_ref, c1_ref, e1_ref, r2_ref, e2_ref, x1_ref, mod_ref, fnw_ref,
                 o_ref, acc_ref):
    eb = pl.program_id(1)
    nsub = PEER_EBLK // PEER_SUB
    keys_per_sub = PEER_SUB // N_KEYS
    keys_per_step = PEER_EBLK // N_KEYS
    key0 = pl.multiple_of(eb * keys_per_step, keys_per_step)

    @pl.when(eb == 0)
    def _():
        acc_ref[...] = jnp.zeros_like(acc_ref)

    def first_matmul(sub):
        return jnp.dot(u_ref[sub * PEER_SUB:(sub + 1) * PEER_SUB, :], ht_ref[...],
                       preferred_element_type=F32)

    def gate_weighted(sub, pre):
        act = pre * (1.0 + lax.erf(pre * (2.0 ** -0.5)))
        gates = []
        for kk in range(keys_per_sub):
            i1 = key0 + (sub * keys_per_sub + kk)
            g = jnp.zeros((N_KEYS, ht_ref.shape[1]), BF16)
            for h in range(PEER_HEADS):
                c1 = _key_row_bf16(c1_ref, h, i1)
                e1 = _key_row_bf16(e1_ref, h, i1)
                g = g + jnp.where(r2_ref[h] < c1, e2_ref[h], jnp.zeros((), BF16)) * e1
            gates.append(g)
        return act.astype(BF16) * jnp.concatenate(gates, axis=0)

    acc = acc_ref[...]
    pre = first_matmul(0)
    for sub in range(nsub):
        nxt = first_matmul(sub + 1) if sub + 1 < nsub else None
        weighted = gate_weighted(sub, pre)
        acc = jnp.dot(vt_ref[:, sub * PEER_SUB:(sub + 1) * PEER_SUB], weighted,
                      preferred_element_type=F32) + acc
        pre = nxt
    acc_ref[...] = acc

    @pl.when(eb == pl.num_programs(1) - 1)
    def _():
        x2 = x1_ref[...] + mod_ref[0, 5:6, :] * acc_ref[...].T
        o_ref[...] = x2 * lax.rsqrt(jnp.mean(x2 * x2, axis=-1, keepdims=True) + EPS) * fnw_ref[...]


def _peer(h2t, u_b, vt_b, c1, e1, r2, e2, x1, mod, fn_w, seq):
    d, t = h2t.shape
    tm = min(PEER_TILE, seq)
    n_exp = u_b.shape[0]
    big = lambda: pl.BlockSpec((PEER_HEADS, N_KEYS, tm), lambda j, e: (0, 0, j))
    return pl.pallas_call(
        _peer_kernel,
        grid=(t // tm, n_exp // PEER_EBLK),
        in_specs=[pl.BlockSpec((d, tm), lambda j, e: (0, j)),
                  pl.BlockSpec((PEER_EBLK, d), lambda j, e: (e, 0)),
                  pl.BlockSpec((None, d, PEER_EBLK), lambda j, e: (e, 0, 0)),
                  big(), big(), big(), big(),
                  pl.BlockSpec((tm, d), lambda j, e: (j, 0)),
                  pl.BlockSpec((1, 6, d), lambda j, e: ((j * tm) // seq, 0, 0)),
                  pl.BlockSpec((1, d), lambda j, e: (0, 0))],
        out_specs=pl.BlockSpec((tm, d), lambda j, e: (j, 0)),
        out_shape=jax.ShapeDtypeStruct((t, d), F32),
        scratch_shapes=[pltpu.VMEM((d, tm), F32)],
        compiler_params=_params(("parallel", "arbitrary")),
        name="peer_dense",
    )(h2t, u_b, vt_b, c1, e1, r2, e2, x1, mod, fn_w)


def _pad_lanes(v, width=LANES):
    flat = v.reshape(1, -1)
    return jnp.pad(flat, ((0, 0), (0, width - flat.shape[1])))


def _head_expand(col0):
    rows = lax.broadcasted_iota(jnp.int32, (LANES, D_SSM), 0)
    heads = lax.broadcasted_iota(jnp.int32, (LANES, D_SSM), 1) // SSM_HEADDIM
    return (rows == heads + col0).astype(BF16)


def kernel(x, c, ctx, c_ctx, ada_w, ada_b, norm1_w, norm2_w, w_in, ssm_conv_w, ssm_conv_b, ssm_dt_bias,
           ssm_a_log, ssm_d, ssm_norm_w, cfm_conv_w, cfm_conv_b, cfm_ln_w, cfm_ln_b, w_out, peer_wq,
           peer_subkeys, peer_u, peer_v, final_norm_w):
    depth = ada_w.shape[0]
    assert depth == 1, "single-layer configuration"
    b, l, d = x.shape
    i = 0

    c_rows = jnp.concatenate([c, c_ctx[None], jnp.zeros((2 * SUBLANES - b - 1, d), F32)], axis=0)
    mod_all = _ada(c_rows, ada_w[i], ada_b[i])
    mod = mod_all[:b].reshape(b, 6, d)
    mod_ctx = mod_all[b:b + 1].reshape(1, 6, d)

    wi = w_in[i]
    o_dt = D_SSM + XBC_DIM
    w_r = jnp.concatenate([wi[:, :o_dt], wi[:, o_dt + 2 * SSM_HEADS:], wi[:, o_dt:o_dt + 2 * SSM_HEADS],
                           jnp.zeros((d, LANES - 2 * SSM_HEADS), F32)], axis=1).astype(BF16)
    z_l, xbc_l, u_l, dt_l = _inproj(x, mod, True, norm1_w[i], w_r)
    _, xbc_c, _, dt_c = _inproj(ctx, mod_ctx, False, norm1_w[i], w_r)

    conv_w = jnp.pad(ssm_conv_w[i], ((0, SUBLANES - SSM_CONV), (0, 0)))
    conv_b = ssm_conv_b[i].reshape(1, XBC_DIM)
    dt_bias = _pad_lanes(ssm_dt_bias[i])
    a_log = _pad_lanes(ssm_a_log[i])
    d_skip = jnp.repeat(ssm_d[i], SSM_HEADDIM).reshape(1, D_SSM)
    zero_state = jnp.zeros((b, SSM_STATE, D_SSM), F32)
    y_dirs = []
    act_c = act_l = None
    for reverse in (False, True):
        expand = _head_expand(SSM_HEADS if reverse else 0)
        args = (conv_w, conv_b, dt_bias, a_log, d_skip, expand, reverse)
        if not reverse:
            _, h_ctx, act_c = _ssd(xbc_c, dt_c, zero_state, *args, add_skip=False, activated=False)
            y_dir, _, act_l = _ssd(xbc_l, dt_l, h_ctx, *args, add_skip=True, activated=False)
        else:
            _, h_ctx = _ssd(act_c, dt_c, zero_state, *args, add_skip=False, activated=True)
            y_dir, _ = _ssd(act_l, dt_l, h_ctx, *args, add_skip=False, activated=True)
        y_dirs.append(y_dir)

    cfm = _cfm(u_l, jnp.pad(cfm_conv_w[i], ((0, 1), (0, 0))), cfm_conv_b[i].reshape(1, D_CONV),
               cfm_ln_w[i].reshape(1, D_CONV), cfm_ln_b[i].reshape(1, D_CONV))

    x1, h2, h2t = _outproj(y_dirs[0], y_dirs[1], z_l, cfm, x, mod, ssm_norm_w[i].reshape(1, D_SSM),
                      w_out[i].astype(BF16), norm2_w[i].reshape(1, d))

    t = b * l
    h2 = h2.reshape(t, d)
    keys_b = peer_subkeys[i].reshape(PEER_HEADS * 2, N_KEYS, PEER_KEY_DIM // 2).astype(BF16)
    c1, e1, r2, e2 = _score(h2, peer_wq[i].astype(BF16), keys_b)
    vt_b = peer_v[i].reshape(-1, PEER_EBLK, d).transpose(0, 2, 1).astype(BF16)
    out = _peer(h2t, peer_u[i].astype(BF16), vt_b, c1, e1, r2, e2,
                x1.reshape(t, d), mod, final_norm_w.reshape(1, d), l)
    return out.reshape(b, l, d)
```

```python
import functools

import jax
import jax.numpy as jnp
from jax import lax
from jax.experimental import pallas as pl
from jax.experimental.pallas import tpu as pltpu

F32 = jnp.float32
BF16 = jnp.bfloat16

D_MODEL = 1024
GRID_W = 64
D_SSM = 512
SSM_HEADDIM = 64
SSM_HEADS = 8
SSM_STATE = 128
SSM_GROUPS = 2
SSM_CONV = 5
SSD_CHUNK = 128
XBC_DIM = D_SSM + 2 * SSM_GROUPS * SSM_STATE
D_CONV = 512
CONV_WIDTH = 31
PEER_HEADS = 8
PEER_KEY_DIM = 256
N_KEYS = 128
PEER_TOPK = 16
EPS = 1e-6

LANES = 128
SUBLANES = 8
VMEM_LIMIT = 56 * 1024 * 1024

IN_TILE = 512
SSD_ROWS = 2
CFM_ROWS = 16
OUT_TILE = 512
SCORE_TILE = 512
PEER_TILE = 512
PEER_EBLK = 2048
PEER_SUB = 256
PEER_SECOND = PEER_EBLK


def _params(sem):
    return pltpu.CompilerParams(dimension_semantics=sem, vmem_limit_bytes=VMEM_LIMIT)


def _silu(v):
    return v * jax.nn.sigmoid(v)


def _split3(v):
    hi = v.astype(BF16)
    r1 = v - hi.astype(F32)
    mid = r1.astype(BF16)
    lo = (r1 - mid.astype(F32)).astype(BF16)
    return hi, mid, lo


def _dot01_right(v, m01):
    return sum(jnp.dot(p, m01, preferred_element_type=F32) for p in _split3(v))


def _dot01_left(m01, v):
    return sum(jnp.dot(m01, p, preferred_element_type=F32) for p in _split3(v))


def _ada_kernel(c_ref, w_ref, b_ref, o_ref):
    sc = _silu(c_ref[...])
    o_ref[...] = jnp.dot(sc.astype(BF16), w_ref[...].astype(BF16),
                         preferred_element_type=F32) + b_ref[...]


def _ada(c_rows, ada_w, ada_b):
    n = ada_w.shape[1]
    tn = 768
    return pl.pallas_call(
        _ada_kernel,
        grid=(n // tn,),
        in_specs=[pl.BlockSpec(c_rows.shape, lambda j: (0, 0)),
                  pl.BlockSpec((D_MODEL, tn), lambda j: (0, j)),
                  pl.BlockSpec((1, tn), lambda j: (0, j))],
        out_specs=pl.BlockSpec((c_rows.shape[0], tn), lambda j: (0, j)),
        out_shape=jax.ShapeDtypeStruct((c_rows.shape[0], n), F32),
        compiler_params=_params(("parallel",)),
        name="ada",
    )(c_rows, ada_w, ada_b.reshape(1, n))


def _inproj_kernel(x_ref, mod_ref, nw_ref, w_ref, z_ref, xbc_ref, u_ref, dt_ref):
    xv = x_ref[0]
    h = xv * lax.rsqrt(jnp.mean(xv * xv, axis=-1, keepdims=True) + EPS) * nw_ref[...]
    h = h * (1.0 + mod_ref[0, 1:2, :]) + mod_ref[0, 0:1, :]
    hb = h.astype(BF16)

    def proj(lo, hi):
        return jnp.dot(hb, w_ref[:, lo:hi], preferred_element_type=F32)

    z_ref[0] = proj(0, D_SSM)
    xbc_ref[0] = proj(D_SSM, D_SSM + XBC_DIM)
    o = D_SSM + XBC_DIM
    u_ref[0] = proj(o, o + D_CONV) * jax.nn.sigmoid(proj(o + D_CONV, o + 2 * D_CONV))
    dt_ref[0] = proj(o + 2 * D_CONV, o + 2 * D_CONV + LANES)


def _inproj(x, mod, per_batch_mod, norm_w, w_r):
    b, l, d = x.shape
    tm = min(IN_TILE, l)
    mod_idx = (lambda i, j: (i, 0, 0)) if per_batch_mod else (lambda i, j: (0, 0, 0))
    tok = lambda w: pl.BlockSpec((1, tm, w), lambda i, j: (i, j, 0))
    return pl.pallas_call(
        _inproj_kernel,
        grid=(b, l // tm),
        in_specs=[tok(d),
                  pl.BlockSpec((1, 6, d), mod_idx),
                  pl.BlockSpec((1, d), lambda i, j: (0, 0)),
                  pl.BlockSpec(w_r.shape, lambda i, j: (0, 0))],
        out_specs=[tok(D_SSM), tok(XBC_DIM), tok(D_CONV), tok(LANES)],
        out_shape=[jax.ShapeDtypeStruct((b, l, w), F32) for w in (D_SSM, XBC_DIM, D_CONV, LANES)],
        compiler_params=_params(("parallel", "parallel")),
        name="inproj",
    )(x, mod, norm_w.reshape(1, d), w_r)


def _ssd_kernel(cur_ref, prev_ref, next_ref, dt_ref, h0_ref, cw_ref, cb_ref, dtb_ref, alog_ref,
                dskip_ref, expand_ref, y_ref, hfin_ref, *rest, reverse, nc, add_skip, activated):
    act_ref, ext_ref, state_ref = (None, *rest) if activated else rest
    q = SSD_CHUNK
    halo = SUBLANES
    pad = (SSM_CONV - 1) // 2
    gw = D_SSM // SSM_GROUPS
    hpg = SSM_HEADS // SSM_GROUPS
    col0 = SSM_HEADS if reverse else 0
    c = pl.program_id(1)
    cc = (nc - 1 - c) if reverse else c

    @pl.when(c == 0)
    def _():
        state_ref[...] = h0_ref[...]

    ri = lax.broadcasted_iota(jnp.int32, (q, q), 0)
    ci = lax.broadcasted_iota(jnp.int32, (q, q), 1)
    causal = (ci >= ri) if reverse else (ci <= ri)
    causal01 = jnp.where(causal, 1.0, 0.0).astype(BF16)
    a_row = -jnp.exp(alog_ref[...])

    for r in range(cur_ref.shape[0]):
        if activated:
            act = cur_ref[r]
        else:
            ext_ref[r, 0:halo, :] = jnp.where(cc == 0, 0.0, prev_ref[r])
            ext_ref[r, halo:halo + q, :] = cur_ref[r]
            ext_ref[r, halo + q:2 * halo + q, :] = jnp.where(cc == nc - 1, 0.0, next_ref[r])
            conv = cb_ref[...] + cw_ref[0:1, :] * ext_ref[r, halo - pad:halo - pad + q, :]
            for k in range(1, SSM_CONV):
                conv = conv + cw_ref[k:k + 1, :] * ext_ref[r, halo - pad + k:halo - pad + k + q, :]
            act = _silu(conv)
            act_ref[r] = act

        dtv = dt_ref[r] + dtb_ref[...]
        dt = jnp.maximum(dtv, 0.0) + jnp.log(1.0 + jnp.exp(-jnp.abs(dtv)))
        acum = _dot01_left(causal01, dt * a_row)
        acum_t = acum.T
        tot = acum[0:1, :] if reverse else acum[q - 1:q, :]
        stacked = jnp.concatenate([dt, jnp.exp(tot - acum), jnp.exp(acum),
                                   jnp.broadcast_to(jnp.exp(tot), (SUBLANES, LANES))], axis=0)
        spread = _dot01_right(stacked, expand_ref[...])
        dt_e, dte_e, ea_e = spread[0:q], spread[q:2 * q], spread[2 * q:3 * q]
        etot_e = spread[3 * q:3 * q + 1]

        xs = act[:, :D_SSM]
        xdt = xs * dt_e
        xdt_b = xdt.astype(BF16)
        w_b = (xdt * dte_e).astype(BF16)
        for g in range(SSM_GROUPS):
            gs = slice(g * gw, (g + 1) * gw)
            b_g = act[:, D_SSM + g * SSM_STATE:D_SSM + (g + 1) * SSM_STATE]
            c_g = act[:, D_SSM + (SSM_GROUPS + g) * SSM_STATE:D_SSM + (SSM_GROUPS + g + 1) * SSM_STATE]
            c_b = c_g.astype(BF16)
            bt_b = b_g.T.astype(BF16)
            cb = jnp.dot(c_b, bt_b, preferred_element_type=F32)
            st = state_ref[r, :, gs]
            y_g = jnp.dot(c_b, st.astype(BF16), preferred_element_type=F32) * ea_e[:, gs]
            y_heads = []
            for hh in range(hpg):
                h = g * hpg + hh
                col = col0 + h
                diff = acum[:, col:col + 1] - acum_t[col:col + 1, :]
                decay = jnp.where(causal, jnp.exp(jnp.minimum(diff, 0.0)), 0.0)
                y_heads.append(jnp.dot((cb * decay).astype(BF16),
                                       xdt_b[:, h * SSM_HEADDIM:(h + 1) * SSM_HEADDIM],
                                       preferred_element_type=F32))
            y_g = y_g + jnp.concatenate(y_heads, axis=-1)
            if add_skip:
                y_g = y_g + dskip_ref[:, gs] * xs[:, gs]
            y_ref[r, :, gs] = y_g
            state_ref[r, :, gs] = st * etot_e[:, gs] + jnp.dot(bt_b, w_b[:, gs], preferred_element_type=F32)
    hfin_ref[...] = state_ref[...]


def _ssd(xbc, dtp, h0, conv_w, conv_b, dt_bias, a_log, d_skip, expand, reverse, add_skip, activated):
    b, l, _ = xbc.shape
    q = SSD_CHUNK
    nc = l // q
    nr = SSD_ROWS if b % SSD_ROWS == 0 else 1
    hb = q // SUBLANES
    pos = (lambda c: nc - 1 - c) if reverse else (lambda c: c)
    const = lambda shape: pl.BlockSpec(shape, lambda i, c: (0,) * len(shape))
    kern = functools.partial(_ssd_kernel, reverse=reverse, nc=nc, add_skip=add_skip, activated=activated)
    chunk = lambda w: pl.BlockSpec((nr, q, w), lambda i, c: (i, pos(c), 0))
    act_spec = [] if activated else [chunk(XBC_DIM)]
    act_shape = [] if activated else [jax.ShapeDtypeStruct((b, l, XBC_DIM), F32)]
    return pl.pallas_call(
        kern,
        grid=(b // nr, nc),
        in_specs=[pl.BlockSpec((nr, q, XBC_DIM), lambda i, c: (i, pos(c), 0)),
                  pl.BlockSpec((nr, SUBLANES, XBC_DIM), lambda i, c: (i, jnp.maximum(pos(c) * hb - 1, 0), 0)),
                  pl.BlockSpec((nr, SUBLANES, XBC_DIM),
                               lambda i, c: (i, jnp.minimum((pos(c) + 1) * hb, l // SUBLANES - 1), 0)),
                  pl.BlockSpec((nr, q, LANES), lambda i, c: (i, pos(c), 0)),
                  pl.BlockSpec((nr, SSM_STATE, D_SSM), lambda i, c: (i, 0, 0)),
                  const(conv_w.shape), const(conv_b.shape), const(dt_bias.shape), const(a_log.shape),
                  const(d_skip.shape), const(expand.shape)],
        out_specs=[chunk(D_SSM), pl.BlockSpec((nr, SSM_STATE, D_SSM), lambda i, c: (i, 0, 0))] + act_spec,
        out_shape=[jax.ShapeDtypeStruct((b, l, D_SSM), F32),
                   jax.ShapeDtypeStruct((b, SSM_STATE, D_SSM), F32)] + act_shape,
        scratch_shapes=[pltpu.VMEM((nr, q + 2 * SUBLANES, XBC_DIM), F32),
                        pltpu.VMEM((nr, SSM_STATE, D_SSM), F32)],
        compiler_params=_params(("parallel", "arbitrary")),
        name="ssd_bwd" if reverse else "ssd_fwd",
    )(xbc, xbc, xbc, dtp, h0, conv_w, conv_b, dt_bias, a_log, d_skip, expand)


def _cfm_kernel(cur_ref, prev_ref, next_ref, w_ref, b_ref, lnw_ref, lnb_ref, o_ref, hpad_ref, vbuf_ref,
                shift_ref, *, nb):
    rows = CFM_ROWS
    band = rows * GRID_W
    half = D_CONV // 2
    pad = (CONV_WIDTH - 1) // 2
    lead = 2 * SUBLANES
    r = pl.program_id(1)

    zeros = jnp.zeros((rows, lead, half), F32)
    hpad_ref[:, 0:lead, :] = zeros
    hpad_ref[:, lead + GRID_W:2 * lead + GRID_W, :] = zeros
    hpad_ref[:, lead:lead + GRID_W, :] = cur_ref[0, :, 0:half].reshape(rows, GRID_W, half)
    vbuf_ref[0:band, :] = jnp.where(r == 0, 0.0, prev_ref[0])
    vbuf_ref[band:2 * band, :] = cur_ref[0, :, half:D_CONV]
    vbuf_ref[2 * band:3 * band, :] = jnp.where(r == nb - 1, 0.0, next_ref[0])

    span = shift_ref.shape[1]

    def row_body(i, carry):
        for s in range(SUBLANES):
            shift_ref[s] = hpad_ref[i, s:s + span, :]
        acc_h = jnp.zeros((GRID_W, half), F32)
        acc_v = jnp.zeros((GRID_W, half), F32)
        for k in range(CONV_WIDTH):
            first = lead - pad + k
            tile0 = first // SUBLANES * SUBLANES
            acc_h = acc_h + w_ref[k:k + 1, 0:half] * shift_ref[first % SUBLANES, tile0:tile0 + GRID_W, :]
            start = pl.multiple_of(band + (i + k - pad) * GRID_W, GRID_W)
            acc_v = acc_v + w_ref[k:k + 1, half:D_CONV] * vbuf_ref[pl.ds(start, GRID_W), :]
        conv = jnp.concatenate([acc_h, acc_v], axis=-1) + b_ref[...]
        mu = jnp.mean(conv, axis=-1, keepdims=True)
        cen = conv - mu
        var = jnp.mean(cen * cen, axis=-1, keepdims=True)
        o_ref[0, pl.ds(pl.multiple_of(i * GRID_W, GRID_W), GRID_W), :] = _silu(
            cen * lax.rsqrt(var + EPS) * lnw_ref[...] + lnb_ref[...])
        return carry

    lax.fori_loop(0, rows, row_body, 0)


def _cfm(u, conv_w, conv_b, ln_w, ln_b):
    b, l, _ = u.shape
    band = CFM_ROWS * GRID_W
    nb = l // band
    half = D_CONV // 2
    const = lambda shape: pl.BlockSpec(shape, lambda i, r: (0,) * len(shape))
    return pl.pallas_call(
        functools.partial(_cfm_kernel, nb=nb),
        grid=(b, nb),
        in_specs=[pl.BlockSpec((1, band, D_CONV), lambda i, r: (i, r, 0)),
                  pl.BlockSpec((1, band, half), lambda i, r: (i, jnp.maximum(r - 1, 0), 1)),
                  pl.BlockSpec((1, band, half), lambda i, r: (i, jnp.minimum(r + 1, nb - 1), 1)),
                  const(conv_w.shape), const(conv_b.shape), const(ln_w.shape), const(ln_b.shape)],
        out_specs=pl.BlockSpec((1, band, D_CONV), lambda i, r: (i, r, 0)),
        out_shape=jax.ShapeDtypeStruct((b, l, D_CONV), F32),
        scratch_shapes=[pltpu.VMEM((CFM_ROWS, GRID_W + 4 * SUBLANES, half), F32),
                        pltpu.VMEM((3 * band, half), F32),
                        pltpu.VMEM((SUBLANES, GRID_W + 3 * SUBLANES, half), F32)],
        compiler_params=_params(("parallel", "parallel")),
        name="cfm",
    )(u, u, u, conv_w, conv_b, ln_w, ln_b)


def _outproj_kernel(yf_ref, yb_ref, z_ref, cfm_ref, x_ref, mod_ref, gnw_ref, w_ref, n2w_ref,
                    x1_ref, h2_ref, h2t_ref):
    gw = D_SSM // SSM_GROUPS
    g = (yf_ref[0] + yb_ref[0]) * _silu(z_ref[0])
    parts = []
    for k in range(SSM_GROUPS):
        gk = g[:, k * gw:(k + 1) * gw]
        parts.append(gk * lax.rsqrt(jnp.mean(gk * gk, axis=-1, keepdims=True) + EPS))
    ssm = jnp.concatenate(parts, axis=-1) * gnw_ref[...]
    mixed = (jnp.dot(ssm.astype(BF16), w_ref[0:D_SSM, :], preferred_element_type=F32)
             + jnp.dot(cfm_ref[0].astype(BF16), w_ref[D_SSM:, :], preferred_element_type=F32))
    x1 = x_ref[0] + mod_ref[0, 2:3, :] * mixed
    x1_ref[0] = x1
    h = x1 * lax.rsqrt(jnp.mean(x1 * x1, axis=-1, keepdims=True) + EPS) * n2w_ref[...]
    h2 = h * (1.0 + mod_ref[0, 4:5, :]) + mod_ref[0, 3:4, :]
    h2_ref[0] = h2.astype(BF16)
    h2t_ref[...] = h2.T.astype(BF16)


def _outproj(yf, yb, z, cfm, x, mod, gn_w, w_out_b, n2_w):
    b, l, d = x.shape
    tm = min(OUT_TILE, l)
    tok = lambda w: pl.BlockSpec((1, tm, w), lambda i, j: (i, j, 0))
    const = lambda shape: pl.BlockSpec(shape, lambda i, j: (0,) * len(shape))
    return pl.pallas_call(
        _outproj_kernel,
        grid=(b, l // tm),
        in_specs=[tok(D_SSM), tok(D_SSM), tok(D_SSM), tok(D_CONV), tok(d),
                  pl.BlockSpec((1, 6, d), lambda i, j: (i, 0, 0)),
                  const(gn_w.shape), const(w_out_b.shape), const(n2_w.shape)],
        out_specs=[tok(d), tok(d), pl.BlockSpec((d, tm), lambda i, j: (0, i * (l // tm) + j))],
        out_shape=[jax.ShapeDtypeStruct((b, l, d), F32), jax.ShapeDtypeStruct((b, l, d), BF16),
                   jax.ShapeDtypeStruct((d, b * l), BF16)],
        compiler_params=_params(("parallel", "parallel")),
        name="outproj",
    )(yf, yb, z, cfm, x, mod, gn_w, w_out_b, n2_w)


def _oddeven_merge(lo, hi, r):
    step = r * 2
    if step < hi - lo:
        yield from _oddeven_merge(lo, hi, step)
        yield from _oddeven_merge(lo + r, hi, step)
        yield from [(i, i + r) for i in range(lo + r, hi - r, step)]
    else:
        yield (lo, lo + r)


def _oddeven_sort(lo, hi):
    if hi > lo:
        mid = lo + (hi - lo) // 2
        yield from _oddeven_sort(lo, mid)
        yield from _oddeven_sort(mid + 1, hi)
        yield from _oddeven_merge(lo, hi, 1)


SORT_NET = tuple(_oddeven_sort(0, PEER_TOPK - 1))


def _cmp_exchange(v, i, j):
    v[i], v[j] = jnp.maximum(v[i], v[j]), jnp.minimum(v[i], v[j])


def _bitonic_sort(v):
    d = len(v) // 2
    while d:
        for i in range(len(v)):
            if not i & d:
                _cmp_exchange(v, i, i + d)
        d //= 2


def _merge_top(v, w):
    n = len(v)
    return [jnp.maximum(v[i], w[n - 1 - i]) for i in range(n)]


def _sublane_rolls():
    shift = SUBLANES // 2
    while shift:
        yield shift
        shift //= 2


def _top_sorted(v):
    for shift in _sublane_rolls():
        v = _merge_top(v, [pltpu.roll(x, shift, 0) for x in v])
        _bitonic_sort(v)
    return v


def _score_kernel(h_ref, wq_ref, keys_ref, neg_ref, c1_ref, e1_ref, r2_ref, e2_ref, q_ref):
    tm = h_ref.shape[0]
    nsub = tm // LANES
    kd = PEER_KEY_DIM // 2
    k = PEER_TOPK
    ntile = N_KEYS // SUBLANES
    q_ref[...] = jnp.dot(h_ref[...], wq_ref[...], preferred_element_type=F32).astype(BF16)

    def body(it, carry):
        h = it // nsub
        tok = pl.multiple_of((it % nsub) * LANES, LANES)
        nt = (((1,), (1,)), ((), ()))

        def score_tiles(p):
            qs = q_ref[pl.ds(tok, LANES), pl.ds(pl.multiple_of((2 * h + p) * kd, kd), kd)]
            s = lax.dot_general(keys_ref[2 * h + p], qs, nt, preferred_element_type=F32)
            return [s[i * SUBLANES:(i + 1) * SUBLANES, :] for i in range(ntile)]

        def top(tiles):
            v = list(tiles)
            for i, j in SORT_NET:
                _cmp_exchange(v, i, j)
            return _top_sorted(v)

        s1 = score_tiles(0)
        s2 = score_tiles(1)
        a = top(s1)
        b = top(s2)

        sub = lax.broadcasted_iota(jnp.int32, (SUBLANES, LANES), 0)
        pick = lambda lo, x, y: jnp.where(sub < lo, x, y)
        base = pick(4, pick(2, pick(1, a[0], a[1]), pick(3, a[2], a[3])), pick(5, b[0], pick(6, b[1], b[2])))
        cand = []
        for m in range(k):
            other = pick(4, b[m], a[4 + m]) if 4 + m < k else b[m]
            cand.append((other + base) + neg_ref[m])
        lists = list(cand)
        rolls = list(_sublane_rolls())
        for shift in rolls[:-1]:
            lists = _merge_top(lists, [pltpu.roll(x, shift, 0) for x in lists])
            _bitonic_sort(lists)
        lists = _merge_top(lists, [pltpu.roll(x, rolls[-1], 0) for x in lists])
        tau = functools.reduce(jnp.minimum, lists)

        best = a[0] + b[0]
        zsum = sum(jnp.where(c >= tau, jnp.exp(c - best), 0.0) for c in cand)
        for shift in rolls:
            zsum = zsum + pltpu.roll(zsum, shift, 0)
        scale = 0.5 / zsum
        top_extra = sum(jnp.where(a[0] + b[j] >= tau, 1.0, 0.0) for j in range(k // 2, k))

        c1_t, e1_t, r2_t, e2_t = [], [], [], []
        for t1, t2 in zip(s1, s2):
            c1 = jnp.zeros_like(t1)
            for j in range(k // 2):
                c1 = jnp.where(t1 + b[j] >= tau, float(j + 1), c1)
            c1_t.append(jnp.where(t1 == a[0], c1 + top_extra, c1))
            e1_t.append(jnp.exp(t1 - a[0]) * scale)
            r2 = jnp.zeros_like(t2)
            for j in range(k):
                r2 = jnp.where(t2 < b[j], float(j + 1), r2)
            r2_t.append(r2)
            e2_t.append(jnp.exp(t2 - b[0]))
        c1_ref[h, :, pl.ds(tok, LANES)] = jnp.concatenate(c1_t, axis=0)
        e1_ref[h, :, pl.ds(tok, LANES)] = jnp.concatenate(e1_t, axis=0)
        r2_ref[h, :, pl.ds(tok, LANES)] = jnp.concatenate(r2_t, axis=0).astype(BF16)
        e2_ref[h, :, pl.ds(tok, LANES)] = jnp.concatenate(e2_t, axis=0).astype(BF16)
        return carry

    lax.fori_loop(0, PEER_HEADS * nsub, body, 0, unroll=2)


def _candidate_mask():
    k = PEER_TOPK
    m = lax.broadcasted_iota(jnp.int32, (k, SUBLANES, LANES), 0)
    s = lax.broadcasted_iota(jnp.int32, (k, SUBLANES, LANES), 1)
    row_ok = (s < 4) & ((s + 1) * (m + 1) <= k)
    col_ok = (s >= 4) & (s < 7) & (4 + m < k) & ((5 + m) * (s - 3) <= k)
    return jnp.where(row_ok | col_ok, 0.0, -jnp.inf).astype(F32)


def _score(h2, wq_b, keys_b):
    t, d = h2.shape
    tm = min(SCORE_TILE, t)
    neg = _candidate_mask()
    big = lambda: pl.BlockSpec((PEER_HEADS, N_KEYS, tm), lambda j: (0, 0, j))
    big_shape = lambda dt: jax.ShapeDtypeStruct((PEER_HEADS, N_KEYS, t), dt)
    return pl.pallas_call(
        _score_kernel,
        grid=(t // tm,),
        in_specs=[pl.BlockSpec((tm, d), lambda j: (j, 0)),
                  pl.BlockSpec(wq_b.shape, lambda j: (0, 0)),
                  pl.BlockSpec(keys_b.shape, lambda j: (0, 0, 0)),
                  pl.BlockSpec(neg.shape, lambda j: (0, 0, 0))],
        out_specs=[big(), big(), big(), big()],
        out_shape=[big_shape(F32), big_shape(F32), big_shape(BF16), big_shape(BF16)],
        scratch_shapes=[pltpu.VMEM((tm, PEER_HEADS * PEER_KEY_DIM), BF16)],
        compiler_params=_params(("parallel",)),
        name="peer_score",
    )(h2, wq_b, keys_b, neg)


def _key_row_bf16(ref, h, key0, j):
    tile = ref[h, pl.ds(key0 + j // SUBLANES * SUBLANES, SUBLANES), :]
    row = tile[j % SUBLANES:j % SUBLANES + 1, :]
    rows = jnp.broadcast_to(row, (2 * SUBLANES, ref.shape[-1])).astype(BF16)
    return jnp.tile(rows, (N_KEYS // rows.shape[0], 1))


def _peer_kernel(ht_ref, u_ref, vt_ref, c1_ref, e1_ref, r2_ref, e2_ref, x1_ref, mod_ref, fnw_ref,
                 o_ref, acc_ref):
    eb = pl.program_id(1)
    nsub = PEER_EBLK // PEER_SUB
    keys_per_sub = PEER_SUB // N_KEYS
    keys_per_step = PEER_EBLK // N_KEYS
    key0 = pl.multiple_of(eb * keys_per_step, keys_per_step)

    @pl.when(eb == 0)
    def _():
        acc_ref[...] = jnp.zeros_like(acc_ref)

    def first_matmul(sub):
        return jnp.dot(u_ref[sub * PEER_SUB:(sub + 1) * PEER_SUB, :], ht_ref[...],
                       preferred_element_type=F32)

    def build_gates(sub):
        gates = []
        for kk in range(keys_per_sub):
            j = sub * keys_per_sub + kk
            g = jnp.zeros((N_KEYS, ht_ref.shape[1]), BF16)
            for h in range(PEER_HEADS):
                c1 = _key_row_bf16(c1_ref, h, key0, j)
                e1 = _key_row_bf16(e1_ref, h, key0, j)
                g = g + jnp.where(r2_ref[h] < c1, e2_ref[h], jnp.zeros((), BF16)) * e1
            gates.append(g)
        return jnp.concatenate(gates, axis=0)

    group = PEER_SECOND // PEER_SUB
    acc = acc_ref[...]
    pre = first_matmul(0)
    pending = []
    for sub in range(nsub):
        nxt = first_matmul(sub + 1) if sub + 1 < nsub else None
        act = pre * (1.0 + lax.erf(pre * (2.0 ** -0.5)))
        pending.append(act.astype(BF16) * build_gates(sub))
        if len(pending) == group:
            lo = (sub + 1 - group) * PEER_SUB
            acc = jnp.dot(vt_ref[:, lo:lo + PEER_SECOND], jnp.concatenate(pending, axis=0),
                          preferred_element_type=F32) + acc
            pending = []
        pre = nxt
    acc_ref[...] = acc

    @pl.when(eb == pl.num_programs(1) - 1)
    def _():
        x2 = x1_ref[...] + mod_ref[0, 5:6, :] * acc_ref[...].T
        o_ref[...] = x2 * lax.rsqrt(jnp.mean(x2 * x2, axis=-1, keepdims=True) + EPS) * fnw_ref[...]


def _peer(h2t, u_b, vt_b, c1, e1, r2, e2, x1, mod, fn_w, seq):
    d, t = h2t.shape
    tm = min(PEER_TILE, seq)
    n_exp = u_b.shape[0]
    big = lambda: pl.BlockSpec((PEER_HEADS, N_KEYS, tm), lambda j, e: (0, 0, j))
    return pl.pallas_call(
        _peer_kernel,
        grid=(t // tm, n_exp // PEER_EBLK),
        in_specs=[pl.BlockSpec((d, tm), lambda j, e: (0, j)),
                  pl.BlockSpec((PEER_EBLK, d), lambda j, e: (e, 0)),
                  pl.BlockSpec((None, d, PEER_EBLK), lambda j, e: (e, 0, 0)),
                  big(), big(), big(), big(),
                  pl.BlockSpec((tm, d), lambda j, e: (j, 0)),
                  pl.BlockSpec((1, 6, d), lambda j, e: ((j * tm) // seq, 0, 0)),
                  pl.BlockSpec((1, d), lambda j, e: (0, 0))],
        out_specs=pl.BlockSpec((tm, d), lambda j, e: (j, 0)),
        out_shape=jax.ShapeDtypeStruct((t, d), F32),
        scratch_shapes=[pltpu.VMEM((d, tm), F32)],
        compiler_params=_params(("parallel", "arbitrary")),
        name="peer_dense",
    )(h2t, u_b, vt_b, c1, e1, r2, e2, x1, mod, fn_w)


def _pad_lanes(v, width=LANES):
    flat = v.reshape(1, -1)
    return jnp.pad(flat, ((0, 0), (0, width - flat.shape[1])))


def _head_expand(col0):
    rows = lax.broadcasted_iota(jnp.int32, (LANES, D_SSM), 0)
    heads = lax.broadcasted_iota(jnp.int32, (LANES, D_SSM), 1) // SSM_HEADDIM
    return (rows == heads + col0).astype(BF16)


def kernel(x, c, ctx, c_ctx, ada_w, ada_b, norm1_w, norm2_w, w_in, ssm_conv_w, ssm_conv_b, ssm_dt_bias,
           ssm_a_log, ssm_d, ssm_norm_w, cfm_conv_w, cfm_conv_b, cfm_ln_w, cfm_ln_b, w_out, peer_wq,
           peer_subkeys, peer_u, peer_v, final_norm_w):
    depth = ada_w.shape[0]
    assert depth == 1, "single-layer configuration"
    b, l, d = x.shape
    i = 0

    c_rows = jnp.concatenate([c, c_ctx[None], jnp.zeros((2 * SUBLANES - b - 1, d), F32)], axis=0)
    mod_all = _ada(c_rows, ada_w[i], ada_b[i])
    mod = mod_all[:b].reshape(b, 6, d)
    mod_ctx = mod_all[b:b + 1].reshape(1, 6, d)

    wi = w_in[i]
    o_dt = D_SSM + XBC_DIM
    w_r = jnp.concatenate([wi[:, :o_dt], wi[:, o_dt + 2 * SSM_HEADS:], wi[:, o_dt:o_dt + 2 * SSM_HEADS],
                           jnp.zeros((d, LANES - 2 * SSM_HEADS), F32)], axis=1).astype(BF16)
    z_l, xbc_l, u_l, dt_l = _inproj(x, mod, True, norm1_w[i], w_r)
    _, xbc_c, _, dt_c = _inproj(ctx, mod_ctx, False, norm1_w[i], w_r)

    conv_w = jnp.pad(ssm_conv_w[i], ((0, SUBLANES - SSM_CONV), (0, 0)))
    conv_b = ssm_conv_b[i].reshape(1, XBC_DIM)
    dt_bias = _pad_lanes(ssm_dt_bias[i])
    a_log = _pad_lanes(ssm_a_log[i])
    d_skip = jnp.repeat(ssm_d[i], SSM_HEADDIM).reshape(1, D_SSM)
    zero_state = jnp.zeros((b, SSM_STATE, D_SSM), F32)
    y_dirs = []
    act_c = act_l = None
    for reverse in (False, True):
        expand = _head_expand(SSM_HEADS if reverse else 0)
        args = (conv_w, conv_b, dt_bias, a_log, d_skip, expand, reverse)
        if not reverse:
            _, h_ctx, act_c = _ssd(xbc_c, dt_c, zero_state, *args, add_skip=False, activated=False)
            y_dir, _, act_l = _ssd(xbc_l, dt_l, h_ctx, *args, add_skip=True, activated=False)
        else:
            _, h_ctx = _ssd(act_c, dt_c, zero_state, *args, add_skip=False, activated=True)
            y_dir, _ = _ssd(act_l, dt_l, h_ctx, *args, add_skip=False, activated=True)
        y_dirs.append(y_dir)

    cfm = _cfm(u_l, jnp.pad(cfm_conv_w[i], ((0, 1), (0, 0))), cfm_conv_b[i].reshape(1, D_CONV),
               cfm_ln_w[i].reshape(1, D_CONV), cfm_ln_b[i].reshape(1, D_CONV))

    x1, h2, h2t = _outproj(y_dirs[0], y_dirs[1], z_l, cfm, x, mod, ssm_norm_w[i].reshape(1, D_SSM),
                      w_out[i].astype(BF16), norm2_w[i].reshape(1, d))

    t = b * l
    h2 = h2.reshape(t, d)
    keys_b = peer_subkeys[i].reshape(PEER_HEADS * 2, N_KEYS, PEER_KEY_DIM // 2).astype(BF16)
    c1, e1, r2, e2 = _score(h2, peer_wq[i].astype(BF16), keys_b)
    vt_b = peer_v[i].reshape(-1, PEER_EBLK, d).transpose(0, 2, 1).astype(BF16)
    out = _peer(h2t, peer_u[i].astype(BF16), vt_b, c1, e1, r2, e2,
                x1.reshape(t, d), mod, final_norm_w.reshape(1, d), l)
    return out.reshape(b, l, d)
```

```python
import functools

import jax
import jax.numpy as jnp
from jax import lax
from jax.experimental import pallas as pl
from jax.experimental.pallas import tpu as pltpu

F32 = jnp.float32
BF16 = jnp.bfloat16

D_MODEL = 1024
GRID_W = 64
D_SSM = 512
SSM_HEADDIM = 64
SSM_HEADS = 8
SSM_STATE = 128
SSM_GROUPS = 2
SSM_CONV = 5
SSD_CHUNK = 128
XBC_DIM = D_SSM + 2 * SSM_GROUPS * SSM_STATE
D_CONV = 512
CONV_WIDTH = 31
PEER_HEADS = 8
PEER_KEY_DIM = 256
N_KEYS = 128
PEER_TOPK = 16
EPS = 1e-6

LANES = 128
SUBLANES = 8
VMEM_LIMIT = 56 * 1024 * 1024

IN_TILE = 512
SSD_ROWS = 2
CFM_ROWS = 16
OUT_TILE = 512
SCORE_TILE = 512
PEER_TILE = 512
PEER_EBLK = 2048
PEER_SUB = 256
BF16_ROWS = 2 * SUBLANES


def _params(sem):
    return pltpu.CompilerParams(dimension_semantics=sem, vmem_limit_bytes=VMEM_LIMIT)


def _silu(v):
    return v * jax.nn.sigmoid(v)


def _split3(v):
    hi = v.astype(BF16)
    r1 = v - hi.astype(F32)
    mid = r1.astype(BF16)
    lo = (r1 - mid.astype(F32)).astype(BF16)
    return hi, mid, lo


def _dot01_right(v, m01):
    return sum(jnp.dot(p, m01, preferred_element_type=F32) for p in _split3(v))


def _dot01_left(m01, v):
    return sum(jnp.dot(m01, p, preferred_element_type=F32) for p in _split3(v))


def _ada_kernel(c_ref, w_ref, b_ref, o_ref):
    sc = _silu(c_ref[...])
    o_ref[...] = jnp.dot(sc.astype(BF16), w_ref[...].astype(BF16),
                         preferred_element_type=F32) + b_ref[...]


def _ada(c_rows, ada_w, ada_b):
    n = ada_w.shape[1]
    tn = 768
    return pl.pallas_call(
        _ada_kernel,
        grid=(n // tn,),
        in_specs=[pl.BlockSpec(c_rows.shape, lambda j: (0, 0)),
                  pl.BlockSpec((D_MODEL, tn), lambda j: (0, j)),
                  pl.BlockSpec((1, tn), lambda j: (0, j))],
        out_specs=pl.BlockSpec((c_rows.shape[0], tn), lambda j: (0, j)),
        out_shape=jax.ShapeDtypeStruct((c_rows.shape[0], n), F32),
        compiler_params=_params(("parallel",)),
        name="ada",
    )(c_rows, ada_w, ada_b.reshape(1, n))


def _inproj_kernel(x_ref, mod_ref, nw_ref, w_ref, z_ref, xbc_ref, u_ref, dt_ref):
    xv = x_ref[0]
    h = xv * lax.rsqrt(jnp.mean(xv * xv, axis=-1, keepdims=True) + EPS) * nw_ref[...]
    h = h * (1.0 + mod_ref[0, 1:2, :]) + mod_ref[0, 0:1, :]
    hb = h.astype(BF16)

    def proj(lo, hi):
        return jnp.dot(hb, w_ref[:, lo:hi], preferred_element_type=F32)

    z_ref[0] = proj(0, D_SSM)
    xbc_ref[0] = proj(D_SSM, D_SSM + XBC_DIM)
    o = D_SSM + XBC_DIM
    u_ref[0] = proj(o, o + D_CONV) * jax.nn.sigmoid(proj(o + D_CONV, o + 2 * D_CONV))
    dt_ref[0] = proj(o + 2 * D_CONV, o + 2 * D_CONV + LANES)


def _inproj(x, mod, per_batch_mod, norm_w, w_r):
    b, l, d = x.shape
    tm = min(IN_TILE, l)
    mod_idx = (lambda i, j: (i, 0, 0)) if per_batch_mod else (lambda i, j: (0, 0, 0))
    tok = lambda w: pl.BlockSpec((1, tm, w), lambda i, j: (i, j, 0))
    return pl.pallas_call(
        _inproj_kernel,
        grid=(b, l // tm),
        in_specs=[tok(d),
                  pl.BlockSpec((1, 6, d), mod_idx),
                  pl.BlockSpec((1, d), lambda i, j: (0, 0)),
                  pl.BlockSpec(w_r.shape, lambda i, j: (0, 0))],
        out_specs=[tok(D_SSM), tok(XBC_DIM), tok(D_CONV), tok(LANES)],
        out_shape=[jax.ShapeDtypeStruct((b, l, w), F32) for w in (D_SSM, XBC_DIM, D_CONV, LANES)],
        compiler_params=_params(("parallel", "parallel")),
        name="inproj",
    )(x, mod, norm_w.reshape(1, d), w_r)


def _ssd_kernel(cur_ref, prev_ref, next_ref, dt_ref, h0_ref, cw_ref, cb_ref, dtb_ref, alog_ref,
                dskip_ref, expand_ref, y_ref, hfin_ref, *rest, reverse, nc, add_skip, activated):
    act_ref, ext_ref, state_ref = (None, *rest) if activated else rest
    q = SSD_CHUNK
    halo = SUBLANES
    pad = (SSM_CONV - 1) // 2
    gw = D_SSM // SSM_GROUPS
    hpg = SSM_HEADS // SSM_GROUPS
    col0 = SSM_HEADS if reverse else 0
    c = pl.program_id(1)
    cc = (nc - 1 - c) if reverse else c

    @pl.when(c == 0)
    def _():
        state_ref[...] = h0_ref[...]

    ri = lax.broadcasted_iota(jnp.int32, (q, q), 0)
    ci = lax.broadcasted_iota(jnp.int32, (q, q), 1)
    causal = (ci >= ri) if reverse else (ci <= ri)
    causal01 = jnp.where(causal, 1.0, 0.0).astype(BF16)
    a_row = -jnp.exp(alog_ref[...])

    for r in range(cur_ref.shape[0]):
        if activated:
            act = cur_ref[r]
        else:
            ext_ref[r, 0:halo, :] = jnp.where(cc == 0, 0.0, prev_ref[r])
            ext_ref[r, halo:halo + q, :] = cur_ref[r]
            ext_ref[r, halo + q:2 * halo + q, :] = jnp.where(cc == nc - 1, 0.0, next_ref[r])
            conv = cb_ref[...] + cw_ref[0:1, :] * ext_ref[r, halo - pad:halo - pad + q, :]
            for k in range(1, SSM_CONV):
                conv = conv + cw_ref[k:k + 1, :] * ext_ref[r, halo - pad + k:halo - pad + k + q, :]
            act = _silu(conv)
            act_ref[r] = act

        dtv = dt_ref[r] + dtb_ref[...]
        dt = jnp.maximum(dtv, 0.0) + jnp.log(1.0 + jnp.exp(-jnp.abs(dtv)))
        acum = _dot01_left(causal01, dt * a_row)
        acum_t = acum.T
        tot = acum[0:1, :] if reverse else acum[q - 1:q, :]
        stacked = jnp.concatenate([dt, jnp.exp(tot - acum), jnp.exp(acum),
                                   jnp.broadcast_to(jnp.exp(tot), (SUBLANES, LANES))], axis=0)
        spread = _dot01_right(stacked, expand_ref[...])
        dt_e, dte_e, ea_e = spread[0:q], spread[q:2 * q], spread[2 * q:3 * q]
        etot_e = spread[3 * q:3 * q + 1]

        xs = act[:, :D_SSM]
        xdt = xs * dt_e
        xdt_b = xdt.astype(BF16)
        w_b = (xdt * dte_e).astype(BF16)
        for g in range(SSM_GROUPS):
            gs = slice(g * gw, (g + 1) * gw)
            b_g = act[:, D_SSM + g * SSM_STATE:D_SSM + (g + 1) * SSM_STATE]
            c_g = act[:, D_SSM + (SSM_GROUPS + g) * SSM_STATE:D_SSM + (SSM_GROUPS + g + 1) * SSM_STATE]
            c_b = c_g.astype(BF16)
            bt_b = b_g.T.astype(BF16)
            cb = jnp.dot(c_b, bt_b, preferred_element_type=F32)
            st = state_ref[r, :, gs]
            y_g = jnp.dot(c_b, st.astype(BF16), preferred_element_type=F32) * ea_e[:, gs]
            y_heads = []
            for hh in range(hpg):
                h = g * hpg + hh
                col = col0 + h
                diff = acum[:, col:col + 1] - acum_t[col:col + 1, :]
                decay = jnp.where(causal, jnp.exp(jnp.minimum(diff, 0.0)), 0.0)
                y_heads.append(jnp.dot((cb * decay).astype(BF16),
                                       xdt_b[:, h * SSM_HEADDIM:(h + 1) * SSM_HEADDIM],
                                       preferred_element_type=F32))
            y_g = y_g + jnp.concatenate(y_heads, axis=-1)
            if add_skip:
                y_g = y_g + dskip_ref[:, gs] * xs[:, gs]
            y_ref[r, :, gs] = y_g
            state_ref[r, :, gs] = st * etot_e[:, gs] + jnp.dot(bt_b, w_b[:, gs], preferred_element_type=F32)
    hfin_ref[...] = state_ref[...]


def _ssd(xbc, dtp, h0, conv_w, conv_b, dt_bias, a_log, d_skip, expand, reverse, add_skip, activated):
    b, l, _ = xbc.shape
    q = SSD_CHUNK
    nc = l // q
    nr = SSD_ROWS if b % SSD_ROWS == 0 else 1
    hb = q // SUBLANES
    pos = (lambda c: nc - 1 - c) if reverse else (lambda c: c)
    const = lambda shape: pl.BlockSpec(shape, lambda i, c: (0,) * len(shape))
    kern = functools.partial(_ssd_kernel, reverse=reverse, nc=nc, add_skip=add_skip, activated=activated)
    chunk = lambda w: pl.BlockSpec((nr, q, w), lambda i, c: (i, pos(c), 0))
    act_spec = [] if activated else [chunk(XBC_DIM)]
    act_shape = [] if activated else [jax.ShapeDtypeStruct((b, l, XBC_DIM), F32)]
    return pl.pallas_call(
        kern,
        grid=(b // nr, nc),
        in_specs=[pl.BlockSpec((nr, q, XBC_DIM), lambda i, c: (i, pos(c), 0)),
                  pl.BlockSpec((nr, SUBLANES, XBC_DIM), lambda i, c: (i, jnp.maximum(pos(c) * hb - 1, 0), 0)),
                  pl.BlockSpec((nr, SUBLANES, XBC_DIM),
                               lambda i, c: (i, jnp.minimum((pos(c) + 1) * hb, l // SUBLANES - 1), 0)),
                  pl.BlockSpec((nr, q, LANES), lambda i, c: (i, pos(c), 0)),
                  pl.BlockSpec((nr, SSM_STATE, D_SSM), lambda i, c: (i, 0, 0)),
                  const(conv_w.shape), const(conv_b.shape), const(dt_bias.shape), const(a_log.shape),
                  const(d_skip.shape), const(expand.shape)],
        out_specs=[chunk(D_SSM), pl.BlockSpec((nr, SSM_STATE, D_SSM), lambda i, c: (i, 0, 0))] + act_spec,
        out_shape=[jax.ShapeDtypeStruct((b, l, D_SSM), F32),
                   jax.ShapeDtypeStruct((b, SSM_STATE, D_SSM), F32)] + act_shape,
        scratch_shapes=[pltpu.VMEM((nr, q + 2 * SUBLANES, XBC_DIM), F32),
                        pltpu.VMEM((nr, SSM_STATE, D_SSM), F32)],
        compiler_params=_params(("parallel", "arbitrary")),
        name="ssd_bwd" if reverse else "ssd_fwd",
    )(xbc, xbc, xbc, dtp, h0, conv_w, conv_b, dt_bias, a_log, d_skip, expand)


def _cfm_kernel(cur_ref, prev_ref, next_ref, w_ref, b_ref, lnw_ref, lnb_ref, o_ref, hpad_ref, vbuf_ref,
                shift_ref, *, nb):
    rows = CFM_ROWS
    band = rows * GRID_W
    half = D_CONV // 2
    pad = (CONV_WIDTH - 1) // 2
    lead = 2 * SUBLANES
    r = pl.program_id(1)

    zeros = jnp.zeros((rows, lead, half), F32)
    hpad_ref[:, 0:lead, :] = zeros
    hpad_ref[:, lead + GRID_W:2 * lead + GRID_W, :] = zeros
    hpad_ref[:, lead:lead + GRID_W, :] = cur_ref[0, :, 0:half].reshape(rows, GRID_W, half)
    vbuf_ref[0:band, :] = jnp.where(r == 0, 0.0, prev_ref[0])
    vbuf_ref[band:2 * band, :] = cur_ref[0, :, half:D_CONV]
    vbuf_ref[2 * band:3 * band, :] = jnp.where(r == nb - 1, 0.0, next_ref[0])

    span = shift_ref.shape[1]

    def row_body(i, carry):
        for s in range(SUBLANES):
            shift_ref[s] = hpad_ref[i, s:s + span, :]
        acc_h = jnp.zeros((GRID_W, half), F32)
        acc_v = jnp.zeros((GRID_W, half), F32)
        for k in range(CONV_WIDTH):
            first = lead - pad + k
            tile0 = first // SUBLANES * SUBLANES
            acc_h = acc_h + w_ref[k:k + 1, 0:half] * shift_ref[first % SUBLANES, tile0:tile0 + GRID_W, :]
            start = pl.multiple_of(band + (i + k - pad) * GRID_W, GRID_W)
            acc_v = acc_v + w_ref[k:k + 1, half:D_CONV] * vbuf_ref[pl.ds(start, GRID_W), :]
        conv = jnp.concatenate([acc_h, acc_v], axis=-1) + b_ref[...]
        mu = jnp.mean(conv, axis=-1, keepdims=True)
        cen = conv - mu
        var = jnp.mean(cen * cen, axis=-1, keepdims=True)
        o_ref[0, pl.ds(pl.multiple_of(i * GRID_W, GRID_W), GRID_W), :] = _silu(
            cen * lax.rsqrt(var + EPS) * lnw_ref[...] + lnb_ref[...])
        return carry

    lax.fori_loop(0, rows, row_body, 0)


def _cfm(u, conv_w, conv_b, ln_w, ln_b):
    b, l, _ = u.shape
    band = CFM_ROWS * GRID_W
    nb = l // band
    half = D_CONV // 2
    const = lambda shape: pl.BlockSpec(shape, lambda i, r: (0,) * len(shape))
    return pl.pallas_call(
        functools.partial(_cfm_kernel, nb=nb),
        grid=(b, nb),
        in_specs=[pl.BlockSpec((1, band, D_CONV), lambda i, r: (i, r, 0)),
                  pl.BlockSpec((1, band, half), lambda i, r: (i, jnp.maximum(r - 1, 0), 1)),
                  pl.BlockSpec((1, band, half), lambda i, r: (i, jnp.minimum(r + 1, nb - 1), 1)),
                  const(conv_w.shape), const(conv_b.shape), const(ln_w.shape), const(ln_b.shape)],
        out_specs=pl.BlockSpec((1, band, D_CONV), lambda i, r: (i, r, 0)),
        out_shape=jax.ShapeDtypeStruct((b, l, D_CONV), F32),
        scratch_shapes=[pltpu.VMEM((CFM_ROWS, GRID_W + 4 * SUBLANES, half), F32),
                        pltpu.VMEM((3 * band, half), F32),
                        pltpu.VMEM((SUBLANES, GRID_W + 3 * SUBLANES, half), F32)],
        compiler_params=_params(("parallel", "parallel")),
        name="cfm",
    )(u, u, u, conv_w, conv_b, ln_w, ln_b)


def _outproj_kernel(yf_ref, yb_ref, z_ref, cfm_ref, x_ref, mod_ref, gnw_ref, w_ref, n2w_ref,
                    x1_ref, h2_ref, h2t_ref):
    gw = D_SSM // SSM_GROUPS
    g = (yf_ref[0] + yb_ref[0]) * _silu(z_ref[0])
    parts = []
    for k in range(SSM_GROUPS):
        gk = g[:, k * gw:(k + 1) * gw]
        parts.append(gk * lax.rsqrt(jnp.mean(gk * gk, axis=-1, keepdims=True) + EPS))
    ssm = jnp.concatenate(parts, axis=-1) * gnw_ref[...]
    mixed = (jnp.dot(ssm.astype(BF16), w_ref[0:D_SSM, :], preferred_element_type=F32)
             + jnp.dot(cfm_ref[0].astype(BF16), w_ref[D_SSM:, :], preferred_element_type=F32))
    x1 = x_ref[0] + mod_ref[0, 2:3, :] * mixed
    x1_ref[0] = x1
    h = x1 * lax.rsqrt(jnp.mean(x1 * x1, axis=-1, keepdims=True) + EPS) * n2w_ref[...]
    h2 = h * (1.0 + mod_ref[0, 4:5, :]) + mod_ref[0, 3:4, :]
    h2_ref[0] = h2.astype(BF16)
    h2t_ref[...] = h2.T.astype(BF16)


def _outproj(yf, yb, z, cfm, x, mod, gn_w, w_out_b, n2_w):
    b, l, d = x.shape
    tm = min(OUT_TILE, l)
    tok = lambda w: pl.BlockSpec((1, tm, w), lambda i, j: (i, j, 0))
    const = lambda shape: pl.BlockSpec(shape, lambda i, j: (0,) * len(shape))
    return pl.pallas_call(
        _outproj_kernel,
        grid=(b, l // tm),
        in_specs=[tok(D_SSM), tok(D_SSM), tok(D_SSM), tok(D_CONV), tok(d),
                  pl.BlockSpec((1, 6, d), lambda i, j: (i, 0, 0)),
                  const(gn_w.shape), const(w_out_b.shape), const(n2_w.shape)],
        out_specs=[tok(d), tok(d), pl.BlockSpec((d, tm), lambda i, j: (0, i * (l // tm) + j))],
        out_shape=[jax.ShapeDtypeStruct((b, l, d), F32), jax.ShapeDtypeStruct((b, l, d), BF16),
                   jax.ShapeDtypeStruct((d, b * l), BF16)],
        compiler_params=_params(("parallel", "parallel")),
        name="outproj",
    )(yf, yb, z, cfm, x, mod, gn_w, w_out_b, n2_w)


def _oddeven_merge(lo, hi, r):
    step = r * 2
    if step < hi - lo:
        yield from _oddeven_merge(lo, hi, step)
        yield from _oddeven_merge(lo + r, hi, step)
        yield from [(i, i + r) for i in range(lo + r, hi - r, step)]
    else:
        yield (lo, lo + r)


def _oddeven_sort(lo, hi):
    if hi > lo:
        mid = lo + (hi - lo) // 2
        yield from _oddeven_sort(lo, mid)
        yield from _oddeven_sort(mid + 1, hi)
        yield from _oddeven_merge(lo, hi, 1)


SORT_NET = tuple(_oddeven_sort(0, PEER_TOPK - 1))


def _cmp_exchange(v, i, j):
    v[i], v[j] = jnp.maximum(v[i], v[j]), jnp.minimum(v[i], v[j])


def _bitonic_sort(v):
    d = len(v) // 2
    while d:
        for i in range(len(v)):
            if not i & d:
                _cmp_exchange(v, i, i + d)
        d //= 2


def _merge_top(v, w):
    n = len(v)
    return [jnp.maximum(v[i], w[n - 1 - i]) for i in range(n)]


def _sublane_rolls():
    shift = SUBLANES // 2
    while shift:
        yield shift
        shift //= 2


def _top_sorted(v):
    for shift in _sublane_rolls():
        v = _merge_top(v, [pltpu.roll(x, shift, 0) for x in v])
        _bitonic_sort(v)
    return v


def _score_kernel(h_ref, wq_ref, keys_ref, neg_ref, c1_ref, e1_ref, r2_ref, e2_ref, q_ref):
    tm = h_ref.shape[0]
    nsub = tm // LANES
    kd = PEER_KEY_DIM // 2
    k = PEER_TOPK
    ntile = N_KEYS // SUBLANES
    q_ref[...] = jnp.dot(h_ref[...], wq_ref[...], preferred_element_type=F32).astype(BF16)

    def body(it, carry):
        h = it // nsub
        tok = pl.multiple_of((it % nsub) * LANES, LANES)
        nt = (((1,), (1,)), ((), ()))

        def score_tiles(p):
            qs = q_ref[pl.ds(tok, LANES), pl.ds(pl.multiple_of((2 * h + p) * kd, kd), kd)]
            s = lax.dot_general(keys_ref[2 * h + p], qs, nt, preferred_element_type=F32)
            return [s[i * SUBLANES:(i + 1) * SUBLANES, :] for i in range(ntile)]

        def top(tiles):
            v = list(tiles)
            for i, j in SORT_NET:
                _cmp_exchange(v, i, j)
            return _top_sorted(v)

        s1 = score_tiles(0)
        s2 = score_tiles(1)
        a = top(s1)
        b = top(s2)

        sub = lax.broadcasted_iota(jnp.int32, (SUBLANES, LANES), 0)
        pick = lambda lo, x, y: jnp.where(sub < lo, x, y)
        base = pick(4, pick(2, pick(1, a[0], a[1]), pick(3, a[2], a[3])), pick(5, b[0], pick(6, b[1], b[2])))
        cand = []
        for m in range(k):
            other = pick(4, b[m], a[4 + m]) if 4 + m < k else b[m]
            cand.append((other + base) + neg_ref[m])
        lists = list(cand)
        rolls = list(_sublane_rolls())
        for shift in rolls[:-1]:
            lists = _merge_top(lists, [pltpu.roll(x, shift, 0) for x in lists])
            _bitonic_sort(lists)
        lists = _merge_top(lists, [pltpu.roll(x, rolls[-1], 0) for x in lists])
        tau = functools.reduce(jnp.minimum, lists)

        best = a[0] + b[0]
        zsum = sum(jnp.where(c >= tau, jnp.exp(c - best), 0.0) for c in cand)
        for shift in rolls:
            zsum = zsum + pltpu.roll(zsum, shift, 0)
        scale = 0.5 / zsum
        top_extra = sum(jnp.where(a[0] + b[j] >= tau, 1.0, 0.0) for j in range(k // 2, k))

        c1_t, e1_t, r2_t, e2_t = [], [], [], []
        for t1, t2 in zip(s1, s2):
            c1 = jnp.zeros_like(t1)
            for j in range(k // 2):
                c1 = jnp.where(t1 + b[j] >= tau, float(j + 1), c1)
            c1_t.append(jnp.where(t1 == a[0], c1 + top_extra, c1))
            e1_t.append(jnp.exp(t1 - a[0]) * scale)
            r2 = jnp.zeros_like(t2)
            for j in range(k):
                r2 = jnp.where(t2 < b[j], float(j + 1), r2)
            r2_t.append(r2)
            e2_t.append(jnp.exp(t2 - b[0]))
        c1_ref[h, :, pl.ds(tok, LANES)] = jnp.concatenate(c1_t, axis=0)
        e1_ref[h, :, pl.ds(tok, LANES)] = jnp.concatenate(e1_t, axis=0)
        r2_ref[h, :, pl.ds(tok, LANES)] = jnp.concatenate(r2_t, axis=0).astype(BF16)
        e2_ref[h, :, pl.ds(tok, LANES)] = jnp.concatenate(e2_t, axis=0).astype(BF16)
        return carry

    lax.fori_loop(0, PEER_HEADS * nsub, body, 0, unroll=4)


def _candidate_mask():
    k = PEER_TOPK
    m = lax.broadcasted_iota(jnp.int32, (k, SUBLANES, LANES), 0)
    s = lax.broadcasted_iota(jnp.int32, (k, SUBLANES, LANES), 1)
    row_ok = (s < 4) & ((s + 1) * (m + 1) <= k)
    col_ok = (s >= 4) & (s < 7) & (4 + m < k) & ((5 + m) * (s - 3) <= k)
    return jnp.where(row_ok | col_ok, 0.0, -jnp.inf).astype(F32)


def _score(h2, wq_b, keys_b):
    t, d = h2.shape
    tm = min(SCORE_TILE, t)
    neg = _candidate_mask()
    big = lambda: pl.BlockSpec((PEER_HEADS, N_KEYS, tm), lambda j: (0, 0, j))
    big_shape = lambda dt: jax.ShapeDtypeStruct((PEER_HEADS, N_KEYS, t), dt)
    return pl.pallas_call(
        _score_kernel,
        grid=(t // tm,),
        in_specs=[pl.BlockSpec((tm, d), lambda j: (j, 0)),
                  pl.BlockSpec(wq_b.shape, lambda j: (0, 0)),
                  pl.BlockSpec(keys_b.shape, lambda j: (0, 0, 0)),
                  pl.BlockSpec(neg.shape, lambda j: (0, 0, 0))],
        out_specs=[big(), big(), big(), big()],
        out_shape=[big_shape(F32), big_shape(F32), big_shape(BF16), big_shape(BF16)],
        scratch_shapes=[pltpu.VMEM((tm, PEER_HEADS * PEER_KEY_DIM), BF16)],
        compiler_params=_params(("parallel",)),
        name="peer_score",
    )(h2, wq_b, keys_b, neg)


def _key_row_bf16(ref, h, key0, j):
    tile = ref[h, pl.ds(key0 + j // SUBLANES * SUBLANES, SUBLANES), :]
    row = tile[j % SUBLANES:j % SUBLANES + 1, :]
    rows = jnp.broadcast_to(row, (BF16_ROWS, ref.shape[-1])).astype(BF16)
    return jnp.tile(rows, (N_KEYS // BF16_ROWS, 1))


def _peer_kernel(ht_ref, u_ref, vt_ref, c1_ref, e1_ref, r2_ref, e2_ref, x1_ref, mod_ref, fnw_ref,
                 o_ref, acc_ref):
    eb = pl.program_id(1)
    nsub = PEER_EBLK // PEER_SUB
    keys_per_sub = PEER_SUB // N_KEYS
    keys_per_step = PEER_EBLK // N_KEYS
    key0 = pl.multiple_of(eb * keys_per_step, keys_per_step)

    @pl.when(eb == 0)
    def _():
        acc_ref[...] = jnp.zeros_like(acc_ref)

    def first_matmul(sub):
        return jnp.dot(u_ref[sub * PEER_SUB:(sub + 1) * PEER_SUB, :], ht_ref[...],
                       preferred_element_type=F32)

    def build_gates(sub):
        gates = []
        for kk in range(keys_per_sub):
            j = sub * keys_per_sub + kk
            g = jnp.zeros((N_KEYS, ht_ref.shape[1]), BF16)
            for h in range(PEER_HEADS):
                c1 = _key_row_bf16(c1_ref, h, key0, j)
                e1 = _key_row_bf16(e1_ref, h, key0, j)
                g = g + jnp.where(r2_ref[h] < c1, e2_ref[h], jnp.zeros((), BF16)) * e1
            gates.append(g)
        return jnp.concatenate(gates, axis=0)

    pre = first_matmul(0)
    weighted = []
    for sub in range(nsub):
        nxt = first_matmul(sub + 1) if sub + 1 < nsub else None
        act = pre * (1.0 + lax.erf(pre * (2.0 ** -0.5)))
        weighted.append(act.astype(BF16) * build_gates(sub))
        pre = nxt
    acc_ref[...] = jnp.dot(vt_ref[...], jnp.concatenate(weighted, axis=0),
                           preferred_element_type=F32) + acc_ref[...]

    @pl.when(eb == pl.num_programs(1) - 1)
    def _():
        x2 = x1_ref[...] + mod_ref[0, 5:6, :] * acc_ref[...].T
        o_ref[...] = x2 * lax.rsqrt(jnp.mean(x2 * x2, axis=-1, keepdims=True) + EPS) * fnw_ref[...]


def _peer(h2t, u_b, vt_b, c1, e1, r2, e2, x1, mod, fn_w, seq):
    d, t = h2t.shape
    tm = min(PEER_TILE, seq)
    n_exp = u_b.shape[0]
    big = lambda: pl.BlockSpec((PEER_HEADS, N_KEYS, tm), lambda j, e: (0, 0, j))
    return pl.pallas_call(
        _peer_kernel,
        grid=(t // tm, n_exp // PEER_EBLK),
        in_specs=[pl.BlockSpec((d, tm), lambda j, e: (0, j)),
                  pl.BlockSpec((PEER_EBLK, d), lambda j, e: (e, 0)),
                  pl.BlockSpec((None, d, PEER_EBLK), lambda j, e: (e, 0, 0)),
                  big(), big(), big(), big(),
                  pl.BlockSpec((tm, d), lambda j, e: (j, 0)),
                  pl.BlockSpec((1, 6, d), lambda j, e: ((j * tm) // seq, 0, 0)),
                  pl.BlockSpec((1, d), lambda j, e: (0, 0))],
        out_specs=pl.BlockSpec((tm, d), lambda j, e: (j, 0)),
        out_shape=jax.ShapeDtypeStruct((t, d), F32),
        scratch_shapes=[pltpu.VMEM((d, tm), F32)],
        compiler_params=_params(("parallel", "arbitrary")),
        name="peer_dense",
    )(h2t, u_b, vt_b, c1, e1, r2, e2, x1, mod, fn_w)


def _pad_lanes(v, width=LANES):
    flat = v.reshape(1, -1)
    return jnp.pad(flat, ((0, 0), (0, width - flat.shape[1])))


def _head_expand(col0):
    rows = lax.broadcasted_iota(jnp.int32, (LANES, D_SSM), 0)
    heads = lax.broadcasted_iota(jnp.int32, (LANES, D_SSM), 1) // SSM_HEADDIM
    return (rows == heads + col0).astype(BF16)


def kernel(x, c, ctx, c_ctx, ada_w, ada_b, norm1_w, norm2_w, w_in, ssm_conv_w, ssm_conv_b, ssm_dt_bias,
           ssm_a_log, ssm_d, ssm_norm_w, cfm_conv_w, cfm_conv_b, cfm_ln_w, cfm_ln_b, w_out, peer_wq,
           peer_subkeys, peer_u, peer_v, final_norm_w):
    depth = ada_w.shape[0]
    assert depth == 1, "single-layer configuration"
    b, l, d = x.shape
    i = 0

    c_rows = jnp.concatenate([c, c_ctx[None], jnp.zeros((2 * SUBLANES - b - 1, d), F32)], axis=0)
    mod_all = _ada(c_rows, ada_w[i], ada_b[i])
    mod = mod_all[:b].reshape(b, 6, d)
    mod_ctx = mod_all[b:b + 1].reshape(1, 6, d)

    wi = w_in[i]
    o_dt = D_SSM + XBC_DIM
    w_r = jnp.concatenate([wi[:, :o_dt], wi[:, o_dt + 2 * SSM_HEADS:], wi[:, o_dt:o_dt + 2 * SSM_HEADS],
                           jnp.zeros((d, LANES - 2 * SSM_HEADS), F32)], axis=1).astype(BF16)
    z_l, xbc_l, u_l, dt_l = _inproj(x, mod, True, norm1_w[i], w_r)
    _, xbc_c, _, dt_c = _inproj(ctx, mod_ctx, False, norm1_w[i], w_r)

    conv_w = jnp.pad(ssm_conv_w[i], ((0, SUBLANES - SSM_CONV), (0, 0)))
    conv_b = ssm_conv_b[i].reshape(1, XBC_DIM)
    dt_bias = _pad_lanes(ssm_dt_bias[i])
    a_log = _pad_lanes(ssm_a_log[i])
    d_skip = jnp.repeat(ssm_d[i], SSM_HEADDIM).reshape(1, D_SSM)
    zero_state = jnp.zeros((b, SSM_STATE, D_SSM), F32)
    y_dirs = []
    act_c = act_l = None
    for reverse in (False, True):
        expand = _head_expand(SSM_HEADS if reverse else 0)
        args = (conv_w, conv_b, dt_bias, a_log, d_skip, expand, reverse)
        if not reverse:
            _, h_ctx, act_c = _ssd(xbc_c, dt_c, zero_state, *args, add_skip=False, activated=False)
            y_dir, _, act_l = _ssd(xbc_l, dt_l, h_ctx, *args, add_skip=True, activated=False)
        else:
            _, h_ctx = _ssd(act_c, dt_c, zero_state, *args, add_skip=False, activated=True)
            y_dir, _ = _ssd(act_l, dt_l, h_ctx, *args, add_skip=False, activated=True)
        y_dirs.append(y_dir)

    cfm = _cfm(u_l, jnp.pad(cfm_conv_w[i], ((0, 1), (0, 0))), cfm_conv_b[i].reshape(1, D_CONV),
               cfm_ln_w[i].reshape(1, D_CONV), cfm_ln_b[i].reshape(1, D_CONV))

    x1, h2, h2t = _outproj(y_dirs[0], y_dirs[1], z_l, cfm, x, mod, ssm_norm_w[i].reshape(1, D_SSM),
                      w_out[i].astype(BF16), norm2_w[i].reshape(1, d))

    t = b * l
    h2 = h2.reshape(t, d)
    keys_b = peer_subkeys[i].reshape(PEER_HEADS * 2, N_KEYS, PEER_KEY_DIM // 2).astype(BF16)
    c1, e1, r2, e2 = _score(h2, peer_wq[i].astype(BF16), keys_b)
    vt_b = peer_v[i].reshape(-1, PEER_EBLK, d).transpose(0, 2, 1).astype(BF16)
    out = _peer(h2t, peer_u[i].astype(BF16), vt_b, c1, e1, r2, e2,
                x1.reshape(t, d), mod, final_norm_w.reshape(1, d), l)
    return out.reshape(b, l, d)
```

```python
import functools

import jax
import jax.numpy as jnp
from jax import lax
from jax.experimental import pallas as pl
from jax.experimental.pallas import tpu as pltpu

F32 = jnp.float32
BF16 = jnp.bfloat16

D_MODEL = 1024
GRID_W = 64
D_SSM = 512
SSM_HEADDIM = 64
SSM_HEADS = 8
SSM_STATE = 128
SSM_GROUPS = 2
SSM_CONV = 5
SSD_CHUNK = 128
XBC_DIM = D_SSM + 2 * SSM_GROUPS * SSM_STATE
D_CONV = 512
CONV_WIDTH = 31
PEER_HEADS = 8
PEER_KEY_DIM = 256
N_KEYS = 128
PEER_TOPK = 16
EPS = 1e-6

LANES = 128
SUBLANES = 8
VMEM_LIMIT = 56 * 1024 * 1024

IN_TILE = 512
SSD_ROWS = 2
CFM_ROWS = 16
OUT_TILE = 512
SCORE_TILE = 512
PEER_TILE = 512
PEER_EBLK = 2048
PEER_SUB = 128
BF16_ROWS = 2 * SUBLANES


def _params(sem):
    return pltpu.CompilerParams(dimension_semantics=sem, vmem_limit_bytes=VMEM_LIMIT)


def _silu(v):
    return v * jax.nn.sigmoid(v)


def _split3(v):
    hi = v.astype(BF16)
    r1 = v - hi.astype(F32)
    mid = r1.astype(BF16)
    lo = (r1 - mid.astype(F32)).astype(BF16)
    return hi, mid, lo


def _dot01_right(v, m01):
    return sum(jnp.dot(p, m01, preferred_element_type=F32) for p in _split3(v))


def _dot01_left(m01, v):
    return sum(jnp.dot(m01, p, preferred_element_type=F32) for p in _split3(v))


def _ada_kernel(c_ref, w_ref, b_ref, o_ref):
    sc = _silu(c_ref[...])
    o_ref[...] = jnp.dot(sc.astype(BF16), w_ref[...].astype(BF16),
                         preferred_element_type=F32) + b_ref[...]


def _ada(c_rows, ada_w, ada_b):
    n = ada_w.shape[1]
    tn = 768
    return pl.pallas_call(
        _ada_kernel,
        grid=(n // tn,),
        in_specs=[pl.BlockSpec(c_rows.shape, lambda j: (0, 0)),
                  pl.BlockSpec((D_MODEL, tn), lambda j: (0, j)),
                  pl.BlockSpec((1, tn), lambda j: (0, j))],
        out_specs=pl.BlockSpec((c_rows.shape[0], tn), lambda j: (0, j)),
        out_shape=jax.ShapeDtypeStruct((c_rows.shape[0], n), F32),
        compiler_params=_params(("parallel",)),
        name="ada",
    )(c_rows, ada_w, ada_b.reshape(1, n))


def _inproj_kernel(x_ref, mod_ref, nw_ref, w_ref, z_ref, xbc_ref, u_ref, dt_ref):
    xv = x_ref[0]
    h = xv * lax.rsqrt(jnp.mean(xv * xv, axis=-1, keepdims=True) + EPS) * nw_ref[...]
    h = h * (1.0 + mod_ref[0, 1:2, :]) + mod_ref[0, 0:1, :]
    hb = h.astype(BF16)

    def proj(lo, hi):
        return jnp.dot(hb, w_ref[:, lo:hi], preferred_element_type=F32)

    z_ref[0] = proj(0, D_SSM)
    xbc_ref[0] = proj(D_SSM, D_SSM + XBC_DIM)
    o = D_SSM + XBC_DIM
    u_ref[0] = proj(o, o + D_CONV) * jax.nn.sigmoid(proj(o + D_CONV, o + 2 * D_CONV))
    dt_ref[0] = proj(o + 2 * D_CONV, o + 2 * D_CONV + LANES)


def _inproj(x, mod, per_batch_mod, norm_w, w_r):
    b, l, d = x.shape
    tm = min(IN_TILE, l)
    mod_idx = (lambda i, j: (i, 0, 0)) if per_batch_mod else (lambda i, j: (0, 0, 0))
    tok = lambda w: pl.BlockSpec((1, tm, w), lambda i, j: (i, j, 0))
    return pl.pallas_call(
        _inproj_kernel,
        grid=(b, l // tm),
        in_specs=[tok(d),
                  pl.BlockSpec((1, 6, d), mod_idx),
                  pl.BlockSpec((1, d), lambda i, j: (0, 0)),
                  pl.BlockSpec(w_r.shape, lambda i, j: (0, 0))],
        out_specs=[tok(D_SSM), tok(XBC_DIM), tok(D_CONV), tok(LANES)],
        out_shape=[jax.ShapeDtypeStruct((b, l, w), F32) for w in (D_SSM, XBC_DIM, D_CONV, LANES)],
        compiler_params=_params(("parallel", "parallel")),
        name="inproj",
    )(x, mod, norm_w.reshape(1, d), w_r)


def _ssd_kernel(cur_ref, prev_ref, next_ref, dt_ref, h0_ref, cw_ref, cb_ref, dtb_ref, alog_ref,
                dskip_ref, expand_ref, y_ref, hfin_ref, *rest, reverse, nc, add_skip, activated):
    act_ref, ext_ref, state_ref = (None, *rest) if activated else rest
    q = SSD_CHUNK
    halo = SUBLANES
    pad = (SSM_CONV - 1) // 2
    gw = D_SSM // SSM_GROUPS
    hpg = SSM_HEADS // SSM_GROUPS
    col0 = SSM_HEADS if reverse else 0
    c = pl.program_id(1)
    cc = (nc - 1 - c) if reverse else c

    @pl.when(c == 0)
    def _():
        state_ref[...] = h0_ref[...]

    ri = lax.broadcasted_iota(jnp.int32, (q, q), 0)
    ci = lax.broadcasted_iota(jnp.int32, (q, q), 1)
    causal = (ci >= ri) if reverse else (ci <= ri)
    causal01 = jnp.where(causal, 1.0, 0.0).astype(BF16)
    a_row = -jnp.exp(alog_ref[...])

    for r in range(cur_ref.shape[0]):
        if activated:
            act = cur_ref[r]
        else:
            ext_ref[r, 0:halo, :] = jnp.where(cc == 0, 0.0, prev_ref[r])
            ext_ref[r, halo:halo + q, :] = cur_ref[r]
            ext_ref[r, halo + q:2 * halo + q, :] = jnp.where(cc == nc - 1, 0.0, next_ref[r])
            conv = cb_ref[...] + cw_ref[0:1, :] * ext_ref[r, halo - pad:halo - pad + q, :]
            for k in range(1, SSM_CONV):
                conv = conv + cw_ref[k:k + 1, :] * ext_ref[r, halo - pad + k:halo - pad + k + q, :]
            act = _silu(conv)
            act_ref[r] = act

        dtv = dt_ref[r] + dtb_ref[...]
        dt = jnp.maximum(dtv, 0.0) + jnp.log(1.0 + jnp.exp(-jnp.abs(dtv)))
        acum = _dot01_left(causal01, dt * a_row)
        acum_t = acum.T
        tot = acum[0:1, :] if reverse else acum[q - 1:q, :]
        stacked = jnp.concatenate([dt, jnp.exp(tot - acum), jnp.exp(acum),
                                   jnp.broadcast_to(jnp.exp(tot), (SUBLANES, LANES))], axis=0)
        spread = _dot01_right(stacked, expand_ref[...])
        dt_e, dte_e, ea_e = spread[0:q], spread[q:2 * q], spread[2 * q:3 * q]
        etot_e = spread[3 * q:3 * q + 1]

        xs = act[:, :D_SSM]
        xdt = xs * dt_e
        xdt_b = xdt.astype(BF16)
        w_b = (xdt * dte_e).astype(BF16)
        for g in range(SSM_GROUPS):
            gs = slice(g * gw, (g + 1) * gw)
            b_g = act[:, D_SSM + g * SSM_STATE:D_SSM + (g + 1) * SSM_STATE]
            c_g = act[:, D_SSM + (SSM_GROUPS + g) * SSM_STATE:D_SSM + (SSM_GROUPS + g + 1) * SSM_STATE]
            c_b = c_g.astype(BF16)
            bt_b = b_g.T.astype(BF16)
            cb = jnp.dot(c_b, bt_b, preferred_element_type=F32)
            st = state_ref[r, :, gs]
            y_g = jnp.dot(c_b, st.astype(BF16), preferred_element_type=F32) * ea_e[:, gs]
            y_heads = []
            for hh in range(hpg):
                h = g * hpg + hh
                col = col0 + h
                diff = acum[:, col:col + 1] - acum_t[col:col + 1, :]
                decay = jnp.where(causal, jnp.exp(jnp.minimum(diff, 0.0)), 0.0)
                y_heads.append(jnp.dot((cb * decay).astype(BF16),
                                       xdt_b[:, h * SSM_HEADDIM:(h + 1) * SSM_HEADDIM],
                                       preferred_element_type=F32))
            y_g = y_g + jnp.concatenate(y_heads, axis=-1)
            if add_skip:
                y_g = y_g + dskip_ref[:, gs] * xs[:, gs]
            y_ref[r, :, gs] = y_g
            state_ref[r, :, gs] = st * etot_e[:, gs] + jnp.dot(bt_b, w_b[:, gs], preferred_element_type=F32)
    hfin_ref[...] = state_ref[...]


def _ssd(xbc, dtp, h0, conv_w, conv_b, dt_bias, a_log, d_skip, expand, reverse, add_skip, activated):
    b, l, _ = xbc.shape
    q = SSD_CHUNK
    nc = l // q
    nr = SSD_ROWS if b % SSD_ROWS == 0 else 1
    hb = q // SUBLANES
    pos = (lambda c: nc - 1 - c) if reverse else (lambda c: c)
    const = lambda shape: pl.BlockSpec(shape, lambda i, c: (0,) * len(shape))
    kern = functools.partial(_ssd_kernel, reverse=reverse, nc=nc, add_skip=add_skip, activated=activated)
    chunk = lambda w: pl.BlockSpec((nr, q, w), lambda i, c: (i, pos(c), 0))
    act_spec = [] if activated else [chunk(XBC_DIM)]
    act_shape = [] if activated else [jax.ShapeDtypeStruct((b, l, XBC_DIM), F32)]
    return pl.pallas_call(
        kern,
        grid=(b // nr, nc),
        in_specs=[pl.BlockSpec((nr, q, XBC_DIM), lambda i, c: (i, pos(c), 0)),
                  pl.BlockSpec((nr, SUBLANES, XBC_DIM), lambda i, c: (i, jnp.maximum(pos(c) * hb - 1, 0), 0)),
                  pl.BlockSpec((nr, SUBLANES, XBC_DIM),
                               lambda i, c: (i, jnp.minimum((pos(c) + 1) * hb, l // SUBLANES - 1), 0)),
                  pl.BlockSpec((nr, q, LANES), lambda i, c: (i, pos(c), 0)),
                  pl.BlockSpec((nr, SSM_STATE, D_SSM), lambda i, c: (i, 0, 0)),
                  const(conv_w.shape), const(conv_b.shape), const(dt_bias.shape), const(a_log.shape),
                  const(d_skip.shape), const(expand.shape)],
        out_specs=[chunk(D_SSM), pl.BlockSpec((nr, SSM_STATE, D_SSM), lambda i, c: (i, 0, 0))] + act_spec,
        out_shape=[jax.ShapeDtypeStruct((b, l, D_SSM), F32),
                   jax.ShapeDtypeStruct((b, SSM_STATE, D_SSM), F32)] + act_shape,
        scratch_shapes=[pltpu.VMEM((nr, q + 2 * SUBLANES, XBC_DIM), F32),
                        pltpu.VMEM((nr, SSM_STATE, D_SSM), F32)],
        compiler_params=_params(("parallel", "arbitrary")),
        name="ssd_bwd" if reverse else "ssd_fwd",
    )(xbc, xbc, xbc, dtp, h0, conv_w, conv_b, dt_bias, a_log, d_skip, expand)


def _cfm_kernel(cur_ref, prev_ref, next_ref, w_ref, b_ref, lnw_ref, lnb_ref, o_ref, hpad_ref, vbuf_ref,
                shift_ref, *, nb):
    rows = CFM_ROWS
    band = rows * GRID_W
    half = D_CONV // 2
    pad = (CONV_WIDTH - 1) // 2
    lead = 2 * SUBLANES
    r = pl.program_id(1)

    zeros = jnp.zeros((rows, lead, half), F32)
    hpad_ref[:, 0:lead, :] = zeros
    hpad_ref[:, lead + GRID_W:2 * lead + GRID_W, :] = zeros
    hpad_ref[:, lead:lead + GRID_W, :] = cur_ref[0, :, 0:half].reshape(rows, GRID_W, half)
    vbuf_ref[0:band, :] = jnp.where(r == 0, 0.0, prev_ref[0])
    vbuf_ref[band:2 * band, :] = cur_ref[0, :, half:D_CONV]
    vbuf_ref[2 * band:3 * band, :] = jnp.where(r == nb - 1, 0.0, next_ref[0])

    span = shift_ref.shape[1]

    def row_body(i, carry):
        for s in range(SUBLANES):
            shift_ref[s] = hpad_ref[i, s:s + span, :]
        acc_h = jnp.zeros((GRID_W, half), F32)
        acc_v = jnp.zeros((GRID_W, half), F32)
        for k in range(CONV_WIDTH):
            first = lead - pad + k
            tile0 = first // SUBLANES * SUBLANES
            acc_h = acc_h + w_ref[k:k + 1, 0:half] * shift_ref[first % SUBLANES, tile0:tile0 + GRID_W, :]
            start = pl.multiple_of(band + (i + k - pad) * GRID_W, GRID_W)
            acc_v = acc_v + w_ref[k:k + 1, half:D_CONV] * vbuf_ref[pl.ds(start, GRID_W), :]
        conv = jnp.concatenate([acc_h, acc_v], axis=-1) + b_ref[...]
        mu = jnp.mean(conv, axis=-1, keepdims=True)
        cen = conv - mu
        var = jnp.mean(cen * cen, axis=-1, keepdims=True)
        o_ref[0, pl.ds(pl.multiple_of(i * GRID_W, GRID_W), GRID_W), :] = _silu(
            cen * lax.rsqrt(var + EPS) * lnw_ref[...] + lnb_ref[...])
        return carry

    lax.fori_loop(0, rows, row_body, 0)


def _cfm(u, conv_w, conv_b, ln_w, ln_b):
    b, l, _ = u.shape
    band = CFM_ROWS * GRID_W
    nb = l // band
    half = D_CONV // 2
    const = lambda shape: pl.BlockSpec(shape, lambda i, r: (0,) * len(shape))
    return pl.pallas_call(
        functools.partial(_cfm_kernel, nb=nb),
        grid=(b, nb),
        in_specs=[pl.BlockSpec((1, band, D_CONV), lambda i, r: (i, r, 0)),
                  pl.BlockSpec((1, band, half), lambda i, r: (i, jnp.maximum(r - 1, 0), 1)),
                  pl.BlockSpec((1, band, half), lambda i, r: (i, jnp.minimum(r + 1, nb - 1), 1)),
                  const(conv_w.shape), const(conv_b.shape), const(ln_w.shape), const(ln_b.shape)],
        out_specs=pl.BlockSpec((1, band, D_CONV), lambda i, r: (i, r, 0)),
        out_shape=jax.ShapeDtypeStruct((b, l, D_CONV), F32),
        scratch_shapes=[pltpu.VMEM((CFM_ROWS, GRID_W + 4 * SUBLANES, half), F32),
                        pltpu.VMEM((3 * band, half), F32),
                        pltpu.VMEM((SUBLANES, GRID_W + 3 * SUBLANES, half), F32)],
        compiler_params=_params(("parallel", "parallel")),
        name="cfm",
    )(u, u, u, conv_w, conv_b, ln_w, ln_b)


def _outproj_kernel(yf_ref, yb_ref, z_ref, cfm_ref, x_ref, mod_ref, gnw_ref, w_ref, n2w_ref,
                    x1_ref, h2_ref, h2t_ref):
    gw = D_SSM // SSM_GROUPS
    g = (yf_ref[0] + yb_ref[0]) * _silu(z_ref[0])
    parts = []
    for k in range(SSM_GROUPS):
        gk = g[:, k * gw:(k + 1) * gw]
        parts.append(gk * lax.rsqrt(jnp.mean(gk * gk, axis=-1, keepdims=True) + EPS))
    ssm = jnp.concatenate(parts, axis=-1) * gnw_ref[...]
    mixed = (jnp.dot(ssm.astype(BF16), w_ref[0:D_SSM, :], preferred_element_type=F32)
             + jnp.dot(cfm_ref[0].astype(BF16), w_ref[D_SSM:, :], preferred_element_type=F32))
    x1 = x_ref[0] + mod_ref[0, 2:3, :] * mixed
    x1_ref[0] = x1
    h = x1 * lax.rsqrt(jnp.mean(x1 * x1, axis=-1, keepdims=True) + EPS) * n2w_ref[...]
    h2 = h * (1.0 + mod_ref[0, 4:5, :]) + mod_ref[0, 3:4, :]
    h2_ref[0] = h2.astype(BF16)
    h2t_ref[...] = h2.T.astype(BF16)


def _outproj(yf, yb, z, cfm, x, mod, gn_w, w_out_b, n2_w):
    b, l, d = x.shape
    tm = min(OUT_TILE, l)
    tok = lambda w: pl.BlockSpec((1, tm, w), lambda i, j: (i, j, 0))
    const = lambda shape: pl.BlockSpec(shape, lambda i, j: (0,) * len(shape))
    return pl.pallas_call(
        _outproj_kernel,
        grid=(b, l // tm),
        in_specs=[tok(D_SSM), tok(D_SSM), tok(D_SSM), tok(D_CONV), tok(d),
                  pl.BlockSpec((1, 6, d), lambda i, j: (i, 0, 0)),
                  const(gn_w.shape), const(w_out_b.shape), const(n2_w.shape)],
        out_specs=[tok(d), tok(d), pl.BlockSpec((d, tm), lambda i, j: (0, i * (l // tm) + j))],
        out_shape=[jax.ShapeDtypeStruct((b, l, d), F32), jax.ShapeDtypeStruct((b, l, d), BF16),
                   jax.ShapeDtypeStruct((d, b * l), BF16)],
        compiler_params=_params(("parallel", "parallel")),
        name="outproj",
    )(yf, yb, z, cfm, x, mod, gn_w, w_out_b, n2_w)


def _oddeven_merge(lo, hi, r):
    step = r * 2
    if step < hi - lo:
        yield from _oddeven_merge(lo, hi, step)
        yield from _oddeven_merge(lo + r, hi, step)
        yield from [(i, i + r) for i in range(lo + r, hi - r, step)]
    else:
        yield (lo, lo + r)


def _oddeven_sort(lo, hi):
    if hi > lo:
        mid = lo + (hi - lo) // 2
        yield from _oddeven_sort(lo, mid)
        yield from _oddeven_sort(mid + 1, hi)
        yield from _oddeven_merge(lo, hi, 1)


SORT_NET = tuple(_oddeven_sort(0, PEER_TOPK - 1))


def _cmp_exchange(v, i, j):
    v[i], v[j] = jnp.maximum(v[i], v[j]), jnp.minimum(v[i], v[j])


def _bitonic_sort(v):
    d = len(v) // 2
    while d:
        for i in range(len(v)):
            if not i & d:
                _cmp_exchange(v, i, i + d)
        d //= 2


def _merge_top(v, w):
    n = len(v)
    return [jnp.maximum(v[i], w[n - 1 - i]) for i in range(n)]


def _sublane_rolls():
    shift = SUBLANES // 2
    while shift:
        yield shift
        shift //= 2


def _top_sorted(v):
    for shift in _sublane_rolls():
        v = _merge_top(v, [pltpu.roll(x, shift, 0) for x in v])
        _bitonic_sort(v)
    return v


def _score_kernel(h_ref, wq_ref, keys_ref, neg_ref, c1_ref, e1_ref, r2_ref, e2_ref, q_ref):
    tm = h_ref.shape[0]
    nsub = tm // LANES
    kd = PEER_KEY_DIM // 2
    k = PEER_TOPK
    ntile = N_KEYS // SUBLANES
    q_ref[...] = jnp.dot(h_ref[...], wq_ref[...], preferred_element_type=F32).astype(BF16)

    def body(it, carry):
        h = it // nsub
        tok = pl.multiple_of((it % nsub) * LANES, LANES)
        nt = (((1,), (1,)), ((), ()))

        def score_tiles(p):
            qs = q_ref[pl.ds(tok, LANES), pl.ds(pl.multiple_of((2 * h + p) * kd, kd), kd)]
            s = lax.dot_general(keys_ref[2 * h + p], qs, nt, preferred_element_type=F32)
            return [s[i * SUBLANES:(i + 1) * SUBLANES, :] for i in range(ntile)]

        def top(tiles):
            v = list(tiles)
            for i, j in SORT_NET:
                _cmp_exchange(v, i, j)
            return _top_sorted(v)

        s1 = score_tiles(0)
        s2 = score_tiles(1)
        a = top(s1)
        b = top(s2)

        sub = lax.broadcasted_iota(jnp.int32, (SUBLANES, LANES), 0)
        pick = lambda lo, x, y: jnp.where(sub < lo, x, y)
        base = pick(4, pick(2, pick(1, a[0], a[1]), pick(3, a[2], a[3])), pick(5, b[0], pick(6, b[1], b[2])))
        cand = []
        for m in range(k):
            other = pick(4, b[m], a[4 + m]) if 4 + m < k else b[m]
            cand.append((other + base) + neg_ref[m])
        lists = list(cand)
        rolls = list(_sublane_rolls())
        for shift in rolls[:-1]:
            lists = _merge_top(lists, [pltpu.roll(x, shift, 0) for x in lists])
            _bitonic_sort(lists)
        lists = _merge_top(lists, [pltpu.roll(x, rolls[-1], 0) for x in lists])
        tau = functools.reduce(jnp.minimum, lists)

        best = a[0] + b[0]
        zsum = sum(jnp.where(c >= tau, jnp.exp(c - best), 0.0) for c in cand)
        for shift in rolls:
            zsum = zsum + pltpu.roll(zsum, shift, 0)
        scale = 0.5 / zsum
        top_extra = sum(jnp.where(a[0] + b[j] >= tau, 1.0, 0.0) for j in range(k // 2, k))

        c1_t, e1_t, r2_t, e2_t = [], [], [], []
        for t1, t2 in zip(s1, s2):
            c1 = jnp.zeros_like(t1)
            for j in range(k // 2):
                c1 = jnp.where(t1 + b[j] >= tau, float(j + 1), c1)
            c1_t.append(jnp.where(t1 == a[0], c1 + top_extra, c1))
            e1_t.append(jnp.exp(t1 - a[0]) * scale)
            r2 = jnp.zeros_like(t2)
            for j in range(k):
                r2 = jnp.where(t2 < b[j], float(j + 1), r2)
            r2_t.append(r2)
            e2_t.append(jnp.exp(t2 - b[0]))
        c1_ref[h, :, pl.ds(tok, LANES)] = jnp.concatenate(c1_t, axis=0)
        e1_ref[h, :, pl.ds(tok, LANES)] = jnp.concatenate(e1_t, axis=0)
        r2_ref[h, :, pl.ds(tok, LANES)] = jnp.concatenate(r2_t, axis=0).astype(BF16)
        e2_ref[h, :, pl.ds(tok, LANES)] = jnp.concatenate(e2_t, axis=0).astype(BF16)
        return carry

    lax.fori_loop(0, PEER_HEADS * nsub, body, 0, unroll=4)


def _candidate_mask():
    k = PEER_TOPK
    m = lax.broadcasted_iota(jnp.int32, (k, SUBLANES, LANES), 0)
    s = lax.broadcasted_iota(jnp.int32, (k, SUBLANES, LANES), 1)
    row_ok = (s < 4) & ((s + 1) * (m + 1) <= k)
    col_ok = (s >= 4) & (s < 7) & (4 + m < k) & ((5 + m) * (s - 3) <= k)
    return jnp.where(row_ok | col_ok, 0.0, -jnp.inf).astype(F32)


def _score(h2, wq_b, keys_b):
    t, d = h2.shape
    tm = min(SCORE_TILE, t)
    neg = _candidate_mask()
    big = lambda: pl.BlockSpec((PEER_HEADS, N_KEYS, tm), lambda j: (0, 0, j))
    big_shape = lambda dt: jax.ShapeDtypeStruct((PEER_HEADS, N_KEYS, t), dt)
    return pl.pallas_call(
        _score_kernel,
        grid=(t // tm,),
        in_specs=[pl.BlockSpec((tm, d), lambda j: (j, 0)),
                  pl.BlockSpec(wq_b.shape, lambda j: (0, 0)),
                  pl.BlockSpec(keys_b.shape, lambda j: (0, 0, 0)),
                  pl.BlockSpec(neg.shape, lambda j: (0, 0, 0))],
        out_specs=[big(), big(), big(), big()],
        out_shape=[big_shape(F32), big_shape(F32), big_shape(BF16), big_shape(BF16)],
        scratch_shapes=[pltpu.VMEM((tm, PEER_HEADS * PEER_KEY_DIM), BF16)],
        compiler_params=_params(("parallel",)),
        name="peer_score",
    )(h2, wq_b, keys_b, neg)


def _key_row_bf16(ref, h, key0, j):
    tile = ref[h, pl.ds(key0 + j // SUBLANES * SUBLANES, SUBLANES), :]
    row = tile[j % SUBLANES:j % SUBLANES + 1, :]
    rows = jnp.broadcast_to(row, (BF16_ROWS, ref.shape[-1])).astype(BF16)
    return jnp.tile(rows, (N_KEYS // BF16_ROWS, 1))


def _peer_kernel(ht_ref, u_ref, vt_ref, c1_ref, e1_ref, r2_ref, e2_ref, x1_ref, mod_ref, fnw_ref,
                 o_ref, acc_ref):
    eb = pl.program_id(1)
    nsub = PEER_EBLK // PEER_SUB
    keys_per_sub = PEER_SUB // N_KEYS
    keys_per_step = PEER_EBLK // N_KEYS
    key0 = pl.multiple_of(eb * keys_per_step, keys_per_step)

    @pl.when(eb == 0)
    def _():
        acc_ref[...] = jnp.zeros_like(acc_ref)

    def first_matmul(sub):
        return jnp.dot(u_ref[sub * PEER_SUB:(sub + 1) * PEER_SUB, :], ht_ref[...],
                       preferred_element_type=F32)

    def build_gates(sub):
        gates = []
        for kk in range(keys_per_sub):
            j = sub * keys_per_sub + kk
            g = jnp.zeros((N_KEYS, ht_ref.shape[1]), BF16)
            for h in range(PEER_HEADS):
                c1 = _key_row_bf16(c1_ref, h, key0, j)
                e1 = _key_row_bf16(e1_ref, h, key0, j)
                g = g + jnp.where(r2_ref[h] < c1, e2_ref[h], jnp.zeros((), BF16)) * e1
            gates.append(g)
        return jnp.concatenate(gates, axis=0)

    pre = first_matmul(0)
    weighted = []
    for sub in range(nsub):
        nxt = first_matmul(sub + 1) if sub + 1 < nsub else None
        act = pre * (1.0 + lax.erf(pre * (2.0 ** -0.5)))
        weighted.append(act.astype(BF16) * build_gates(sub))
        pre = nxt
    acc_ref[...] = jnp.dot(vt_ref[...], jnp.concatenate(weighted, axis=0),
                           preferred_element_type=F32) + acc_ref[...]

    @pl.when(eb == pl.num_programs(1) - 1)
    def _():
        x2 = x1_ref[...] + mod_ref[0, 5:6, :] * acc_ref[...].T
        o_ref[...] = x2 * lax.rsqrt(jnp.mean(x2 * x2, axis=-1, keepdims=True) + EPS) * fnw_ref[...]


def _peer(h2t, u_b, vt_b, c1, e1, r2, e2, x1, mod, fn_w, seq):
    d, t = h2t.shape
    tm = min(PEER_TILE, seq)
    n_exp = u_b.shape[0]
    big = lambda: pl.BlockSpec((PEER_HEADS, N_KEYS, tm), lambda j, e: (0, 0, j))
    return pl.pallas_call(
        _peer_kernel,
        grid=(t // tm, n_exp // PEER_EBLK),
        in_specs=[pl.BlockSpec((d, tm), lambda j, e: (0, j)),
                  pl.BlockSpec((PEER_EBLK, d), lambda j, e: (e, 0)),
                  pl.BlockSpec((None, d, PEER_EBLK), lambda j, e: (e, 0, 0)),
                  big(), big(), big(), big(),
                  pl.BlockSpec((tm, d), lambda j, e: (j, 0)),
                  pl.BlockSpec((1, 6, d), lambda j, e: ((j * tm) // seq, 0, 0)),
                  pl.BlockSpec((1, d), lambda j, e: (0, 0))],
        out_specs=pl.BlockSpec((tm, d), lambda j, e: (j, 0)),
        out_shape=jax.ShapeDtypeStruct((t, d), F32),
        scratch_shapes=[pltpu.VMEM((d, tm), F32)],
        compiler_params=_params(("parallel", "arbitrary")),
        name="peer_dense",
    )(h2t, u_b, vt_b, c1, e1, r2, e2, x1, mod, fn_w)


def _pad_lanes(v, width=LANES):
    flat = v.reshape(1, -1)
    return jnp.pad(flat, ((0, 0), (0, width - flat.shape[1])))


def _head_expand(col0):
    rows = lax.broadcasted_iota(jnp.int32, (LANES, D_SSM), 0)
    heads = lax.broadcasted_iota(jnp.int32, (LANES, D_SSM), 1) // SSM_HEADDIM
    return (rows == heads + col0).astype(BF16)


def kernel(x, c, ctx, c_ctx, ada_w, ada_b, norm1_w, norm2_w, w_in, ssm_conv_w, ssm_conv_b, ssm_dt_bias,
           ssm_a_log, ssm_d, ssm_norm_w, cfm_conv_w, cfm_conv_b, cfm_ln_w, cfm_ln_b, w_out, peer_wq,
           peer_subkeys, peer_u, peer_v, final_norm_w):
    depth = ada_w.shape[0]
    assert depth == 1, "single-layer configuration"
    b, l, d = x.shape
    i = 0

    c_rows = jnp.concatenate([c, c_ctx[None], jnp.zeros((2 * SUBLANES - b - 1, d), F32)], axis=0)
    mod_all = _ada(c_rows, ada_w[i], ada_b[i])
    mod = mod_all[:b].reshape(b, 6, d)
    mod_ctx = mod_all[b:b + 1].reshape(1, 6, d)

    wi = w_in[i]
    o_dt = D_SSM + XBC_DIM
    w_r = jnp.concatenate([wi[:, :o_dt], wi[:, o_dt + 2 * SSM_HEADS:], wi[:, o_dt:o_dt + 2 * SSM_HEADS],
                           jnp.zeros((d, LANES - 2 * SSM_HEADS), F32)], axis=1).astype(BF16)
    z_l, xbc_l, u_l, dt_l = _inproj(x, mod, True, norm1_w[i], w_r)
    _, xbc_c, _, dt_c = _inproj(ctx, mod_ctx, False, norm1_w[i], w_r)

    conv_w = jnp.pad(ssm_conv_w[i], ((0, SUBLANES - SSM_CONV), (0, 0)))
    conv_b = ssm_conv_b[i].reshape(1, XBC_DIM)
    dt_bias = _pad_lanes(ssm_dt_bias[i])
    a_log = _pad_lanes(ssm_a_log[i])
    d_skip = jnp.repeat(ssm_d[i], SSM_HEADDIM).reshape(1, D_SSM)
    zero_state = jnp.zeros((b, SSM_STATE, D_SSM), F32)
    y_dirs = []
    act_c = act_l = None
    for reverse in (False, True):
        expand = _head_expand(SSM_HEADS if reverse else 0)
        args = (conv_w, conv_b, dt_bias, a_log, d_skip, expand, reverse)
        if not reverse:
            _, h_ctx, act_c = _ssd(xbc_c, dt_c, zero_state, *args, add_skip=False, activated=False)
            y_dir, _, act_l = _ssd(xbc_l, dt_l, h_ctx, *args, add_skip=True, activated=False)
        else:
            _, h_ctx = _ssd(act_c, dt_c, zero_state, *args, add_skip=False, activated=True)
            y_dir, _ = _ssd(act_l, dt_l, h_ctx, *args, add_skip=False, activated=True)
        y_dirs.append(y_dir)

    cfm = _cfm(u_l, jnp.pad(cfm_conv_w[i], ((0, 1), (0, 0))), cfm_conv_b[i].reshape(1, D_CONV),
               cfm_ln_w[i].reshape(1, D_CONV), cfm_ln_b[i].reshape(1, D_CONV))

    x1, h2, h2t = _outproj(y_dirs[0], y_dirs[1], z_l, cfm, x, mod, ssm_norm_w[i].reshape(1, D_SSM),
                      w_out[i].astype(BF16), norm2_w[i].reshape(1, d))

    t = b * l
    h2 = h2.reshape(t, d)
    keys_b = peer_subkeys[i].reshape(PEER_HEADS * 2, N_KEYS, PEER_KEY_DIM // 2).astype(BF16)
    c1, e1, r2, e2 = _score(h2, peer_wq[i].astype(BF16), keys_b)
    vt_b = peer_v[i].reshape(-1, PEER_EBLK, d).transpose(0, 2, 1).astype(BF16)
    out = _peer(h2t, peer_u[i].astype(BF16), vt_b, c1, e1, r2, e2,
                x1.reshape(t, d), mod, final_norm_w.reshape(1, d), l)
    return out.reshape(b, l, d)
```

```python
import functools

import jax
import jax.numpy as jnp
from jax import lax
from jax.experimental import pallas as pl
from jax.experimental.pallas import tpu as pltpu

F32 = jnp.float32
BF16 = jnp.bfloat16

D_MODEL = 1024
GRID_W = 64
D_SSM = 512
SSM_HEADDIM = 64
SSM_HEADS = 8
SSM_STATE = 128
SSM_GROUPS = 2
SSM_CONV = 5
SSD_CHUNK = 128
XBC_DIM = D_SSM + 2 * SSM_GROUPS * SSM_STATE
D_CONV = 512
CONV_WIDTH = 31
PEER_HEADS = 8
PEER_KEY_DIM = 256
N_KEYS = 128
PEER_TOPK = 16
EPS = 1e-6

LANES = 128
SUBLANES = 8
VMEM_LIMIT = 56 * 1024 * 1024

IN_TILE = 512
SSD_ROWS = 4
CFM_ROWS = 16
OUT_TILE = 512
SCORE_TILE = 512
PEER_TILE = 512
PEER_EBLK = 2048
PEER_SUBS = (128,) * 16
assert sum(PEER_SUBS) == PEER_EBLK and all(s % N_KEYS == 0 for s in PEER_SUBS)
BF16_ROWS = 2 * SUBLANES


def _params(sem):
    return pltpu.CompilerParams(dimension_semantics=sem, vmem_limit_bytes=VMEM_LIMIT)


def _silu(v):
    return v * jax.nn.sigmoid(v)


def _split3(v):
    hi = v.astype(BF16)
    r1 = v - hi.astype(F32)
    mid = r1.astype(BF16)
    lo = (r1 - mid.astype(F32)).astype(BF16)
    return hi, mid, lo


def _dot01_right(v, m01):
    return sum(jnp.dot(p, m01, preferred_element_type=F32) for p in _split3(v))


def _dot01_left(m01, v):
    return sum(jnp.dot(m01, p, preferred_element_type=F32) for p in _split3(v))


def _ada_kernel(c_ref, w_ref, b_ref, o_ref):
    sc = _silu(c_ref[...])
    o_ref[...] = jnp.dot(sc.astype(BF16), w_ref[...].astype(BF16),
                         preferred_element_type=F32) + b_ref[...]


def _ada(c_rows, ada_w, ada_b):
    n = ada_w.shape[1]
    tn = 768
    return pl.pallas_call(
        _ada_kernel,
        grid=(n // tn,),
        in_specs=[pl.BlockSpec(c_rows.shape, lambda j: (0, 0)),
                  pl.BlockSpec((D_MODEL, tn), lambda j: (0, j)),
                  pl.BlockSpec((1, tn), lambda j: (0, j))],
        out_specs=pl.BlockSpec((c_rows.shape[0], tn), lambda j: (0, j)),
        out_shape=jax.ShapeDtypeStruct((c_rows.shape[0], n), F32),
        compiler_params=_params(("parallel",)),
        name="ada",
    )(c_rows, ada_w, ada_b.reshape(1, n))


def _inproj_kernel(x_ref, mod_ref, nw_ref, w_ref, z_ref, xbc_ref, u_ref, dt_ref):
    xv = x_ref[0]
    h = xv * lax.rsqrt(jnp.mean(xv * xv, axis=-1, keepdims=True) + EPS) * nw_ref[...]
    h = h * (1.0 + mod_ref[0, 1:2, :]) + mod_ref[0, 0:1, :]
    hb = h.astype(BF16)

    def proj(lo, hi):
        return jnp.dot(hb, w_ref[:, lo:hi], preferred_element_type=F32)

    z_ref[0] = proj(0, D_SSM)
    xbc_ref[0] = proj(D_SSM, D_SSM + XBC_DIM)
    o = D_SSM + XBC_DIM
    u_ref[0] = proj(o, o + D_CONV) * jax.nn.sigmoid(proj(o + D_CONV, o + 2 * D_CONV))
    dt_ref[0] = proj(o + 2 * D_CONV, o + 2 * D_CONV + LANES)


def _inproj(x, mod, per_batch_mod, norm_w, w_r):
    b, l, d = x.shape
    tm = min(IN_TILE, l)
    mod_idx = (lambda i, j: (i, 0, 0)) if per_batch_mod else (lambda i, j: (0, 0, 0))
    tok = lambda w: pl.BlockSpec((1, tm, w), lambda i, j: (i, j, 0))
    return pl.pallas_call(
        _inproj_kernel,
        grid=(b, l // tm),
        in_specs=[tok(d),
                  pl.BlockSpec((1, 6, d), mod_idx),
                  pl.BlockSpec((1, d), lambda i, j: (0, 0)),
                  pl.BlockSpec(w_r.shape, lambda i, j: (0, 0))],
        out_specs=[tok(D_SSM), tok(XBC_DIM), tok(D_CONV), tok(LANES)],
        out_shape=[jax.ShapeDtypeStruct((b, l, w), F32) for w in (D_SSM, XBC_DIM, D_CONV, LANES)],
        compiler_params=_params(("parallel", "parallel")),
        name="inproj",
    )(x, mod, norm_w.reshape(1, d), w_r)


def _ssd_kernel(cur_ref, prev_ref, next_ref, dt_ref, h0_ref, cw_ref, cb_ref, dtb_ref, alog_ref,
                dskip_ref, expand_ref, y_ref, hfin_ref, *rest, reverse, nc, add_skip, activated):
    act_ref, ext_ref, state_ref = (None, *rest) if activated else rest
    q = SSD_CHUNK
    halo = SUBLANES
    pad = (SSM_CONV - 1) // 2
    gw = D_SSM // SSM_GROUPS
    hpg = SSM_HEADS // SSM_GROUPS
    col0 = SSM_HEADS if reverse else 0
    c = pl.program_id(1)
    cc = (nc - 1 - c) if reverse else c

    @pl.when(c == 0)
    def _():
        state_ref[...] = h0_ref[...]

    ri = lax.broadcasted_iota(jnp.int32, (q, q), 0)
    ci = lax.broadcasted_iota(jnp.int32, (q, q), 1)
    causal = (ci >= ri) if reverse else (ci <= ri)
    causal01 = jnp.where(causal, 1.0, 0.0).astype(BF16)
    a_row = -jnp.exp(alog_ref[...])

    for r in range(cur_ref.shape[0]):
        if activated:
            act = cur_ref[r]
        else:
            ext_ref[r, 0:halo, :] = jnp.where(cc == 0, 0.0, prev_ref[r])
            ext_ref[r, halo:halo + q, :] = cur_ref[r]
            ext_ref[r, halo + q:2 * halo + q, :] = jnp.where(cc == nc - 1, 0.0, next_ref[r])
            conv = cb_ref[...] + cw_ref[0:1, :] * ext_ref[r, halo - pad:halo - pad + q, :]
            for k in range(1, SSM_CONV):
                conv = conv + cw_ref[k:k + 1, :] * ext_ref[r, halo - pad + k:halo - pad + k + q, :]
            act = _silu(conv)
            act_ref[r] = act

        dtv = dt_ref[r] + dtb_ref[...]
        dt = jnp.maximum(dtv, 0.0) + jnp.log(1.0 + jnp.exp(-jnp.abs(dtv)))
        acum = _dot01_left(causal01, dt * a_row)
        acum_t = acum.T
        tot = acum[0:1, :] if reverse else acum[q - 1:q, :]
        stacked = jnp.concatenate([dt, jnp.exp(tot - acum), jnp.exp(acum),
                                   jnp.broadcast_to(jnp.exp(tot), (SUBLANES, LANES))], axis=0)
        spread = _dot01_right(stacked, expand_ref[...])
        dt_e, dte_e, ea_e = spread[0:q], spread[q:2 * q], spread[2 * q:3 * q]
        etot_e = spread[3 * q:3 * q + 1]

        xs = act[:, :D_SSM]
        xdt = xs * dt_e
        xdt_b = xdt.astype(BF16)
        w_b = (xdt * dte_e).astype(BF16)
        for g in range(SSM_GROUPS):
            gs = slice(g * gw, (g + 1) * gw)
            b_g = act[:, D_SSM + g * SSM_STATE:D_SSM + (g + 1) * SSM_STATE]
            c_g = act[:, D_SSM + (SSM_GROUPS + g) * SSM_STATE:D_SSM + (SSM_GROUPS + g + 1) * SSM_STATE]
            c_b = c_g.astype(BF16)
            bt_b = b_g.T.astype(BF16)
            cb = jnp.dot(c_b, bt_b, preferred_element_type=F32)
            st = state_ref[r, :, gs]
            y_g = jnp.dot(c_b, st.astype(BF16), preferred_element_type=F32) * ea_e[:, gs]
            y_heads = []
            for hh in range(hpg):
                h = g * hpg + hh
                col = col0 + h
                diff = acum[:, col:col + 1] - acum_t[col:col + 1, :]
                decay = jnp.where(causal, jnp.exp(jnp.minimum(diff, 0.0)), 0.0)
                y_heads.append(jnp.dot((cb * decay).astype(BF16),
                                       xdt_b[:, h * SSM_HEADDIM:(h + 1) * SSM_HEADDIM],
                                       preferred_element_type=F32))
            y_g = y_g + jnp.concatenate(y_heads, axis=-1)
            if add_skip:
                y_g = y_g + dskip_ref[:, gs] * xs[:, gs]
            y_ref[r, :, gs] = y_g
            state_ref[r, :, gs] = st * etot_e[:, gs] + jnp.dot(bt_b, w_b[:, gs], preferred_element_type=F32)
    hfin_ref[...] = state_ref[...]


def _ssd(xbc, dtp, h0, conv_w, conv_b, dt_bias, a_log, d_skip, expand, reverse, add_skip, activated):
    b, l, _ = xbc.shape
    q = SSD_CHUNK
    nc = l // q
    nr = SSD_ROWS if b % SSD_ROWS == 0 else 1
    hb = q // SUBLANES
    pos = (lambda c: nc - 1 - c) if reverse else (lambda c: c)
    const = lambda shape: pl.BlockSpec(shape, lambda i, c: (0,) * len(shape))
    kern = functools.partial(_ssd_kernel, reverse=reverse, nc=nc, add_skip=add_skip, activated=activated)
    chunk = lambda w: pl.BlockSpec((nr, q, w), lambda i, c: (i, pos(c), 0))
    act_spec = [] if activated else [chunk(XBC_DIM)]
    act_shape = [] if activated else [jax.ShapeDtypeStruct((b, l, XBC_DIM), F32)]
    return pl.pallas_call(
        kern,
        grid=(b // nr, nc),
        in_specs=[pl.BlockSpec((nr, q, XBC_DIM), lambda i, c: (i, pos(c), 0)),
                  pl.BlockSpec((nr, SUBLANES, XBC_DIM), lambda i, c: (i, jnp.maximum(pos(c) * hb - 1, 0), 0)),
                  pl.BlockSpec((nr, SUBLANES, XBC_DIM),
                               lambda i, c: (i, jnp.minimum((pos(c) + 1) * hb, l // SUBLANES - 1), 0)),
                  pl.BlockSpec((nr, q, LANES), lambda i, c: (i, pos(c), 0)),
                  pl.BlockSpec((nr, SSM_STATE, D_SSM), lambda i, c: (i, 0, 0)),
                  const(conv_w.shape), const(conv_b.shape), const(dt_bias.shape), const(a_log.shape),
                  const(d_skip.shape), const(expand.shape)],
        out_specs=[chunk(D_SSM), pl.BlockSpec((nr, SSM_STATE, D_SSM), lambda i, c: (i, 0, 0))] + act_spec,
        out_shape=[jax.ShapeDtypeStruct((b, l, D_SSM), F32),
                   jax.ShapeDtypeStruct((b, SSM_STATE, D_SSM), F32)] + act_shape,
        scratch_shapes=[pltpu.VMEM((nr, q + 2 * SUBLANES, XBC_DIM), F32),
                        pltpu.VMEM((nr, SSM_STATE, D_SSM), F32)],
        compiler_params=_params(("parallel", "arbitrary")),
        name="ssd_bwd" if reverse else "ssd_fwd",
    )(xbc, xbc, xbc, dtp, h0, conv_w, conv_b, dt_bias, a_log, d_skip, expand)


def _cfm_kernel(cur_ref, prev_ref, next_ref, w_ref, b_ref, lnw_ref, lnb_ref, o_ref, hpad_ref, vbuf_ref,
                shift_ref, *, nb):
    rows = CFM_ROWS
    band = rows * GRID_W
    half = D_CONV // 2
    pad = (CONV_WIDTH - 1) // 2
    lead = 2 * SUBLANES
    r = pl.program_id(1)

    zeros = jnp.zeros((rows, lead, half), F32)
    hpad_ref[:, 0:lead, :] = zeros
    hpad_ref[:, lead + GRID_W:2 * lead + GRID_W, :] = zeros
    hpad_ref[:, lead:lead + GRID_W, :] = cur_ref[0, :, 0:half].reshape(rows, GRID_W, half)
    vbuf_ref[0:band, :] = jnp.where(r == 0, 0.0, prev_ref[0])
    vbuf_ref[band:2 * band, :] = cur_ref[0, :, half:D_CONV]
    vbuf_ref[2 * band:3 * band, :] = jnp.where(r == nb - 1, 0.0, next_ref[0])

    span = shift_ref.shape[1]

    def row_body(i, carry):
        for s in range(SUBLANES):
            shift_ref[s] = hpad_ref[i, s:s + span, :]
        acc_h = jnp.zeros((GRID_W, half), F32)
        acc_v = jnp.zeros((GRID_W, half), F32)
        for k in range(CONV_WIDTH):
            first = lead - pad + k
            tile0 = first // SUBLANES * SUBLANES
            acc_h = acc_h + w_ref[k:k + 1, 0:half] * shift_ref[first % SUBLANES, tile0:tile0 + GRID_W, :]
            start = pl.multiple_of(band + (i + k - pad) * GRID_W, GRID_W)
            acc_v = acc_v + w_ref[k:k + 1, half:D_CONV] * vbuf_ref[pl.ds(start, GRID_W), :]
        conv = jnp.concatenate([acc_h, acc_v], axis=-1) + b_ref[...]
        mu = jnp.mean(conv, axis=-1, keepdims=True)
        cen = conv - mu
        var = jnp.mean(cen * cen, axis=-1, keepdims=True)
        o_ref[0, pl.ds(pl.multiple_of(i * GRID_W, GRID_W), GRID_W), :] = _silu(
            cen * lax.rsqrt(var + EPS) * lnw_ref[...] + lnb_ref[...]).astype(BF16)
        return carry

    lax.fori_loop(0, rows, row_body, 0)


def _cfm(u, conv_w, conv_b, ln_w, ln_b):
    b, l, _ = u.shape
    band = CFM_ROWS * GRID_W
    nb = l // band
    half = D_CONV // 2
    const = lambda shape: pl.BlockSpec(shape, lambda i, r: (0,) * len(shape))
    return pl.pallas_call(
        functools.partial(_cfm_kernel, nb=nb),
        grid=(b, nb),
        in_specs=[pl.BlockSpec((1, band, D_CONV), lambda i, r: (i, r, 0)),
                  pl.BlockSpec((1, band, half), lambda i, r: (i, jnp.maximum(r - 1, 0), 1)),
                  pl.BlockSpec((1, band, half), lambda i, r: (i, jnp.minimum(r + 1, nb - 1), 1)),
                  const(conv_w.shape), const(conv_b.shape), const(ln_w.shape), const(ln_b.shape)],
        out_specs=pl.BlockSpec((1, band, D_CONV), lambda i, r: (i, r, 0)),
        out_shape=jax.ShapeDtypeStruct((b, l, D_CONV), BF16),
        scratch_shapes=[pltpu.VMEM((CFM_ROWS, GRID_W + 4 * SUBLANES, half), F32),
                        pltpu.VMEM((3 * band, half), F32),
                        pltpu.VMEM((SUBLANES, GRID_W + 3 * SUBLANES, half), F32)],
        compiler_params=_params(("parallel", "parallel")),
        name="cfm",
    )(u, u, u, conv_w, conv_b, ln_w, ln_b)


def _outproj_kernel(yf_ref, yb_ref, z_ref, cfm_ref, x_ref, mod_ref, gnw_ref, w_ref, n2w_ref,
                    x1_ref, h2_ref, h2t_ref):
    gw = D_SSM // SSM_GROUPS
    g = (yf_ref[0] + yb_ref[0]) * _silu(z_ref[0])
    parts = []
    for k in range(SSM_GROUPS):
        gk = g[:, k * gw:(k + 1) * gw]
        parts.append(gk * lax.rsqrt(jnp.mean(gk * gk, axis=-1, keepdims=True) + EPS))
    ssm = jnp.concatenate(parts, axis=-1) * gnw_ref[...]
    mixed = (jnp.dot(ssm.astype(BF16), w_ref[0:D_SSM, :], preferred_element_type=F32)
             + jnp.dot(cfm_ref[0], w_ref[D_SSM:, :], preferred_element_type=F32))
    x1 = x_ref[0] + mod_ref[0, 2:3, :] * mixed
    x1_ref[0] = x1
    h = x1 * lax.rsqrt(jnp.mean(x1 * x1, axis=-1, keepdims=True) + EPS) * n2w_ref[...]
    h2 = h * (1.0 + mod_ref[0, 4:5, :]) + mod_ref[0, 3:4, :]
    h2_ref[0] = h2.astype(BF16)
    h2t_ref[...] = h2.T.astype(BF16)


def _outproj(yf, yb, z, cfm, x, mod, gn_w, w_out_b, n2_w):
    b, l, d = x.shape
    tm = min(OUT_TILE, l)
    tok = lambda w: pl.BlockSpec((1, tm, w), lambda i, j: (i, j, 0))
    const = lambda shape: pl.BlockSpec(shape, lambda i, j: (0,) * len(shape))
    return pl.pallas_call(
        _outproj_kernel,
        grid=(b, l // tm),
        in_specs=[tok(D_SSM), tok(D_SSM), tok(D_SSM), tok(D_CONV), tok(d),
                  pl.BlockSpec((1, 6, d), lambda i, j: (i, 0, 0)),
                  const(gn_w.shape), const(w_out_b.shape), const(n2_w.shape)],
        out_specs=[tok(d), tok(d), pl.BlockSpec((d, tm), lambda i, j: (0, i * (l // tm) + j))],
        out_shape=[jax.ShapeDtypeStruct((b, l, d), F32), jax.ShapeDtypeStruct((b, l, d), BF16),
                   jax.ShapeDtypeStruct((d, b * l), BF16)],
        compiler_params=_params(("parallel", "parallel")),
        name="outproj",
    )(yf, yb, z, cfm, x, mod, gn_w, w_out_b, n2_w)


def _oddeven_merge(lo, hi, r):
    step = r * 2
    if step < hi - lo:
        yield from _oddeven_merge(lo, hi, step)
        yield from _oddeven_merge(lo + r, hi, step)
        yield from [(i, i + r) for i in range(lo + r, hi - r, step)]
    else:
        yield (lo, lo + r)


def _oddeven_sort(lo, hi):
    if hi > lo:
        mid = lo + (hi - lo) // 2
        yield from _oddeven_sort(lo, mid)
        yield from _oddeven_sort(mid + 1, hi)
        yield from _oddeven_merge(lo, hi, 1)


SORT_NET = tuple(_oddeven_sort(0, PEER_TOPK - 1))


def _cmp_exchange(v, i, j):
    v[i], v[j] = jnp.maximum(v[i], v[j]), jnp.minimum(v[i], v[j])


def _bitonic_sort(v):
    d = len(v) // 2
    while d:
        for i in range(len(v)):
            if not i & d:
                _cmp_exchange(v, i, i + d)
        d //= 2


def _merge_top(v, w):
    n = len(v)
    return [jnp.maximum(v[i], w[n - 1 - i]) for i in range(n)]


def _sublane_rolls():
    shift = SUBLANES // 2
    while shift:
        yield shift
        shift //= 2


def _top_sorted(v):
    for shift in _sublane_rolls():
        v = _merge_top(v, [pltpu.roll(x, shift, 0) for x in v])
        _bitonic_sort(v)
    return v


def _score_kernel(h_ref, wq_ref, keys_ref, neg_ref, c1_ref, e1_ref, r2_ref, e2_ref, q_ref):
    tm = h_ref.shape[0]
    nsub = tm // LANES
    kd = PEER_KEY_DIM // 2
    k = PEER_TOPK
    ntile = N_KEYS // SUBLANES
    q_ref[...] = jnp.dot(h_ref[...], wq_ref[...], preferred_element_type=F32).astype(BF16)

    def body(it, carry):
        h = it // nsub
        tok = pl.multiple_of((it % nsub) * LANES, LANES)
        nt = (((1,), (1,)), ((), ()))

        def score_tiles(p):
            qs = q_ref[pl.ds(tok, LANES), pl.ds(pl.multiple_of((2 * h + p) * kd, kd), kd)]
            s = lax.dot_general(keys_ref[2 * h + p], qs, nt, preferred_element_type=F32)
            return [s[i * SUBLANES:(i + 1) * SUBLANES, :] for i in range(ntile)]

        def top(tiles):
            v = list(tiles)
            for i, j in SORT_NET:
                _cmp_exchange(v, i, j)
            return _top_sorted(v)

        s1 = score_tiles(0)
        s2 = score_tiles(1)
        a = top(s1)
        b = top(s2)

        sub = lax.broadcasted_iota(jnp.int32, (SUBLANES, LANES), 0)
        pick = lambda lo, x, y: jnp.where(sub < lo, x, y)
        base = pick(4, pick(2, pick(1, a[0], a[1]), pick(3, a[2], a[3])), pick(5, b[0], pick(6, b[1], b[2])))
        cand = []
        for m in range(k):
            other = pick(4, b[m], a[4 + m]) if 4 + m < k else b[m]
            cand.append((other + base) + neg_ref[m])
        lists = list(cand)
        rolls = list(_sublane_rolls())
        for shift in rolls[:-1]:
            lists = _merge_top(lists, [pltpu.roll(x, shift, 0) for x in lists])
            _bitonic_sort(lists)
        lists = _merge_top(lists, [pltpu.roll(x, rolls[-1], 0) for x in lists])
        tau = functools.reduce(jnp.minimum, lists)

        best = a[0] + b[0]
        zsum = sum(jnp.where(c >= tau, jnp.exp(c - best), 0.0) for c in cand)
        for shift in rolls:
            zsum = zsum + pltpu.roll(zsum, shift, 0)
        scale = 0.5 / zsum
        top_extra = sum(jnp.where(a[0] + b[j] >= tau, 1.0, 0.0) for j in range(k // 2, k))

        c1_t, e1_t, r2_t, e2_t = [], [], [], []
        for t1, t2 in zip(s1, s2):
            c1 = jnp.zeros_like(t1)
            for j in range(k // 2):
                c1 = jnp.where(t1 + b[j] >= tau, float(j + 1), c1)
            c1_t.append(jnp.where(t1 == a[0], c1 + top_extra, c1))
            e1_t.append(jnp.exp(t1 - a[0]) * scale)
            r2 = jnp.zeros_like(t2)
            for j in range(k):
                r2 = jnp.where(t2 < b[j], float(j + 1), r2)
            r2_t.append(r2)
            e2_t.append(jnp.exp(t2 - b[0]))
        c1_ref[h, :, pl.ds(tok, LANES)] = jnp.concatenate(c1_t, axis=0)
        e1_ref[h, :, pl.ds(tok, LANES)] = jnp.concatenate(e1_t, axis=0)
        r2_ref[h, :, pl.ds(tok, LANES)] = jnp.concatenate(r2_t, axis=0).astype(BF16)
        e2_ref[h, :, pl.ds(tok, LANES)] = jnp.concatenate(e2_t, axis=0).astype(BF16)
        return carry

    lax.fori_loop(0, PEER_HEADS * nsub, body, 0, unroll=4)


def _candidate_mask():
    k = PEER_TOPK
    m = lax.broadcasted_iota(jnp.int32, (k, SUBLANES, LANES), 0)
    s = lax.broadcasted_iota(jnp.int32, (k, SUBLANES, LANES), 1)
    row_ok = (s < 4) & ((s + 1) * (m + 1) <= k)
    col_ok = (s >= 4) & (s < 7) & (4 + m < k) & ((5 + m) * (s - 3) <= k)
    return jnp.where(row_ok | col_ok, 0.0, -jnp.inf).astype(F32)


def _score(h2, wq_b, keys_b):
    t, d = h2.shape
    tm = min(SCORE_TILE, t)
    neg = _candidate_mask()
    big = lambda: pl.BlockSpec((PEER_HEADS, N_KEYS, tm), lambda j: (0, 0, j))
    big_shape = lambda dt: jax.ShapeDtypeStruct((PEER_HEADS, N_KEYS, t), dt)
    return pl.pallas_call(
        _score_kernel,
        grid=(t // tm,),
        in_specs=[pl.BlockSpec((tm, d), lambda j: (j, 0)),
                  pl.BlockSpec(wq_b.shape, lambda j: (0, 0)),
                  pl.BlockSpec(keys_b.shape, lambda j: (0, 0, 0)),
                  pl.BlockSpec(neg.shape, lambda j: (0, 0, 0))],
        out_specs=[big(), big(), big(), big()],
        out_shape=[big_shape(F32), big_shape(F32), big_shape(BF16), big_shape(BF16)],
        scratch_shapes=[pltpu.VMEM((tm, PEER_HEADS * PEER_KEY_DIM), BF16)],
        compiler_params=_params(("parallel",)),
        name="peer_score",
    )(h2, wq_b, keys_b, neg)


def _key_row_bf16(ref, h, key0, j):
    tile = ref[h, pl.ds(key0 + j // SUBLANES * SUBLANES, SUBLANES), :]
    row = tile[j % SUBLANES:j % SUBLANES + 1, :]
    rows = jnp.broadcast_to(row, (BF16_ROWS, ref.shape[-1])).astype(BF16)
    return jnp.tile(rows, (N_KEYS // BF16_ROWS, 1))


def _peer_kernel(ht_ref, u_ref, vt_ref, c1_ref, e1_ref, r2_ref, e2_ref, x1_ref, mod_ref, fnw_ref,
                 o_ref, acc_ref):
    eb = pl.program_id(1)
    keys_per_step = PEER_EBLK // N_KEYS
    key0 = pl.multiple_of(eb * keys_per_step, keys_per_step)
    starts = [sum(PEER_SUBS[:i]) for i in range(len(PEER_SUBS))]

    @pl.when(eb == 0)
    def _():
        acc_ref[...] = jnp.zeros_like(acc_ref)

    def first_matmul(sub):
        return jnp.dot(u_ref[starts[sub]:starts[sub] + PEER_SUBS[sub], :], ht_ref[...],
                       preferred_element_type=F32)

    def build_gates(sub):
        gates = []
        for kk in range(PEER_SUBS[sub] // N_KEYS):
            j = starts[sub] // N_KEYS + kk
            g = jnp.zeros((N_KEYS, ht_ref.shape[1]), BF16)
            for h in range(PEER_HEADS):
                c1 = _key_row_bf16(c1_ref, h, key0, j)
                e1 = _key_row_bf16(e1_ref, h, key0, j)
                g = g + jnp.where(r2_ref[h] < c1, e2_ref[h], jnp.zeros((), BF16)) * e1
            gates.append(g)
        return jnp.concatenate(gates, axis=0)

    nsub = len(PEER_SUBS)
    pre = first_matmul(0)
    weighted = []
    for sub in range(nsub):
        nxt = first_matmul(sub + 1) if sub + 1 < nsub else None
        act = pre * (1.0 + lax.erf(pre * (2.0 ** -0.5)))
        weighted.append(act.astype(BF16) * build_gates(sub))
        pre = nxt
    acc_ref[...] = jnp.dot(vt_ref[...], jnp.concatenate(weighted, axis=0),
                           preferred_element_type=F32) + acc_ref[...]

    @pl.when(eb == pl.num_programs(1) - 1)
    def _():
        x2 = x1_ref[...] + mod_ref[0, 5:6, :] * acc_ref[...].T
        o_ref[...] = x2 * lax.rsqrt(jnp.mean(x2 * x2, axis=-1, keepdims=True) + EPS) * fnw_ref[...]


def _peer(h2t, u_b, vt_b, c1, e1, r2, e2, x1, mod, fn_w, seq):
    d, t = h2t.shape
    tm = min(PEER_TILE, seq)
    n_exp = u_b.shape[0]
    big = lambda: pl.BlockSpec((PEER_HEADS, N_KEYS, tm), lambda j, e: (0, 0, j))
    return pl.pallas_call(
        _peer_kernel,
        grid=(t // tm, n_exp // PEER_EBLK),
        in_specs=[pl.BlockSpec((d, tm), lambda j, e: (0, j)),
                  pl.BlockSpec((PEER_EBLK, d), lambda j, e: (e, 0)),
                  pl.BlockSpec((None, d, PEER_EBLK), lambda j, e: (e, 0, 0)),
                  big(), big(), big(), big(),
                  pl.BlockSpec((tm, d), lambda j, e: (j, 0)),
                  pl.BlockSpec((1, 6, d), lambda j, e: ((j * tm) // seq, 0, 0)),
                  pl.BlockSpec((1, d), lambda j, e: (0, 0))],
        out_specs=pl.BlockSpec((tm, d), lambda j, e: (j, 0)),
        out_shape=jax.ShapeDtypeStruct((t, d), F32),
        scratch_shapes=[pltpu.VMEM((d, tm), F32)],
        compiler_params=_params(("parallel", "arbitrary")),
        name="peer_dense",
    )(h2t, u_b, vt_b, c1, e1, r2, e2, x1, mod, fn_w)


def _pad_lanes(v, width=LANES):
    flat = v.reshape(1, -1)
    return jnp.pad(flat, ((0, 0), (0, width - flat.shape[1])))


def _head_expand(col0):
    rows = lax.broadcasted_iota(jnp.int32, (LANES, D_SSM), 0)
    heads = lax.broadcasted_iota(jnp.int32, (LANES, D_SSM), 1) // SSM_HEADDIM
    return (rows == heads + col0).astype(BF16)


def kernel(x, c, ctx, c_ctx, ada_w, ada_b, norm1_w, norm2_w, w_in, ssm_conv_w, ssm_conv_b, ssm_dt_bias,
           ssm_a_log, ssm_d, ssm_norm_w, cfm_conv_w, cfm_conv_b, cfm_ln_w, cfm_ln_b, w_out, peer_wq,
           peer_subkeys, peer_u, peer_v, final_norm_w):
    depth = ada_w.shape[0]
    assert depth == 1, "single-layer configuration"
    b, l, d = x.shape
    i = 0

    c_rows = jnp.concatenate([c, c_ctx[None], jnp.zeros((2 * SUBLANES - b - 1, d), F32)], axis=0)
    mod_all = _ada(c_rows, ada_w[i], ada_b[i])
    mod = mod_all[:b].reshape(b, 6, d)
    mod_ctx = mod_all[b:b + 1].reshape(1, 6, d)

    wi = w_in[i]
    o_dt = D_SSM + XBC_DIM
    w_r = jnp.concatenate([wi[:, :o_dt], wi[:, o_dt + 2 * SSM_HEADS:], wi[:, o_dt:o_dt + 2 * SSM_HEADS],
                           jnp.zeros((d, LANES - 2 * SSM_HEADS), F32)], axis=1).astype(BF16)
    z_l, xbc_l, u_l, dt_l = _inproj(x, mod, True, norm1_w[i], w_r)
    _, xbc_c, _, dt_c = _inproj(ctx, mod_ctx, False, norm1_w[i], w_r)

    conv_w = jnp.pad(ssm_conv_w[i], ((0, SUBLANES - SSM_CONV), (0, 0)))
    conv_b = ssm_conv_b[i].reshape(1, XBC_DIM)
    dt_bias = _pad_lanes(ssm_dt_bias[i])
    a_log = _pad_lanes(ssm_a_log[i])
    d_skip = jnp.repeat(ssm_d[i], SSM_HEADDIM).reshape(1, D_SSM)
    zero_state = jnp.zeros((b, SSM_STATE, D_SSM), F32)
    y_dirs = []
    act_c = act_l = None
    for reverse in (False, True):
        expand = _head_expand(SSM_HEADS if reverse else 0)
        args = (conv_w, conv_b, dt_bias, a_log, d_skip, expand, reverse)
        if not reverse:
            _, h_ctx, act_c = _ssd(xbc_c, dt_c, zero_state, *args, add_skip=False, activated=False)
            y_dir, _, act_l = _ssd(xbc_l, dt_l, h_ctx, *args, add_skip=True, activated=False)
        else:
            _, h_ctx = _ssd(act_c, dt_c, zero_state, *args, add_skip=False, activated=True)
            y_dir, _ = _ssd(act_l, dt_l, h_ctx, *args, add_skip=False, activated=True)
        y_dirs.append(y_dir)

    cfm = _cfm(u_l, jnp.pad(cfm_conv_w[i], ((0, 1), (0, 0))), cfm_conv_b[i].reshape(1, D_CONV),
               cfm_ln_w[i].reshape(1, D_CONV), cfm_ln_b[i].reshape(1, D_CONV))

    x1, h2, h2t = _outproj(y_dirs[0], y_dirs[1], z_l, cfm, x, mod, ssm_norm_w[i].reshape(1, D_SSM),
                      w_out[i].astype(BF16), norm2_w[i].reshape(1, d))

    t = b * l
    h2 = h2.reshape(t, d)
    keys_b = peer_subkeys[i].reshape(PEER_HEADS * 2, N_KEYS, PEER_KEY_DIM // 2).astype(BF16)
    c1, e1, r2, e2 = _score(h2, peer_wq[i].astype(BF16), keys_b)
    vt_b = peer_v[i].reshape(-1, PEER_EBLK, d).transpose(0, 2, 1).astype(BF16)
    out = _peer(h2t, peer_u[i].astype(BF16), vt_b, c1, e1, r2, e2,
                x1.reshape(t, d), mod, final_norm_w.reshape(1, d), l)
    return out.reshape(b, l, d)
```

```python
import functools

import jax
import jax.numpy as jnp
from jax import lax
from jax.experimental import pallas as pl
from jax.experimental.pallas import tpu as pltpu

F32 = jnp.float32
BF16 = jnp.bfloat16

D_MODEL = 1024
GRID_W = 64
D_SSM = 512
SSM_HEADDIM = 64
SSM_HEADS = 8
SSM_STATE = 128
SSM_GROUPS = 2
SSM_CONV = 5
SSD_CHUNK = 128
XBC_DIM = D_SSM + 2 * SSM_GROUPS * SSM_STATE
D_CONV = 512
CONV_WIDTH = 31
PEER_HEADS = 8
PEER_KEY_DIM = 256
N_KEYS = 128
PEER_TOPK = 16
EPS = 1e-6

LANES = 128
SUBLANES = 8
VMEM_LIMIT = 56 * 1024 * 1024

IN_TILE = 512
SSD_ROWS = 8
CFM_ROWS = 16
OUT_TILE = 512
PEER_TILE = 512
PEER_EBLK = 2048
PEER_SUBS = (128,) * 16
assert sum(PEER_SUBS) == PEER_EBLK and all(s % N_KEYS == 0 for s in PEER_SUBS)
BF16_ROWS = 2 * SUBLANES


def _params(sem):
    return pltpu.CompilerParams(dimension_semantics=sem, vmem_limit_bytes=VMEM_LIMIT)


def _silu(v):
    return v * jax.nn.sigmoid(v)


def _split3(v):
    hi = v.astype(BF16)
    r1 = v - hi.astype(F32)
    mid = r1.astype(BF16)
    lo = (r1 - mid.astype(F32)).astype(BF16)
    return hi, mid, lo


def _dot01_right(v, m01):
    return sum(jnp.dot(p, m01, preferred_element_type=F32) for p in _split3(v))


def _dot01_left(m01, v):
    return sum(jnp.dot(m01, p, preferred_element_type=F32) for p in _split3(v))


def _ada_kernel(c_ref, w_ref, b_ref, o_ref):
    sc = _silu(c_ref[...])
    o_ref[...] = jnp.dot(sc.astype(BF16), w_ref[...].astype(BF16),
                         preferred_element_type=F32) + b_ref[...]


def _ada(c_rows, ada_w, ada_b):
    n = ada_w.shape[1]
    tn = 768
    return pl.pallas_call(
        _ada_kernel,
        grid=(n // tn,),
        in_specs=[pl.BlockSpec(c_rows.shape, lambda j: (0, 0)),
                  pl.BlockSpec((D_MODEL, tn), lambda j: (0, j)),
                  pl.BlockSpec((1, tn), lambda j: (0, j))],
        out_specs=pl.BlockSpec((c_rows.shape[0], tn), lambda j: (0, j)),
        out_shape=jax.ShapeDtypeStruct((c_rows.shape[0], n), F32),
        compiler_params=_params(("parallel",)),
        name="ada",
    )(c_rows, ada_w, ada_b.reshape(1, n))


def _inproj_kernel(x_ref, mod_ref, nw_ref, w_ref, z_ref, xbc_ref, u_ref, dt_ref):
    xv = x_ref[0]
    h = xv * lax.rsqrt(jnp.mean(xv * xv, axis=-1, keepdims=True) + EPS) * nw_ref[...]
    h = h * (1.0 + mod_ref[0, 1:2, :]) + mod_ref[0, 0:1, :]
    hb = h.astype(BF16)

    def proj(lo, hi):
        return jnp.dot(hb, w_ref[:, lo:hi], preferred_element_type=F32)

    z_ref[0] = proj(0, D_SSM)
    xbc_ref[0] = proj(D_SSM, D_SSM + XBC_DIM)
    o = D_SSM + XBC_DIM
    u_ref[0] = proj(o, o + D_CONV) * jax.nn.sigmoid(proj(o + D_CONV, o + 2 * D_CONV))
    dt_ref[0] = proj(o + 2 * D_CONV, o + 2 * D_CONV + LANES)


def _inproj(x, mod, per_batch_mod, norm_w, w_r):
    b, l, d = x.shape
    tm = min(IN_TILE, l)
    mod_idx = (lambda i, j: (i, 0, 0)) if per_batch_mod else (lambda i, j: (0, 0, 0))
    tok = lambda w: pl.BlockSpec((1, tm, w), lambda i, j: (i, j, 0))
    return pl.pallas_call(
        _inproj_kernel,
        grid=(b, l // tm),
        in_specs=[tok(d),
                  pl.BlockSpec((1, 6, d), mod_idx),
                  pl.BlockSpec((1, d), lambda i, j: (0, 0)),
                  pl.BlockSpec(w_r.shape, lambda i, j: (0, 0))],
        out_specs=[tok(D_SSM), tok(XBC_DIM), tok(D_CONV), tok(LANES)],
        out_shape=[jax.ShapeDtypeStruct((b, l, w), F32) for w in (D_SSM, XBC_DIM, D_CONV, LANES)],
        compiler_params=_params(("parallel", "parallel")),
        name="inproj",
    )(x, mod, norm_w.reshape(1, d), w_r)


def _ssd_kernel(cur_ref, prev_ref, next_ref, dt_ref, h0_ref, cw_ref, cb_ref, dtb_ref, alog_ref,
                dskip_ref, expand_ref, y_ref, hfin_ref, *rest, reverse, nc, add_skip, activated):
    act_ref, ext_ref, state_ref = (None, *rest) if activated else rest
    q = SSD_CHUNK
    halo = SUBLANES
    pad = (SSM_CONV - 1) // 2
    gw = D_SSM // SSM_GROUPS
    hpg = SSM_HEADS // SSM_GROUPS
    col0 = SSM_HEADS if reverse else 0
    c = pl.program_id(1)
    cc = (nc - 1 - c) if reverse else c

    @pl.when(c == 0)
    def _():
        state_ref[...] = h0_ref[...]

    ri = lax.broadcasted_iota(jnp.int32, (q, q), 0)
    ci = lax.broadcasted_iota(jnp.int32, (q, q), 1)
    causal = (ci >= ri) if reverse else (ci <= ri)
    causal01 = jnp.where(causal, 1.0, 0.0).astype(BF16)
    a_row = -jnp.exp(alog_ref[...])

    for r in range(cur_ref.shape[0]):
        if activated:
            act = cur_ref[r]
        else:
            ext_ref[r, 0:halo, :] = jnp.where(cc == 0, 0.0, prev_ref[r])
            ext_ref[r, halo:halo + q, :] = cur_ref[r]
            ext_ref[r, halo + q:2 * halo + q, :] = jnp.where(cc == nc - 1, 0.0, next_ref[r])
            conv = cb_ref[...] + cw_ref[0:1, :] * ext_ref[r, halo - pad:halo - pad + q, :]
            for k in range(1, SSM_CONV):
                conv = conv + cw_ref[k:k + 1, :] * ext_ref[r, halo - pad + k:halo - pad + k + q, :]
            act = _silu(conv)
            act_ref[r] = act

        dtv = dt_ref[r] + dtb_ref[...]
        dt = jnp.maximum(dtv, 0.0) + jnp.log(1.0 + jnp.exp(-jnp.abs(dtv)))
        acum = _dot01_left(causal01, dt * a_row)
        acum_t = acum.T
        tot = acum[0:1, :] if reverse else acum[q - 1:q, :]
        stacked = jnp.concatenate([dt, jnp.exp(tot - acum), jnp.exp(acum),
                                   jnp.broadcast_to(jnp.exp(tot), (SUBLANES, LANES))], axis=0)
        spread = _dot01_right(stacked, expand_ref[...])
        dt_e, dte_e, ea_e = spread[0:q], spread[q:2 * q], spread[2 * q:3 * q]
        etot_e = spread[3 * q:3 * q + 1]

        xs = act[:, :D_SSM]
        xdt = xs * dt_e
        xdt_b = xdt.astype(BF16)
        w_b = (xdt * dte_e).astype(BF16)
        for g in range(SSM_GROUPS):
            gs = slice(g * gw, (g + 1) * gw)
            b_g = act[:, D_SSM + g * SSM_STATE:D_SSM + (g + 1) * SSM_STATE]
            c_g = act[:, D_SSM + (SSM_GROUPS + g) * SSM_STATE:D_SSM + (SSM_GROUPS + g + 1) * SSM_STATE]
            c_b = c_g.astype(BF16)
            bt_b = b_g.T.astype(BF16)
            cb = jnp.dot(c_b, bt_b, preferred_element_type=F32)
            st = state_ref[r, :, gs]
            y_g = jnp.dot(c_b, st.astype(BF16), preferred_element_type=F32) * ea_e[:, gs]
            y_heads = []
            for hh in range(hpg):
                h = g * hpg + hh
                col = col0 + h
                diff = acum[:, col:col + 1] - acum_t[col:col + 1, :]
                decay = jnp.where(causal, jnp.exp(jnp.minimum(diff, 0.0)), 0.0)
                y_heads.append(jnp.dot((cb * decay).astype(BF16),
                                       xdt_b[:, h * SSM_HEADDIM:(h + 1) * SSM_HEADDIM],
                                       preferred_element_type=F32))
            y_g = y_g + jnp.concatenate(y_heads, axis=-1)
            if add_skip:
                y_g = y_g + dskip_ref[:, gs] * xs[:, gs]
            y_ref[r, :, gs] = y_g
            state_ref[r, :, gs] = st * etot_e[:, gs] + jnp.dot(bt_b, w_b[:, gs], preferred_element_type=F32)
    hfin_ref[...] = state_ref[...]


def _ssd(xbc, dtp, h0, conv_w, conv_b, dt_bias, a_log, d_skip, expand, reverse, add_skip, activated):
    b, l, _ = xbc.shape
    q = SSD_CHUNK
    nc = l // q
    nr = SSD_ROWS if b % SSD_ROWS == 0 else 1
    hb = q // SUBLANES
    pos = (lambda c: nc - 1 - c) if reverse else (lambda c: c)
    const = lambda shape: pl.BlockSpec(shape, lambda i, c: (0,) * len(shape))
    kern = functools.partial(_ssd_kernel, reverse=reverse, nc=nc, add_skip=add_skip, activated=activated)
    chunk = lambda w: pl.BlockSpec((nr, q, w), lambda i, c: (i, pos(c), 0))
    act_spec = [] if activated else [chunk(XBC_DIM)]
    act_shape = [] if activated else [jax.ShapeDtypeStruct((b, l, XBC_DIM), F32)]
    return pl.pallas_call(
        kern,
        grid=(b // nr, nc),
        in_specs=[pl.BlockSpec((nr, q, XBC_DIM), lambda i, c: (i, pos(c), 0)),
                  pl.BlockSpec((nr, SUBLANES, XBC_DIM), lambda i, c: (i, jnp.maximum(pos(c) * hb - 1, 0), 0)),
                  pl.BlockSpec((nr, SUBLANES, XBC_DIM),
                               lambda i, c: (i, jnp.minimum((pos(c) + 1) * hb, l // SUBLANES - 1), 0)),
                  pl.BlockSpec((nr, q, LANES), lambda i, c: (i, pos(c), 0)),
                  pl.BlockSpec((nr, SSM_STATE, D_SSM), lambda i, c: (i, 0, 0)),
                  const(conv_w.shape), const(conv_b.shape), const(dt_bias.shape), const(a_log.shape),
                  const(d_skip.shape), const(expand.shape)],
        out_specs=[chunk(D_SSM), pl.BlockSpec((nr, SSM_STATE, D_SSM), lambda i, c: (i, 0, 0))] + act_spec,
        out_shape=[jax.ShapeDtypeStruct((b, l, D_SSM), F32),
                   jax.ShapeDtypeStruct((b, SSM_STATE, D_SSM), F32)] + act_shape,
        scratch_shapes=[pltpu.VMEM((nr, q + 2 * SUBLANES, XBC_DIM), F32),
                        pltpu.VMEM((nr, SSM_STATE, D_SSM), F32)],
        compiler_params=_params(("parallel", "arbitrary")),
        name="ssd_bwd" if reverse else "ssd_fwd",
    )(xbc, xbc, xbc, dtp, h0, conv_w, conv_b, dt_bias, a_log, d_skip, expand)


def _cfm_kernel(cur_ref, prev_ref, next_ref, w_ref, b_ref, lnw_ref, lnb_ref, o_ref, hpad_ref, vbuf_ref,
                shift_ref, *, nb):
    rows = CFM_ROWS
    band = rows * GRID_W
    half = D_CONV // 2
    pad = (CONV_WIDTH - 1) // 2
    lead = 2 * SUBLANES
    r = pl.program_id(1)

    zeros = jnp.zeros((rows, lead, half), F32)
    hpad_ref[:, 0:lead, :] = zeros
    hpad_ref[:, lead + GRID_W:2 * lead + GRID_W, :] = zeros
    hpad_ref[:, lead:lead + GRID_W, :] = cur_ref[0, :, 0:half].reshape(rows, GRID_W, half)
    vbuf_ref[0:band, :] = jnp.where(r == 0, 0.0, prev_ref[0])
    vbuf_ref[band:2 * band, :] = cur_ref[0, :, half:D_CONV]
    vbuf_ref[2 * band:3 * band, :] = jnp.where(r == nb - 1, 0.0, next_ref[0])

    span = shift_ref.shape[1]

    def row_body(i, carry):
        for s in range(SUBLANES):
            shift_ref[s] = hpad_ref[i, s:s + span, :]
        acc_h = jnp.zeros((GRID_W, half), F32)
        acc_v = jnp.zeros((GRID_W, half), F32)
        for k in range(CONV_WIDTH):
            first = lead - pad + k
            tile0 = first // SUBLANES * SUBLANES
            acc_h = acc_h + w_ref[k:k + 1, 0:half] * shift_ref[first % SUBLANES, tile0:tile0 + GRID_W, :]
            start = pl.multiple_of(band + (i + k - pad) * GRID_W, GRID_W)
            acc_v = acc_v + w_ref[k:k + 1, half:D_CONV] * vbuf_ref[pl.ds(start, GRID_W), :]
        conv = jnp.concatenate([acc_h, acc_v], axis=-1) + b_ref[...]
        mu = jnp.mean(conv, axis=-1, keepdims=True)
        cen = conv - mu
        var = jnp.mean(cen * cen, axis=-1, keepdims=True)
        o_ref[0, pl.ds(pl.multiple_of(i * GRID_W, GRID_W), GRID_W), :] = _silu(
            cen * lax.rsqrt(var + EPS) * lnw_ref[...] + lnb_ref[...]).astype(BF16)
        return carry

    lax.fori_loop(0, rows, row_body, 0)


def _cfm(u, conv_w, conv_b, ln_w, ln_b):
    b, l, _ = u.shape
    band = CFM_ROWS * GRID_W
    nb = l // band
    half = D_CONV // 2
    const = lambda shape: pl.BlockSpec(shape, lambda i, r: (0,) * len(shape))
    return pl.pallas_call(
        functools.partial(_cfm_kernel, nb=nb),
        grid=(b, nb),
        in_specs=[pl.BlockSpec((1, band, D_CONV), lambda i, r: (i, r, 0)),
                  pl.BlockSpec((1, band, half), lambda i, r: (i, jnp.maximum(r - 1, 0), 1)),
                  pl.BlockSpec((1, band, half), lambda i, r: (i, jnp.minimum(r + 1, nb - 1), 1)),
                  const(conv_w.shape), const(conv_b.shape), const(ln_w.shape), const(ln_b.shape)],
        out_specs=pl.BlockSpec((1, band, D_CONV), lambda i, r: (i, r, 0)),
        out_shape=jax.ShapeDtypeStruct((b, l, D_CONV), BF16),
        scratch_shapes=[pltpu.VMEM((CFM_ROWS, GRID_W + 4 * SUBLANES, half), F32),
                        pltpu.VMEM((3 * band, half), F32),
                        pltpu.VMEM((SUBLANES, GRID_W + 3 * SUBLANES, half), F32)],
        compiler_params=_params(("parallel", "parallel")),
        name="cfm",
    )(u, u, u, conv_w, conv_b, ln_w, ln_b)


def _oddeven_merge(lo, hi, r):
    step = r * 2
    if step < hi - lo:
        yield from _oddeven_merge(lo, hi, step)
        yield from _oddeven_merge(lo + r, hi, step)
        yield from [(i, i + r) for i in range(lo + r, hi - r, step)]
    else:
        yield (lo, lo + r)


def _oddeven_sort(lo, hi):
    if hi > lo:
        mid = lo + (hi - lo) // 2
        yield from _oddeven_sort(lo, mid)
        yield from _oddeven_sort(mid + 1, hi)
        yield from _oddeven_merge(lo, hi, 1)


SORT_NET = tuple(_oddeven_sort(0, PEER_TOPK - 1))


def _cmp_exchange(v, i, j):
    v[i], v[j] = jnp.maximum(v[i], v[j]), jnp.minimum(v[i], v[j])


def _bitonic_sort(v):
    d = len(v) // 2
    while d:
        for i in range(len(v)):
            if not i & d:
                _cmp_exchange(v, i, i + d)
        d //= 2


def _merge_top(v, w):
    n = len(v)
    return [jnp.maximum(v[i], w[n - 1 - i]) for i in range(n)]


def _sublane_rolls():
    shift = SUBLANES // 2
    while shift:
        yield shift
        shift //= 2


def _top_sorted(v):
    for shift in _sublane_rolls():
        v = _merge_top(v, [pltpu.roll(x, shift, 0) for x in v])
        _bitonic_sort(v)
    return v


def _score_tokens(q_ref, keys_ref, neg_ref, c1_ref, e1_ref, r2_ref, e2_ref):
    tm = q_ref.shape[0]
    nsub = tm // LANES
    kd = PEER_KEY_DIM // 2
    k = PEER_TOPK
    ntile = N_KEYS // SUBLANES

    def body(it, carry):
        h = it // nsub
        tok = pl.multiple_of((it % nsub) * LANES, LANES)
        nt = (((1,), (1,)), ((), ()))

        def score_tiles(p):
            qs = q_ref[pl.ds(tok, LANES), pl.ds(pl.multiple_of((2 * h + p) * kd, kd), kd)]
            s = lax.dot_general(keys_ref[2 * h + p], qs, nt, preferred_element_type=F32)
            return [s[i * SUBLANES:(i + 1) * SUBLANES, :] for i in range(ntile)]

        def top(tiles):
            v = list(tiles)
            for i, j in SORT_NET:
                _cmp_exchange(v, i, j)
            return _top_sorted(v)

        s1 = score_tiles(0)
        s2 = score_tiles(1)
        a = top(s1)
        b = top(s2)

        sub = lax.broadcasted_iota(jnp.int32, (SUBLANES, LANES), 0)
        pick = lambda lo, x, y: jnp.where(sub < lo, x, y)
        base = pick(4, pick(2, pick(1, a[0], a[1]), pick(3, a[2], a[3])), pick(5, b[0], pick(6, b[1], b[2])))
        cand = []
        for m in range(k):
            other = pick(4, b[m], a[4 + m]) if 4 + m < k else b[m]
            cand.append((other + base) + neg_ref[m])
        lists = list(cand)
        rolls = list(_sublane_rolls())
        for shift in rolls[:-1]:
            lists = _merge_top(lists, [pltpu.roll(x, shift, 0) for x in lists])
            _bitonic_sort(lists)
        lists = _merge_top(lists, [pltpu.roll(x, rolls[-1], 0) for x in lists])
        tau = functools.reduce(jnp.minimum, lists)

        best = a[0] + b[0]
        zsum = sum(jnp.where(c >= tau, jnp.exp(c - best), 0.0) for c in cand)
        for shift in rolls:
            zsum = zsum + pltpu.roll(zsum, shift, 0)
        scale = 0.5 / zsum
        top_extra = sum(jnp.where(a[0] + b[j] >= tau, 1.0, 0.0) for j in range(k // 2, k))

        c1_t, e1_t, r2_t, e2_t = [], [], [], []
        for t1, t2 in zip(s1, s2):
            c1 = jnp.zeros_like(t1)
            for j in range(k // 2):
                c1 = jnp.where(t1 + b[j] >= tau, float(j + 1), c1)
            c1_t.append(jnp.where(t1 == a[0], c1 + top_extra, c1))
            e1_t.append(jnp.exp(t1 - a[0]) * scale)
            r2 = jnp.zeros_like(t2)
            for j in range(k):
                r2 = jnp.where(t2 < b[j], float(j + 1), r2)
            r2_t.append(r2)
            e2_t.append(jnp.exp(t2 - b[0]))
        c1_ref[h, :, pl.ds(tok, LANES)] = jnp.concatenate(c1_t, axis=0)
        e1_ref[h, :, pl.ds(tok, LANES)] = jnp.concatenate(e1_t, axis=0)
        r2_ref[h, :, pl.ds(tok, LANES)] = jnp.concatenate(r2_t, axis=0).astype(BF16)
        e2_ref[h, :, pl.ds(tok, LANES)] = jnp.concatenate(e2_t, axis=0).astype(BF16)
        return carry

    lax.fori_loop(0, PEER_HEADS * nsub, body, 0, unroll=4)


def _candidate_mask():
    k = PEER_TOPK
    m = lax.broadcasted_iota(jnp.int32, (k, SUBLANES, LANES), 0)
    s = lax.broadcasted_iota(jnp.int32, (k, SUBLANES, LANES), 1)
    row_ok = (s < 4) & ((s + 1) * (m + 1) <= k)
    col_ok = (s >= 4) & (s < 7) & (4 + m < k) & ((5 + m) * (s - 3) <= k)
    return jnp.where(row_ok | col_ok, 0.0, -jnp.inf).astype(F32)


def _outscore_kernel(yf_ref, yb_ref, z_ref, cfm_ref, x_ref, mod_ref, gnw_ref, w_ref, n2w_ref,
                     wq_ref, keys_ref, neg_ref,
                     x1_ref, h2t_ref, c1_ref, e1_ref, r2_ref, e2_ref, q_ref):
    gw = D_SSM // SSM_GROUPS
    g = (yf_ref[0] + yb_ref[0]) * _silu(z_ref[0])
    parts = []
    for k in range(SSM_GROUPS):
        gk = g[:, k * gw:(k + 1) * gw]
        parts.append(gk * lax.rsqrt(jnp.mean(gk * gk, axis=-1, keepdims=True) + EPS))
    ssm = jnp.concatenate(parts, axis=-1) * gnw_ref[...]
    mixed = (jnp.dot(ssm.astype(BF16), w_ref[0:D_SSM, :], preferred_element_type=F32)
             + jnp.dot(cfm_ref[0], w_ref[D_SSM:, :], preferred_element_type=F32))
    x1 = x_ref[0] + mod_ref[0, 2:3, :] * mixed
    x1_ref[0] = x1
    h = x1 * lax.rsqrt(jnp.mean(x1 * x1, axis=-1, keepdims=True) + EPS) * n2w_ref[...]
    h2 = h * (1.0 + mod_ref[0, 4:5, :]) + mod_ref[0, 3:4, :]
    h2t_ref[...] = h2.T.astype(BF16)
    q_ref[...] = jnp.dot(h2.astype(BF16), wq_ref[...], preferred_element_type=F32).astype(BF16)
    _score_tokens(q_ref, keys_ref, neg_ref, c1_ref, e1_ref, r2_ref, e2_ref)


def _outscore(yf, yb, z, cfm, x, mod, gn_w, w_out_b, n2_w, wq_b, keys_b):
    b, l, d = x.shape
    t = b * l
    tm = min(OUT_TILE, l)
    neg = _candidate_mask()
    tile = lambda i, j: i * (l // tm) + j
    tok = lambda w: pl.BlockSpec((1, tm, w), lambda i, j: (i, j, 0))
    const = lambda shape: pl.BlockSpec(shape, lambda i, j: (0,) * len(shape))
    big = lambda: pl.BlockSpec((PEER_HEADS, N_KEYS, tm), lambda i, j: (0, 0, tile(i, j)))
    big_shape = lambda dt: jax.ShapeDtypeStruct((PEER_HEADS, N_KEYS, t), dt)
    return pl.pallas_call(
        _outscore_kernel,
        grid=(b, l // tm),
        in_specs=[tok(D_SSM), tok(D_SSM), tok(D_SSM), tok(D_CONV), tok(d),
                  pl.BlockSpec((1, 6, d), lambda i, j: (i, 0, 0)),
                  const(gn_w.shape), const(w_out_b.shape), const(n2_w.shape),
                  const(wq_b.shape), const(keys_b.shape), const(neg.shape)],
        out_specs=[tok(d), pl.BlockSpec((d, tm), lambda i, j: (0, tile(i, j))), big(), big(), big(), big()],
        out_shape=[jax.ShapeDtypeStruct((b, l, d), F32), jax.ShapeDtypeStruct((d, t), BF16),
                   big_shape(F32), big_shape(F32), big_shape(BF16), big_shape(BF16)],
        scratch_shapes=[pltpu.VMEM((tm, PEER_HEADS * PEER_KEY_DIM), BF16)],
        compiler_params=_params(("parallel", "parallel")),
        name="outproj_score",
    )(yf, yb, z, cfm, x, mod, gn_w, w_out_b, n2_w, wq_b, keys_b, neg)


def _key_row_bf16(ref, h, key0, j):
    tile = ref[h, pl.ds(key0 + j // SUBLANES * SUBLANES, SUBLANES), :]
    row = tile[j % SUBLANES:j % SUBLANES + 1, :]
    rows = jnp.broadcast_to(row, (BF16_ROWS, ref.shape[-1])).astype(BF16)
    return jnp.tile(rows, (N_KEYS // BF16_ROWS, 1))


def _peer_kernel(ht_ref, u_ref, vt_ref, c1_ref, e1_ref, r2_ref, e2_ref, x1_ref, mod_ref, fnw_ref,
                 o_ref, acc_ref):
    eb = pl.program_id(1)
    keys_per_step = PEER_EBLK // N_KEYS
    key0 = pl.multiple_of(eb * keys_per_step, keys_per_step)
    starts = [sum(PEER_SUBS[:i]) for i in range(len(PEER_SUBS))]

    @pl.when(eb == 0)
    def _():
        acc_ref[...] = jnp.zeros_like(acc_ref)

    def first_matmul(sub):
        return jnp.dot(u_ref[starts[sub]:starts[sub] + PEER_SUBS[sub], :], ht_ref[...],
                       preferred_element_type=F32)

    def build_gates(sub):
        gates = []
        for kk in range(PEER_SUBS[sub] // N_KEYS):
            j = starts[sub] // N_KEYS + kk
            g = jnp.zeros((N_KEYS, ht_ref.shape[1]), BF16)
            for h in range(PEER_HEADS):
                c1 = _key_row_bf16(c1_ref, h, key0, j)
                e1 = _key_row_bf16(e1_ref, h, key0, j)
                g = g + jnp.where(r2_ref[h] < c1, e2_ref[h], jnp.zeros((), BF16)) * e1
            gates.append(g)
        return jnp.concatenate(gates, axis=0)

    nsub = len(PEER_SUBS)
    pre = first_matmul(0)
    weighted = []
    for sub in range(nsub):
        nxt = first_matmul(sub + 1) if sub + 1 < nsub else None
        act = pre * (1.0 + lax.erf(pre * (2.0 ** -0.5)))
        weighted.append(act.astype(BF16) * build_gates(sub))
        pre = nxt
    acc_ref[...] = jnp.dot(vt_ref[...], jnp.concatenate(weighted, axis=0),
                           preferred_element_type=F32) + acc_ref[...]

    @pl.when(eb == pl.num_programs(1) - 1)
    def _():
        x2 = x1_ref[...] + mod_ref[0, 5:6, :] * acc_ref[...].T
        o_ref[...] = x2 * lax.rsqrt(jnp.mean(x2 * x2, axis=-1, keepdims=True) + EPS) * fnw_ref[...]


def _peer(h2t, u_b, vt_b, c1, e1, r2, e2, x1, mod, fn_w, seq):
    d, t = h2t.shape
    tm = min(PEER_TILE, seq)
    n_exp = u_b.shape[0]
    big = lambda: pl.BlockSpec((PEER_HEADS, N_KEYS, tm), lambda j, e: (0, 0, j))
    return pl.pallas_call(
        _peer_kernel,
        grid=(t // tm, n_exp // PEER_EBLK),
        in_specs=[pl.BlockSpec((d, tm), lambda j, e: (0, j)),
                  pl.BlockSpec((PEER_EBLK, d), lambda j, e: (e, 0)),
                  pl.BlockSpec((None, d, PEER_EBLK), lambda j, e: (e, 0, 0)),
                  big(), big(), big(), big(),
                  pl.BlockSpec((tm, d), lambda j, e: (j, 0)),
                  pl.BlockSpec((1, 6, d), lambda j, e: ((j * tm) // seq, 0, 0)),
                  pl.BlockSpec((1, d), lambda j, e: (0, 0))],
        out_specs=pl.BlockSpec((tm, d), lambda j, e: (j, 0)),
        out_shape=jax.ShapeDtypeStruct((t, d), F32),
        scratch_shapes=[pltpu.VMEM((d, tm), F32)],
        compiler_params=_params(("parallel", "arbitrary")),
        name="peer_dense",
    )(h2t, u_b, vt_b, c1, e1, r2, e2, x1, mod, fn_w)


def _pad_lanes(v, width=LANES):
    flat = v.reshape(1, -1)
    return jnp.pad(flat, ((0, 0), (0, width - flat.shape[1])))


def _head_expand(col0):
    rows = lax.broadcasted_iota(jnp.int32, (LANES, D_SSM), 0)
    heads = lax.broadcasted_iota(jnp.int32, (LANES, D_SSM), 1) // SSM_HEADDIM
    return (rows == heads + col0).astype(BF16)


def kernel(x, c, ctx, c_ctx, ada_w, ada_b, norm1_w, norm2_w, w_in, ssm_conv_w, ssm_conv_b, ssm_dt_bias,
           ssm_a_log, ssm_d, ssm_norm_w, cfm_conv_w, cfm_conv_b, cfm_ln_w, cfm_ln_b, w_out, peer_wq,
           peer_subkeys, peer_u, peer_v, final_norm_w):
    depth = ada_w.shape[0]
    assert depth == 1, "single-layer configuration"
    b, l, d = x.shape
    i = 0

    c_rows = jnp.concatenate([c, c_ctx[None], jnp.zeros((2 * SUBLANES - b - 1, d), F32)], axis=0)
    mod_all = _ada(c_rows, ada_w[i], ada_b[i])
    mod = mod_all[:b].reshape(b, 6, d)
    mod_ctx = mod_all[b:b + 1].reshape(1, 6, d)

    wi = w_in[i]
    o_dt = D_SSM + XBC_DIM
    w_r = jnp.concatenate([wi[:, :o_dt], wi[:, o_dt + 2 * SSM_HEADS:], wi[:, o_dt:o_dt + 2 * SSM_HEADS],
                           jnp.zeros((d, LANES - 2 * SSM_HEADS), F32)], axis=1).astype(BF16)
    z_l, xbc_l, u_l, dt_l = _inproj(x, mod, True, norm1_w[i], w_r)
    _, xbc_c, _, dt_c = _inproj(ctx, mod_ctx, False, norm1_w[i], w_r)

    conv_w = jnp.pad(ssm_conv_w[i], ((0, SUBLANES - SSM_CONV), (0, 0)))
    conv_b = ssm_conv_b[i].reshape(1, XBC_DIM)
    dt_bias = _pad_lanes(ssm_dt_bias[i])
    a_log = _pad_lanes(ssm_a_log[i])
    d_skip = jnp.repeat(ssm_d[i], SSM_HEADDIM).reshape(1, D_SSM)
    zero_state = jnp.zeros((b, SSM_STATE, D_SSM), F32)
    y_dirs = []
    act_c = act_l = None
    for reverse in (False, True):
        expand = _head_expand(SSM_HEADS if reverse else 0)
        args = (conv_w, conv_b, dt_bias, a_log, d_skip, expand, reverse)
        if not reverse:
            _, h_ctx, act_c = _ssd(xbc_c, dt_c, zero_state, *args, add_skip=False, activated=False)
            y_dir, _, act_l = _ssd(xbc_l, dt_l, h_ctx, *args, add_skip=True, activated=False)
        else:
            _, h_ctx = _ssd(act_c, dt_c, zero_state, *args, add_skip=False, activated=True)
            y_dir, _ = _ssd(act_l, dt_l, h_ctx, *args, add_skip=False, activated=True)
        y_dirs.append(y_dir)

    cfm = _cfm(u_l, jnp.pad(cfm_conv_w[i], ((0, 1), (0, 0))), cfm_conv_b[i].reshape(1, D_CONV),
               cfm_ln_w[i].reshape(1, D_CONV), cfm_ln_b[i].reshape(1, D_CONV))

    t = b * l
    keys_b = peer_subkeys[i].reshape(PEER_HEADS * 2, N_KEYS, PEER_KEY_DIM // 2).astype(BF16)
    x1, h2t, c1, e1, r2, e2 = _outscore(y_dirs[0], y_dirs[1], z_l, cfm, x, mod,
                                        ssm_norm_w[i].reshape(1, D_SSM), w_out[i].astype(BF16),
                                        norm2_w[i].reshape(1, d), peer_wq[i].astype(BF16), keys_b)
    vt_b = peer_v[i].reshape(-1, PEER_EBLK, d).transpose(0, 2, 1).astype(BF16)
    out = _peer(h2t, peer_u[i].astype(BF16), vt_b, c1, e1, r2, e2,
                x1.reshape(t, d), mod, final_norm_w.reshape(1, d), l)
    return out.reshape(b, l, d)
```

```python
import functools

import jax
import jax.numpy as jnp
from jax import lax
from jax.experimental import pallas as pl
from jax.experimental.pallas import tpu as pltpu

F32 = jnp.float32
BF16 = jnp.bfloat16

D_MODEL = 1024
GRID_W = 64
D_SSM = 512
SSM_HEADDIM = 64
SSM_HEADS = 8
SSM_STATE = 128
SSM_GROUPS = 2
SSM_CONV = 5
SSD_CHUNK = 128
XBC_DIM = D_SSM + 2 * SSM_GROUPS * SSM_STATE
D_CONV = 512
CONV_WIDTH = 31
PEER_HEADS = 8
PEER_KEY_DIM = 256
N_KEYS = 128
PEER_TOPK = 16
EPS = 1e-6

LANES = 128
SUBLANES = 8
VMEM_LIMIT = 56 * 1024 * 1024

IN_TILE = 512
SSD_ROWS = 8
CFM_ROWS = 16
OUT_TILE = 512
PEER_TILE = 512
PEER_EBLK = 2048
PEER_SUBS = (128,) * 16
assert sum(PEER_SUBS) == PEER_EBLK and all(s % N_KEYS == 0 for s in PEER_SUBS)
BF16_ROWS = 2 * SUBLANES


def _params(sem):
    return pltpu.CompilerParams(dimension_semantics=sem, vmem_limit_bytes=VMEM_LIMIT)


def _silu(v):
    return v * jax.nn.sigmoid(v)


def _split3(v):
    hi = v.astype(BF16)
    r1 = v - hi.astype(F32)
    mid = r1.astype(BF16)
    lo = (r1 - mid.astype(F32)).astype(BF16)
    return hi, mid, lo


def _dot01_right(v, m01):
    return sum(jnp.dot(p, m01, preferred_element_type=F32) for p in _split3(v))


def _dot01_left(m01, v):
    return sum(jnp.dot(m01, p, preferred_element_type=F32) for p in _split3(v))


def _ada_kernel(c_ref, w_ref, b_ref, o_ref):
    sc = _silu(c_ref[...])
    o_ref[...] = jnp.dot(sc.astype(BF16), w_ref[...].astype(BF16),
                         preferred_element_type=F32) + b_ref[...]


def _ada(c_rows, ada_w, ada_b):
    n = ada_w.shape[1]
    tn = 768
    return pl.pallas_call(
        _ada_kernel,
        grid=(n // tn,),
        in_specs=[pl.BlockSpec(c_rows.shape, lambda j: (0, 0)),
                  pl.BlockSpec((D_MODEL, tn), lambda j: (0, j)),
                  pl.BlockSpec((1, tn), lambda j: (0, j))],
        out_specs=pl.BlockSpec((c_rows.shape[0], tn), lambda j: (0, j)),
        out_shape=jax.ShapeDtypeStruct((c_rows.shape[0], n), F32),
        compiler_params=_params(("parallel",)),
        name="ada",
    )(c_rows, ada_w, ada_b.reshape(1, n))


def _inproj_kernel(x_ref, mod_ref, nw_ref, w_ref, z_ref, xbc_ref, u_ref, dt_ref):
    xv = x_ref[0]
    h = xv * lax.rsqrt(jnp.mean(xv * xv, axis=-1, keepdims=True) + EPS) * nw_ref[...]
    h = h * (1.0 + mod_ref[0, 1:2, :]) + mod_ref[0, 0:1, :]
    hb = h.astype(BF16)

    def proj(lo, hi):
        return jnp.dot(hb, w_ref[:, lo:hi], preferred_element_type=F32)

    z_ref[0] = proj(0, D_SSM)
    xbc_ref[0] = proj(D_SSM, D_SSM + XBC_DIM)
    o = D_SSM + XBC_DIM
    u_ref[0] = proj(o, o + D_CONV) * jax.nn.sigmoid(proj(o + D_CONV, o + 2 * D_CONV))
    dt_ref[0] = proj(o + 2 * D_CONV, o + 2 * D_CONV + LANES)


def _inproj(x, mod, per_batch_mod, norm_w, w_r):
    b, l, d = x.shape
    tm = min(IN_TILE, l)
    mod_idx = (lambda i, j: (i, 0, 0)) if per_batch_mod else (lambda i, j: (0, 0, 0))
    tok = lambda w: pl.BlockSpec((1, tm, w), lambda i, j: (i, j, 0))
    return pl.pallas_call(
        _inproj_kernel,
        grid=(b, l // tm),
        in_specs=[tok(d),
                  pl.BlockSpec((1, 6, d), mod_idx),
                  pl.BlockSpec((1, d), lambda i, j: (0, 0)),
                  pl.BlockSpec(w_r.shape, lambda i, j: (0, 0))],
        out_specs=[tok(D_SSM), tok(XBC_DIM), tok(D_CONV), tok(LANES)],
        out_shape=[jax.ShapeDtypeStruct((b, l, w), F32) for w in (D_SSM, XBC_DIM, D_CONV, LANES)],
        compiler_params=_params(("parallel", "parallel")),
        name="inproj",
    )(x, mod, norm_w.reshape(1, d), w_r)


def _ssd_kernel(cur_ref, prev_ref, next_ref, dt_ref, h0_ref, cw_ref, cb_ref, dtb_ref, alog_ref,
                dskip_ref, expand_ref, y_ref, hfin_ref, *rest, reverse, nc, add_skip, activated):
    act_ref, ext_ref, state_ref = (None, *rest) if activated else rest
    q = SSD_CHUNK
    halo = SUBLANES
    pad = (SSM_CONV - 1) // 2
    gw = D_SSM // SSM_GROUPS
    hpg = SSM_HEADS // SSM_GROUPS
    col0 = SSM_HEADS if reverse else 0
    c = pl.program_id(1)
    cc = (nc - 1 - c) if reverse else c

    @pl.when(c == 0)
    def _():
        state_ref[...] = h0_ref[...]

    ri = lax.broadcasted_iota(jnp.int32, (q, q), 0)
    ci = lax.broadcasted_iota(jnp.int32, (q, q), 1)
    causal = (ci >= ri) if reverse else (ci <= ri)
    causal01 = jnp.where(causal, 1.0, 0.0).astype(BF16)
    a_row = -jnp.exp(alog_ref[...])

    for r in range(cur_ref.shape[0]):
        if activated:
            act = cur_ref[r]
        else:
            ext_ref[r, 0:halo, :] = jnp.where(cc == 0, 0.0, prev_ref[r])
            ext_ref[r, halo:halo + q, :] = cur_ref[r]
            ext_ref[r, halo + q:2 * halo + q, :] = jnp.where(cc == nc - 1, 0.0, next_ref[r])
            conv = cb_ref[...] + cw_ref[0:1, :] * ext_ref[r, halo - pad:halo - pad + q, :]
            for k in range(1, SSM_CONV):
                conv = conv + cw_ref[k:k + 1, :] * ext_ref[r, halo - pad + k:halo - pad + k + q, :]
            act = _silu(conv)
            act_ref[r] = act

        dtv = dt_ref[r] + dtb_ref[...]
        dt = jnp.maximum(dtv, 0.0) + jnp.log(1.0 + jnp.exp(-jnp.abs(dtv)))
        acum = _dot01_left(causal01, dt * a_row)
        acum_t = acum.T
        tot = acum[0:1, :] if reverse else acum[q - 1:q, :]
        stacked = jnp.concatenate([dt, jnp.exp(tot - acum), jnp.exp(acum),
                                   jnp.broadcast_to(jnp.exp(tot), (SUBLANES, LANES))], axis=0)
        spread = _dot01_right(stacked, expand_ref[...])
        dt_e, dte_e, ea_e = spread[0:q], spread[q:2 * q], spread[2 * q:3 * q]
        etot_e = spread[3 * q:3 * q + 1]

        xs = act[:, :D_SSM]
        xdt = xs * dt_e
        xdt_b = xdt.astype(BF16)
        w_b = (xdt * dte_e).astype(BF16)
        for g in range(SSM_GROUPS):
            gs = slice(g * gw, (g + 1) * gw)
            b_g = act[:, D_SSM + g * SSM_STATE:D_SSM + (g + 1) * SSM_STATE]
            c_g = act[:, D_SSM + (SSM_GROUPS + g) * SSM_STATE:D_SSM + (SSM_GROUPS + g + 1) * SSM_STATE]
            c_b = c_g.astype(BF16)
            bt_b = b_g.T.astype(BF16)
            cb = jnp.dot(c_b, bt_b, preferred_element_type=F32)
            st = state_ref[r, :, gs]
            y_g = jnp.dot(c_b, st.astype(BF16), preferred_element_type=F32) * ea_e[:, gs]
            y_heads = []
            for hh in range(hpg):
                h = g * hpg + hh
                col = col0 + h
                diff = acum[:, col:col + 1] - acum_t[col:col + 1, :]
                decay = jnp.where(causal, jnp.exp(jnp.minimum(diff, 0.0)), 0.0)
                y_heads.append(jnp.dot((cb * decay).astype(BF16),
                                       xdt_b[:, h * SSM_HEADDIM:(h + 1) * SSM_HEADDIM],
                                       preferred_element_type=F32))
            y_g = y_g + jnp.concatenate(y_heads, axis=-1)
            if add_skip:
                y_g = y_g + dskip_ref[:, gs] * xs[:, gs]
            y_ref[r, :, gs] = y_g
            state_ref[r, :, gs] = st * etot_e[:, gs] + jnp.dot(bt_b, w_b[:, gs], preferred_element_type=F32)
    hfin_ref[...] = state_ref[...]


def _ssd(xbc, dtp, h0, conv_w, conv_b, dt_bias, a_log, d_skip, expand, reverse, add_skip, activated):
    b, l, _ = xbc.shape
    q = SSD_CHUNK
    nc = l // q
    nr = SSD_ROWS if b % SSD_ROWS == 0 else 1
    hb = q // SUBLANES
    pos = (lambda c: nc - 1 - c) if reverse else (lambda c: c)
    const = lambda shape: pl.BlockSpec(shape, lambda i, c: (0,) * len(shape))
    kern = functools.partial(_ssd_kernel, reverse=reverse, nc=nc, add_skip=add_skip, activated=activated)
    chunk = lambda w: pl.BlockSpec((nr, q, w), lambda i, c: (i, pos(c), 0))
    act_spec = [] if activated else [chunk(XBC_DIM)]
    act_shape = [] if activated else [jax.ShapeDtypeStruct((b, l, XBC_DIM), F32)]
    return pl.pallas_call(
        kern,
        grid=(b // nr, nc),
        in_specs=[pl.BlockSpec((nr, q, XBC_DIM), lambda i, c: (i, pos(c), 0)),
                  pl.BlockSpec((nr, SUBLANES, XBC_DIM), lambda i, c: (i, jnp.maximum(pos(c) * hb - 1, 0), 0)),
                  pl.BlockSpec((nr, SUBLANES, XBC_DIM),
                               lambda i, c: (i, jnp.minimum((pos(c) + 1) * hb, l // SUBLANES - 1), 0)),
                  pl.BlockSpec((nr, q, LANES), lambda i, c: (i, pos(c), 0)),
                  pl.BlockSpec((nr, SSM_STATE, D_SSM), lambda i, c: (i, 0, 0)),
                  const(conv_w.shape), const(conv_b.shape), const(dt_bias.shape), const(a_log.shape),
                  const(d_skip.shape), const(expand.shape)],
        out_specs=[chunk(D_SSM), pl.BlockSpec((nr, SSM_STATE, D_SSM), lambda i, c: (i, 0, 0))] + act_spec,
        out_shape=[jax.ShapeDtypeStruct((b, l, D_SSM), F32),
                   jax.ShapeDtypeStruct((b, SSM_STATE, D_SSM), F32)] + act_shape,
        scratch_shapes=[pltpu.VMEM((nr, q + 2 * SUBLANES, XBC_DIM), F32),
                        pltpu.VMEM((nr, SSM_STATE, D_SSM), F32)],
        compiler_params=_params(("parallel", "arbitrary")),
        name="ssd_bwd" if reverse else "ssd_fwd",
    )(xbc, xbc, xbc, dtp, h0, conv_w, conv_b, dt_bias, a_log, d_skip, expand)


def _cfm_kernel(cur_ref, prev_ref, next_ref, w_ref, b_ref, lnw_ref, lnb_ref, o_ref, hpad_ref, vbuf_ref,
                shift_ref, *, nb):
    rows = CFM_ROWS
    band = rows * GRID_W
    half = D_CONV // 2
    pad = (CONV_WIDTH - 1) // 2
    lead = 2 * SUBLANES
    r = pl.program_id(1)

    zeros = jnp.zeros((rows, lead, half), F32)
    hpad_ref[:, 0:lead, :] = zeros
    hpad_ref[:, lead + GRID_W:2 * lead + GRID_W, :] = zeros
    hpad_ref[:, lead:lead + GRID_W, :] = cur_ref[0, :, 0:half].reshape(rows, GRID_W, half)
    vbuf_ref[0:band, :] = jnp.where(r == 0, 0.0, prev_ref[0])
    vbuf_ref[band:2 * band, :] = cur_ref[0, :, half:D_CONV]
    vbuf_ref[2 * band:3 * band, :] = jnp.where(r == nb - 1, 0.0, next_ref[0])

    span = shift_ref.shape[1]

    def row_body(i, carry):
        for s in range(SUBLANES):
            shift_ref[s] = hpad_ref[i, s:s + span, :]
        acc_h = jnp.zeros((GRID_W, half), F32)
        acc_v = jnp.zeros((GRID_W, half), F32)
        for k in range(CONV_WIDTH):
            first = lead - pad + k
            tile0 = first // SUBLANES * SUBLANES
            acc_h = acc_h + w_ref[k:k + 1, 0:half] * shift_ref[first % SUBLANES, tile0:tile0 + GRID_W, :]
            start = pl.multiple_of(band + (i + k - pad) * GRID_W, GRID_W)
            acc_v = acc_v + w_ref[k:k + 1, half:D_CONV] * vbuf_ref[pl.ds(start, GRID_W), :]
        conv = jnp.concatenate([acc_h, acc_v], axis=-1) + b_ref[...]
        mu = jnp.mean(conv, axis=-1, keepdims=True)
        cen = conv - mu
        var = jnp.mean(cen * cen, axis=-1, keepdims=True)
        o_ref[0, pl.ds(pl.multiple_of(i * GRID_W, GRID_W), GRID_W), :] = _silu(
            cen * lax.rsqrt(var + EPS) * lnw_ref[...] + lnb_ref[...]).astype(BF16)
        return carry

    lax.fori_loop(0, rows, row_body, 0)


def _cfm(u, conv_w, conv_b, ln_w, ln_b):
    b, l, _ = u.shape
    band = CFM_ROWS * GRID_W
    nb = l // band
    half = D_CONV // 2
    const = lambda shape: pl.BlockSpec(shape, lambda i, r: (0,) * len(shape))
    return pl.pallas_call(
        functools.partial(_cfm_kernel, nb=nb),
        grid=(b, nb),
        in_specs=[pl.BlockSpec((1, band, D_CONV), lambda i, r: (i, r, 0)),
                  pl.BlockSpec((1, band, half), lambda i, r: (i, jnp.maximum(r - 1, 0), 1)),
                  pl.BlockSpec((1, band, half), lambda i, r: (i, jnp.minimum(r + 1, nb - 1), 1)),
                  const(conv_w.shape), const(conv_b.shape), const(ln_w.shape), const(ln_b.shape)],
        out_specs=pl.BlockSpec((1, band, D_CONV), lambda i, r: (i, r, 0)),
        out_shape=jax.ShapeDtypeStruct((b, l, D_CONV), BF16),
        scratch_shapes=[pltpu.VMEM((CFM_ROWS, GRID_W + 4 * SUBLANES, half), F32),
                        pltpu.VMEM((3 * band, half), F32),
                        pltpu.VMEM((SUBLANES, GRID_W + 3 * SUBLANES, half), F32)],
        compiler_params=_params(("parallel", "parallel")),
        name="cfm",
    )(u, u, u, conv_w, conv_b, ln_w, ln_b)


def _oddeven_merge(lo, hi, r):
    step = r * 2
    if step < hi - lo:
        yield from _oddeven_merge(lo, hi, step)
        yield from _oddeven_merge(lo + r, hi, step)
        yield from [(i, i + r) for i in range(lo + r, hi - r, step)]
    else:
        yield (lo, lo + r)


def _oddeven_sort(lo, hi):
    if hi > lo:
        mid = lo + (hi - lo) // 2
        yield from _oddeven_sort(lo, mid)
        yield from _oddeven_sort(mid + 1, hi)
        yield from _oddeven_merge(lo, hi, 1)


SORT_NET = tuple(_oddeven_sort(0, PEER_TOPK - 1))


def _cmp_exchange(v, i, j):
    v[i], v[j] = jnp.maximum(v[i], v[j]), jnp.minimum(v[i], v[j])


def _bitonic_sort(v):
    d = len(v) // 2
    while d:
        for i in range(len(v)):
            if not i & d:
                _cmp_exchange(v, i, i + d)
        d //= 2


def _merge_top(v, w):
    n = len(v)
    return [jnp.maximum(v[i], w[n - 1 - i]) for i in range(n)]


def _sublane_rolls():
    shift = SUBLANES // 2
    while shift:
        yield shift
        shift //= 2


def _top_sorted(v):
    for shift in _sublane_rolls():
        v = _merge_top(v, [pltpu.roll(x, shift, 0) for x in v])
        _bitonic_sort(v)
    return v


def _score_tokens(h_ref, wq_ref, keys_ref, neg_ref, c1_ref, e1_ref, r2_ref, e2_ref, q_ref):
    tm = h_ref.shape[0]
    nsub = tm // LANES
    kd = PEER_KEY_DIM // 2
    k = PEER_TOPK
    ntile = N_KEYS // SUBLANES

    def project(h, slot):
        first = h * PEER_KEY_DIM
        cols = pl.ds(first if isinstance(h, int) else pl.multiple_of(first, PEER_KEY_DIM), PEER_KEY_DIM)
        q_ref[slot] = jnp.dot(h_ref[...], wq_ref[:, cols], preferred_element_type=F32).astype(BF16)

    def score_block(h, slot, tok):
        nt = (((1,), (1,)), ((), ()))

        def score_tiles(p):
            qs = q_ref[slot, tok:tok + LANES, p * kd:(p + 1) * kd]
            s = lax.dot_general(keys_ref[2 * h + p], qs, nt, preferred_element_type=F32)
            return [s[i * SUBLANES:(i + 1) * SUBLANES, :] for i in range(ntile)]

        def top(tiles):
            v = list(tiles)
            for i, j in SORT_NET:
                _cmp_exchange(v, i, j)
            return _top_sorted(v)

        s1 = score_tiles(0)
        s2 = score_tiles(1)
        a = top(s1)
        b = top(s2)

        sub = lax.broadcasted_iota(jnp.int32, (SUBLANES, LANES), 0)
        pick = lambda lo, x, y: jnp.where(sub < lo, x, y)
        base = pick(4, pick(2, pick(1, a[0], a[1]), pick(3, a[2], a[3])), pick(5, b[0], pick(6, b[1], b[2])))
        cand = []
        for m in range(k):
            other = pick(4, b[m], a[4 + m]) if 4 + m < k else b[m]
            cand.append((other + base) + neg_ref[m])
        lists = list(cand)
        rolls = list(_sublane_rolls())
        for shift in rolls[:-1]:
            lists = _merge_top(lists, [pltpu.roll(x, shift, 0) for x in lists])
            _bitonic_sort(lists)
        lists = _merge_top(lists, [pltpu.roll(x, rolls[-1], 0) for x in lists])
        tau = functools.reduce(jnp.minimum, lists)

        best = a[0] + b[0]
        zsum = sum(jnp.where(c >= tau, jnp.exp(c - best), 0.0) for c in cand)
        for shift in rolls:
            zsum = zsum + pltpu.roll(zsum, shift, 0)
        scale = 0.5 / zsum
        top_extra = sum(jnp.where(a[0] + b[j] >= tau, 1.0, 0.0) for j in range(k // 2, k))

        c1_t, e1_t, r2_t, e2_t = [], [], [], []
        for t1, t2 in zip(s1, s2):
            c1 = jnp.zeros_like(t1)
            for j in range(k // 2):
                c1 = jnp.where(t1 + b[j] >= tau, float(j + 1), c1)
            c1_t.append(jnp.where(t1 == a[0], c1 + top_extra, c1))
            e1_t.append(jnp.exp(t1 - a[0]) * scale)
            r2 = jnp.zeros_like(t2)
            for j in range(k):
                r2 = jnp.where(t2 < b[j], float(j + 1), r2)
            r2_t.append(r2)
            e2_t.append(jnp.exp(t2 - b[0]))
        c1_ref[h, :, tok:tok + LANES] = jnp.concatenate(c1_t, axis=0)
        e1_ref[h, :, tok:tok + LANES] = jnp.concatenate(e1_t, axis=0)
        r2_ref[h, :, tok:tok + LANES] = jnp.concatenate(r2_t, axis=0).astype(BF16)
        e2_ref[h, :, tok:tok + LANES] = jnp.concatenate(e2_t, axis=0).astype(BF16)

    def head_pair(pair, carry):
        h = 2 * pair
        project(h + 1, 1)
        for sb in range(nsub):
            score_block(h, 0, sb * LANES)
        project(jnp.minimum(h + 2, PEER_HEADS - 1), 0)
        for sb in range(nsub):
            score_block(h + 1, 1, sb * LANES)
        return carry

    project(0, 0)
    lax.fori_loop(0, PEER_HEADS // 2, head_pair, 0)


def _candidate_mask():
    k = PEER_TOPK
    m = lax.broadcasted_iota(jnp.int32, (k, SUBLANES, LANES), 0)
    s = lax.broadcasted_iota(jnp.int32, (k, SUBLANES, LANES), 1)
    row_ok = (s < 4) & ((s + 1) * (m + 1) <= k)
    col_ok = (s >= 4) & (s < 7) & (4 + m < k) & ((5 + m) * (s - 3) <= k)
    return jnp.where(row_ok | col_ok, 0.0, -jnp.inf).astype(F32)


def _outscore_kernel(yf_ref, yb_ref, z_ref, cfm_ref, x_ref, mod_ref, gnw_ref, w_ref, n2w_ref,
                     wq_ref, keys_ref, neg_ref,
                     x1_ref, h2t_ref, c1_ref, e1_ref, r2_ref, e2_ref, hb_ref, q_ref):
    gw = D_SSM // SSM_GROUPS
    g = (yf_ref[0] + yb_ref[0]) * _silu(z_ref[0])
    parts = []
    for k in range(SSM_GROUPS):
        gk = g[:, k * gw:(k + 1) * gw]
        parts.append(gk * lax.rsqrt(jnp.mean(gk * gk, axis=-1, keepdims=True) + EPS))
    ssm = jnp.concatenate(parts, axis=-1) * gnw_ref[...]
    mixed = (jnp.dot(ssm.astype(BF16), w_ref[0:D_SSM, :], preferred_element_type=F32)
             + jnp.dot(cfm_ref[0], w_ref[D_SSM:, :], preferred_element_type=F32))
    x1 = x_ref[0] + mod_ref[0, 2:3, :] * mixed
    x1_ref[0] = x1
    h = x1 * lax.rsqrt(jnp.mean(x1 * x1, axis=-1, keepdims=True) + EPS) * n2w_ref[...]
    h2 = h * (1.0 + mod_ref[0, 4:5, :]) + mod_ref[0, 3:4, :]
    h2t_ref[...] = h2.T.astype(BF16)
    hb_ref[...] = h2.astype(BF16)
    _score_tokens(hb_ref, wq_ref, keys_ref, neg_ref, c1_ref, e1_ref, r2_ref, e2_ref, q_ref)


def _outscore(yf, yb, z, cfm, x, mod, gn_w, w_out_b, n2_w, wq_b, keys_b):
    b, l, d = x.shape
    t = b * l
    tm = min(OUT_TILE, l)
    neg = _candidate_mask()
    tile = lambda i, j: i * (l // tm) + j
    tok = lambda w: pl.BlockSpec((1, tm, w), lambda i, j: (i, j, 0))
    const = lambda shape: pl.BlockSpec(shape, lambda i, j: (0,) * len(shape))
    big = lambda: pl.BlockSpec((PEER_HEADS, N_KEYS, tm), lambda i, j: (0, 0, tile(i, j)))
    big_shape = lambda dt: jax.ShapeDtypeStruct((PEER_HEADS, N_KEYS, t), dt)
    return pl.pallas_call(
        _outscore_kernel,
        grid=(b, l // tm),
        in_specs=[tok(D_SSM), tok(D_SSM), tok(D_SSM), tok(D_CONV), tok(d),
                  pl.BlockSpec((1, 6, d), lambda i, j: (i, 0, 0)),
                  const(gn_w.shape), const(w_out_b.shape), const(n2_w.shape),
                  const(wq_b.shape), const(keys_b.shape), const(neg.shape)],
        out_specs=[tok(d), pl.BlockSpec((d, tm), lambda i, j: (0, tile(i, j))), big(), big(), big(), big()],
        out_shape=[jax.ShapeDtypeStruct((b, l, d), F32), jax.ShapeDtypeStruct((d, t), BF16),
                   big_shape(F32), big_shape(F32), big_shape(BF16), big_shape(BF16)],
        scratch_shapes=[pltpu.VMEM((tm, d), BF16), pltpu.VMEM((2, tm, PEER_KEY_DIM), BF16)],
        compiler_params=_params(("parallel", "parallel")),
        name="outproj_score",
    )(yf, yb, z, cfm, x, mod, gn_w, w_out_b, n2_w, wq_b, keys_b, neg)


def _key_row_bf16(ref, h, key0, j):
    tile = ref[h, pl.ds(key0 + j // SUBLANES * SUBLANES, SUBLANES), :]
    row = tile[j % SUBLANES:j % SUBLANES + 1, :]
    rows = jnp.broadcast_to(row, (BF16_ROWS, ref.shape[-1])).astype(BF16)
    return jnp.tile(rows, (N_KEYS // BF16_ROWS, 1))


def _peer_kernel(ht_ref, u_ref, vt_ref, c1_ref, e1_ref, r2_ref, e2_ref, x1_ref, mod_ref, fnw_ref,
                 o_ref, acc_ref):
    eb = pl.program_id(1)
    keys_per_step = PEER_EBLK // N_KEYS
    key0 = pl.multiple_of(eb * keys_per_step, keys_per_step)
    starts = [sum(PEER_SUBS[:i]) for i in range(len(PEER_SUBS))]

    @pl.when(eb == 0)
    def _():
        acc_ref[...] = jnp.zeros_like(acc_ref)

    def first_matmul(sub):
        return jnp.dot(u_ref[starts[sub]:starts[sub] + PEER_SUBS[sub], :], ht_ref[...],
                       preferred_element_type=F32)

    def build_gates(sub):
        gates = []
        for kk in range(PEER_SUBS[sub] // N_KEYS):
            j = starts[sub] // N_KEYS + kk
            g = jnp.zeros((N_KEYS, ht_ref.shape[1]), BF16)
            for h in range(PEER_HEADS):
                c1 = _key_row_bf16(c1_ref, h, key0, j)
                e1 = _key_row_bf16(e1_ref, h, key0, j)
                g = g + jnp.where(r2_ref[h] < c1, e2_ref[h], jnp.zeros((), BF16)) * e1
            gates.append(g)
        return jnp.concatenate(gates, axis=0)

    nsub = len(PEER_SUBS)
    pre = first_matmul(0)
    weighted = []
    for sub in range(nsub):
        nxt = first_matmul(sub + 1) if sub + 1 < nsub else None
        act = pre * (1.0 + lax.erf(pre * (2.0 ** -0.5)))
        weighted.append(act.astype(BF16) * build_gates(sub))
        pre = nxt
    acc_ref[...] = jnp.dot(vt_ref[...], jnp.concatenate(weighted, axis=0),
                           preferred_element_type=F32) + acc_ref[...]

    @pl.when(eb == pl.num_programs(1) - 1)
    def _():
        x2 = x1_ref[...] + mod_ref[0, 5:6, :] * acc_ref[...].T
        o_ref[...] = x2 * lax.rsqrt(jnp.mean(x2 * x2, axis=-1, keepdims=True) + EPS) * fnw_ref[...]


def _peer(h2t, u_b, vt_b, c1, e1, r2, e2, x1, mod, fn_w, seq):
    d, t = h2t.shape
    tm = min(PEER_TILE, seq)
    n_exp = u_b.shape[0]
    big = lambda: pl.BlockSpec((PEER_HEADS, N_KEYS, tm), lambda j, e: (0, 0, j))
    return pl.pallas_call(
        _peer_kernel,
        grid=(t // tm, n_exp // PEER_EBLK),
        in_specs=[pl.BlockSpec((d, tm), lambda j, e: (0, j)),
                  pl.BlockSpec((PEER_EBLK, d), lambda j, e: (e, 0)),
                  pl.BlockSpec((None, d, PEER_EBLK), lambda j, e: (e, 0, 0)),
                  big(), big(), big(), big(),
                  pl.BlockSpec((tm, d), lambda j, e: (j, 0)),
                  pl.BlockSpec((1, 6, d), lambda j, e: ((j * tm) // seq, 0, 0)),
                  pl.BlockSpec((1, d), lambda j, e: (0, 0))],
        out_specs=pl.BlockSpec((tm, d), lambda j, e: (j, 0)),
        out_shape=jax.ShapeDtypeStruct((t, d), F32),
        scratch_shapes=[pltpu.VMEM((d, tm), F32)],
        compiler_params=_params(("parallel", "arbitrary")),
        name="peer_dense",
    )(h2t, u_b, vt_b, c1, e1, r2, e2, x1, mod, fn_w)


def _pad_lanes(v, width=LANES):
    flat = v.reshape(1, -1)
    return jnp.pad(flat, ((0, 0), (0, width - flat.shape[1])))


def _head_expand(col0):
    rows = lax.broadcasted_iota(jnp.int32, (LANES, D_SSM), 0)
    heads = lax.broadcasted_iota(jnp.int32, (LANES, D_SSM), 1) // SSM_HEADDIM
    return (rows == heads + col0).astype(BF16)


def kernel(x, c, ctx, c_ctx, ada_w, ada_b, norm1_w, norm2_w, w_in, ssm_conv_w, ssm_conv_b, ssm_dt_bias,
           ssm_a_log, ssm_d, ssm_norm_w, cfm_conv_w, cfm_conv_b, cfm_ln_w, cfm_ln_b, w_out, peer_wq,
           peer_subkeys, peer_u, peer_v, final_norm_w):
    depth = ada_w.shape[0]
    assert depth == 1, "single-layer configuration"
    b, l, d = x.shape
    i = 0

    c_rows = jnp.concatenate([c, c_ctx[None], jnp.zeros((2 * SUBLANES - b - 1, d), F32)], axis=0)
    mod_all = _ada(c_rows, ada_w[i], ada_b[i])
    mod = mod_all[:b].reshape(b, 6, d)
    mod_ctx = mod_all[b:b + 1].reshape(1, 6, d)

    wi = w_in[i]
    o_dt = D_SSM + XBC_DIM
    w_r = jnp.concatenate([wi[:, :o_dt], wi[:, o_dt + 2 * SSM_HEADS:], wi[:, o_dt:o_dt + 2 * SSM_HEADS],
                           jnp.zeros((d, LANES - 2 * SSM_HEADS), F32)], axis=1).astype(BF16)
    z_l, xbc_l, u_l, dt_l = _inproj(x, mod, True, norm1_w[i], w_r)
    _, xbc_c, _, dt_c = _inproj(ctx, mod_ctx, False, norm1_w[i], w_r)

    conv_w = jnp.pad(ssm_conv_w[i], ((0, SUBLANES - SSM_CONV), (0, 0)))
    conv_b = ssm_conv_b[i].reshape(1, XBC_DIM)
    dt_bias = _pad_lanes(ssm_dt_bias[i])
    a_log = _pad_lanes(ssm_a_log[i])
    d_skip = jnp.repeat(ssm_d[i], SSM_HEADDIM).reshape(1, D_SSM)
    zero_state = jnp.zeros((b, SSM_STATE, D_SSM), F32)
    y_dirs = []
    act_c = act_l = None
    for reverse in (False, True):
        expand = _head_expand(SSM_HEADS if reverse else 0)
        args = (conv_w, conv_b, dt_bias, a_log, d_skip, expand, reverse)
        if not reverse:
            _, h_ctx, act_c = _ssd(xbc_c, dt_c, zero_state, *args, add_skip=False, activated=False)
            y_dir, _, act_l = _ssd(xbc_l, dt_l, h_ctx, *args, add_skip=True, activated=False)
        else:
            _, h_ctx = _ssd(act_c, dt_c, zero_state, *args, add_skip=False, activated=True)
            y_dir, _ = _ssd(act_l, dt_l, h_ctx, *args, add_skip=False, activated=True)
        y_dirs.append(y_dir)

    cfm = _cfm(u_l, jnp.pad(cfm_conv_w[i], ((0, 1), (0, 0))), cfm_conv_b[i].reshape(1, D_CONV),
               cfm_ln_w[i].reshape(1, D_CONV), cfm_ln_b[i].reshape(1, D_CONV))

    t = b * l
    keys_b = peer_subkeys[i].reshape(PEER_HEADS * 2, N_KEYS, PEER_KEY_DIM // 2).astype(BF16)
    x1, h2t, c1, e1, r2, e2 = _outscore(y_dirs[0], y_dirs[1], z_l, cfm, x, mod,
                                        ssm_norm_w[i].reshape(1, D_SSM), w_out[i].astype(BF16),
                                        norm2_w[i].reshape(1, d), peer_wq[i].astype(BF16), keys_b)
    vt_b = peer_v[i].reshape(-1, PEER_EBLK, d).transpose(0, 2, 1).astype(BF16)
    out = _peer(h2t, peer_u[i].astype(BF16), vt_b, c1, e1, r2, e2,
                x1.reshape(t, d), mod, final_norm_w.reshape(1, d), l)
    return out.reshape(b, l, d)
```

```python
import functools

import jax
import jax.numpy as jnp
from jax import lax
from jax.experimental import pallas as pl
from jax.experimental.pallas import tpu as pltpu

F32 = jnp.float32
BF16 = jnp.bfloat16

D_MODEL = 1024
GRID_W = 64
D_SSM = 512
SSM_HEADDIM = 64
SSM_HEADS = 8
SSM_STATE = 128
SSM_GROUPS = 2
SSM_CONV = 5
SSD_CHUNK = 128
XBC_DIM = D_SSM + 2 * SSM_GROUPS * SSM_STATE
D_CONV = 512
CONV_WIDTH = 31
PEER_HEADS = 8
PEER_KEY_DIM = 256
N_KEYS = 128
PEER_TOPK = 16
EPS = 1e-6

LANES = 128
SUBLANES = 8
VMEM_LIMIT = 56 * 1024 * 1024

IN_TILE = 512
SSD_ROWS = 8
CFM_ROWS = 16
OUT_TILE = 512
PEER_TILE = 512
PEER_EBLK = 2048
PEER_SUBS = (128,) * 16
assert sum(PEER_SUBS) == PEER_EBLK and all(s % N_KEYS == 0 for s in PEER_SUBS)
BF16_ROWS = 2 * SUBLANES


def _params(sem):
    return pltpu.CompilerParams(dimension_semantics=sem, vmem_limit_bytes=VMEM_LIMIT)


def _silu(v):
    return v * jax.nn.sigmoid(v)


def _split3(v):
    hi = v.astype(BF16)
    r1 = v - hi.astype(F32)
    mid = r1.astype(BF16)
    lo = (r1 - mid.astype(F32)).astype(BF16)
    return hi, mid, lo


def _dot01_right(v, m01):
    return sum(jnp.dot(p, m01, preferred_element_type=F32) for p in _split3(v))


def _dot01_left(m01, v):
    return sum(jnp.dot(m01, p, preferred_element_type=F32) for p in _split3(v))


def _ada_kernel(c_ref, w_ref, b_ref, o_ref):
    sc = _silu(c_ref[...])
    o_ref[...] = jnp.dot(sc.astype(BF16), w_ref[...].astype(BF16),
                         preferred_element_type=F32) + b_ref[...]


def _ada(c_rows, ada_w, ada_b):
    n = ada_w.shape[1]
    tn = 768
    return pl.pallas_call(
        _ada_kernel,
        grid=(n // tn,),
        in_specs=[pl.BlockSpec(c_rows.shape, lambda j: (0, 0)),
                  pl.BlockSpec((D_MODEL, tn), lambda j: (0, j)),
                  pl.BlockSpec((1, tn), lambda j: (0, j))],
        out_specs=pl.BlockSpec((c_rows.shape[0], tn), lambda j: (0, j)),
        out_shape=jax.ShapeDtypeStruct((c_rows.shape[0], n), F32),
        compiler_params=_params(("parallel",)),
        name="ada",
    )(c_rows, ada_w, ada_b.reshape(1, n))


def _inproj_kernel(x_ref, mod_ref, nw_ref, w_ref, z_ref, xbc_ref, u_ref, dt_ref):
    xv = x_ref[0]
    h = xv * lax.rsqrt(jnp.mean(xv * xv, axis=-1, keepdims=True) + EPS) * nw_ref[...]
    h = h * (1.0 + mod_ref[0, 1:2, :]) + mod_ref[0, 0:1, :]
    hb = h.astype(BF16)

    def proj(lo, hi):
        return jnp.dot(hb, w_ref[:, lo:hi], preferred_element_type=F32)

    z_ref[0] = proj(0, D_SSM)
    xbc_ref[0] = proj(D_SSM, D_SSM + XBC_DIM)
    o = D_SSM + XBC_DIM
    u_ref[0] = proj(o, o + D_CONV) * jax.nn.sigmoid(proj(o + D_CONV, o + 2 * D_CONV))
    dt_ref[0] = proj(o + 2 * D_CONV, o + 2 * D_CONV + LANES)


def _inproj(x, mod, per_batch_mod, norm_w, w_r):
    b, l, d = x.shape
    tm = min(IN_TILE, l)
    mod_idx = (lambda i, j: (i, 0, 0)) if per_batch_mod else (lambda i, j: (0, 0, 0))
    tok = lambda w: pl.BlockSpec((1, tm, w), lambda i, j: (i, j, 0))
    return pl.pallas_call(
        _inproj_kernel,
        grid=(b, l // tm),
        in_specs=[tok(d),
                  pl.BlockSpec((1, 6, d), mod_idx),
                  pl.BlockSpec((1, d), lambda i, j: (0, 0)),
                  pl.BlockSpec(w_r.shape, lambda i, j: (0, 0))],
        out_specs=[tok(D_SSM), tok(XBC_DIM), tok(D_CONV), tok(LANES)],
        out_shape=[jax.ShapeDtypeStruct((b, l, w), F32) for w in (D_SSM, XBC_DIM, D_CONV, LANES)],
        compiler_params=_params(("parallel", "parallel")),
        name="inproj",
    )(x, mod, norm_w.reshape(1, d), w_r)


def _ssd_kernel(cur_ref, prev_ref, next_ref, dt_ref, h0_ref, cw_ref, cb_ref, dtb_ref, alog_ref,
                dskip_ref, expand_ref, y_ref, hfin_ref, *rest, reverse, nc, add_skip, activated):
    act_ref, ext_ref, state_ref = (None, *rest) if activated else rest
    q = SSD_CHUNK
    halo = SUBLANES
    pad = (SSM_CONV - 1) // 2
    gw = D_SSM // SSM_GROUPS
    hpg = SSM_HEADS // SSM_GROUPS
    col0 = SSM_HEADS if reverse else 0
    c = pl.program_id(1)
    cc = (nc - 1 - c) if reverse else c

    @pl.when(c == 0)
    def _():
        state_ref[...] = h0_ref[...]

    ri = lax.broadcasted_iota(jnp.int32, (q, q), 0)
    ci = lax.broadcasted_iota(jnp.int32, (q, q), 1)
    causal = (ci >= ri) if reverse else (ci <= ri)
    causal01 = jnp.where(causal, 1.0, 0.0).astype(BF16)
    a_row = -jnp.exp(alog_ref[...])

    for r in range(cur_ref.shape[0]):
        if activated:
            act = cur_ref[r]
        else:
            ext_ref[r, 0:halo, :] = jnp.where(cc == 0, 0.0, prev_ref[r])
            ext_ref[r, halo:halo + q, :] = cur_ref[r]
            ext_ref[r, halo + q:2 * halo + q, :] = jnp.where(cc == nc - 1, 0.0, next_ref[r])
            conv = cb_ref[...] + cw_ref[0:1, :] * ext_ref[r, halo - pad:halo - pad + q, :]
            for k in range(1, SSM_CONV):
                conv = conv + cw_ref[k:k + 1, :] * ext_ref[r, halo - pad + k:halo - pad + k + q, :]
            act = _silu(conv)
            act_ref[r] = act

        dtv = dt_ref[r] + dtb_ref[...]
        dt = jnp.maximum(dtv, 0.0) + jnp.log(1.0 + jnp.exp(-jnp.abs(dtv)))
        acum = _dot01_left(causal01, dt * a_row)
        acum_t = acum.T
        tot = acum[0:1, :] if reverse else acum[q - 1:q, :]
        stacked = jnp.concatenate([dt, jnp.exp(tot - acum), jnp.exp(acum),
                                   jnp.broadcast_to(jnp.exp(tot), (SUBLANES, LANES))], axis=0)
        spread = _dot01_right(stacked, expand_ref[...])
        dt_e, dte_e, ea_e = spread[0:q], spread[q:2 * q], spread[2 * q:3 * q]
        etot_e = spread[3 * q:3 * q + 1]

        xs = act[:, :D_SSM]
        xdt = xs * dt_e
        xdt_b = xdt.astype(BF16)
        w_b = (xdt * dte_e).astype(BF16)
        for g in range(SSM_GROUPS):
            gs = slice(g * gw, (g + 1) * gw)
            b_g = act[:, D_SSM + g * SSM_STATE:D_SSM + (g + 1) * SSM_STATE]
            c_g = act[:, D_SSM + (SSM_GROUPS + g) * SSM_STATE:D_SSM + (SSM_GROUPS + g + 1) * SSM_STATE]
            c_b = c_g.astype(BF16)
            bt_b = b_g.T.astype(BF16)
            cb = jnp.dot(c_b, bt_b, preferred_element_type=F32)
            st = state_ref[r, :, gs]
            y_g = jnp.dot(c_b, st.astype(BF16), preferred_element_type=F32) * ea_e[:, gs]
            y_heads = []
            for hh in range(hpg):
                h = g * hpg + hh
                col = col0 + h
                diff = acum[:, col:col + 1] - acum_t[col:col + 1, :]
                decay = jnp.where(causal, jnp.exp(jnp.minimum(diff, 0.0)), 0.0)
                y_heads.append(jnp.dot((cb * decay).astype(BF16),
                                       xdt_b[:, h * SSM_HEADDIM:(h + 1) * SSM_HEADDIM],
                                       preferred_element_type=F32))
            y_g = y_g + jnp.concatenate(y_heads, axis=-1)
            if add_skip:
                y_g = y_g + dskip_ref[:, gs] * xs[:, gs]
            y_ref[r, :, gs] = y_g
            state_ref[r, :, gs] = st * etot_e[:, gs] + jnp.dot(bt_b, w_b[:, gs], preferred_element_type=F32)
    hfin_ref[...] = state_ref[...]


def _ssd(xbc, dtp, h0, conv_w, conv_b, dt_bias, a_log, d_skip, expand, reverse, add_skip, activated):
    b, l, _ = xbc.shape
    q = SSD_CHUNK
    nc = l // q
    nr = SSD_ROWS if b % SSD_ROWS == 0 else 1
    hb = q // SUBLANES
    pos = (lambda c: nc - 1 - c) if reverse else (lambda c: c)
    const = lambda shape: pl.BlockSpec(shape, lambda i, c: (0,) * len(shape))
    kern = functools.partial(_ssd_kernel, reverse=reverse, nc=nc, add_skip=add_skip, activated=activated)
    chunk = lambda w: pl.BlockSpec((nr, q, w), lambda i, c: (i, pos(c), 0))
    act_spec = [] if activated else [chunk(XBC_DIM)]
    act_shape = [] if activated else [jax.ShapeDtypeStruct((b, l, XBC_DIM), F32)]
    return pl.pallas_call(
        kern,
        grid=(b // nr, nc),
        in_specs=[pl.BlockSpec((nr, q, XBC_DIM), lambda i, c: (i, pos(c), 0)),
                  pl.BlockSpec((nr, SUBLANES, XBC_DIM), lambda i, c: (i, jnp.maximum(pos(c) * hb - 1, 0), 0)),
                  pl.BlockSpec((nr, SUBLANES, XBC_DIM),
                               lambda i, c: (i, jnp.minimum((pos(c) + 1) * hb, l // SUBLANES - 1), 0)),
                  pl.BlockSpec((nr, q, LANES), lambda i, c: (i, pos(c), 0)),
                  pl.BlockSpec((nr, SSM_STATE, D_SSM), lambda i, c: (i, 0, 0)),
                  const(conv_w.shape), const(conv_b.shape), const(dt_bias.shape), const(a_log.shape),
                  const(d_skip.shape), const(expand.shape)],
        out_specs=[chunk(D_SSM), pl.BlockSpec((nr, SSM_STATE, D_SSM), lambda i, c: (i, 0, 0))] + act_spec,
        out_shape=[jax.ShapeDtypeStruct((b, l, D_SSM), F32),
                   jax.ShapeDtypeStruct((b, SSM_STATE, D_SSM), F32)] + act_shape,
        scratch_shapes=[pltpu.VMEM((nr, q + 2 * SUBLANES, XBC_DIM), F32),
                        pltpu.VMEM((nr, SSM_STATE, D_SSM), F32)],
        compiler_params=_params(("parallel", "arbitrary")),
        name="ssd_bwd" if reverse else "ssd_fwd",
    )(xbc, xbc, xbc, dtp, h0, conv_w, conv_b, dt_bias, a_log, d_skip, expand)


def _cfm_kernel(cur_ref, prev_ref, next_ref, w_ref, b_ref, lnw_ref, lnb_ref, o_ref, hpad_ref, vbuf_ref,
                shift_ref, *, nb):
    rows = CFM_ROWS
    band = rows * GRID_W
    half = D_CONV // 2
    pad = (CONV_WIDTH - 1) // 2
    lead = 2 * SUBLANES
    r = pl.program_id(1)

    zeros = jnp.zeros((rows, lead, half), F32)
    hpad_ref[:, 0:lead, :] = zeros
    hpad_ref[:, lead + GRID_W:2 * lead + GRID_W, :] = zeros
    hpad_ref[:, lead:lead + GRID_W, :] = cur_ref[0, :, 0:half].reshape(rows, GRID_W, half)
    vbuf_ref[0:band, :] = jnp.where(r == 0, 0.0, prev_ref[0])
    vbuf_ref[band:2 * band, :] = cur_ref[0, :, half:D_CONV]
    vbuf_ref[2 * band:3 * band, :] = jnp.where(r == nb - 1, 0.0, next_ref[0])

    span = shift_ref.shape[1]

    def row_body(i, carry):
        for s in range(SUBLANES):
            shift_ref[s] = hpad_ref[i, s:s + span, :]
        acc_h = jnp.zeros((GRID_W, half), F32)
        acc_v = jnp.zeros((GRID_W, half), F32)
        for k in range(CONV_WIDTH):
            first = lead - pad + k
            tile0 = first // SUBLANES * SUBLANES
            acc_h = acc_h + w_ref[k:k + 1, 0:half] * shift_ref[first % SUBLANES, tile0:tile0 + GRID_W, :]
            start = pl.multiple_of(band + (i + k - pad) * GRID_W, GRID_W)
            acc_v = acc_v + w_ref[k:k + 1, half:D_CONV] * vbuf_ref[pl.ds(start, GRID_W), :]
        conv = jnp.concatenate([acc_h, acc_v], axis=-1) + b_ref[...]
        mu = jnp.mean(conv, axis=-1, keepdims=True)
        cen = conv - mu
        var = jnp.mean(cen * cen, axis=-1, keepdims=True)
        o_ref[0, pl.ds(pl.multiple_of(i * GRID_W, GRID_W), GRID_W), :] = _silu(
            cen * lax.rsqrt(var + EPS) * lnw_ref[...] + lnb_ref[...]).astype(BF16)
        return carry

    lax.fori_loop(0, rows, row_body, 0)


def _cfm(u, conv_w, conv_b, ln_w, ln_b):
    b, l, _ = u.shape
    band = CFM_ROWS * GRID_W
    nb = l // band
    half = D_CONV // 2
    const = lambda shape: pl.BlockSpec(shape, lambda i, r: (0,) * len(shape))
    return pl.pallas_call(
        functools.partial(_cfm_kernel, nb=nb),
        grid=(b, nb),
        in_specs=[pl.BlockSpec((1, band, D_CONV), lambda i, r: (i, r, 0)),
                  pl.BlockSpec((1, band, half), lambda i, r: (i, jnp.maximum(r - 1, 0), 1)),
                  pl.BlockSpec((1, band, half), lambda i, r: (i, jnp.minimum(r + 1, nb - 1), 1)),
                  const(conv_w.shape), const(conv_b.shape), const(ln_w.shape), const(ln_b.shape)],
        out_specs=pl.BlockSpec((1, band, D_CONV), lambda i, r: (i, r, 0)),
        out_shape=jax.ShapeDtypeStruct((b, l, D_CONV), BF16),
        scratch_shapes=[pltpu.VMEM((CFM_ROWS, GRID_W + 4 * SUBLANES, half), F32),
                        pltpu.VMEM((3 * band, half), F32),
                        pltpu.VMEM((SUBLANES, GRID_W + 3 * SUBLANES, half), F32)],
        compiler_params=_params(("parallel", "parallel")),
        name="cfm",
    )(u, u, u, conv_w, conv_b, ln_w, ln_b)


def _oddeven_merge(lo, hi, r):
    step = r * 2
    if step < hi - lo:
        yield from _oddeven_merge(lo, hi, step)
        yield from _oddeven_merge(lo + r, hi, step)
        yield from [(i, i + r) for i in range(lo + r, hi - r, step)]
    else:
        yield (lo, lo + r)


def _oddeven_sort(lo, hi):
    if hi > lo:
        mid = lo + (hi - lo) // 2
        yield from _oddeven_sort(lo, mid)
        yield from _oddeven_sort(mid + 1, hi)
        yield from _oddeven_merge(lo, hi, 1)


SORT_NET = tuple(_oddeven_sort(0, PEER_TOPK - 1))


def _cmp_exchange(v, i, j):
    v[i], v[j] = jnp.maximum(v[i], v[j]), jnp.minimum(v[i], v[j])


def _bitonic_sort(v):
    d = len(v) // 2
    while d:
        for i in range(len(v)):
            if not i & d:
                _cmp_exchange(v, i, i + d)
        d //= 2


def _merge_top(v, w):
    n = len(v)
    return [jnp.maximum(v[i], w[n - 1 - i]) for i in range(n)]


def _sublane_rolls():
    shift = SUBLANES // 2
    while shift:
        yield shift
        shift //= 2


def _top_sorted(v):
    for shift in _sublane_rolls():
        v = _merge_top(v, [pltpu.roll(x, shift, 0) for x in v])
        _bitonic_sort(v)
    return v


def _score_tokens(q_ref, keys_ref, neg_ref, c1_ref, e1_ref, r2_ref, e2_ref):
    tm = q_ref.shape[0]
    nsub = tm // LANES
    kd = PEER_KEY_DIM // 2
    k = PEER_TOPK
    ntile = N_KEYS // SUBLANES

    def body(it, carry):
        h = it // nsub
        tok = pl.multiple_of((it % nsub) * LANES, LANES)
        nt = (((1,), (1,)), ((), ()))

        def score_tiles(p):
            qs = q_ref[pl.ds(tok, LANES), pl.ds(pl.multiple_of((2 * h + p) * kd, kd), kd)]
            s = lax.dot_general(keys_ref[2 * h + p], qs, nt, preferred_element_type=F32)
            return [s[i * SUBLANES:(i + 1) * SUBLANES, :] for i in range(ntile)]

        def top(tiles):
            v = list(tiles)
            for i, j in SORT_NET:
                _cmp_exchange(v, i, j)
            return _top_sorted(v)

        s1 = score_tiles(0)
        s2 = score_tiles(1)
        a = top(s1)
        b = top(s2)

        sub = lax.broadcasted_iota(jnp.int32, (SUBLANES, LANES), 0)
        pick = lambda lo, x, y: jnp.where(sub < lo, x, y)
        base = pick(4, pick(2, pick(1, a[0], a[1]), pick(3, a[2], a[3])), pick(5, b[0], pick(6, b[1], b[2])))
        cand = []
        for m in range(k):
            other = pick(4, b[m], a[4 + m]) if 4 + m < k else b[m]
            cand.append((other + base) + neg_ref[m])
        lists = list(cand)
        rolls = list(_sublane_rolls())
        for shift in rolls[:-1]:
            lists = _merge_top(lists, [pltpu.roll(x, shift, 0) for x in lists])
            _bitonic_sort(lists)
        lists = _merge_top(lists, [pltpu.roll(x, rolls[-1], 0) for x in lists])
        tau = functools.reduce(jnp.minimum, lists)

        best = a[0] + b[0]
        zsum = sum(jnp.where(c >= tau, jnp.exp(c - best), 0.0) for c in cand)
        for shift in rolls:
            zsum = zsum + pltpu.roll(zsum, shift, 0)
        scale = 0.5 / zsum
        top_extra = sum(jnp.where(a[0] + b[j] >= tau, 1.0, 0.0) for j in range(k // 2, k))

        c1_t, e1_t, r2_t, e2_t = [], [], [], []
        for t1, t2 in zip(s1, s2):
            c1 = jnp.zeros_like(t1)
            for j in range(k // 2):
                c1 = jnp.where(t1 + b[j] >= tau, float(j + 1), c1)
            c1_t.append(jnp.where(t1 == a[0], c1 + top_extra, c1))
            e1_t.append(jnp.exp(t1 - a[0]) * scale)
            r2 = jnp.zeros_like(t2)
            for j in range(k):
                r2 = jnp.where(t2 < b[j], float(j + 1), r2)
            r2_t.append(r2)
            e2_t.append(jnp.exp(t2 - b[0]))
        c1_ref[h, :, pl.ds(tok, LANES)] = jnp.concatenate(c1_t, axis=0)
        e1_ref[h, :, pl.ds(tok, LANES)] = jnp.concatenate(e1_t, axis=0)
        r2_ref[h, :, pl.ds(tok, LANES)] = jnp.concatenate(r2_t, axis=0).astype(BF16)
        e2_ref[h, :, pl.ds(tok, LANES)] = jnp.concatenate(e2_t, axis=0).astype(BF16)
        return carry

    lax.fori_loop(0, PEER_HEADS * nsub, body, 0, unroll=4)


def _candidate_mask():
    k = PEER_TOPK
    m = lax.broadcasted_iota(jnp.int32, (k, SUBLANES, LANES), 0)
    s = lax.broadcasted_iota(jnp.int32, (k, SUBLANES, LANES), 1)
    row_ok = (s < 4) & ((s + 1) * (m + 1) <= k)
    col_ok = (s >= 4) & (s < 7) & (4 + m < k) & ((5 + m) * (s - 3) <= k)
    return jnp.where(row_ok | col_ok, 0.0, -jnp.inf).astype(F32)


def _outscore_kernel(yf_ref, yb_ref, z_ref, cfm_ref, x_ref, mod_ref, gnw_ref, w_ref, n2w_ref,
                     wq_ref, keys_ref, neg_ref, u_ref, v_ref,
                     x1_ref, h2t_ref, c1_ref, e1_ref, r2_ref, e2_ref, ub_ref, vt_ref, q_ref):
    ub_ref[...] = u_ref[...].astype(BF16)
    vt_ref[...] = v_ref[...].T.astype(BF16)
    gw = D_SSM // SSM_GROUPS
    g = (yf_ref[0] + yb_ref[0]) * _silu(z_ref[0])
    parts = []
    for k in range(SSM_GROUPS):
        gk = g[:, k * gw:(k + 1) * gw]
        parts.append(gk * lax.rsqrt(jnp.mean(gk * gk, axis=-1, keepdims=True) + EPS))
    ssm = jnp.concatenate(parts, axis=-1) * gnw_ref[...]
    mixed = (jnp.dot(ssm.astype(BF16), w_ref[0:D_SSM, :], preferred_element_type=F32)
             + jnp.dot(cfm_ref[0], w_ref[D_SSM:, :], preferred_element_type=F32))
    x1 = x_ref[0] + mod_ref[0, 2:3, :] * mixed
    x1_ref[0] = x1
    h = x1 * lax.rsqrt(jnp.mean(x1 * x1, axis=-1, keepdims=True) + EPS) * n2w_ref[...]
    h2 = h * (1.0 + mod_ref[0, 4:5, :]) + mod_ref[0, 3:4, :]
    h2t_ref[...] = h2.T.astype(BF16)
    q_ref[...] = jnp.dot(h2.astype(BF16), wq_ref[...], preferred_element_type=F32).astype(BF16)
    _score_tokens(q_ref, keys_ref, neg_ref, c1_ref, e1_ref, r2_ref, e2_ref)


def _outscore(yf, yb, z, cfm, x, mod, gn_w, w_out_b, n2_w, wq_b, keys_b, u_tab, v_tab):
    b, l, d = x.shape
    t = b * l
    tm = min(OUT_TILE, l)
    neg = _candidate_mask()
    n_exp = u_tab.shape[0]
    rows = n_exp // (t // tm)
    assert rows * (t // tm) == n_exp and PEER_EBLK % rows == 0 and rows % LANES == 0
    per_slab = PEER_EBLK // rows
    tile = lambda i, j: i * (l // tm) + j
    tok = lambda w: pl.BlockSpec((1, tm, w), lambda i, j: (i, j, 0))
    const = lambda shape: pl.BlockSpec(shape, lambda i, j: (0,) * len(shape))
    big = lambda: pl.BlockSpec((PEER_HEADS, N_KEYS, tm), lambda i, j: (0, 0, tile(i, j)))
    big_shape = lambda dt: jax.ShapeDtypeStruct((PEER_HEADS, N_KEYS, t), dt)
    return pl.pallas_call(
        _outscore_kernel,
        grid=(b, l // tm),
        in_specs=[tok(D_SSM), tok(D_SSM), tok(D_SSM), tok(D_CONV), tok(d),
                  pl.BlockSpec((1, 6, d), lambda i, j: (i, 0, 0)),
                  const(gn_w.shape), const(w_out_b.shape), const(n2_w.shape),
                  const(wq_b.shape), const(keys_b.shape), const(neg.shape),
                  pl.BlockSpec((rows, d), lambda i, j: (tile(i, j), 0)),
                  pl.BlockSpec((rows, d), lambda i, j: (tile(i, j), 0))],
        out_specs=[tok(d), pl.BlockSpec((d, tm), lambda i, j: (0, tile(i, j))), big(), big(), big(), big(),
                   pl.BlockSpec((rows, d), lambda i, j: (tile(i, j), 0)),
                   pl.BlockSpec((None, d, rows), lambda i, j: (tile(i, j) // per_slab, 0, tile(i, j) % per_slab))],
        out_shape=[jax.ShapeDtypeStruct((b, l, d), F32), jax.ShapeDtypeStruct((d, t), BF16),
                   big_shape(F32), big_shape(F32), big_shape(BF16), big_shape(BF16),
                   jax.ShapeDtypeStruct((n_exp, d), BF16),
                   jax.ShapeDtypeStruct((n_exp // PEER_EBLK, d, PEER_EBLK), BF16)],
        scratch_shapes=[pltpu.VMEM((tm, PEER_HEADS * PEER_KEY_DIM), BF16)],
        compiler_params=_params(("parallel", "parallel")),
        name="outproj_score",
    )(yf, yb, z, cfm, x, mod, gn_w, w_out_b, n2_w, wq_b, keys_b, neg, u_tab, v_tab)


def _key_row_bf16(ref, h, key0, j):
    tile = ref[h, pl.ds(key0 + j // SUBLANES * SUBLANES, SUBLANES), :]
    row = tile[j % SUBLANES:j % SUBLANES + 1, :]
    rows = jnp.broadcast_to(row, (BF16_ROWS, ref.shape[-1])).astype(BF16)
    return jnp.tile(rows, (N_KEYS // BF16_ROWS, 1))


def _peer_kernel(ht_ref, u_ref, vt_ref, c1_ref, e1_ref, r2_ref, e2_ref, x1_ref, mod_ref, fnw_ref,
                 o_ref, acc_ref):
    eb = pl.program_id(1)
    keys_per_step = PEER_EBLK // N_KEYS
    key0 = pl.multiple_of(eb * keys_per_step, keys_per_step)
    starts = [sum(PEER_SUBS[:i]) for i in range(len(PEER_SUBS))]

    @pl.when(eb == 0)
    def _():
        acc_ref[...] = jnp.zeros_like(acc_ref)

    def first_matmul(sub):
        return jnp.dot(u_ref[starts[sub]:starts[sub] + PEER_SUBS[sub], :], ht_ref[...],
                       preferred_element_type=F32)

    def build_gates(sub):
        gates = []
        for kk in range(PEER_SUBS[sub] // N_KEYS):
            j = starts[sub] // N_KEYS + kk
            g = jnp.zeros((N_KEYS, ht_ref.shape[1]), BF16)
            for h in range(PEER_HEADS):
                c1 = _key_row_bf16(c1_ref, h, key0, j)
                e1 = _key_row_bf16(e1_ref, h, key0, j)
                g = g + jnp.where(r2_ref[h] < c1, e2_ref[h], jnp.zeros((), BF16)) * e1
            gates.append(g)
        return jnp.concatenate(gates, axis=0)

    nsub = len(PEER_SUBS)
    pre = first_matmul(0)
    weighted = []
    for sub in range(nsub):
        nxt = first_matmul(sub + 1) if sub + 1 < nsub else None
        act = pre * (1.0 + lax.erf(pre * (2.0 ** -0.5)))
        weighted.append(act.astype(BF16) * build_gates(sub))
        pre = nxt
    acc_ref[...] = jnp.dot(vt_ref[...], jnp.concatenate(weighted, axis=0),
                           preferred_element_type=F32) + acc_ref[...]

    @pl.when(eb == pl.num_programs(1) - 1)
    def _():
        x2 = x1_ref[...] + mod_ref[0, 5:6, :] * acc_ref[...].T
        o_ref[...] = x2 * lax.rsqrt(jnp.mean(x2 * x2, axis=-1, keepdims=True) + EPS) * fnw_ref[...]


def _peer(h2t, u_b, vt_b, c1, e1, r2, e2, x1, mod, fn_w, seq):
    d, t = h2t.shape
    tm = min(PEER_TILE, seq)
    n_exp = u_b.shape[0]
    big = lambda: pl.BlockSpec((PEER_HEADS, N_KEYS, tm), lambda j, e: (0, 0, j))
    return pl.pallas_call(
        _peer_kernel,
        grid=(t // tm, n_exp // PEER_EBLK),
        in_specs=[pl.BlockSpec((d, tm), lambda j, e: (0, j)),
                  pl.BlockSpec((PEER_EBLK, d), lambda j, e: (e, 0)),
                  pl.BlockSpec((None, d, PEER_EBLK), lambda j, e: (e, 0, 0)),
                  big(), big(), big(), big(),
                  pl.BlockSpec((tm, d), lambda j, e: (j, 0)),
                  pl.BlockSpec((1, 6, d), lambda j, e: ((j * tm) // seq, 0, 0)),
                  pl.BlockSpec((1, d), lambda j, e: (0, 0))],
        out_specs=pl.BlockSpec((tm, d), lambda j, e: (j, 0)),
        out_shape=jax.ShapeDtypeStruct((t, d), F32),
        scratch_shapes=[pltpu.VMEM((d, tm), F32)],
        compiler_params=_params(("parallel", "arbitrary")),
        name="peer_dense",
    )(h2t, u_b, vt_b, c1, e1, r2, e2, x1, mod, fn_w)


def _pad_lanes(v, width=LANES):
    flat = v.reshape(1, -1)
    return jnp.pad(flat, ((0, 0), (0, width - flat.shape[1])))


def _head_expand(col0):
    rows = lax.broadcasted_iota(jnp.int32, (LANES, D_SSM), 0)
    heads = lax.broadcasted_iota(jnp.int32, (LANES, D_SSM), 1) // SSM_HEADDIM
    return (rows == heads + col0).astype(BF16)


def kernel(x, c, ctx, c_ctx, ada_w, ada_b, norm1_w, norm2_w, w_in, ssm_conv_w, ssm_conv_b, ssm_dt_bias,
           ssm_a_log, ssm_d, ssm_norm_w, cfm_conv_w, cfm_conv_b, cfm_ln_w, cfm_ln_b, w_out, peer_wq,
           peer_subkeys, peer_u, peer_v, final_norm_w):
    depth = ada_w.shape[0]
    assert depth == 1, "single-layer configuration"
    b, l, d = x.shape
    i = 0

    c_rows = jnp.concatenate([c, c_ctx[None], jnp.zeros((2 * SUBLANES - b - 1, d), F32)], axis=0)
    mod_all = _ada(c_rows, ada_w[i], ada_b[i])
    mod = mod_all[:b].reshape(b, 6, d)
    mod_ctx = mod_all[b:b + 1].reshape(1, 6, d)

    wi = w_in[i]
    o_dt = D_SSM + XBC_DIM
    w_r = jnp.concatenate([wi[:, :o_dt], wi[:, o_dt + 2 * SSM_HEADS:], wi[:, o_dt:o_dt + 2 * SSM_HEADS],
                           jnp.zeros((d, LANES - 2 * SSM_HEADS), F32)], axis=1).astype(BF16)
    z_l, xbc_l, u_l, dt_l = _inproj(x, mod, True, norm1_w[i], w_r)
    _, xbc_c, _, dt_c = _inproj(ctx, mod_ctx, False, norm1_w[i], w_r)

    conv_w = jnp.pad(ssm_conv_w[i], ((0, SUBLANES - SSM_CONV), (0, 0)))
    conv_b = ssm_conv_b[i].reshape(1, XBC_DIM)
    dt_bias = _pad_lanes(ssm_dt_bias[i])
    a_log = _pad_lanes(ssm_a_log[i])
    d_skip = jnp.repeat(ssm_d[i], SSM_HEADDIM).reshape(1, D_SSM)
    zero_state = jnp.zeros((b, SSM_STATE, D_SSM), F32)
    y_dirs = []
    act_c = act_l = None
    for reverse in (False, True):
        expand = _head_expand(SSM_HEADS if reverse else 0)
        args = (conv_w, conv_b, dt_bias, a_log, d_skip, expand, reverse)
        if not reverse:
            _, h_ctx, act_c = _ssd(xbc_c, dt_c, zero_state, *args, add_skip=False, activated=False)
            y_dir, _, act_l = _ssd(xbc_l, dt_l, h_ctx, *args, add_skip=True, activated=False)
        else:
            _, h_ctx = _ssd(act_c, dt_c, zero_state, *args, add_skip=False, activated=True)
            y_dir, _ = _ssd(act_l, dt_l, h_ctx, *args, add_skip=False, activated=True)
        y_dirs.append(y_dir)

    cfm = _cfm(u_l, jnp.pad(cfm_conv_w[i], ((0, 1), (0, 0))), cfm_conv_b[i].reshape(1, D_CONV),
               cfm_ln_w[i].reshape(1, D_CONV), cfm_ln_b[i].reshape(1, D_CONV))

    t = b * l
    keys_b = peer_subkeys[i].reshape(PEER_HEADS * 2, N_KEYS, PEER_KEY_DIM // 2).astype(BF16)
    x1, h2t, c1, e1, r2, e2, u_b, vt_b = _outscore(
        y_dirs[0], y_dirs[1], z_l, cfm, x, mod, ssm_norm_w[i].reshape(1, D_SSM), w_out[i].astype(BF16),
        norm2_w[i].reshape(1, d), peer_wq[i].astype(BF16), keys_b, peer_u[i], peer_v[i])
    out = _peer(h2t, u_b, vt_b, c1, e1, r2, e2,
                x1.reshape(t, d), mod, final_norm_w.reshape(1, d), l)
    return out.reshape(b, l, d)
```

```python
import functools

import jax
import jax.numpy as jnp
from jax import lax
from jax.experimental import pallas as pl
from jax.experimental.pallas import tpu as pltpu

F32 = jnp.float32
BF16 = jnp.bfloat16

D_MODEL = 1024
GRID_W = 64
D_SSM = 512
SSM_HEADDIM = 64
SSM_HEADS = 8
SSM_STATE = 128
SSM_GROUPS = 2
SSM_CONV = 5
SSD_CHUNK = 128
XBC_DIM = D_SSM + 2 * SSM_GROUPS * SSM_STATE
D_CONV = 512
CONV_WIDTH = 31
PEER_HEADS = 8
PEER_KEY_DIM = 256
N_KEYS = 128
PEER_TOPK = 16
EPS = 1e-6

LANES = 128
SUBLANES = 8
VMEM_LIMIT = 56 * 1024 * 1024

ADA_TILE = 768
IN_TILE = 512
SSD_ROWS = 8
CFM_ROWS = 16
OUT_TILE = 512
PEER_TILE = 512
PEER_EBLK = 2048
PEER_SUBS = (128,) * 16
assert sum(PEER_SUBS) == PEER_EBLK and all(s % N_KEYS == 0 for s in PEER_SUBS)
BF16_ROWS = 2 * SUBLANES


def _params(sem):
    return pltpu.CompilerParams(dimension_semantics=sem, vmem_limit_bytes=VMEM_LIMIT)


def _silu(v):
    return v * jax.nn.sigmoid(v)


def _split3(v):
    hi = v.astype(BF16)
    r1 = v - hi.astype(F32)
    mid = r1.astype(BF16)
    lo = (r1 - mid.astype(F32)).astype(BF16)
    return hi, mid, lo


def _dot01_right(v, m01):
    return sum(jnp.dot(p, m01, preferred_element_type=F32) for p in _split3(v))


def _dot01_left(m01, v):
    return sum(jnp.dot(m01, p, preferred_element_type=F32) for p in _split3(v))


def _ada_kernel(c_ref, w_ref, b_ref, o_ref):
    sc = _silu(c_ref[...])
    o_ref[...] = jnp.dot(sc.astype(BF16), w_ref[...].astype(BF16),
                         preferred_element_type=F32) + b_ref[...]


def _ada(c_rows, ada_w, ada_b):
    n = ada_w.shape[1]
    tn = ADA_TILE
    return pl.pallas_call(
        _ada_kernel,
        grid=(n // tn,),
        in_specs=[pl.BlockSpec(c_rows.shape, lambda j: (0, 0)),
                  pl.BlockSpec((D_MODEL, tn), lambda j: (0, j)),
                  pl.BlockSpec((1, tn), lambda j: (0, j))],
        out_specs=pl.BlockSpec((c_rows.shape[0], tn), lambda j: (0, j)),
        out_shape=jax.ShapeDtypeStruct((c_rows.shape[0], n), F32),
        compiler_params=_params(("parallel",)),
        name="ada",
    )(c_rows, ada_w, ada_b.reshape(1, n))


def _inproj_kernel(x_ref, mod_ref, nw_ref, w_ref, z_ref, xbc_ref, u_ref, dt_ref):
    xv = x_ref[0]
    h = xv * lax.rsqrt(jnp.mean(xv * xv, axis=-1, keepdims=True) + EPS) * nw_ref[...]
    h = h * (1.0 + mod_ref[0, 1:2, :]) + mod_ref[0, 0:1, :]
    hb = h.astype(BF16)

    def proj(lo, hi):
        return jnp.dot(hb, w_ref[:, lo:hi], preferred_element_type=F32)

    z_ref[0] = proj(0, D_SSM)
    xbc_ref[0] = proj(D_SSM, D_SSM + XBC_DIM)
    o = D_SSM + XBC_DIM
    u_ref[0] = proj(o, o + D_CONV) * jax.nn.sigmoid(proj(o + D_CONV, o + 2 * D_CONV))
    dt_ref[0] = proj(o + 2 * D_CONV, o + 2 * D_CONV + LANES)


def _inproj(x, mod, per_batch_mod, norm_w, w_r):
    b, l, d = x.shape
    tm = min(IN_TILE, l)
    mod_idx = (lambda i, j: (i, 0, 0)) if per_batch_mod else (lambda i, j: (0, 0, 0))
    tok = lambda w: pl.BlockSpec((1, tm, w), lambda i, j: (i, j, 0))
    return pl.pallas_call(
        _inproj_kernel,
        grid=(b, l // tm),
        in_specs=[tok(d),
                  pl.BlockSpec((1, 6, d), mod_idx),
                  pl.BlockSpec((1, d), lambda i, j: (0, 0)),
                  pl.BlockSpec(w_r.shape, lambda i, j: (0, 0))],
        out_specs=[tok(D_SSM), tok(XBC_DIM), tok(D_CONV), tok(LANES)],
        out_shape=[jax.ShapeDtypeStruct((b, l, w), F32) for w in (D_SSM, XBC_DIM, D_CONV, LANES)],
        compiler_params=_params(("parallel", "parallel")),
        name="inproj",
    )(x, mod, norm_w.reshape(1, d), w_r)


def _ssd_kernel(cur_ref, prev_ref, next_ref, dt_ref, h0_ref, cw_ref, cb_ref, dtb_ref, alog_ref,
                dskip_ref, expand_ref, y_ref, hfin_ref, *rest, reverse, nc, add_skip, activated):
    act_ref, ext_ref, state_ref = (None, *rest) if activated else rest
    q = SSD_CHUNK
    halo = SUBLANES
    pad = (SSM_CONV - 1) // 2
    gw = D_SSM // SSM_GROUPS
    hpg = SSM_HEADS // SSM_GROUPS
    col0 = SSM_HEADS if reverse else 0
    c = pl.program_id(1)
    cc = (nc - 1 - c) if reverse else c

    @pl.when(c == 0)
    def _():
        state_ref[...] = h0_ref[...]

    ri = lax.broadcasted_iota(jnp.int32, (q, q), 0)
    ci = lax.broadcasted_iota(jnp.int32, (q, q), 1)
    causal = (ci >= ri) if reverse else (ci <= ri)
    causal01 = jnp.where(causal, 1.0, 0.0).astype(BF16)
    a_row = -jnp.exp(alog_ref[...])

    for r in range(cur_ref.shape[0]):
        if activated:
            act = cur_ref[r]
        else:
            ext_ref[r, 0:halo, :] = jnp.where(cc == 0, 0.0, prev_ref[r])
            ext_ref[r, halo:halo + q, :] = cur_ref[r]
            ext_ref[r, halo + q:2 * halo + q, :] = jnp.where(cc == nc - 1, 0.0, next_ref[r])
            conv = cb_ref[...] + cw_ref[0:1, :] * ext_ref[r, halo - pad:halo - pad + q, :]
            for k in range(1, SSM_CONV):
                conv = conv + cw_ref[k:k + 1, :] * ext_ref[r, halo - pad + k:halo - pad + k + q, :]
            act = _silu(conv)
            act_ref[r] = act

        dtv = dt_ref[r] + dtb_ref[...]
        dt = jnp.maximum(dtv, 0.0) + jnp.log(1.0 + jnp.exp(-jnp.abs(dtv)))
        acum = _dot01_left(causal01, dt * a_row)
        acum_t = acum.T
        tot = acum[0:1, :] if reverse else acum[q - 1:q, :]
        stacked = jnp.concatenate([dt, jnp.exp(tot - acum), jnp.exp(acum),
                                   jnp.broadcast_to(jnp.exp(tot), (SUBLANES, LANES))], axis=0)
        spread = _dot01_right(stacked, expand_ref[...])
        dt_e, dte_e, ea_e = spread[0:q], spread[q:2 * q], spread[2 * q:3 * q]
        etot_e = spread[3 * q:3 * q + 1]

        xs = act[:, :D_SSM]
        xdt = xs * dt_e
        xdt_b = xdt.astype(BF16)
        w_b = (xdt * dte_e).astype(BF16)
        for g in range(SSM_GROUPS):
            gs = slice(g * gw, (g + 1) * gw)
            b_g = act[:, D_SSM + g * SSM_STATE:D_SSM + (g + 1) * SSM_STATE]
            c_g = act[:, D_SSM + (SSM_GROUPS + g) * SSM_STATE:D_SSM + (SSM_GROUPS + g + 1) * SSM_STATE]
            c_b = c_g.astype(BF16)
            bt_b = b_g.T.astype(BF16)
            cb = jnp.dot(c_b, bt_b, preferred_element_type=F32)
            st = state_ref[r, :, gs]
            y_g = jnp.dot(c_b, st.astype(BF16), preferred_element_type=F32) * ea_e[:, gs]
            y_heads = []
            for hh in range(hpg):
                h = g * hpg + hh
                col = col0 + h
                diff = acum[:, col:col + 1] - acum_t[col:col + 1, :]
                decay = jnp.where(causal, jnp.exp(jnp.minimum(diff, 0.0)), 0.0)
                y_heads.append(jnp.dot((cb * decay).astype(BF16),
                                       xdt_b[:, h * SSM_HEADDIM:(h + 1) * SSM_HEADDIM],
                                       preferred_element_type=F32))
            y_g = y_g + jnp.concatenate(y_heads, axis=-1)
            if add_skip:
                y_g = y_g + dskip_ref[:, gs] * xs[:, gs]
            y_ref[r, :, gs] = y_g
            state_ref[r, :, gs] = st * etot_e[:, gs] + jnp.dot(bt_b, w_b[:, gs], preferred_element_type=F32)
    hfin_ref[...] = state_ref[...]


def _ssd(xbc, dtp, h0, conv_w, conv_b, dt_bias, a_log, d_skip, expand, reverse, add_skip, activated):
    b, l, _ = xbc.shape
    q = SSD_CHUNK
    nc = l // q
    nr = SSD_ROWS if b % SSD_ROWS == 0 else 1
    hb = q // SUBLANES
    pos = (lambda c: nc - 1 - c) if reverse else (lambda c: c)
    const = lambda shape: pl.BlockSpec(shape, lambda i, c: (0,) * len(shape))
    kern = functools.partial(_ssd_kernel, reverse=reverse, nc=nc, add_skip=add_skip, activated=activated)
    chunk = lambda w: pl.BlockSpec((nr, q, w), lambda i, c: (i, pos(c), 0))
    act_spec = [] if activated else [chunk(XBC_DIM)]
    act_shape = [] if activated else [jax.ShapeDtypeStruct((b, l, XBC_DIM), F32)]
    return pl.pallas_call(
        kern,
        grid=(b // nr, nc),
        in_specs=[pl.BlockSpec((nr, q, XBC_DIM), lambda i, c: (i, pos(c), 0)),
                  pl.BlockSpec((nr, SUBLANES, XBC_DIM), lambda i, c: (i, jnp.maximum(pos(c) * hb - 1, 0), 0)),
                  pl.BlockSpec((nr, SUBLANES, XBC_DIM),
                               lambda i, c: (i, jnp.minimum((pos(c) + 1) * hb, l // SUBLANES - 1), 0)),
                  pl.BlockSpec((nr, q, LANES), lambda i, c: (i, pos(c), 0)),
                  pl.BlockSpec((nr, SSM_STATE, D_SSM), lambda i, c: (i, 0, 0)),
                  const(conv_w.shape), const(conv_b.shape), const(dt_bias.shape), const(a_log.shape),
                  const(d_skip.shape), const(expand.shape)],
        out_specs=[chunk(D_SSM), pl.BlockSpec((nr, SSM_STATE, D_SSM), lambda i, c: (i, 0, 0))] + act_spec,
        out_shape=[jax.ShapeDtypeStruct((b, l, D_SSM), F32),
                   jax.ShapeDtypeStruct((b, SSM_STATE, D_SSM), F32)] + act_shape,
        scratch_shapes=[pltpu.VMEM((nr, q + 2 * SUBLANES, XBC_DIM), F32),
                        pltpu.VMEM((nr, SSM_STATE, D_SSM), F32)],
        compiler_params=_params(("parallel", "arbitrary")),
        name="ssd_bwd" if reverse else "ssd_fwd",
    )(xbc, xbc, xbc, dtp, h0, conv_w, conv_b, dt_bias, a_log, d_skip, expand)


def _cfm_kernel(cur_ref, prev_ref, next_ref, w_ref, b_ref, lnw_ref, lnb_ref, o_ref, hpad_ref, vbuf_ref,
                shift_ref, *, nb):
    rows = CFM_ROWS
    band = rows * GRID_W
    half = D_CONV // 2
    pad = (CONV_WIDTH - 1) // 2
    lead = 2 * SUBLANES
    r = pl.program_id(1)

    zeros = jnp.zeros((rows, lead, half), F32)
    hpad_ref[:, 0:lead, :] = zeros
    hpad_ref[:, lead + GRID_W:2 * lead + GRID_W, :] = zeros
    hpad_ref[:, lead:lead + GRID_W, :] = cur_ref[0, :, 0:half].reshape(rows, GRID_W, half)
    vbuf_ref[0:band, :] = jnp.where(r == 0, 0.0, prev_ref[0])
    vbuf_ref[band:2 * band, :] = cur_ref[0, :, half:D_CONV]
    vbuf_ref[2 * band:3 * band, :] = jnp.where(r == nb - 1, 0.0, next_ref[0])

    span = shift_ref.shape[1]

    def row_body(i, carry):
        for s in range(SUBLANES):
            shift_ref[s] = hpad_ref[i, s:s + span, :]
        acc_h = jnp.zeros((GRID_W, half), F32)
        acc_v = jnp.zeros((GRID_W, half), F32)
        for k in range(CONV_WIDTH):
            first = lead - pad + k
            tile0 = first // SUBLANES * SUBLANES
            acc_h = acc_h + w_ref[k:k + 1, 0:half] * shift_ref[first % SUBLANES, tile0:tile0 + GRID_W, :]
            start = pl.multiple_of(band + (i + k - pad) * GRID_W, GRID_W)
            acc_v = acc_v + w_ref[k:k + 1, half:D_CONV] * vbuf_ref[pl.ds(start, GRID_W), :]
        conv = jnp.concatenate([acc_h, acc_v], axis=-1) + b_ref[...]
        mu = jnp.mean(conv, axis=-1, keepdims=True)
        cen = conv - mu
        var = jnp.mean(cen * cen, axis=-1, keepdims=True)
        o_ref[0, pl.ds(pl.multiple_of(i * GRID_W, GRID_W), GRID_W), :] = _silu(
            cen * lax.rsqrt(var + EPS) * lnw_ref[...] + lnb_ref[...]).astype(BF16)
        return carry

    lax.fori_loop(0, rows, row_body, 0)


def _cfm(u, conv_w, conv_b, ln_w, ln_b):
    b, l, _ = u.shape
    band = CFM_ROWS * GRID_W
    nb = l // band
    half = D_CONV // 2
    const = lambda shape: pl.BlockSpec(shape, lambda i, r: (0,) * len(shape))
    return pl.pallas_call(
        functools.partial(_cfm_kernel, nb=nb),
        grid=(b, nb),
        in_specs=[pl.BlockSpec((1, band, D_CONV), lambda i, r: (i, r, 0)),
                  pl.BlockSpec((1, band, half), lambda i, r: (i, jnp.maximum(r - 1, 0), 1)),
                  pl.BlockSpec((1, band, half), lambda i, r: (i, jnp.minimum(r + 1, nb - 1), 1)),
                  const(conv_w.shape), const(conv_b.shape), const(ln_w.shape), const(ln_b.shape)],
        out_specs=pl.BlockSpec((1, band, D_CONV), lambda i, r: (i, r, 0)),
        out_shape=jax.ShapeDtypeStruct((b, l, D_CONV), BF16),
        scratch_shapes=[pltpu.VMEM((CFM_ROWS, GRID_W + 4 * SUBLANES, half), F32),
                        pltpu.VMEM((3 * band, half), F32),
                        pltpu.VMEM((SUBLANES, GRID_W + 3 * SUBLANES, half), F32)],
        compiler_params=_params(("parallel", "parallel")),
        name="cfm",
    )(u, u, u, conv_w, conv_b, ln_w, ln_b)


def _oddeven_merge(lo, hi, r):
    step = r * 2
    if step < hi - lo:
        yield from _oddeven_merge(lo, hi, step)
        yield from _oddeven_merge(lo + r, hi, step)
        yield from [(i, i + r) for i in range(lo + r, hi - r, step)]
    else:
        yield (lo, lo + r)


def _oddeven_sort(lo, hi):
    if hi > lo:
        mid = lo + (hi - lo) // 2
        yield from _oddeven_sort(lo, mid)
        yield from _oddeven_sort(mid + 1, hi)
        yield from _oddeven_merge(lo, hi, 1)


SORT_NET = tuple(_oddeven_sort(0, PEER_TOPK - 1))


def _cmp_exchange(v, i, j):
    v[i], v[j] = jnp.maximum(v[i], v[j]), jnp.minimum(v[i], v[j])


def _bitonic_sort(v):
    d = len(v) // 2
    while d:
        for i in range(len(v)):
            if not i & d:
                _cmp_exchange(v, i, i + d)
        d //= 2


def _merge_top(v, w):
    n = len(v)
    return [jnp.maximum(v[i], w[n - 1 - i]) for i in range(n)]


def _sublane_rolls():
    shift = SUBLANES // 2
    while shift:
        yield shift
        shift //= 2


def _top_sorted(v):
    for shift in _sublane_rolls():
        v = _merge_top(v, [pltpu.roll(x, shift, 0) for x in v])
        _bitonic_sort(v)
    return v


def _score_tokens(q_ref, keys_ref, neg_ref, c1_ref, e1_ref, r2_ref, e2_ref):
    tm = q_ref.shape[0]
    nsub = tm // LANES
    kd = PEER_KEY_DIM // 2
    k = PEER_TOPK
    ntile = N_KEYS // SUBLANES

    def body(it, carry):
        h = it // nsub
        tok = pl.multiple_of((it % nsub) * LANES, LANES)
        nt = (((1,), (1,)), ((), ()))

        def score_tiles(p):
            qs = q_ref[pl.ds(tok, LANES), pl.ds(pl.multiple_of((2 * h + p) * kd, kd), kd)]
            s = lax.dot_general(keys_ref[2 * h + p], qs, nt, preferred_element_type=F32)
            return [s[i * SUBLANES:(i + 1) * SUBLANES, :] for i in range(ntile)]

        def top(tiles):
            v = list(tiles)
            for i, j in SORT_NET:
                _cmp_exchange(v, i, j)
            return _top_sorted(v)

        s1 = score_tiles(0)
        s2 = score_tiles(1)
        a = top(s1)
        b = top(s2)

        sub = lax.broadcasted_iota(jnp.int32, (SUBLANES, LANES), 0)
        pick = lambda lo, x, y: jnp.where(sub < lo, x, y)
        base = pick(4, pick(2, pick(1, a[0], a[1]), pick(3, a[2], a[3])), pick(5, b[0], pick(6, b[1], b[2])))
        cand = []
        for m in range(k):
            other = pick(4, b[m], a[4 + m]) if 4 + m < k else b[m]
            cand.append((other + base) + neg_ref[m])
        lists = list(cand)
        rolls = list(_sublane_rolls())
        for shift in rolls[:-1]:
            lists = _merge_top(lists, [pltpu.roll(x, shift, 0) for x in lists])
            _bitonic_sort(lists)
        lists = _merge_top(lists, [pltpu.roll(x, rolls[-1], 0) for x in lists])
        tau = functools.reduce(jnp.minimum, lists)

        best = a[0] + b[0]
        zsum = sum(jnp.where(c >= tau, jnp.exp(c - best), 0.0) for c in cand)
        for shift in rolls:
            zsum = zsum + pltpu.roll(zsum, shift, 0)
        scale = 0.5 / zsum
        top_extra = sum(jnp.where(a[0] + b[j] >= tau, 1.0, 0.0) for j in range(k // 2, k))

        c1_t, e1_t, r2_t, e2_t = [], [], [], []
        for t1, t2 in zip(s1, s2):
            c1 = jnp.zeros_like(t1)
            for j in range(k // 2):
                c1 = jnp.where(t1 + b[j] >= tau, float(j + 1), c1)
            c1_t.append(jnp.where(t1 == a[0], c1 + top_extra, c1))
            e1_t.append(jnp.exp(t1 - a[0]) * scale)
            r2 = jnp.zeros_like(t2)
            for j in range(k):
                r2 = jnp.where(t2 < b[j], float(j + 1), r2)
            r2_t.append(r2)
            e2_t.append(jnp.exp(t2 - b[0]))
        c1_ref[h, :, pl.ds(tok, LANES)] = jnp.concatenate(c1_t, axis=0)
        e1_ref[h, :, pl.ds(tok, LANES)] = jnp.concatenate(e1_t, axis=0)
        r2_ref[h, :, pl.ds(tok, LANES)] = jnp.concatenate(r2_t, axis=0).astype(BF16)
        e2_ref[h, :, pl.ds(tok, LANES)] = jnp.concatenate(e2_t, axis=0).astype(BF16)
        return carry

    lax.fori_loop(0, PEER_HEADS * nsub, body, 0, unroll=4)


def _candidate_mask():
    k = PEER_TOPK
    m = lax.broadcasted_iota(jnp.int32, (k, SUBLANES, LANES), 0)
    s = lax.broadcasted_iota(jnp.int32, (k, SUBLANES, LANES), 1)
    row_ok = (s < 4) & ((s + 1) * (m + 1) <= k)
    col_ok = (s >= 4) & (s < 7) & (4 + m < k) & ((5 + m) * (s - 3) <= k)
    return jnp.where(row_ok | col_ok, 0.0, -jnp.inf).astype(F32)


def _outscore_kernel(yf_ref, yb_ref, z_ref, cfm_ref, x_ref, mod_ref, gnw_ref, w_ref, n2w_ref,
                     wq_ref, keys_ref, neg_ref, u_ref, v_ref,
                     x1_ref, h2t_ref, c1_ref, e1_ref, r2_ref, e2_ref, ub_ref, vt_ref, q_ref):
    ub_ref[...] = u_ref[...].astype(BF16)
    vt_ref[...] = v_ref[...].T.astype(BF16)
    gw = D_SSM // SSM_GROUPS
    g = (yf_ref[0] + yb_ref[0]) * _silu(z_ref[0])
    parts = []
    for k in range(SSM_GROUPS):
        gk = g[:, k * gw:(k + 1) * gw]
        parts.append(gk * lax.rsqrt(jnp.mean(gk * gk, axis=-1, keepdims=True) + EPS))
    ssm = jnp.concatenate(parts, axis=-1) * gnw_ref[...]
    mixed = (jnp.dot(ssm.astype(BF16), w_ref[0:D_SSM, :], preferred_element_type=F32)
             + jnp.dot(cfm_ref[0], w_ref[D_SSM:, :], preferred_element_type=F32))
    x1 = x_ref[0] + mod_ref[0, 2:3, :] * mixed
    x1_ref[0] = x1
    h = x1 * lax.rsqrt(jnp.mean(x1 * x1, axis=-1, keepdims=True) + EPS) * n2w_ref[...]
    h2 = h * (1.0 + mod_ref[0, 4:5, :]) + mod_ref[0, 3:4, :]
    h2t_ref[...] = h2.T.astype(BF16)
    q_ref[...] = jnp.dot(h2.astype(BF16), wq_ref[...], preferred_element_type=F32).astype(BF16)
    _score_tokens(q_ref, keys_ref, neg_ref, c1_ref, e1_ref, r2_ref, e2_ref)


def _outscore(yf, yb, z, cfm, x, mod, gn_w, w_out_b, n2_w, wq_b, keys_b, u_tab, v_tab):
    b, l, d = x.shape
    t = b * l
    tm = min(OUT_TILE, l)
    neg = _candidate_mask()
    n_exp = u_tab.shape[0]
    rows = n_exp // (t // tm)
    assert rows * (t // tm) == n_exp and PEER_EBLK % rows == 0 and rows % LANES == 0
    per_slab = PEER_EBLK // rows
    tile = lambda i, j: i * (l // tm) + j
    tok = lambda w: pl.BlockSpec((1, tm, w), lambda i, j: (i, j, 0))
    const = lambda shape: pl.BlockSpec(shape, lambda i, j: (0,) * len(shape))
    big = lambda: pl.BlockSpec((PEER_HEADS, N_KEYS, tm), lambda i, j: (0, 0, tile(i, j)))
    big_shape = lambda dt: jax.ShapeDtypeStruct((PEER_HEADS, N_KEYS, t), dt)
    return pl.pallas_call(
        _outscore_kernel,
        grid=(b, l // tm),
        in_specs=[tok(D_SSM), tok(D_SSM), tok(D_SSM), tok(D_CONV), tok(d),
                  pl.BlockSpec((1, 6, d), lambda i, j: (i, 0, 0)),
                  const(gn_w.shape), const(w_out_b.shape), const(n2_w.shape),
                  const(wq_b.shape), const(keys_b.shape), const(neg.shape),
                  pl.BlockSpec((rows, d), lambda i, j: (tile(i, j), 0)),
                  pl.BlockSpec((rows, d), lambda i, j: (tile(i, j), 0))],
        out_specs=[tok(d), pl.BlockSpec((d, tm), lambda i, j: (0, tile(i, j))), big(), big(), big(), big(),
                   pl.BlockSpec((rows, d), lambda i, j: (tile(i, j), 0)),
                   pl.BlockSpec((None, d, rows), lambda i, j: (tile(i, j) // per_slab, 0, tile(i, j) % per_slab))],
        out_shape=[jax.ShapeDtypeStruct((b, l, d), F32), jax.ShapeDtypeStruct((d, t), BF16),
                   big_shape(F32), big_shape(F32), big_shape(BF16), big_shape(BF16),
                   jax.ShapeDtypeStruct((n_exp, d), BF16),
                   jax.ShapeDtypeStruct((n_exp // PEER_EBLK, d, PEER_EBLK), BF16)],
        scratch_shapes=[pltpu.VMEM((tm, PEER_HEADS * PEER_KEY_DIM), BF16)],
        compiler_params=_params(("parallel", "parallel")),
        name="outproj_score",
    )(yf, yb, z, cfm, x, mod, gn_w, w_out_b, n2_w, wq_b, keys_b, neg, u_tab, v_tab)


def _key_row_bf16(ref, h, key0, j):
    tile = ref[h, pl.ds(key0 + j // SUBLANES * SUBLANES, SUBLANES), :]
    row = tile[j % SUBLANES:j % SUBLANES + 1, :]
    rows = jnp.broadcast_to(row, (BF16_ROWS, ref.shape[-1])).astype(BF16)
    return jnp.tile(rows, (N_KEYS // BF16_ROWS, 1))


def _peer_kernel(ht_ref, u_ref, vt_ref, c1_ref, e1_ref, r2_ref, e2_ref, x1_ref, mod_ref, fnw_ref,
                 o_ref, acc_ref):
    eb = pl.program_id(1)
    keys_per_step = PEER_EBLK // N_KEYS
    key0 = pl.multiple_of(eb * keys_per_step, keys_per_step)
    starts = [sum(PEER_SUBS[:i]) for i in range(len(PEER_SUBS))]

    @pl.when(eb == 0)
    def _():
        zero_rows = jnp.zeros((acc_ref.shape[0], LANES), BF16)
        acc_ref[...] = jnp.dot(zero_rows, ht_ref[0:LANES, :], preferred_element_type=F32)

    def first_matmul(sub):
        return jnp.dot(u_ref[starts[sub]:starts[sub] + PEER_SUBS[sub], :], ht_ref[...],
                       preferred_element_type=F32)

    def build_gates(sub):
        gates = []
        for kk in range(PEER_SUBS[sub] // N_KEYS):
            j = starts[sub] // N_KEYS + kk
            g = jnp.zeros((N_KEYS, ht_ref.shape[1]), BF16)
            for h in range(PEER_HEADS):
                c1 = _key_row_bf16(c1_ref, h, key0, j)
                e1 = _key_row_bf16(e1_ref, h, key0, j)
                g = g + jnp.where(r2_ref[h] < c1, e2_ref[h], jnp.zeros((), BF16)) * e1
            gates.append(g)
        return jnp.concatenate(gates, axis=0)

    nsub = len(PEER_SUBS)
    pre = first_matmul(0)
    weighted = []
    for sub in range(nsub):
        nxt = first_matmul(sub + 1) if sub + 1 < nsub else None
        act = pre * (1.0 + lax.erf(pre * (2.0 ** -0.5)))
        weighted.append(act.astype(BF16) * build_gates(sub))
        pre = nxt
    acc_ref[...] = jnp.dot(vt_ref[...], jnp.concatenate(weighted, axis=0),
                           preferred_element_type=F32) + acc_ref[...]

    @pl.when(eb == pl.num_programs(1) - 1)
    def _():
        x2 = x1_ref[...] + mod_ref[0, 5:6, :] * acc_ref[...].T
        o_ref[...] = x2 * lax.rsqrt(jnp.mean(x2 * x2, axis=-1, keepdims=True) + EPS) * fnw_ref[...]


def _peer(h2t, u_b, vt_b, c1, e1, r2, e2, x1, mod, fn_w, seq):
    d, t = h2t.shape
    tm = min(PEER_TILE, seq)
    n_exp = u_b.shape[0]
    big = lambda: pl.BlockSpec((PEER_HEADS, N_KEYS, tm), lambda j, e: (0, 0, j))
    return pl.pallas_call(
        _peer_kernel,
        grid=(t // tm, n_exp // PEER_EBLK),
        in_specs=[pl.BlockSpec((d, tm), lambda j, e: (0, j)),
                  pl.BlockSpec((PEER_EBLK, d), lambda j, e: (e, 0)),
                  pl.BlockSpec((None, d, PEER_EBLK), lambda j, e: (e, 0, 0)),
                  big(), big(), big(), big(),
                  pl.BlockSpec((tm, d), lambda j, e: (j, 0)),
                  pl.BlockSpec((1, 6, d), lambda j, e: ((j * tm) // seq, 0, 0)),
                  pl.BlockSpec((1, d), lambda j, e: (0, 0))],
        out_specs=pl.BlockSpec((tm, d), lambda j, e: (j, 0)),
        out_shape=jax.ShapeDtypeStruct((t, d), F32),
        scratch_shapes=[pltpu.VMEM((d, tm), F32)],
        compiler_params=_params(("parallel", "arbitrary")),
        name="peer_dense",
    )(h2t, u_b, vt_b, c1, e1, r2, e2, x1, mod, fn_w)


def _pad_lanes(v, width=LANES):
    flat = v.reshape(1, -1)
    return jnp.pad(flat, ((0, 0), (0, width - flat.shape[1])))


def _head_expand(col0):
    rows = lax.broadcasted_iota(jnp.int32, (LANES, D_SSM), 0)
    heads = lax.broadcasted_iota(jnp.int32, (LANES, D_SSM), 1) // SSM_HEADDIM
    return (rows == heads + col0).astype(BF16)


def kernel(x, c, ctx, c_ctx, ada_w, ada_b, norm1_w, norm2_w, w_in, ssm_conv_w, ssm_conv_b, ssm_dt_bias,
           ssm_a_log, ssm_d, ssm_norm_w, cfm_conv_w, cfm_conv_b, cfm_ln_w, cfm_ln_b, w_out, peer_wq,
           peer_subkeys, peer_u, peer_v, final_norm_w):
    depth = ada_w.shape[0]
    assert depth == 1, "single-layer configuration"
    b, l, d = x.shape
    i = 0

    c_rows = jnp.concatenate([c, c_ctx[None], jnp.zeros((2 * SUBLANES - b - 1, d), F32)], axis=0)
    mod_all = _ada(c_rows, ada_w[i], ada_b[i])
    mod = mod_all[:b].reshape(b, 6, d)
    mod_ctx = mod_all[b:b + 1].reshape(1, 6, d)

    wi = w_in[i]
    o_dt = D_SSM + XBC_DIM
    w_r = jnp.concatenate([wi[:, :o_dt], wi[:, o_dt + 2 * SSM_HEADS:], wi[:, o_dt:o_dt + 2 * SSM_HEADS],
                           jnp.zeros((d, LANES - 2 * SSM_HEADS), F32)], axis=1).astype(BF16)
    z_l, xbc_l, u_l, dt_l = _inproj(x, mod, True, norm1_w[i], w_r)
    _, xbc_c, _, dt_c = _inproj(ctx, mod_ctx, False, norm1_w[i], w_r)

    conv_w = jnp.pad(ssm_conv_w[i], ((0, SUBLANES - SSM_CONV), (0, 0)))
    conv_b = ssm_conv_b[i].reshape(1, XBC_DIM)
    dt_bias = _pad_lanes(ssm_dt_bias[i])
    a_log = _pad_lanes(ssm_a_log[i])
    d_skip = jnp.repeat(ssm_d[i], SSM_HEADDIM).reshape(1, D_SSM)
    zero_state = jnp.zeros((b, SSM_STATE, D_SSM), F32)
    y_dirs = []
    act_c = act_l = None
    for reverse in (False, True):
        expand = _head_expand(SSM_HEADS if reverse else 0)
        args = (conv_w, conv_b, dt_bias, a_log, d_skip, expand, reverse)
        if not reverse:
            _, h_ctx, act_c = _ssd(xbc_c, dt_c, zero_state, *args, add_skip=False, activated=False)
            y_dir, _, act_l = _ssd(xbc_l, dt_l, h_ctx, *args, add_skip=True, activated=False)
        else:
            _, h_ctx = _ssd(act_c, dt_c, zero_state, *args, add_skip=False, activated=True)
            y_dir, _ = _ssd(act_l, dt_l, h_ctx, *args, add_skip=False, activated=True)
        y_dirs.append(y_dir)

    cfm = _cfm(u_l, jnp.pad(cfm_conv_w[i], ((0, 1), (0, 0))), cfm_conv_b[i].reshape(1, D_CONV),
               cfm_ln_w[i].reshape(1, D_CONV), cfm_ln_b[i].reshape(1, D_CONV))

    t = b * l
    keys_b = peer_subkeys[i].reshape(PEER_HEADS * 2, N_KEYS, PEER_KEY_DIM // 2).astype(BF16)
    x1, h2t, c1, e1, r2, e2, u_b, vt_b = _outscore(
        y_dirs[0], y_dirs[1], z_l, cfm, x, mod, ssm_norm_w[i].reshape(1, D_SSM), w_out[i].astype(BF16),
        norm2_w[i].reshape(1, d), peer_wq[i].astype(BF16), keys_b, peer_u[i], peer_v[i])
    out = _peer(h2t, u_b, vt_b, c1, e1, r2, e2,
                x1.reshape(t, d), mod, final_norm_w.reshape(1, d), l)
    return out.reshape(b, l, d)
```

```python
import functools

import jax
import jax.numpy as jnp
from jax import lax
from jax.experimental import pallas as pl
from jax.experimental.pallas import tpu as pltpu

F32 = jnp.float32
BF16 = jnp.bfloat16

D_MODEL = 1024
GRID_W = 64
D_SSM = 512
SSM_HEADDIM = 64
SSM_HEADS = 8
SSM_STATE = 128
SSM_GROUPS = 2
SSM_CONV = 5
SSD_CHUNK = 128
XBC_DIM = D_SSM + 2 * SSM_GROUPS * SSM_STATE
D_CONV = 512
CONV_WIDTH = 31
PEER_HEADS = 8
PEER_KEY_DIM = 256
N_KEYS = 128
PEER_TOPK = 16
EPS = 1e-6

LANES = 128
SUBLANES = 8
VMEM_LIMIT = 56 * 1024 * 1024

IN_TILE = 512
SSD_ROWS = 8
CFM_ROWS = 16
OUT_TILE = 512
PEER_TILE = 512
PEER_EBLK = 2048
PEER_SUBS = (128,) * 16
assert sum(PEER_SUBS) == PEER_EBLK and all(s % N_KEYS == 0 for s in PEER_SUBS)
BF16_ROWS = 2 * SUBLANES


def _params(sem):
    return pltpu.CompilerParams(dimension_semantics=sem, vmem_limit_bytes=VMEM_LIMIT)


def _silu(v):
    return v * jax.nn.sigmoid(v)


def _split3(v):
    hi = v.astype(BF16)
    r1 = v - hi.astype(F32)
    mid = r1.astype(BF16)
    lo = (r1 - mid.astype(F32)).astype(BF16)
    return hi, mid, lo


def _dot01_right(v, m01):
    return sum(jnp.dot(p, m01, preferred_element_type=F32) for p in _split3(v))


def _dot01_left(m01, v):
    return sum(jnp.dot(m01, p, preferred_element_type=F32) for p in _split3(v))


def _ada_kernel(c_ref, w_ref, b_ref, o_ref):
    sc = _silu(c_ref[...])
    o_ref[...] = jnp.dot(sc.astype(BF16), w_ref[...].astype(BF16),
                         preferred_element_type=F32) + b_ref[...]


def _ada(c_rows, ada_w, ada_b):
    n = ada_w.shape[1]
    tn = 768
    return pl.pallas_call(
        _ada_kernel,
        grid=(n // tn,),
        in_specs=[pl.BlockSpec(c_rows.shape, lambda j: (0, 0)),
                  pl.BlockSpec((D_MODEL, tn), lambda j: (0, j)),
                  pl.BlockSpec((1, tn), lambda j: (0, j))],
        out_specs=pl.BlockSpec((c_rows.shape[0], tn), lambda j: (0, j)),
        out_shape=jax.ShapeDtypeStruct((c_rows.shape[0], n), F32),
        compiler_params=_params(("parallel",)),
        name="ada",
    )(c_rows, ada_w, ada_b.reshape(1, n))


def _inproj_kernel(x_ref, mod_ref, nw_ref, w_ref, z_ref, xbc_ref, u_ref, dt_ref):
    xv = x_ref[0]
    h = xv * lax.rsqrt(jnp.mean(xv * xv, axis=-1, keepdims=True) + EPS) * nw_ref[...]
    h = h * (1.0 + mod_ref[0, 1:2, :]) + mod_ref[0, 0:1, :]
    hb = h.astype(BF16)

    def proj(lo, hi):
        return jnp.dot(hb, w_ref[:, lo:hi], preferred_element_type=F32)

    z_ref[0] = proj(0, D_SSM)
    xbc_ref[0] = proj(D_SSM, D_SSM + XBC_DIM)
    o = D_SSM + XBC_DIM
    u_ref[0] = proj(o, o + D_CONV) * jax.nn.sigmoid(proj(o + D_CONV, o + 2 * D_CONV))
    dt_ref[0] = proj(o + 2 * D_CONV, o + 2 * D_CONV + LANES)


def _inproj(x, mod, per_batch_mod, norm_w, w_r):
    b, l, d = x.shape
    tm = min(IN_TILE, l)
    mod_idx = (lambda i, j: (i, 0, 0)) if per_batch_mod else (lambda i, j: (0, 0, 0))
    tok = lambda w: pl.BlockSpec((1, tm, w), lambda i, j: (i, j, 0))
    return pl.pallas_call(
        _inproj_kernel,
        grid=(b, l // tm),
        in_specs=[tok(d),
                  pl.BlockSpec((1, 6, d), mod_idx),
                  pl.BlockSpec((1, d), lambda i, j: (0, 0)),
                  pl.BlockSpec(w_r.shape, lambda i, j: (0, 0))],
        out_specs=[tok(D_SSM), tok(XBC_DIM), tok(D_CONV), tok(LANES)],
        out_shape=[jax.ShapeDtypeStruct((b, l, w), F32) for w in (D_SSM, XBC_DIM, D_CONV, LANES)],
        compiler_params=_params(("parallel", "parallel")),
        name="inproj",
    )(x, mod, norm_w.reshape(1, d), w_r)


def _ssd_kernel(cur_ref, prev_ref, next_ref, dt_ref, h0_ref, cw_ref, cb_ref, dtb_ref, alog_ref,
                dskip_ref, expand_ref, y_ref, hfin_ref, *rest, reverse, nc, add_skip, activated):
    act_ref, ext_ref, state_ref = (None, *rest) if activated else rest
    q = SSD_CHUNK
    halo = SUBLANES
    pad = (SSM_CONV - 1) // 2
    gw = D_SSM // SSM_GROUPS
    hpg = SSM_HEADS // SSM_GROUPS
    col0 = SSM_HEADS if reverse else 0
    c = pl.program_id(1)
    cc = (nc - 1 - c) if reverse else c

    @pl.when(c == 0)
    def _():
        state_ref[...] = h0_ref[...]

    ri = lax.broadcasted_iota(jnp.int32, (q, q), 0)
    ci = lax.broadcasted_iota(jnp.int32, (q, q), 1)
    causal = (ci >= ri) if reverse else (ci <= ri)
    causal01 = jnp.where(causal, 1.0, 0.0).astype(BF16)
    a_row = -jnp.exp(alog_ref[...])

    for r in range(cur_ref.shape[0]):
        if activated:
            act = cur_ref[r]
        else:
            ext_ref[r, 0:halo, :] = jnp.where(cc == 0, 0.0, prev_ref[r])
            ext_ref[r, halo:halo + q, :] = cur_ref[r]
            ext_ref[r, halo + q:2 * halo + q, :] = jnp.where(cc == nc - 1, 0.0, next_ref[r])
            conv = cb_ref[...] + cw_ref[0:1, :] * ext_ref[r, halo - pad:halo - pad + q, :]
            for k in range(1, SSM_CONV):
                conv = conv + cw_ref[k:k + 1, :] * ext_ref[r, halo - pad + k:halo - pad + k + q, :]
            act = _silu(conv)
            act_ref[r] = act

        dtv = dt_ref[r] + dtb_ref[...]
        dt = jnp.maximum(dtv, 0.0) + jnp.log(1.0 + jnp.exp(-jnp.abs(dtv)))
        acum = _dot01_left(causal01, dt * a_row)
        acum_t = acum.T
        tot = acum[0:1, :] if reverse else acum[q - 1:q, :]
        stacked = jnp.concatenate([dt, jnp.exp(tot - acum), jnp.exp(acum),
                                   jnp.broadcast_to(jnp.exp(tot), (SUBLANES, LANES))], axis=0)
        spread = _dot01_right(stacked, expand_ref[...])
        dt_e, dte_e, ea_e = spread[0:q], spread[q:2 * q], spread[2 * q:3 * q]
        etot_e = spread[3 * q:3 * q + 1]

        xs = act[:, :D_SSM]
        xdt = xs * dt_e
        xdt_b = xdt.astype(BF16)
        w_b = (xdt * dte_e).astype(BF16)
        for g in range(SSM_GROUPS):
            gs = slice(g * gw, (g + 1) * gw)
            b_g = act[:, D_SSM + g * SSM_STATE:D_SSM + (g + 1) * SSM_STATE]
            c_g = act[:, D_SSM + (SSM_GROUPS + g) * SSM_STATE:D_SSM + (SSM_GROUPS + g + 1) * SSM_STATE]
            c_b = c_g.astype(BF16)
            bt_b = b_g.T.astype(BF16)
            cb = jnp.dot(c_b, bt_b, preferred_element_type=F32)
            st = state_ref[r, :, gs]
            y_g = jnp.dot(c_b, st.astype(BF16), preferred_element_type=F32) * ea_e[:, gs]
            y_heads = []
            for hh in range(hpg):
                h = g * hpg + hh
                col = col0 + h
                diff = acum[:, col:col + 1] - acum_t[col:col + 1, :]
                decay = jnp.where(causal, jnp.exp(jnp.minimum(diff, 0.0)), 0.0)
                y_heads.append(jnp.dot((cb * decay).astype(BF16),
                                       xdt_b[:, h * SSM_HEADDIM:(h + 1) * SSM_HEADDIM],
                                       preferred_element_type=F32))
            y_g = y_g + jnp.concatenate(y_heads, axis=-1)
            if add_skip:
                y_g = y_g + dskip_ref[:, gs] * xs[:, gs]
            y_ref[r, :, gs] = y_g
            state_ref[r, :, gs] = st * etot_e[:, gs] + jnp.dot(bt_b, w_b[:, gs], preferred_element_type=F32)
    hfin_ref[...] = state_ref[...]


def _ssd(xbc, dtp, h0, conv_w, conv_b, dt_bias, a_log, d_skip, expand, reverse, add_skip, activated):
    b, l, _ = xbc.shape
    q = SSD_CHUNK
    nc = l // q
    nr = SSD_ROWS if b % SSD_ROWS == 0 else 1
    hb = q // SUBLANES
    pos = (lambda c: nc - 1 - c) if reverse else (lambda c: c)
    const = lambda shape: pl.BlockSpec(shape, lambda i, c: (0,) * len(shape))
    kern = functools.partial(_ssd_kernel, reverse=reverse, nc=nc, add_skip=add_skip, activated=activated)
    chunk = lambda w: pl.BlockSpec((nr, q, w), lambda i, c: (i, pos(c), 0))
    act_spec = [] if activated else [chunk(XBC_DIM)]
    act_shape = [] if activated else [jax.ShapeDtypeStruct((b, l, XBC_DIM), F32)]
    return pl.pallas_call(
        kern,
        grid=(b // nr, nc),
        in_specs=[pl.BlockSpec((nr, q, XBC_DIM), lambda i, c: (i, pos(c), 0)),
                  pl.BlockSpec((nr, SUBLANES, XBC_DIM), lambda i, c: (i, jnp.maximum(pos(c) * hb - 1, 0), 0)),
                  pl.BlockSpec((nr, SUBLANES, XBC_DIM),
                               lambda i, c: (i, jnp.minimum((pos(c) + 1) * hb, l // SUBLANES - 1), 0)),
                  pl.BlockSpec((nr, q, LANES), lambda i, c: (i, pos(c), 0)),
                  pl.BlockSpec((nr, SSM_STATE, D_SSM), lambda i, c: (i, 0, 0)),
                  const(conv_w.shape), const(conv_b.shape), const(dt_bias.shape), const(a_log.shape),
                  const(d_skip.shape), const(expand.shape)],
        out_specs=[chunk(D_SSM), pl.BlockSpec((nr, SSM_STATE, D_SSM), lambda i, c: (i, 0, 0))] + act_spec,
        out_shape=[jax.ShapeDtypeStruct((b, l, D_SSM), F32),
                   jax.ShapeDtypeStruct((b, SSM_STATE, D_SSM), F32)] + act_shape,
        scratch_shapes=[pltpu.VMEM((nr, q + 2 * SUBLANES, XBC_DIM), F32),
                        pltpu.VMEM((nr, SSM_STATE, D_SSM), F32)],
        compiler_params=_params(("parallel", "arbitrary")),
        name="ssd_bwd" if reverse else "ssd_fwd",
    )(xbc, xbc, xbc, dtp, h0, conv_w, conv_b, dt_bias, a_log, d_skip, expand)


def _cfm_kernel(cur_ref, prev_ref, next_ref, w_ref, b_ref, lnw_ref, lnb_ref, o_ref, hpad_ref, vbuf_ref,
                shift_ref, *, nb):
    rows = CFM_ROWS
    band = rows * GRID_W
    half = D_CONV // 2
    pad = (CONV_WIDTH - 1) // 2
    lead = 2 * SUBLANES
    r = pl.program_id(1)

    zeros = jnp.zeros((rows, lead, half), F32)
    hpad_ref[:, 0:lead, :] = zeros
    hpad_ref[:, lead + GRID_W:2 * lead + GRID_W, :] = zeros
    hpad_ref[:, lead:lead + GRID_W, :] = cur_ref[0, :, 0:half].reshape(rows, GRID_W, half)
    vbuf_ref[0:band, :] = jnp.where(r == 0, 0.0, prev_ref[0])
    vbuf_ref[band:2 * band, :] = cur_ref[0, :, half:D_CONV]
    vbuf_ref[2 * band:3 * band, :] = jnp.where(r == nb - 1, 0.0, next_ref[0])

    span = shift_ref.shape[1]

    def row_body(i, carry):
        for s in range(SUBLANES):
            shift_ref[s] = hpad_ref[i, s:s + span, :]
        acc_h = jnp.zeros((GRID_W, half), F32)
        acc_v = jnp.zeros((GRID_W, half), F32)
        for k in range(CONV_WIDTH):
            first = lead - pad + k
            tile0 = first // SUBLANES * SUBLANES
            acc_h = acc_h + w_ref[k:k + 1, 0:half] * shift_ref[first % SUBLANES, tile0:tile0 + GRID_W, :]
            start = pl.multiple_of(band + (i + k - pad) * GRID_W, GRID_W)
            acc_v = acc_v + w_ref[k:k + 1, half:D_CONV] * vbuf_ref[pl.ds(start, GRID_W), :]
        conv = jnp.concatenate([acc_h, acc_v], axis=-1) + b_ref[...]
        mu = jnp.mean(conv, axis=-1, keepdims=True)
        cen = conv - mu
        var = jnp.mean(cen * cen, axis=-1, keepdims=True)
        o_ref[0, pl.ds(pl.multiple_of(i * GRID_W, GRID_W), GRID_W), :] = _silu(
            cen * lax.rsqrt(var + EPS) * lnw_ref[...] + lnb_ref[...]).astype(BF16)
        return carry

    lax.fori_loop(0, rows, row_body, 0, unroll=4)


def _cfm(u, conv_w, conv_b, ln_w, ln_b):
    b, l, _ = u.shape
    band = CFM_ROWS * GRID_W
    nb = l // band
    half = D_CONV // 2
    const = lambda shape: pl.BlockSpec(shape, lambda i, r: (0,) * len(shape))
    return pl.pallas_call(
        functools.partial(_cfm_kernel, nb=nb),
        grid=(b, nb),
        in_specs=[pl.BlockSpec((1, band, D_CONV), lambda i, r: (i, r, 0)),
                  pl.BlockSpec((1, band, half), lambda i, r: (i, jnp.maximum(r - 1, 0), 1)),
                  pl.BlockSpec((1, band, half), lambda i, r: (i, jnp.minimum(r + 1, nb - 1), 1)),
                  const(conv_w.shape), const(conv_b.shape), const(ln_w.shape), const(ln_b.shape)],
        out_specs=pl.BlockSpec((1, band, D_CONV), lambda i, r: (i, r, 0)),
        out_shape=jax.ShapeDtypeStruct((b, l, D_CONV), BF16),
        scratch_shapes=[pltpu.VMEM((CFM_ROWS, GRID_W + 4 * SUBLANES, half), F32),
                        pltpu.VMEM((3 * band, half), F32),
                        pltpu.VMEM((SUBLANES, GRID_W + 3 * SUBLANES, half), F32)],
        compiler_params=_params(("parallel", "parallel")),
        name="cfm",
    )(u, u, u, conv_w, conv_b, ln_w, ln_b)


def _oddeven_merge(lo, hi, r):
    step = r * 2
    if step < hi - lo:
        yield from _oddeven_merge(lo, hi, step)
        yield from _oddeven_merge(lo + r, hi, step)
        yield from [(i, i + r) for i in range(lo + r, hi - r, step)]
    else:
        yield (lo, lo + r)


def _oddeven_sort(lo, hi):
    if hi > lo:
        mid = lo + (hi - lo) // 2
        yield from _oddeven_sort(lo, mid)
        yield from _oddeven_sort(mid + 1, hi)
        yield from _oddeven_merge(lo, hi, 1)


SORT_NET = tuple(_oddeven_sort(0, PEER_TOPK - 1))


def _cmp_exchange(v, i, j):
    v[i], v[j] = jnp.maximum(v[i], v[j]), jnp.minimum(v[i], v[j])


def _bitonic_sort(v):
    d = len(v) // 2
    while d:
        for i in range(len(v)):
            if not i & d:
                _cmp_exchange(v, i, i + d)
        d //= 2


def _merge_top(v, w):
    n = len(v)
    return [jnp.maximum(v[i], w[n - 1 - i]) for i in range(n)]


def _sublane_rolls():
    shift = SUBLANES // 2
    while shift:
        yield shift
        shift //= 2


def _top_sorted(v):
    for shift in _sublane_rolls():
        v = _merge_top(v, [pltpu.roll(x, shift, 0) for x in v])
        _bitonic_sort(v)
    return v


def _score_tokens(q_ref, keys_ref, neg_ref, c1_ref, e1_ref, r2_ref, e2_ref):
    tm = q_ref.shape[0]
    nsub = tm // LANES
    kd = PEER_KEY_DIM // 2
    k = PEER_TOPK
    ntile = N_KEYS // SUBLANES

    def body(it, carry):
        h = it // nsub
        tok = pl.multiple_of((it % nsub) * LANES, LANES)
        nt = (((1,), (1,)), ((), ()))

        def score_tiles(p):
            qs = q_ref[pl.ds(tok, LANES), pl.ds(pl.multiple_of((2 * h + p) * kd, kd), kd)]
            s = lax.dot_general(keys_ref[2 * h + p], qs, nt, preferred_element_type=F32)
            return [s[i * SUBLANES:(i + 1) * SUBLANES, :] for i in range(ntile)]

        def top(tiles):
            v = list(tiles)
            for i, j in SORT_NET:
                _cmp_exchange(v, i, j)
            return _top_sorted(v)

        s1 = score_tiles(0)
        s2 = score_tiles(1)
        a = top(s1)
        b = top(s2)

        sub = lax.broadcasted_iota(jnp.int32, (SUBLANES, LANES), 0)
        pick = lambda lo, x, y: jnp.where(sub < lo, x, y)
        base = pick(4, pick(2, pick(1, a[0], a[1]), pick(3, a[2], a[3])), pick(5, b[0], pick(6, b[1], b[2])))
        cand = []
        for m in range(k):
            other = pick(4, b[m], a[4 + m]) if 4 + m < k else b[m]
            cand.append((other + base) + neg_ref[m])
        lists = list(cand)
        rolls = list(_sublane_rolls())
        for shift in rolls[:-1]:
            lists = _merge_top(lists, [pltpu.roll(x, shift, 0) for x in lists])
            _bitonic_sort(lists)
        lists = _merge_top(lists, [pltpu.roll(x, rolls[-1], 0) for x in lists])
        tau = functools.reduce(jnp.minimum, lists)

        best = a[0] + b[0]
        zsum = sum(jnp.where(c >= tau, jnp.exp(c - best), 0.0) for c in cand)
        for shift in rolls:
            zsum = zsum + pltpu.roll(zsum, shift, 0)
        scale = 0.5 / zsum
        top_extra = sum(jnp.where(a[0] + b[j] >= tau, 1.0, 0.0) for j in range(k // 2, k))

        c1_t, e1_t, r2_t, e2_t = [], [], [], []
        for t1, t2 in zip(s1, s2):
            c1 = jnp.zeros_like(t1)
            for j in range(k // 2):
                c1 = jnp.where(t1 + b[j] >= tau, float(j + 1), c1)
            c1_t.append(jnp.where(t1 == a[0], c1 + top_extra, c1))
            e1_t.append(jnp.exp(t1 - a[0]) * scale)
            r2 = jnp.zeros_like(t2)
            for j in range(k):
                r2 = jnp.where(t2 < b[j], float(j + 1), r2)
            r2_t.append(r2)
            e2_t.append(jnp.exp(t2 - b[0]))
        c1_ref[h, :, pl.ds(tok, LANES)] = jnp.concatenate(c1_t, axis=0)
        e1_ref[h, :, pl.ds(tok, LANES)] = jnp.concatenate(e1_t, axis=0)
        r2_ref[h, :, pl.ds(tok, LANES)] = jnp.concatenate(r2_t, axis=0).astype(BF16)
        e2_ref[h, :, pl.ds(tok, LANES)] = jnp.concatenate(e2_t, axis=0).astype(BF16)
        return carry

    lax.fori_loop(0, PEER_HEADS * nsub, body, 0, unroll=8)


def _candidate_mask():
    k = PEER_TOPK
    m = lax.broadcasted_iota(jnp.int32, (k, SUBLANES, LANES), 0)
    s = lax.broadcasted_iota(jnp.int32, (k, SUBLANES, LANES), 1)
    row_ok = (s < 4) & ((s + 1) * (m + 1) <= k)
    col_ok = (s >= 4) & (s < 7) & (4 + m < k) & ((5 + m) * (s - 3) <= k)
    return jnp.where(row_ok | col_ok, 0.0, -jnp.inf).astype(F32)


def _outscore_kernel(yf_ref, yb_ref, z_ref, cfm_ref, x_ref, mod_ref, gnw_ref, w_ref, n2w_ref,
                     wq_ref, keys_ref, neg_ref, u_ref, v_ref,
                     x1_ref, h2t_ref, c1_ref, e1_ref, r2_ref, e2_ref, ub_ref, vt_ref, q_ref):
    ub_ref[...] = u_ref[...].astype(BF16)
    vt_ref[...] = v_ref[...].T.astype(BF16)
    gw = D_SSM // SSM_GROUPS
    g = (yf_ref[0] + yb_ref[0]) * _silu(z_ref[0])
    parts = []
    for k in range(SSM_GROUPS):
        gk = g[:, k * gw:(k + 1) * gw]
        parts.append(gk * lax.rsqrt(jnp.mean(gk * gk, axis=-1, keepdims=True) + EPS))
    ssm = jnp.concatenate(parts, axis=-1) * gnw_ref[...]
    mixed = (jnp.dot(ssm.astype(BF16), w_ref[0:D_SSM, :], preferred_element_type=F32)
             + jnp.dot(cfm_ref[0], w_ref[D_SSM:, :], preferred_element_type=F32))
    x1 = x_ref[0] + mod_ref[0, 2:3, :] * mixed
    x1_ref[0] = x1
    h = x1 * lax.rsqrt(jnp.mean(x1 * x1, axis=-1, keepdims=True) + EPS) * n2w_ref[...]
    h2 = h * (1.0 + mod_ref[0, 4:5, :]) + mod_ref[0, 3:4, :]
    h2t_ref[...] = h2.T.astype(BF16)
    q_ref[...] = jnp.dot(h2.astype(BF16), wq_ref[...], preferred_element_type=F32).astype(BF16)
    _score_tokens(q_ref, keys_ref, neg_ref, c1_ref, e1_ref, r2_ref, e2_ref)


def _outscore(yf, yb, z, cfm, x, mod, gn_w, w_out_b, n2_w, wq_b, keys_b, u_tab, v_tab):
    b, l, d = x.shape
    t = b * l
    tm = min(OUT_TILE, l)
    neg = _candidate_mask()
    n_exp = u_tab.shape[0]
    rows = n_exp // (t // tm)
    assert rows * (t // tm) == n_exp and PEER_EBLK % rows == 0 and rows % LANES == 0
    per_slab = PEER_EBLK // rows
    tile = lambda i, j: i * (l // tm) + j
    tok = lambda w: pl.BlockSpec((1, tm, w), lambda i, j: (i, j, 0))
    const = lambda shape: pl.BlockSpec(shape, lambda i, j: (0,) * len(shape))
    big = lambda: pl.BlockSpec((PEER_HEADS, N_KEYS, tm), lambda i, j: (0, 0, tile(i, j)))
    big_shape = lambda dt: jax.ShapeDtypeStruct((PEER_HEADS, N_KEYS, t), dt)
    return pl.pallas_call(
        _outscore_kernel,
        grid=(b, l // tm),
        in_specs=[tok(D_SSM), tok(D_SSM), tok(D_SSM), tok(D_CONV), tok(d),
                  pl.BlockSpec((1, 6, d), lambda i, j: (i, 0, 0)),
                  const(gn_w.shape), const(w_out_b.shape), const(n2_w.shape),
                  const(wq_b.shape), const(keys_b.shape), const(neg.shape),
                  pl.BlockSpec((rows, d), lambda i, j: (tile(i, j), 0)),
                  pl.BlockSpec((rows, d), lambda i, j: (tile(i, j), 0))],
        out_specs=[tok(d), pl.BlockSpec((d, tm), lambda i, j: (0, tile(i, j))), big(), big(), big(), big(),
                   pl.BlockSpec((rows, d), lambda i, j: (tile(i, j), 0)),
                   pl.BlockSpec((None, d, rows), lambda i, j: (tile(i, j) // per_slab, 0, tile(i, j) % per_slab))],
        out_shape=[jax.ShapeDtypeStruct((b, l, d), F32), jax.ShapeDtypeStruct((d, t), BF16),
                   big_shape(F32), big_shape(F32), big_shape(BF16), big_shape(BF16),
                   jax.ShapeDtypeStruct((n_exp, d), BF16),
                   jax.ShapeDtypeStruct((n_exp // PEER_EBLK, d, PEER_EBLK), BF16)],
        scratch_shapes=[pltpu.VMEM((tm, PEER_HEADS * PEER_KEY_DIM), BF16)],
        compiler_params=_params(("parallel", "parallel")),
        name="outproj_score",
    )(yf, yb, z, cfm, x, mod, gn_w, w_out_b, n2_w, wq_b, keys_b, neg, u_tab, v_tab)


def _key_row_bf16(ref, h, key0, j):
    tile = ref[h, pl.ds(key0 + j // SUBLANES * SUBLANES, SUBLANES), :]
    row = tile[j % SUBLANES:j % SUBLANES + 1, :]
    rows = jnp.broadcast_to(row, (BF16_ROWS, ref.shape[-1])).astype(BF16)
    return jnp.tile(rows, (N_KEYS // BF16_ROWS, 1))


def _peer_kernel(ht_ref, u_ref, vt_ref, c1_ref, e1_ref, r2_ref, e2_ref, x1_ref, mod_ref, fnw_ref,
                 o_ref, acc_ref):
    eb = pl.program_id(1)
    keys_per_step = PEER_EBLK // N_KEYS
    key0 = pl.multiple_of(eb * keys_per_step, keys_per_step)
    starts = [sum(PEER_SUBS[:i]) for i in range(len(PEER_SUBS))]

    @pl.when(eb == 0)
    def _():
        acc_ref[...] = jnp.zeros_like(acc_ref)

    def first_matmul(sub):
        return jnp.dot(u_ref[starts[sub]:starts[sub] + PEER_SUBS[sub], :], ht_ref[...],
                       preferred_element_type=F32)

    def build_gates(sub):
        gates = []
        for kk in range(PEER_SUBS[sub] // N_KEYS):
            j = starts[sub] // N_KEYS + kk
            g = jnp.zeros((N_KEYS, ht_ref.shape[1]), BF16)
            for h in range(PEER_HEADS):
                c1 = _key_row_bf16(c1_ref, h, key0, j)
                e1 = _key_row_bf16(e1_ref, h, key0, j)
                g = g + jnp.where(r2_ref[h] < c1, e2_ref[h], jnp.zeros((), BF16)) * e1
            gates.append(g)
        return jnp.concatenate(gates, axis=0)

    nsub = len(PEER_SUBS)
    pre = first_matmul(0)
    weighted = []
    for sub in range(nsub):
        nxt = first_matmul(sub + 1) if sub + 1 < nsub else None
        act = pre * (1.0 + lax.erf(pre * (2.0 ** -0.5)))
        weighted.append(act.astype(BF16) * build_gates(sub))
        pre = nxt
    acc_ref[...] = jnp.dot(vt_ref[...], jnp.concatenate(weighted, axis=0),
                           preferred_element_type=F32) + acc_ref[...]

    @pl.when(eb == pl.num_programs(1) - 1)
    def _():
        x2 = x1_ref[...] + mod_ref[0, 5:6, :] * acc_ref[...].T
        o_ref[...] = x2 * lax.rsqrt(jnp.mean(x2 * x2, axis=-1, keepdims=True) + EPS) * fnw_ref[...]


def _peer(h2t, u_b, vt_b, c1, e1, r2, e2, x1, mod, fn_w, seq):
    d, t = h2t.shape
    tm = min(PEER_TILE, seq)
    n_exp = u_b.shape[0]
    big = lambda: pl.BlockSpec((PEER_HEADS, N_KEYS, tm), lambda j, e: (0, 0, j))
    return pl.pallas_call(
        _peer_kernel,
        grid=(t // tm, n_exp // PEER_EBLK),
        in_specs=[pl.BlockSpec((d, tm), lambda j, e: (0, j)),
                  pl.BlockSpec((PEER_EBLK, d), lambda j, e: (e, 0)),
                  pl.BlockSpec((None, d, PEER_EBLK), lambda j, e: (e, 0, 0)),
                  big(), big(), big(), big(),
                  pl.BlockSpec((tm, d), lambda j, e: (j, 0)),
                  pl.BlockSpec((1, 6, d), lambda j, e: ((j * tm) // seq, 0, 0)),
                  pl.BlockSpec((1, d), lambda j, e: (0, 0))],
        out_specs=pl.BlockSpec((tm, d), lambda j, e: (j, 0)),
        out_shape=jax.ShapeDtypeStruct((t, d), F32),
        scratch_shapes=[pltpu.VMEM((d, tm), F32)],
        compiler_params=_params(("parallel", "arbitrary")),
        name="peer_dense",
    )(h2t, u_b, vt_b, c1, e1, r2, e2, x1, mod, fn_w)


def _pad_lanes(v, width=LANES):
    flat = v.reshape(1, -1)
    return jnp.pad(flat, ((0, 0), (0, width - flat.shape[1])))


def _head_expand(col0):
    rows = lax.broadcasted_iota(jnp.int32, (LANES, D_SSM), 0)
    heads = lax.broadcasted_iota(jnp.int32, (LANES, D_SSM), 1) // SSM_HEADDIM
    return (rows == heads + col0).astype(BF16)


def kernel(x, c, ctx, c_ctx, ada_w, ada_b, norm1_w, norm2_w, w_in, ssm_conv_w, ssm_conv_b, ssm_dt_bias,
           ssm_a_log, ssm_d, ssm_norm_w, cfm_conv_w, cfm_conv_b, cfm_ln_w, cfm_ln_b, w_out, peer_wq,
           peer_subkeys, peer_u, peer_v, final_norm_w):
    depth = ada_w.shape[0]
    assert depth == 1, "single-layer configuration"
    b, l, d = x.shape
    i = 0

    c_rows = jnp.concatenate([c, c_ctx[None], jnp.zeros((2 * SUBLANES - b - 1, d), F32)], axis=0)
    mod_all = _ada(c_rows, ada_w[i], ada_b[i])
    mod = mod_all[:b].reshape(b, 6, d)
    mod_ctx = mod_all[b:b + 1].reshape(1, 6, d)

    wi = w_in[i]
    o_dt = D_SSM + XBC_DIM
    w_r = jnp.concatenate([wi[:, :o_dt], wi[:, o_dt + 2 * SSM_HEADS:], wi[:, o_dt:o_dt + 2 * SSM_HEADS],
                           jnp.zeros((d, LANES - 2 * SSM_HEADS), F32)], axis=1).astype(BF16)
    z_l, xbc_l, u_l, dt_l = _inproj(x, mod, True, norm1_w[i], w_r)
    _, xbc_c, _, dt_c = _inproj(ctx, mod_ctx, False, norm1_w[i], w_r)

    conv_w = jnp.pad(ssm_conv_w[i], ((0, SUBLANES - SSM_CONV), (0, 0)))
    conv_b = ssm_conv_b[i].reshape(1, XBC_DIM)
    dt_bias = _pad_lanes(ssm_dt_bias[i])
    a_log = _pad_lanes(ssm_a_log[i])
    d_skip = jnp.repeat(ssm_d[i], SSM_HEADDIM).reshape(1, D_SSM)
    zero_state = jnp.zeros((b, SSM_STATE, D_SSM), F32)
    y_dirs = []
    act_c = act_l = None
    for reverse in (False, True):
        expand = _head_expand(SSM_HEADS if reverse else 0)
        args = (conv_w, conv_b, dt_bias, a_log, d_skip, expand, reverse)
        if not reverse:
            _, h_ctx, act_c = _ssd(xbc_c, dt_c, zero_state, *args, add_skip=False, activated=False)
            y_dir, _, act_l = _ssd(xbc_l, dt_l, h_ctx, *args, add_skip=True, activated=False)
        else:
            _, h_ctx = _ssd(act_c, dt_c, zero_state, *args, add_skip=False, activated=True)
            y_dir, _ = _ssd(act_l, dt_l, h_ctx, *args, add_skip=False, activated=True)
        y_dirs.append(y_dir)

    cfm = _cfm(u_l, jnp.pad(cfm_conv_w[i], ((0, 1), (0, 0))), cfm_conv_b[i].reshape(1, D_CONV),
               cfm_ln_w[i].reshape(1, D_CONV), cfm_ln_b[i].reshape(1, D_CONV))

    t = b * l
    keys_b = peer_subkeys[i].reshape(PEER_HEADS * 2, N_KEYS, PEER_KEY_DIM // 2).astype(BF16)
    x1, h2t, c1, e1, r2, e2, u_b, vt_b = _outscore(
        y_dirs[0], y_dirs[1], z_l, cfm, x, mod, ssm_norm_w[i].reshape(1, D_SSM), w_out[i].astype(BF16),
        norm2_w[i].reshape(1, d), peer_wq[i].astype(BF16), keys_b, peer_u[i], peer_v[i])
    out = _peer(h2t, u_b, vt_b, c1, e1, r2, e2,
                x1.reshape(t, d), mod, final_norm_w.reshape(1, d), l)
    return out.reshape(b, l, d)
```

```python
import functools

import jax
import jax.numpy as jnp
from jax import lax
from jax.experimental import pallas as pl
from jax.experimental.pallas import tpu as pltpu

F32 = jnp.float32
BF16 = jnp.bfloat16

D_MODEL = 1024
GRID_W = 64
D_SSM = 512
SSM_HEADDIM = 64
SSM_HEADS = 8
SSM_STATE = 128
SSM_GROUPS = 2
SSM_CONV = 5
SSD_CHUNK = 128
XBC_DIM = D_SSM + 2 * SSM_GROUPS * SSM_STATE
D_CONV = 512
CONV_WIDTH = 31
PEER_HEADS = 8
PEER_KEY_DIM = 256
N_KEYS = 128
PEER_TOPK = 16
EPS = 1e-6

LANES = 128
SUBLANES = 8
VMEM_LIMIT = 56 * 1024 * 1024

ADA_TILE = 768
IN_TILE = 512
SSD_ROWS = 8
CFM_ROWS = 16
OUT_TILE = 512
PEER_TILE = 512
PEER_EBLK = 2048
PEER_SUBS = (128,) * 16
assert sum(PEER_SUBS) == PEER_EBLK and all(s % N_KEYS == 0 for s in PEER_SUBS)
BF16_ROWS = 2 * SUBLANES


def _params(sem):
    return pltpu.CompilerParams(dimension_semantics=sem, vmem_limit_bytes=VMEM_LIMIT)


def _silu(v):
    return v * jax.nn.sigmoid(v)


def _split3(v):
    hi = v.astype(BF16)
    r1 = v - hi.astype(F32)
    mid = r1.astype(BF16)
    lo = (r1 - mid.astype(F32)).astype(BF16)
    return hi, mid, lo


def _dot01_right(v, m01):
    return sum(jnp.dot(p, m01, preferred_element_type=F32) for p in _split3(v))


def _dot01_left(m01, v):
    return sum(jnp.dot(m01, p, preferred_element_type=F32) for p in _split3(v))


def _ada_kernel(c_ref, w_ref, b_ref, o_ref):
    sc = _silu(c_ref[...])
    o_ref[...] = jnp.dot(sc.astype(BF16), w_ref[...].astype(BF16),
                         preferred_element_type=F32) + b_ref[...]


def _ada(c_rows, ada_w, ada_b):
    n = ada_w.shape[1]
    tn = ADA_TILE
    return pl.pallas_call(
        _ada_kernel,
        grid=(n // tn,),
        in_specs=[pl.BlockSpec(c_rows.shape, lambda j: (0, 0)),
                  pl.BlockSpec((D_MODEL, tn), lambda j: (0, j)),
                  pl.BlockSpec((1, tn), lambda j: (0, j))],
        out_specs=pl.BlockSpec((c_rows.shape[0], tn), lambda j: (0, j)),
        out_shape=jax.ShapeDtypeStruct((c_rows.shape[0], n), F32),
        compiler_params=_params(("parallel",)),
        name="ada",
    )(c_rows, ada_w, ada_b.reshape(1, n))


def _inproj_kernel(x_ref, mod_ref, nw_ref, w_ref, z_ref, xbc_ref, u_ref, dt_ref):
    xv = x_ref[0]
    h = xv * lax.rsqrt(jnp.mean(xv * xv, axis=-1, keepdims=True) + EPS) * nw_ref[...]
    h = h * (1.0 + mod_ref[0, 1:2, :]) + mod_ref[0, 0:1, :]
    hb = h.astype(BF16)

    def proj(lo, hi):
        return jnp.dot(hb, w_ref[:, lo:hi], preferred_element_type=F32)

    z_ref[0] = proj(0, D_SSM)
    xbc_ref[0] = proj(D_SSM, D_SSM + XBC_DIM)
    o = D_SSM + XBC_DIM
    u_ref[0] = proj(o, o + D_CONV) * jax.nn.sigmoid(proj(o + D_CONV, o + 2 * D_CONV))
    dt_ref[0] = proj(o + 2 * D_CONV, o + 2 * D_CONV + LANES)


def _inproj(x, mod, per_batch_mod, norm_w, w_r):
    b, l, d = x.shape
    tm = min(IN_TILE, l)
    mod_idx = (lambda i, j: (i, 0, 0)) if per_batch_mod else (lambda i, j: (0, 0, 0))
    tok = lambda w: pl.BlockSpec((1, tm, w), lambda i, j: (i, j, 0))
    return pl.pallas_call(
        _inproj_kernel,
        grid=(b, l // tm),
        in_specs=[tok(d),
                  pl.BlockSpec((1, 6, d), mod_idx),
                  pl.BlockSpec((1, d), lambda i, j: (0, 0)),
                  pl.BlockSpec(w_r.shape, lambda i, j: (0, 0))],
        out_specs=[tok(D_SSM), tok(XBC_DIM), tok(D_CONV), tok(LANES)],
        out_shape=[jax.ShapeDtypeStruct((b, l, w), F32) for w in (D_SSM, XBC_DIM, D_CONV, LANES)],
        compiler_params=_params(("parallel", "parallel")),
        name="inproj",
    )(x, mod, norm_w.reshape(1, d), w_r)


def _ssd_kernel(cur_ref, prev_ref, next_ref, dt_ref, h0_ref, cw_ref, cb_ref, dtb_ref, alog_ref,
                dskip_ref, expand_ref, y_ref, hfin_ref, *rest, reverse, nc, add_skip, activated):
    act_ref, ext_ref, state_ref = (None, *rest) if activated else rest
    q = SSD_CHUNK
    halo = SUBLANES
    pad = (SSM_CONV - 1) // 2
    gw = D_SSM // SSM_GROUPS
    hpg = SSM_HEADS // SSM_GROUPS
    col0 = SSM_HEADS if reverse else 0
    c = pl.program_id(1)
    cc = (nc - 1 - c) if reverse else c

    @pl.when(c == 0)
    def _():
        state_ref[...] = h0_ref[...]

    ri = lax.broadcasted_iota(jnp.int32, (q, q), 0)
    ci = lax.broadcasted_iota(jnp.int32, (q, q), 1)
    causal = (ci >= ri) if reverse else (ci <= ri)
    causal01 = jnp.where(causal, 1.0, 0.0).astype(BF16)
    a_row = -jnp.exp(alog_ref[...])

    for r in range(cur_ref.shape[0]):
        if activated:
            act = cur_ref[r]
        else:
            ext_ref[r, 0:halo, :] = jnp.where(cc == 0, 0.0, prev_ref[r])
            ext_ref[r, halo:halo + q, :] = cur_ref[r]
            ext_ref[r, halo + q:2 * halo + q, :] = jnp.where(cc == nc - 1, 0.0, next_ref[r])
            conv = cb_ref[...] + cw_ref[0:1, :] * ext_ref[r, halo - pad:halo - pad + q, :]
            for k in range(1, SSM_CONV):
                conv = conv + cw_ref[k:k + 1, :] * ext_ref[r, halo - pad + k:halo - pad + k + q, :]
            act = _silu(conv)
            act_ref[r] = act

        dtv = dt_ref[r] + dtb_ref[...]
        dt = jnp.maximum(dtv, 0.0) + jnp.log(1.0 + jnp.exp(-jnp.abs(dtv)))
        acum = _dot01_left(causal01, dt * a_row)
        acum_t = acum.T
        tot = acum[0:1, :] if reverse else acum[q - 1:q, :]
        stacked = jnp.concatenate([dt, jnp.exp(tot - acum), jnp.exp(acum),
                                   jnp.broadcast_to(jnp.exp(tot), (SUBLANES, LANES))], axis=0)
        spread = _dot01_right(stacked, expand_ref[...])
        dt_e, dte_e, ea_e = spread[0:q], spread[q:2 * q], spread[2 * q:3 * q]
        etot_e = spread[3 * q:3 * q + 1]

        xs = act[:, :D_SSM]
        xdt = xs * dt_e
        xdt_b = xdt.astype(BF16)
        w_b = (xdt * dte_e).astype(BF16)
        for g in range(SSM_GROUPS):
            gs = slice(g * gw, (g + 1) * gw)
            b_g = act[:, D_SSM + g * SSM_STATE:D_SSM + (g + 1) * SSM_STATE]
            c_g = act[:, D_SSM + (SSM_GROUPS + g) * SSM_STATE:D_SSM + (SSM_GROUPS + g + 1) * SSM_STATE]
            c_b = c_g.astype(BF16)
            bt_b = b_g.T.astype(BF16)
            cb = jnp.dot(c_b, bt_b, preferred_element_type=F32)
            st = state_ref[r, :, gs]
            y_g = jnp.dot(c_b, st.astype(BF16), preferred_element_type=F32) * ea_e[:, gs]
            y_heads = []
            for hh in range(hpg):
                h = g * hpg + hh
                col = col0 + h
                diff = acum[:, col:col + 1] - acum_t[col:col + 1, :]
                decay = jnp.where(causal, jnp.exp(jnp.minimum(diff, 0.0)), 0.0)
                y_heads.append(jnp.dot((cb * decay).astype(BF16),
                                       xdt_b[:, h * SSM_HEADDIM:(h + 1) * SSM_HEADDIM],
                                       preferred_element_type=F32))
            y_g = y_g + jnp.concatenate(y_heads, axis=-1)
            if add_skip:
                y_g = y_g + dskip_ref[:, gs] * xs[:, gs]
            y_ref[r, :, gs] = y_g
            state_ref[r, :, gs] = st * etot_e[:, gs] + jnp.dot(bt_b, w_b[:, gs], preferred_element_type=F32)
    hfin_ref[...] = state_ref[...]


def _ssd(xbc, dtp, h0, conv_w, conv_b, dt_bias, a_log, d_skip, expand, reverse, add_skip, activated):
    b, l, _ = xbc.shape
    q = SSD_CHUNK
    nc = l // q
    nr = SSD_ROWS if b % SSD_ROWS == 0 else 1
    hb = q // SUBLANES
    pos = (lambda c: nc - 1 - c) if reverse else (lambda c: c)
    const = lambda shape: pl.BlockSpec(shape, lambda i, c: (0,) * len(shape))
    kern = functools.partial(_ssd_kernel, reverse=reverse, nc=nc, add_skip=add_skip, activated=activated)
    chunk = lambda w: pl.BlockSpec((nr, q, w), lambda i, c: (i, pos(c), 0))
    act_spec = [] if activated else [chunk(XBC_DIM)]
    act_shape = [] if activated else [jax.ShapeDtypeStruct((b, l, XBC_DIM), F32)]
    return pl.pallas_call(
        kern,
        grid=(b // nr, nc),
        in_specs=[pl.BlockSpec((nr, q, XBC_DIM), lambda i, c: (i, pos(c), 0)),
                  pl.BlockSpec((nr, SUBLANES, XBC_DIM), lambda i, c: (i, jnp.maximum(pos(c) * hb - 1, 0), 0)),
                  pl.BlockSpec((nr, SUBLANES, XBC_DIM),
                               lambda i, c: (i, jnp.minimum((pos(c) + 1) * hb, l // SUBLANES - 1), 0)),
                  pl.BlockSpec((nr, q, LANES), lambda i, c: (i, pos(c), 0)),
                  pl.BlockSpec((nr, SSM_STATE, D_SSM), lambda i, c: (i, 0, 0)),
                  const(conv_w.shape), const(conv_b.shape), const(dt_bias.shape), const(a_log.shape),
                  const(d_skip.shape), const(expand.shape)],
        out_specs=[chunk(D_SSM), pl.BlockSpec((nr, SSM_STATE, D_SSM), lambda i, c: (i, 0, 0))] + act_spec,
        out_shape=[jax.ShapeDtypeStruct((b, l, D_SSM), F32),
                   jax.ShapeDtypeStruct((b, SSM_STATE, D_SSM), F32)] + act_shape,
        scratch_shapes=[pltpu.VMEM((nr, q + 2 * SUBLANES, XBC_DIM), F32),
                        pltpu.VMEM((nr, SSM_STATE, D_SSM), F32)],
        compiler_params=_params(("parallel", "arbitrary")),
        name="ssd_bwd" if reverse else "ssd_fwd",
    )(xbc, xbc, xbc, dtp, h0, conv_w, conv_b, dt_bias, a_log, d_skip, expand)


def _cfm_kernel(cur_ref, prev_ref, next_ref, w_ref, b_ref, lnw_ref, lnb_ref, o_ref, hpad_ref, vbuf_ref,
                shift_ref, *, nb):
    rows = CFM_ROWS
    band = rows * GRID_W
    half = D_CONV // 2
    pad = (CONV_WIDTH - 1) // 2
    lead = 2 * SUBLANES
    r = pl.program_id(1)

    zeros = jnp.zeros((rows, lead, half), F32)
    hpad_ref[:, 0:lead, :] = zeros
    hpad_ref[:, lead + GRID_W:2 * lead + GRID_W, :] = zeros
    hpad_ref[:, lead:lead + GRID_W, :] = cur_ref[0, :, 0:half].reshape(rows, GRID_W, half)
    vbuf_ref[0:band, :] = jnp.where(r == 0, 0.0, prev_ref[0])
    vbuf_ref[band:2 * band, :] = cur_ref[0, :, half:D_CONV]
    vbuf_ref[2 * band:3 * band, :] = jnp.where(r == nb - 1, 0.0, next_ref[0])

    span = shift_ref.shape[1]

    def row_body(i, carry):
        for s in range(SUBLANES):
            shift_ref[s] = hpad_ref[i, s:s + span, :]
        acc_h = jnp.zeros((GRID_W, half), F32)
        acc_v = jnp.zeros((GRID_W, half), F32)
        for k in range(CONV_WIDTH):
            first = lead - pad + k
            tile0 = first // SUBLANES * SUBLANES
            acc_h = acc_h + w_ref[k:k + 1, 0:half] * shift_ref[first % SUBLANES, tile0:tile0 + GRID_W, :]
            start = pl.multiple_of(band + (i + k - pad) * GRID_W, GRID_W)
            acc_v = acc_v + w_ref[k:k + 1, half:D_CONV] * vbuf_ref[pl.ds(start, GRID_W), :]
        conv = jnp.concatenate([acc_h, acc_v], axis=-1) + b_ref[...]
        mu = jnp.mean(conv, axis=-1, keepdims=True)
        cen = conv - mu
        var = jnp.mean(cen * cen, axis=-1, keepdims=True)
        o_ref[0, pl.ds(pl.multiple_of(i * GRID_W, GRID_W), GRID_W), :] = _silu(
            cen * lax.rsqrt(var + EPS) * lnw_ref[...] + lnb_ref[...]).astype(BF16)
        return carry

    lax.fori_loop(0, rows, row_body, 0, unroll=4)


def _cfm(u, conv_w, conv_b, ln_w, ln_b):
    b, l, _ = u.shape
    band = CFM_ROWS * GRID_W
    nb = l // band
    half = D_CONV // 2
    const = lambda shape: pl.BlockSpec(shape, lambda i, r: (0,) * len(shape))
    return pl.pallas_call(
        functools.partial(_cfm_kernel, nb=nb),
        grid=(b, nb),
        in_specs=[pl.BlockSpec((1, band, D_CONV), lambda i, r: (i, r, 0)),
                  pl.BlockSpec((1, band, half), lambda i, r: (i, jnp.maximum(r - 1, 0), 1)),
                  pl.BlockSpec((1, band, half), lambda i, r: (i, jnp.minimum(r + 1, nb - 1), 1)),
                  const(conv_w.shape), const(conv_b.shape), const(ln_w.shape), const(ln_b.shape)],
        out_specs=pl.BlockSpec((1, band, D_CONV), lambda i, r: (i, r, 0)),
        out_shape=jax.ShapeDtypeStruct((b, l, D_CONV), BF16),
        scratch_shapes=[pltpu.VMEM((CFM_ROWS, GRID_W + 4 * SUBLANES, half), F32),
                        pltpu.VMEM((3 * band, half), F32),
                        pltpu.VMEM((SUBLANES, GRID_W + 3 * SUBLANES, half), F32)],
        compiler_params=_params(("parallel", "parallel")),
        name="cfm",
    )(u, u, u, conv_w, conv_b, ln_w, ln_b)


def _oddeven_merge(lo, hi, r):
    step = r * 2
    if step < hi - lo:
        yield from _oddeven_merge(lo, hi, step)
        yield from _oddeven_merge(lo + r, hi, step)
        yield from [(i, i + r) for i in range(lo + r, hi - r, step)]
    else:
        yield (lo, lo + r)


def _oddeven_sort(lo, hi):
    if hi > lo:
        mid = lo + (hi - lo) // 2
        yield from _oddeven_sort(lo, mid)
        yield from _oddeven_sort(mid + 1, hi)
        yield from _oddeven_merge(lo, hi, 1)


SORT_NET = tuple(_oddeven_sort(0, PEER_TOPK - 1))


def _cmp_exchange(v, i, j):
    v[i], v[j] = jnp.maximum(v[i], v[j]), jnp.minimum(v[i], v[j])


def _bitonic_sort(v):
    d = len(v) // 2
    while d:
        for i in range(len(v)):
            if not i & d:
                _cmp_exchange(v, i, i + d)
        d //= 2


def _merge_top(v, w):
    n = len(v)
    return [jnp.maximum(v[i], w[n - 1 - i]) for i in range(n)]


def _sublane_rolls():
    shift = SUBLANES // 2
    while shift:
        yield shift
        shift //= 2


def _top_sorted(v):
    for shift in _sublane_rolls():
        v = _merge_top(v, [pltpu.roll(x, shift, 0) for x in v])
        _bitonic_sort(v)
    return v


def _score_tokens(q_ref, keys_ref, neg_ref, c1_ref, e1_ref, r2_ref, e2_ref):
    tm = q_ref.shape[0]
    nsub = tm // LANES
    kd = PEER_KEY_DIM // 2
    k = PEER_TOPK
    ntile = N_KEYS // SUBLANES

    def body(it, carry):
        h = it // nsub
        tok = pl.multiple_of((it % nsub) * LANES, LANES)
        nt = (((1,), (1,)), ((), ()))

        def score_tiles(p):
            qs = q_ref[pl.ds(tok, LANES), pl.ds(pl.multiple_of((2 * h + p) * kd, kd), kd)]
            s = lax.dot_general(keys_ref[2 * h + p], qs, nt, preferred_element_type=F32)
            return [s[i * SUBLANES:(i + 1) * SUBLANES, :] for i in range(ntile)]

        def top(tiles):
            v = list(tiles)
            for i, j in SORT_NET:
                _cmp_exchange(v, i, j)
            return _top_sorted(v)

        s1 = score_tiles(0)
        s2 = score_tiles(1)
        a = top(s1)
        b = top(s2)

        sub = lax.broadcasted_iota(jnp.int32, (SUBLANES, LANES), 0)
        pick = lambda lo, x, y: jnp.where(sub < lo, x, y)
        base = pick(4, pick(2, pick(1, a[0], a[1]), pick(3, a[2], a[3])), pick(5, b[0], pick(6, b[1], b[2])))
        cand = []
        for m in range(k):
            other = pick(4, b[m], a[4 + m]) if 4 + m < k else b[m]
            cand.append((other + base) + neg_ref[m])
        lists = list(cand)
        rolls = list(_sublane_rolls())
        for shift in rolls[:-1]:
            lists = _merge_top(lists, [pltpu.roll(x, shift, 0) for x in lists])
            _bitonic_sort(lists)
        lists = _merge_top(lists, [pltpu.roll(x, rolls[-1], 0) for x in lists])
        tau = functools.reduce(jnp.minimum, lists)

        best = a[0] + b[0]
        zsum = sum(jnp.where(c >= tau, jnp.exp(c - best), 0.0) for c in cand)
        for shift in rolls:
            zsum = zsum + pltpu.roll(zsum, shift, 0)
        scale = 0.5 / zsum
        top_extra = sum(jnp.where(a[0] + b[j] >= tau, 1.0, 0.0) for j in range(k // 2, k))

        c1_t, e1_t, r2_t, e2_t = [], [], [], []
        for t1, t2 in zip(s1, s2):
            c1 = jnp.zeros_like(t1)
            for j in range(k // 2):
                c1 = jnp.where(t1 + b[j] >= tau, float(j + 1), c1)
            c1_t.append(jnp.where(t1 == a[0], c1 + top_extra, c1))
            e1_t.append(jnp.exp(t1 - a[0]) * scale)
            r2 = jnp.zeros_like(t2)
            for j in range(k):
                r2 = jnp.where(t2 < b[j], float(j + 1), r2)
            r2_t.append(r2)
            e2_t.append(jnp.exp(t2 - b[0]))
        c1_ref[h, :, pl.ds(tok, LANES)] = jnp.concatenate(c1_t, axis=0)
        e1_ref[h, :, pl.ds(tok, LANES)] = jnp.concatenate(e1_t, axis=0)
        r2_ref[h, :, pl.ds(tok, LANES)] = jnp.concatenate(r2_t, axis=0).astype(BF16)
        e2_ref[h, :, pl.ds(tok, LANES)] = jnp.concatenate(e2_t, axis=0).astype(BF16)
        return carry

    lax.fori_loop(0, PEER_HEADS * nsub, body, 0, unroll=8)


def _candidate_mask():
    k = PEER_TOPK
    m = lax.broadcasted_iota(jnp.int32, (k, SUBLANES, LANES), 0)
    s = lax.broadcasted_iota(jnp.int32, (k, SUBLANES, LANES), 1)
    row_ok = (s < 4) & ((s + 1) * (m + 1) <= k)
    col_ok = (s >= 4) & (s < 7) & (4 + m < k) & ((5 + m) * (s - 3) <= k)
    return jnp.where(row_ok | col_ok, 0.0, -jnp.inf).astype(F32)


def _outscore_kernel(yf_ref, yb_ref, z_ref, cfm_ref, x_ref, mod_ref, gnw_ref, w_ref, n2w_ref,
                     wq_ref, keys_ref, neg_ref, u_ref, v_ref,
                     x1_ref, h2t_ref, c1_ref, e1_ref, r2_ref, e2_ref, ub_ref, vt_ref, q_ref):
    ub_ref[...] = u_ref[...].astype(BF16)
    vt_ref[...] = v_ref[...].T.astype(BF16)
    gw = D_SSM // SSM_GROUPS
    g = (yf_ref[0] + yb_ref[0]) * _silu(z_ref[0])
    parts = []
    for k in range(SSM_GROUPS):
        gk = g[:, k * gw:(k + 1) * gw]
        parts.append(gk * lax.rsqrt(jnp.mean(gk * gk, axis=-1, keepdims=True) + EPS))
    ssm = jnp.concatenate(parts, axis=-1) * gnw_ref[...]
    mixed = (jnp.dot(ssm.astype(BF16), w_ref[0:D_SSM, :], preferred_element_type=F32)
             + jnp.dot(cfm_ref[0], w_ref[D_SSM:, :], preferred_element_type=F32))
    x1 = x_ref[0] + mod_ref[0, 2:3, :] * mixed
    x1_ref[0] = x1
    h = x1 * lax.rsqrt(jnp.mean(x1 * x1, axis=-1, keepdims=True) + EPS) * n2w_ref[...]
    h2 = h * (1.0 + mod_ref[0, 4:5, :]) + mod_ref[0, 3:4, :]
    h2t_ref[...] = h2.T.astype(BF16)
    q_ref[...] = jnp.dot(h2.astype(BF16), wq_ref[...], preferred_element_type=F32).astype(BF16)
    _score_tokens(q_ref, keys_ref, neg_ref, c1_ref, e1_ref, r2_ref, e2_ref)


def _outscore(yf, yb, z, cfm, x, mod, gn_w, w_out_b, n2_w, wq_b, keys_b, u_tab, v_tab):
    b, l, d = x.shape
    t = b * l
    tm = min(OUT_TILE, l)
    neg = _candidate_mask()
    n_exp = u_tab.shape[0]
    rows = n_exp // (t // tm)
    assert rows * (t // tm) == n_exp and PEER_EBLK % rows == 0 and rows % LANES == 0
    per_slab = PEER_EBLK // rows
    tile = lambda i, j: i * (l // tm) + j
    tok = lambda w: pl.BlockSpec((1, tm, w), lambda i, j: (i, j, 0))
    const = lambda shape: pl.BlockSpec(shape, lambda i, j: (0,) * len(shape))
    big = lambda: pl.BlockSpec((PEER_HEADS, N_KEYS, tm), lambda i, j: (0, 0, tile(i, j)))
    big_shape = lambda dt: jax.ShapeDtypeStruct((PEER_HEADS, N_KEYS, t), dt)
    return pl.pallas_call(
        _outscore_kernel,
        grid=(b, l // tm),
        in_specs=[tok(D_SSM), tok(D_SSM), tok(D_SSM), tok(D_CONV), tok(d),
                  pl.BlockSpec((1, 6, d), lambda i, j: (i, 0, 0)),
                  const(gn_w.shape), const(w_out_b.shape), const(n2_w.shape),
                  const(wq_b.shape), const(keys_b.shape), const(neg.shape),
                  pl.BlockSpec((rows, d), lambda i, j: (tile(i, j), 0)),
                  pl.BlockSpec((rows, d), lambda i, j: (tile(i, j), 0))],
        out_specs=[tok(d), pl.BlockSpec((d, tm), lambda i, j: (0, tile(i, j))), big(), big(), big(), big(),
                   pl.BlockSpec((rows, d), lambda i, j: (tile(i, j), 0)),
                   pl.BlockSpec((None, d, rows), lambda i, j: (tile(i, j) // per_slab, 0, tile(i, j) % per_slab))],
        out_shape=[jax.ShapeDtypeStruct((b, l, d), F32), jax.ShapeDtypeStruct((d, t), BF16),
                   big_shape(F32), big_shape(F32), big_shape(BF16), big_shape(BF16),
                   jax.ShapeDtypeStruct((n_exp, d), BF16),
                   jax.ShapeDtypeStruct((n_exp // PEER_EBLK, d, PEER_EBLK), BF16)],
        scratch_shapes=[pltpu.VMEM((tm, PEER_HEADS * PEER_KEY_DIM), BF16)],
        compiler_params=_params(("parallel", "parallel")),
        name="outproj_score",
    )(yf, yb, z, cfm, x, mod, gn_w, w_out_b, n2_w, wq_b, keys_b, neg, u_tab, v_tab)


def _key_row_bf16(ref, h, key0, j):
    tile = ref[h, pl.ds(key0 + j // SUBLANES * SUBLANES, SUBLANES), :]
    row = tile[j % SUBLANES:j % SUBLANES + 1, :]
    rows = jnp.broadcast_to(row, (BF16_ROWS, ref.shape[-1])).astype(BF16)
    return jnp.tile(rows, (N_KEYS // BF16_ROWS, 1))


def _peer_kernel(ht_ref, u_ref, vt_ref, c1_ref, e1_ref, r2_ref, e2_ref, x1_ref, mod_ref, fnw_ref,
                 o_ref, acc_ref):
    eb = pl.program_id(1)
    keys_per_step = PEER_EBLK // N_KEYS
    key0 = pl.multiple_of(eb * keys_per_step, keys_per_step)
    starts = [sum(PEER_SUBS[:i]) for i in range(len(PEER_SUBS))]

    @pl.when(eb == 0)
    def _():
        acc_ref[...] = jnp.zeros_like(acc_ref)

    def first_matmul(sub):
        return jnp.dot(u_ref[starts[sub]:starts[sub] + PEER_SUBS[sub], :], ht_ref[...],
                       preferred_element_type=F32)

    def build_gates(sub):
        gates = []
        for kk in range(PEER_SUBS[sub] // N_KEYS):
            j = starts[sub] // N_KEYS + kk
            g = jnp.zeros((N_KEYS, ht_ref.shape[1]), BF16)
            for h in range(PEER_HEADS):
                c1 = _key_row_bf16(c1_ref, h, key0, j)
                e1 = _key_row_bf16(e1_ref, h, key0, j)
                g = g + jnp.where(r2_ref[h] < c1, e2_ref[h], jnp.zeros((), BF16)) * e1
            gates.append(g)
        return jnp.concatenate(gates, axis=0)

    nsub = len(PEER_SUBS)
    pre = first_matmul(0)
    weighted = []
    for sub in range(nsub):
        nxt = first_matmul(sub + 1) if sub + 1 < nsub else None
        act = pre * (1.0 + lax.erf(pre * (2.0 ** -0.5)))
        weighted.append(act.astype(BF16) * build_gates(sub))
        pre = nxt
    acc_ref[...] = jnp.dot(vt_ref[...], jnp.concatenate(weighted, axis=0),
                           preferred_element_type=F32) + acc_ref[...]

    @pl.when(eb == pl.num_programs(1) - 1)
    def _():
        x2 = x1_ref[...] + mod_ref[0, 5:6, :] * acc_ref[...].T
        o_ref[...] = x2 * lax.rsqrt(jnp.mean(x2 * x2, axis=-1, keepdims=True) + EPS) * fnw_ref[...]


def _peer(h2t, u_b, vt_b, c1, e1, r2, e2, x1, mod, fn_w, seq):
    d, t = h2t.shape
    tm = min(PEER_TILE, seq)
    n_exp = u_b.shape[0]
    big = lambda: pl.BlockSpec((PEER_HEADS, N_KEYS, tm), lambda j, e: (0, 0, j))
    return pl.pallas_call(
        _peer_kernel,
        grid=(t // tm, n_exp // PEER_EBLK),
        in_specs=[pl.BlockSpec((d, tm), lambda j, e: (0, j)),
                  pl.BlockSpec((PEER_EBLK, d), lambda j, e: (e, 0)),
                  pl.BlockSpec((None, d, PEER_EBLK), lambda j, e: (e, 0, 0)),
                  big(), big(), big(), big(),
                  pl.BlockSpec((tm, d), lambda j, e: (j, 0)),
                  pl.BlockSpec((1, 6, d), lambda j, e: ((j * tm) // seq, 0, 0)),
                  pl.BlockSpec((1, d), lambda j, e: (0, 0))],
        out_specs=pl.BlockSpec((tm, d), lambda j, e: (j, 0)),
        out_shape=jax.ShapeDtypeStruct((t, d), F32),
        scratch_shapes=[pltpu.VMEM((d, tm), F32)],
        compiler_params=_params(("parallel", "arbitrary")),
        name="peer_dense",
    )(h2t, u_b, vt_b, c1, e1, r2, e2, x1, mod, fn_w)


def _pad_lanes(v, width=LANES):
    flat = v.reshape(1, -1)
    return jnp.pad(flat, ((0, 0), (0, width - flat.shape[1])))


def _head_expand(col0):
    rows = lax.broadcasted_iota(jnp.int32, (LANES, D_SSM), 0)
    heads = lax.broadcasted_iota(jnp.int32, (LANES, D_SSM), 1) // SSM_HEADDIM
    return (rows == heads + col0).astype(BF16)


def kernel(x, c, ctx, c_ctx, ada_w, ada_b, norm1_w, norm2_w, w_in, ssm_conv_w, ssm_conv_b, ssm_dt_bias,
           ssm_a_log, ssm_d, ssm_norm_w, cfm_conv_w, cfm_conv_b, cfm_ln_w, cfm_ln_b, w_out, peer_wq,
           peer_subkeys, peer_u, peer_v, final_norm_w):
    depth = ada_w.shape[0]
    assert depth == 1, "single-layer configuration"
    b, l, d = x.shape
    i = 0

    c_rows = jnp.concatenate([c, c_ctx[None], jnp.zeros((2 * SUBLANES - b - 1, d), F32)], axis=0)
    mod_all = _ada(c_rows, ada_w[i], ada_b[i])
    mod = mod_all[:b].reshape(b, 6, d)
    mod_ctx = mod_all[b:b + 1].reshape(1, 6, d)

    wi = w_in[i]
    o_dt = D_SSM + XBC_DIM
    w_r = jnp.concatenate([wi[:, :o_dt], wi[:, o_dt + 2 * SSM_HEADS:], wi[:, o_dt:o_dt + 2 * SSM_HEADS],
                           jnp.zeros((d, LANES - 2 * SSM_HEADS), F32)], axis=1).astype(BF16)
    z_l, xbc_l, u_l, dt_l = _inproj(x, mod, True, norm1_w[i], w_r)
    _, xbc_c, _, dt_c = _inproj(ctx, mod_ctx, False, norm1_w[i], w_r)

    conv_w = jnp.pad(ssm_conv_w[i], ((0, SUBLANES - SSM_CONV), (0, 0)))
    conv_b = ssm_conv_b[i].reshape(1, XBC_DIM)
    dt_bias = _pad_lanes(ssm_dt_bias[i])
    a_log = _pad_lanes(ssm_a_log[i])
    d_skip = jnp.repeat(ssm_d[i], SSM_HEADDIM).reshape(1, D_SSM)
    zero_state = jnp.zeros((b, SSM_STATE, D_SSM), F32)
    y_dirs = []
    act_c = act_l = None
    for reverse in (False, True):
        expand = _head_expand(SSM_HEADS if reverse else 0)
        args = (conv_w, conv_b, dt_bias, a_log, d_skip, expand, reverse)
        if not reverse:
            _, h_ctx, act_c = _ssd(xbc_c, dt_c, zero_state, *args, add_skip=False, activated=False)
            y_dir, _, act_l = _ssd(xbc_l, dt_l, h_ctx, *args, add_skip=True, activated=False)
        else:
            _, h_ctx = _ssd(act_c, dt_c, zero_state, *args, add_skip=False, activated=True)
            y_dir, _ = _ssd(act_l, dt_l, h_ctx, *args, add_skip=False, activated=True)
        y_dirs.append(y_dir)

    cfm = _cfm(u_l, jnp.pad(cfm_conv_w[i], ((0, 1), (0, 0))), cfm_conv_b[i].reshape(1, D_CONV),
               cfm_ln_w[i].reshape(1, D_CONV), cfm_ln_b[i].reshape(1, D_CONV))

    t = b * l
    keys_b = peer_subkeys[i].reshape(PEER_HEADS * 2, N_KEYS, PEER_KEY_DIM // 2).astype(BF16)
    x1, h2t, c1, e1, r2, e2, u_b, vt_b = _outscore(
        y_dirs[0], y_dirs[1], z_l, cfm, x, mod, ssm_norm_w[i].reshape(1, D_SSM), w_out[i].astype(BF16),
        norm2_w[i].reshape(1, d), peer_wq[i].astype(BF16), keys_b, peer_u[i], peer_v[i])
    out = _peer(h2t, u_b, vt_b, c1, e1, r2, e2,
                x1.reshape(t, d), mod, final_norm_w.reshape(1, d), l)
    return out.reshape(b, l, d)
```

```python
import functools

import jax
import jax.numpy as jnp
from jax import lax
from jax.experimental import pallas as pl
from jax.experimental.pallas import tpu as pltpu

F32 = jnp.float32
BF16 = jnp.bfloat16

D_MODEL = 1024
GRID_W = 64
D_SSM = 512
SSM_HEADDIM = 64
SSM_HEADS = 8
SSM_STATE = 128
SSM_GROUPS = 2
SSM_CONV = 5
SSD_CHUNK = 128
XBC_DIM = D_SSM + 2 * SSM_GROUPS * SSM_STATE
D_CONV = 512
CONV_WIDTH = 31
PEER_HEADS = 8
PEER_KEY_DIM = 256
N_KEYS = 128
PEER_TOPK = 16
EPS = 1e-6

LANES = 128
SUBLANES = 8
VMEM_LIMIT = 56 * 1024 * 1024

ADA_TILE = 768
IN_TILE = 512
SSD_ROWS = 8
CFM_ROWS = 16
OUT_TILE = 512
PEER_TILE = 512
PEER_EBLK = 2048
PEER_SUBS = (128,) * 16
assert sum(PEER_SUBS) == PEER_EBLK and all(s % N_KEYS == 0 for s in PEER_SUBS)
BF16_ROWS = 2 * SUBLANES


def _params(sem):
    return pltpu.CompilerParams(dimension_semantics=sem, vmem_limit_bytes=VMEM_LIMIT)


def _silu(v):
    return v * jax.nn.sigmoid(v)


def _split3(v):
    hi = v.astype(BF16)
    r1 = v - hi.astype(F32)
    mid = r1.astype(BF16)
    lo = (r1 - mid.astype(F32)).astype(BF16)
    return hi, mid, lo


def _dot01_right(v, m01):
    return sum(jnp.dot(p, m01, preferred_element_type=F32) for p in _split3(v))


def _dot01_left(m01, v):
    return sum(jnp.dot(m01, p, preferred_element_type=F32) for p in _split3(v))


def _ada_kernel(c_ref, w_ref, b_ref, o_ref):
    sc = _silu(c_ref[...])
    o_ref[...] = jnp.dot(sc.astype(BF16), w_ref[...].astype(BF16),
                         preferred_element_type=F32) + b_ref[...]


def _ada(c_rows, ada_w, ada_b):
    n = ada_w.shape[1]
    tn = ADA_TILE
    return pl.pallas_call(
        _ada_kernel,
        grid=(n // tn,),
        in_specs=[pl.BlockSpec(c_rows.shape, lambda j: (0, 0)),
                  pl.BlockSpec((D_MODEL, tn), lambda j: (0, j)),
                  pl.BlockSpec((1, tn), lambda j: (0, j))],
        out_specs=pl.BlockSpec((c_rows.shape[0], tn), lambda j: (0, j)),
        out_shape=jax.ShapeDtypeStruct((c_rows.shape[0], n), F32),
        compiler_params=_params(("parallel",)),
        name="ada",
    )(c_rows, ada_w, ada_b.reshape(1, n))


def _inproj_kernel(x_ref, mod_ref, nw_ref, w_ref, z_ref, xbc_ref, u_ref, dt_ref):
    xv = x_ref[0]
    h = xv * lax.rsqrt(jnp.mean(xv * xv, axis=-1, keepdims=True) + EPS) * nw_ref[...]
    h = h * (1.0 + mod_ref[0, 1:2, :]) + mod_ref[0, 0:1, :]
    hb = h.astype(BF16)

    def proj(lo, hi):
        return jnp.dot(hb, w_ref[:, lo:hi], preferred_element_type=F32)

    z_ref[0] = proj(0, D_SSM)
    xbc_ref[0] = proj(D_SSM, D_SSM + XBC_DIM)
    o = D_SSM + XBC_DIM
    u_ref[0] = proj(o, o + D_CONV) * jax.nn.sigmoid(proj(o + D_CONV, o + 2 * D_CONV))
    dt_ref[0] = proj(o + 2 * D_CONV, o + 2 * D_CONV + LANES)


def _inproj(x, mod, per_batch_mod, norm_w, w_r):
    b, l, d = x.shape
    tm = min(IN_TILE, l)
    mod_idx = (lambda i, j: (i, 0, 0)) if per_batch_mod else (lambda i, j: (0, 0, 0))
    tok = lambda w: pl.BlockSpec((1, tm, w), lambda i, j: (i, j, 0))
    return pl.pallas_call(
        _inproj_kernel,
        grid=(b, l // tm),
        in_specs=[tok(d),
                  pl.BlockSpec((1, 6, d), mod_idx),
                  pl.BlockSpec((1, d), lambda i, j: (0, 0)),
                  pl.BlockSpec(w_r.shape, lambda i, j: (0, 0))],
        out_specs=[tok(D_SSM), tok(XBC_DIM), tok(D_CONV), tok(LANES)],
        out_shape=[jax.ShapeDtypeStruct((b, l, w), F32) for w in (D_SSM, XBC_DIM, D_CONV, LANES)],
        compiler_params=_params(("parallel", "parallel")),
        name="inproj",
    )(x, mod, norm_w.reshape(1, d), w_r)


def _ssd_kernel(cur_ref, prev_ref, next_ref, dt_ref, h0_ref, cw_ref, cb_ref, dtb_ref, alog_ref,
                dskip_ref, expand_ref, y_ref, hfin_ref, *rest, reverse, nc, add_skip, activated):
    act_ref, ext_ref, state_ref = (None, *rest) if activated else rest
    q = SSD_CHUNK
    halo = SUBLANES
    pad = (SSM_CONV - 1) // 2
    gw = D_SSM // SSM_GROUPS
    hpg = SSM_HEADS // SSM_GROUPS
    col0 = SSM_HEADS if reverse else 0
    c = pl.program_id(1)
    cc = (nc - 1 - c) if reverse else c

    @pl.when(c == 0)
    def _():
        state_ref[...] = h0_ref[...]

    ri = lax.broadcasted_iota(jnp.int32, (q, q), 0)
    ci = lax.broadcasted_iota(jnp.int32, (q, q), 1)
    causal = (ci >= ri) if reverse else (ci <= ri)
    causal01 = jnp.where(causal, 1.0, 0.0).astype(BF16)
    a_row = -jnp.exp(alog_ref[...])

    for r in range(cur_ref.shape[0]):
        if activated:
            act = cur_ref[r]
        else:
            ext_ref[r, 0:halo, :] = jnp.where(cc == 0, 0.0, prev_ref[r])
            ext_ref[r, halo:halo + q, :] = cur_ref[r]
            ext_ref[r, halo + q:2 * halo + q, :] = jnp.where(cc == nc - 1, 0.0, next_ref[r])
            conv = cb_ref[...] + cw_ref[0:1, :] * ext_ref[r, halo - pad:halo - pad + q, :]
            for k in range(1, SSM_CONV):
                conv = conv + cw_ref[k:k + 1, :] * ext_ref[r, halo - pad + k:halo - pad + k + q, :]
            act = _silu(conv)
            act_ref[r] = act

        dtv = dt_ref[r] + dtb_ref[...]
        dt = jnp.maximum(dtv, 0.0) + jnp.log(1.0 + jnp.exp(-jnp.abs(dtv)))
        acum = _dot01_left(causal01, dt * a_row)
        acum_t = acum.T
        tot = acum[0:1, :] if reverse else acum[q - 1:q, :]
        stacked = jnp.concatenate([dt, acum, jnp.broadcast_to(tot, (SUBLANES, LANES))], axis=0)
        spread = _dot01_right(stacked, expand_ref[...])
        dt_e, acum_e, tot_e = spread[0:q], spread[q:2 * q], spread[2 * q:2 * q + 1]
        dte_e = jnp.exp(tot_e - acum_e)
        ea_e = jnp.exp(acum_e)
        etot_e = jnp.exp(tot_e)

        xs = act[:, :D_SSM]
        xdt = xs * dt_e
        xdt_b = xdt.astype(BF16)
        w_b = (xdt * dte_e).astype(BF16)
        for g in range(SSM_GROUPS):
            gs = slice(g * gw, (g + 1) * gw)
            b_g = act[:, D_SSM + g * SSM_STATE:D_SSM + (g + 1) * SSM_STATE]
            c_g = act[:, D_SSM + (SSM_GROUPS + g) * SSM_STATE:D_SSM + (SSM_GROUPS + g + 1) * SSM_STATE]
            c_b = c_g.astype(BF16)
            bt_b = b_g.T.astype(BF16)
            cb = jnp.dot(c_b, bt_b, preferred_element_type=F32)
            st = state_ref[r, :, gs]
            y_g = jnp.dot(c_b, st.astype(BF16), preferred_element_type=F32) * ea_e[:, gs]
            y_heads = []
            for hh in range(hpg):
                h = g * hpg + hh
                col = col0 + h
                diff = acum[:, col:col + 1] - acum_t[col:col + 1, :]
                decay = jnp.where(causal, jnp.exp(jnp.minimum(diff, 0.0)), 0.0)
                y_heads.append(jnp.dot((cb * decay).astype(BF16),
                                       xdt_b[:, h * SSM_HEADDIM:(h + 1) * SSM_HEADDIM],
                                       preferred_element_type=F32))
            y_g = y_g + jnp.concatenate(y_heads, axis=-1)
            if add_skip:
                y_g = y_g + dskip_ref[:, gs] * xs[:, gs]
            y_ref[r, :, gs] = y_g
            state_ref[r, :, gs] = st * etot_e[:, gs] + jnp.dot(bt_b, w_b[:, gs], preferred_element_type=F32)
    hfin_ref[...] = state_ref[...]


def _ssd(xbc, dtp, h0, conv_w, conv_b, dt_bias, a_log, d_skip, expand, reverse, add_skip, activated):
    b, l, _ = xbc.shape
    q = SSD_CHUNK
    nc = l // q
    nr = SSD_ROWS if b % SSD_ROWS == 0 else 1
    hb = q // SUBLANES
    pos = (lambda c: nc - 1 - c) if reverse else (lambda c: c)
    const = lambda shape: pl.BlockSpec(shape, lambda i, c: (0,) * len(shape))
    kern = functools.partial(_ssd_kernel, reverse=reverse, nc=nc, add_skip=add_skip, activated=activated)
    chunk = lambda w: pl.BlockSpec((nr, q, w), lambda i, c: (i, pos(c), 0))
    act_spec = [] if activated else [chunk(XBC_DIM)]
    act_shape = [] if activated else [jax.ShapeDtypeStruct((b, l, XBC_DIM), F32)]
    return pl.pallas_call(
        kern,
        grid=(b // nr, nc),
        in_specs=[pl.BlockSpec((nr, q, XBC_DIM), lambda i, c: (i, pos(c), 0)),
                  pl.BlockSpec((nr, SUBLANES, XBC_DIM), lambda i, c: (i, jnp.maximum(pos(c) * hb - 1, 0), 0)),
                  pl.BlockSpec((nr, SUBLANES, XBC_DIM),
                               lambda i, c: (i, jnp.minimum((pos(c) + 1) * hb, l // SUBLANES - 1), 0)),
                  pl.BlockSpec((nr, q, LANES), lambda i, c: (i, pos(c), 0)),
                  pl.BlockSpec((nr, SSM_STATE, D_SSM), lambda i, c: (i, 0, 0)),
                  const(conv_w.shape), const(conv_b.shape), const(dt_bias.shape), const(a_log.shape),
                  const(d_skip.shape), const(expand.shape)],
        out_specs=[chunk(D_SSM), pl.BlockSpec((nr, SSM_STATE, D_SSM), lambda i, c: (i, 0, 0))] + act_spec,
        out_shape=[jax.ShapeDtypeStruct((b, l, D_SSM), F32),
                   jax.ShapeDtypeStruct((b, SSM_STATE, D_SSM), F32)] + act_shape,
        scratch_shapes=[pltpu.VMEM((nr, q + 2 * SUBLANES, XBC_DIM), F32),
                        pltpu.VMEM((nr, SSM_STATE, D_SSM), F32)],
        compiler_params=_params(("parallel", "arbitrary")),
        name="ssd_bwd" if reverse else "ssd_fwd",
    )(xbc, xbc, xbc, dtp, h0, conv_w, conv_b, dt_bias, a_log, d_skip, expand)


def _cfm_kernel(cur_ref, prev_ref, next_ref, w_ref, b_ref, lnw_ref, lnb_ref, o_ref, hpad_ref, vbuf_ref,
                shift_ref, *, nb):
    rows = CFM_ROWS
    band = rows * GRID_W
    half = D_CONV // 2
    pad = (CONV_WIDTH - 1) // 2
    lead = 2 * SUBLANES
    r = pl.program_id(1)

    zeros = jnp.zeros((rows, lead, half), F32)
    hpad_ref[:, 0:lead, :] = zeros
    hpad_ref[:, lead + GRID_W:2 * lead + GRID_W, :] = zeros
    hpad_ref[:, lead:lead + GRID_W, :] = cur_ref[0, :, 0:half].reshape(rows, GRID_W, half)
    vbuf_ref[0:band, :] = jnp.where(r == 0, 0.0, prev_ref[0])
    vbuf_ref[band:2 * band, :] = cur_ref[0, :, half:D_CONV]
    vbuf_ref[2 * band:3 * band, :] = jnp.where(r == nb - 1, 0.0, next_ref[0])

    span = shift_ref.shape[1]

    def row_body(i, carry):
        for s in range(SUBLANES):
            shift_ref[s] = hpad_ref[i, s:s + span, :]
        acc_h = jnp.zeros((GRID_W, half), F32)
        acc_v = jnp.zeros((GRID_W, half), F32)
        for k in range(CONV_WIDTH):
            first = lead - pad + k
            tile0 = first // SUBLANES * SUBLANES
            acc_h = acc_h + w_ref[k:k + 1, 0:half] * shift_ref[first % SUBLANES, tile0:tile0 + GRID_W, :]
            start = pl.multiple_of(band + (i + k - pad) * GRID_W, GRID_W)
            acc_v = acc_v + w_ref[k:k + 1, half:D_CONV] * vbuf_ref[pl.ds(start, GRID_W), :]
        conv = jnp.concatenate([acc_h, acc_v], axis=-1) + b_ref[...]
        mu = jnp.mean(conv, axis=-1, keepdims=True)
        cen = conv - mu
        var = jnp.mean(cen * cen, axis=-1, keepdims=True)
        o_ref[0, pl.ds(pl.multiple_of(i * GRID_W, GRID_W), GRID_W), :] = _silu(
            cen * lax.rsqrt(var + EPS) * lnw_ref[...] + lnb_ref[...]).astype(BF16)
        return carry

    lax.fori_loop(0, rows, row_body, 0, unroll=4)


def _cfm(u, conv_w, conv_b, ln_w, ln_b):
    b, l, _ = u.shape
    band = CFM_ROWS * GRID_W
    nb = l // band
    half = D_CONV // 2
    const = lambda shape: pl.BlockSpec(shape, lambda i, r: (0,) * len(shape))
    return pl.pallas_call(
        functools.partial(_cfm_kernel, nb=nb),
        grid=(b, nb),
        in_specs=[pl.BlockSpec((1, band, D_CONV), lambda i, r: (i, r, 0)),
                  pl.BlockSpec((1, band, half), lambda i, r: (i, jnp.maximum(r - 1, 0), 1)),
                  pl.BlockSpec((1, band, half), lambda i, r: (i, jnp.minimum(r + 1, nb - 1), 1)),
                  const(conv_w.shape), const(conv_b.shape), const(ln_w.shape), const(ln_b.shape)],
        out_specs=pl.BlockSpec((1, band, D_CONV), lambda i, r: (i, r, 0)),
        out_shape=jax.ShapeDtypeStruct((b, l, D_CONV), BF16),
        scratch_shapes=[pltpu.VMEM((CFM_ROWS, GRID_W + 4 * SUBLANES, half), F32),
                        pltpu.VMEM((3 * band, half), F32),
                        pltpu.VMEM((SUBLANES, GRID_W + 3 * SUBLANES, half), F32)],
        compiler_params=_params(("parallel", "parallel")),
        name="cfm",
    )(u, u, u, conv_w, conv_b, ln_w, ln_b)


def _oddeven_merge(lo, hi, r):
    step = r * 2
    if step < hi - lo:
        yield from _oddeven_merge(lo, hi, step)
        yield from _oddeven_merge(lo + r, hi, step)
        yield from [(i, i + r) for i in range(lo + r, hi - r, step)]
    else:
        yield (lo, lo + r)


def _oddeven_sort(lo, hi):
    if hi > lo:
        mid = lo + (hi - lo) // 2
        yield from _oddeven_sort(lo, mid)
        yield from _oddeven_sort(mid + 1, hi)
        yield from _oddeven_merge(lo, hi, 1)


SORT_NET = tuple(_oddeven_sort(0, PEER_TOPK - 1))


def _cmp_exchange(v, i, j):
    v[i], v[j] = jnp.maximum(v[i], v[j]), jnp.minimum(v[i], v[j])


def _bitonic_sort(v):
    d = len(v) // 2
    while d:
        for i in range(len(v)):
            if not i & d:
                _cmp_exchange(v, i, i + d)
        d //= 2


def _merge_top(v, w):
    n = len(v)
    return [jnp.maximum(v[i], w[n - 1 - i]) for i in range(n)]


def _sublane_rolls():
    shift = SUBLANES // 2
    while shift:
        yield shift
        shift //= 2


def _top_sorted(v):
    for shift in _sublane_rolls():
        v = _merge_top(v, [pltpu.roll(x, shift, 0) for x in v])
        _bitonic_sort(v)
    return v


def _score_tokens(q_ref, keys_ref, neg_ref, c1_ref, e1_ref, r2_ref, e2_ref):
    tm = q_ref.shape[0]
    nsub = tm // LANES
    kd = PEER_KEY_DIM // 2
    k = PEER_TOPK
    ntile = N_KEYS // SUBLANES

    def body(it, carry):
        h = it // nsub
        tok = pl.multiple_of((it % nsub) * LANES, LANES)
        nt = (((1,), (1,)), ((), ()))

        def score_tiles(p):
            qs = q_ref[pl.ds(tok, LANES), pl.ds(pl.multiple_of((2 * h + p) * kd, kd), kd)]
            s = lax.dot_general(keys_ref[2 * h + p], qs, nt, preferred_element_type=F32)
            return [s[i * SUBLANES:(i + 1) * SUBLANES, :] for i in range(ntile)]

        def top(tiles):
            v = list(tiles)
            for i, j in SORT_NET:
                _cmp_exchange(v, i, j)
            return _top_sorted(v)

        s1 = score_tiles(0)
        s2 = score_tiles(1)
        a = top(s1)
        b = top(s2)

        sub = lax.broadcasted_iota(jnp.int32, (SUBLANES, LANES), 0)
        pick = lambda lo, x, y: jnp.where(sub < lo, x, y)
        base = pick(4, pick(2, pick(1, a[0], a[1]), pick(3, a[2], a[3])), pick(5, b[0], pick(6, b[1], b[2])))
        cand = []
        for m in range(k):
            other = pick(4, b[m], a[4 + m]) if 4 + m < k else b[m]
            cand.append((other + base) + neg_ref[m])
        lists = list(cand)
        rolls = list(_sublane_rolls())
        for shift in rolls[:-1]:
            lists = _merge_top(lists, [pltpu.roll(x, shift, 0) for x in lists])
            _bitonic_sort(lists)
        lists = _merge_top(lists, [pltpu.roll(x, rolls[-1], 0) for x in lists])
        tau = functools.reduce(jnp.minimum, lists)

        best = a[0] + b[0]
        zsum = sum(jnp.where(c >= tau, jnp.exp(c - best), 0.0) for c in cand)
        for shift in rolls:
            zsum = zsum + pltpu.roll(zsum, shift, 0)
        scale = 0.5 / zsum
        top_extra = sum(jnp.where(a[0] + b[j] >= tau, 1.0, 0.0) for j in range(k // 2, k))

        c1_t, e1_t, r2_t, e2_t = [], [], [], []
        for t1, t2 in zip(s1, s2):
            c1 = jnp.zeros_like(t1)
            for j in range(k // 2):
                c1 = jnp.where(t1 + b[j] >= tau, float(j + 1), c1)
            c1_t.append(jnp.where(t1 == a[0], c1 + top_extra, c1))
            e1_t.append(jnp.exp(t1 - a[0]) * scale)
            r2 = jnp.zeros_like(t2)
            for j in range(k):
                r2 = jnp.where(t2 < b[j], float(j + 1), r2)
            r2_t.append(r2)
            e2_t.append(jnp.exp(t2 - b[0]))
        c1_ref[h, :, pl.ds(tok, LANES)] = jnp.concatenate(c1_t, axis=0)
        e1_ref[h, :, pl.ds(tok, LANES)] = jnp.concatenate(e1_t, axis=0)
        r2_ref[h, :, pl.ds(tok, LANES)] = jnp.concatenate(r2_t, axis=0).astype(BF16)
        e2_ref[h, :, pl.ds(tok, LANES)] = jnp.concatenate(e2_t, axis=0).astype(BF16)
        return carry

    lax.fori_loop(0, PEER_HEADS * nsub, body, 0, unroll=8)


def _candidate_mask():
    k = PEER_TOPK
    m = lax.broadcasted_iota(jnp.int32, (k, SUBLANES, LANES), 0)
    s = lax.broadcasted_iota(jnp.int32, (k, SUBLANES, LANES), 1)
    row_ok = (s < 4) & ((s + 1) * (m + 1) <= k)
    col_ok = (s >= 4) & (s < 7) & (4 + m < k) & ((5 + m) * (s - 3) <= k)
    return jnp.where(row_ok | col_ok, 0.0, -jnp.inf).astype(F32)


def _outscore_kernel(yf_ref, yb_ref, z_ref, cfm_ref, x_ref, mod_ref, gnw_ref, w_ref, n2w_ref,
                     wq_ref, keys_ref, neg_ref, u_ref, v_ref,
                     x1_ref, h2t_ref, c1_ref, e1_ref, r2_ref, e2_ref, ub_ref, vt_ref, q_ref):
    ub_ref[...] = u_ref[...].astype(BF16)
    vt_ref[...] = v_ref[...].T.astype(BF16)
    gw = D_SSM // SSM_GROUPS
    g = (yf_ref[0] + yb_ref[0]) * _silu(z_ref[0])
    parts = []
    for k in range(SSM_GROUPS):
        gk = g[:, k * gw:(k + 1) * gw]
        parts.append(gk * lax.rsqrt(jnp.mean(gk * gk, axis=-1, keepdims=True) + EPS))
    ssm = jnp.concatenate(parts, axis=-1) * gnw_ref[...]
    mixed = (jnp.dot(ssm.astype(BF16), w_ref[0:D_SSM, :], preferred_element_type=F32)
             + jnp.dot(cfm_ref[0], w_ref[D_SSM:, :], preferred_element_type=F32))
    x1 = x_ref[0] + mod_ref[0, 2:3, :] * mixed
    x1_ref[0] = x1
    h = x1 * lax.rsqrt(jnp.mean(x1 * x1, axis=-1, keepdims=True) + EPS) * n2w_ref[...]
    h2 = h * (1.0 + mod_ref[0, 4:5, :]) + mod_ref[0, 3:4, :]
    h2t_ref[...] = h2.T.astype(BF16)
    q_ref[...] = jnp.dot(h2.astype(BF16), wq_ref[...], preferred_element_type=F32).astype(BF16)
    _score_tokens(q_ref, keys_ref, neg_ref, c1_ref, e1_ref, r2_ref, e2_ref)


def _outscore(yf, yb, z, cfm, x, mod, gn_w, w_out_b, n2_w, wq_b, keys_b, u_tab, v_tab):
    b, l, d = x.shape
    t = b * l
    tm = min(OUT_TILE, l)
    neg = _candidate_mask()
    n_exp = u_tab.shape[0]
    rows = n_exp // (t // tm)
    assert rows * (t // tm) == n_exp and PEER_EBLK % rows == 0 and rows % LANES == 0
    per_slab = PEER_EBLK // rows
    tile = lambda i, j: i * (l // tm) + j
    tok = lambda w: pl.BlockSpec((1, tm, w), lambda i, j: (i, j, 0))
    const = lambda shape: pl.BlockSpec(shape, lambda i, j: (0,) * len(shape))
    big = lambda: pl.BlockSpec((PEER_HEADS, N_KEYS, tm), lambda i, j: (0, 0, tile(i, j)))
    big_shape = lambda dt: jax.ShapeDtypeStruct((PEER_HEADS, N_KEYS, t), dt)
    return pl.pallas_call(
        _outscore_kernel,
        grid=(b, l // tm),
        in_specs=[tok(D_SSM), tok(D_SSM), tok(D_SSM), tok(D_CONV), tok(d),
                  pl.BlockSpec((1, 6, d), lambda i, j: (i, 0, 0)),
                  const(gn_w.shape), const(w_out_b.shape), const(n2_w.shape),
                  const(wq_b.shape), const(keys_b.shape), const(neg.shape),
                  pl.BlockSpec((rows, d), lambda i, j: (tile(i, j), 0)),
                  pl.BlockSpec((rows, d), lambda i, j: (tile(i, j), 0))],
        out_specs=[tok(d), pl.BlockSpec((d, tm), lambda i, j: (0, tile(i, j))), big(), big(), big(), big(),
                   pl.BlockSpec((rows, d), lambda i, j: (tile(i, j), 0)),
                   pl.BlockSpec((None, d, rows), lambda i, j: (tile(i, j) // per_slab, 0, tile(i, j) % per_slab))],
        out_shape=[jax.ShapeDtypeStruct((b, l, d), F32), jax.ShapeDtypeStruct((d, t), BF16),
                   big_shape(F32), big_shape(F32), big_shape(BF16), big_shape(BF16),
                   jax.ShapeDtypeStruct((n_exp, d), BF16),
                   jax.ShapeDtypeStruct((n_exp // PEER_EBLK, d, PEER_EBLK), BF16)],
        scratch_shapes=[pltpu.VMEM((tm, PEER_HEADS * PEER_KEY_DIM), BF16)],
        compiler_params=_params(("parallel", "parallel")),
        name="outproj_score",
    )(yf, yb, z, cfm, x, mod, gn_w, w_out_b, n2_w, wq_b, keys_b, neg, u_tab, v_tab)


def _key_row_bf16(ref, h, key0, j):
    tile = ref[h, pl.ds(key0 + j // SUBLANES * SUBLANES, SUBLANES), :]
    row = tile[j % SUBLANES:j % SUBLANES + 1, :]
    rows = jnp.broadcast_to(row, (BF16_ROWS, ref.shape[-1])).astype(BF16)
    return jnp.tile(rows, (N_KEYS // BF16_ROWS, 1))


def _peer_kernel(ht_ref, u_ref, vt_ref, c1_ref, e1_ref, r2_ref, e2_ref, x1_ref, mod_ref, fnw_ref,
                 o_ref, acc_ref):
    eb = pl.program_id(1)
    keys_per_step = PEER_EBLK // N_KEYS
    key0 = pl.multiple_of(eb * keys_per_step, keys_per_step)
    starts = [sum(PEER_SUBS[:i]) for i in range(len(PEER_SUBS))]

    @pl.when(eb == 0)
    def _():
        acc_ref[...] = jnp.zeros_like(acc_ref)

    def first_matmul(sub):
        return jnp.dot(u_ref[starts[sub]:starts[sub] + PEER_SUBS[sub], :], ht_ref[...],
                       preferred_element_type=F32)

    def build_gates(sub):
        gates = []
        for kk in range(PEER_SUBS[sub] // N_KEYS):
            j = starts[sub] // N_KEYS + kk
            g = jnp.zeros((N_KEYS, ht_ref.shape[1]), BF16)
            for h in range(PEER_HEADS):
                c1 = _key_row_bf16(c1_ref, h, key0, j)
                e1 = _key_row_bf16(e1_ref, h, key0, j)
                g = g + jnp.where(r2_ref[h] < c1, e2_ref[h], jnp.zeros((), BF16)) * e1
            gates.append(g)
        return jnp.concatenate(gates, axis=0)

    nsub = len(PEER_SUBS)
    pre = first_matmul(0)
    weighted = []
    for sub in range(nsub):
        nxt = first_matmul(sub + 1) if sub + 1 < nsub else None
        act = pre * (1.0 + lax.erf(pre * (2.0 ** -0.5)))
        weighted.append(act.astype(BF16) * build_gates(sub))
        pre = nxt
    acc_ref[...] = jnp.dot(vt_ref[...], jnp.concatenate(weighted, axis=0),
                           preferred_element_type=F32) + acc_ref[...]

    @pl.when(eb == pl.num_programs(1) - 1)
    def _():
        x2 = x1_ref[...] + mod_ref[0, 5:6, :] * acc_ref[...].T
        o_ref[...] = x2 * lax.rsqrt(jnp.mean(x2 * x2, axis=-1, keepdims=True) + EPS) * fnw_ref[...]


def _peer(h2t, u_b, vt_b, c1, e1, r2, e2, x1, mod, fn_w, seq):
    d, t = h2t.shape
    tm = min(PEER_TILE, seq)
    n_exp = u_b.shape[0]
    big = lambda: pl.BlockSpec((PEER_HEADS, N_KEYS, tm), lambda j, e: (0, 0, j))
    return pl.pallas_call(
        _peer_kernel,
        grid=(t // tm, n_exp // PEER_EBLK),
        in_specs=[pl.BlockSpec((d, tm), lambda j, e: (0, j)),
                  pl.BlockSpec((PEER_EBLK, d), lambda j, e: (e, 0)),
                  pl.BlockSpec((None, d, PEER_EBLK), lambda j, e: (e, 0, 0)),
                  big(), big(), big(), big(),
                  pl.BlockSpec((tm, d), lambda j, e: (j, 0)),
                  pl.BlockSpec((1, 6, d), lambda j, e: ((j * tm) // seq, 0, 0)),
                  pl.BlockSpec((1, d), lambda j, e: (0, 0))],
        out_specs=pl.BlockSpec((tm, d), lambda j, e: (j, 0)),
        out_shape=jax.ShapeDtypeStruct((t, d), F32),
        scratch_shapes=[pltpu.VMEM((d, tm), F32)],
        compiler_params=_params(("parallel", "arbitrary")),
        name="peer_dense",
    )(h2t, u_b, vt_b, c1, e1, r2, e2, x1, mod, fn_w)


def _pad_lanes(v, width=LANES):
    flat = v.reshape(1, -1)
    return jnp.pad(flat, ((0, 0), (0, width - flat.shape[1])))


def _head_expand(col0):
    rows = lax.broadcasted_iota(jnp.int32, (LANES, D_SSM), 0)
    heads = lax.broadcasted_iota(jnp.int32, (LANES, D_SSM), 1) // SSM_HEADDIM
    return (rows == heads + col0).astype(BF16)


def kernel(x, c, ctx, c_ctx, ada_w, ada_b, norm1_w, norm2_w, w_in, ssm_conv_w, ssm_conv_b, ssm_dt_bias,
           ssm_a_log, ssm_d, ssm_norm_w, cfm_conv_w, cfm_conv_b, cfm_ln_w, cfm_ln_b, w_out, peer_wq,
           peer_subkeys, peer_u, peer_v, final_norm_w):
    depth = ada_w.shape[0]
    assert depth == 1, "single-layer configuration"
    b, l, d = x.shape
    i = 0

    c_rows = jnp.concatenate([c, c_ctx[None], jnp.zeros((2 * SUBLANES - b - 1, d), F32)], axis=0)
    mod_all = _ada(c_rows, ada_w[i], ada_b[i])
    mod = mod_all[:b].reshape(b, 6, d)
    mod_ctx = mod_all[b:b + 1].reshape(1, 6, d)

    wi = w_in[i]
    o_dt = D_SSM + XBC_DIM
    w_r = jnp.concatenate([wi[:, :o_dt], wi[:, o_dt + 2 * SSM_HEADS:], wi[:, o_dt:o_dt + 2 * SSM_HEADS],
                           jnp.zeros((d, LANES - 2 * SSM_HEADS), F32)], axis=1).astype(BF16)
    z_l, xbc_l, u_l, dt_l = _inproj(x, mod, True, norm1_w[i], w_r)
    _, xbc_c, _, dt_c = _inproj(ctx, mod_ctx, False, norm1_w[i], w_r)

    conv_w = jnp.pad(ssm_conv_w[i], ((0, SUBLANES - SSM_CONV), (0, 0)))
    conv_b = ssm_conv_b[i].reshape(1, XBC_DIM)
    dt_bias = _pad_lanes(ssm_dt_bias[i])
    a_log = _pad_lanes(ssm_a_log[i])
    d_skip = jnp.repeat(ssm_d[i], SSM_HEADDIM).reshape(1, D_SSM)
    zero_state = jnp.zeros((b, SSM_STATE, D_SSM), F32)
    y_dirs = []
    act_c = act_l = None
    for reverse in (False, True):
        expand = _head_expand(SSM_HEADS if reverse else 0)
        args = (conv_w, conv_b, dt_bias, a_log, d_skip, expand, reverse)
        if not reverse:
            _, h_ctx, act_c = _ssd(xbc_c, dt_c, zero_state, *args, add_skip=False, activated=False)
            y_dir, _, act_l = _ssd(xbc_l, dt_l, h_ctx, *args, add_skip=True, activated=False)
        else:
            _, h_ctx = _ssd(act_c, dt_c, zero_state, *args, add_skip=False, activated=True)
            y_dir, _ = _ssd(act_l, dt_l, h_ctx, *args, add_skip=False, activated=True)
        y_dirs.append(y_dir)

    cfm = _cfm(u_l, jnp.pad(cfm_conv_w[i], ((0, 1), (0, 0))), cfm_conv_b[i].reshape(1, D_CONV),
               cfm_ln_w[i].reshape(1, D_CONV), cfm_ln_b[i].reshape(1, D_CONV))

    t = b * l
    keys_b = peer_subkeys[i].reshape(PEER_HEADS * 2, N_KEYS, PEER_KEY_DIM // 2).astype(BF16)
    x1, h2t, c1, e1, r2, e2, u_b, vt_b = _outscore(
        y_dirs[0], y_dirs[1], z_l, cfm, x, mod, ssm_norm_w[i].reshape(1, D_SSM), w_out[i].astype(BF16),
        norm2_w[i].reshape(1, d), peer_wq[i].astype(BF16), keys_b, peer_u[i], peer_v[i])
    out = _peer(h2t, u_b, vt_b, c1, e1, r2, e2,
                x1.reshape(t, d), mod, final_norm_w.reshape(1, d), l)
    return out.reshape(b, l, d)
```

```python
import functools

import jax
import jax.numpy as jnp
from jax import lax
from jax.experimental import pallas as pl
from jax.experimental.pallas import tpu as pltpu

F32 = jnp.float32
BF16 = jnp.bfloat16

D_MODEL = 1024
GRID_W = 64
D_SSM = 512
SSM_HEADDIM = 64
SSM_HEADS = 8
SSM_STATE = 128
SSM_GROUPS = 2
SSM_CONV = 5
SSD_CHUNK = 128
XBC_DIM = D_SSM + 2 * SSM_GROUPS * SSM_STATE
D_CONV = 512
CONV_WIDTH = 31
PEER_HEADS = 8
PEER_KEY_DIM = 256
N_KEYS = 128
PEER_TOPK = 16
EPS = 1e-6

LANES = 128
SUBLANES = 8
VMEM_LIMIT = 56 * 1024 * 1024

ADA_TILE = 768
IN_TILE = 512
SSD_ROWS = 8
CFM_ROWS = 16
OUT_TILE = 512
PEER_TILE = 512
PEER_EBLK = 2048
PEER_SUBS = (128,) * 16
assert sum(PEER_SUBS) == PEER_EBLK and all(s % N_KEYS == 0 for s in PEER_SUBS)
BF16_ROWS = 2 * SUBLANES


def _params(sem):
    return pltpu.CompilerParams(dimension_semantics=sem, vmem_limit_bytes=VMEM_LIMIT)


def _silu(v):
    return v * jax.nn.sigmoid(v)


def _split3(v):
    hi = v.astype(BF16)
    r1 = v - hi.astype(F32)
    mid = r1.astype(BF16)
    lo = (r1 - mid.astype(F32)).astype(BF16)
    return hi, mid, lo


def _dot01_right(v, m01):
    return sum(jnp.dot(p, m01, preferred_element_type=F32) for p in _split3(v))


def _dot01_left(m01, v):
    return sum(jnp.dot(m01, p, preferred_element_type=F32) for p in _split3(v))


def _ada_kernel(c_ref, w_ref, b_ref, o_ref):
    sc = _silu(c_ref[...])
    o_ref[...] = jnp.dot(sc.astype(BF16), w_ref[...].astype(BF16),
                         preferred_element_type=F32) + b_ref[...]


def _ada(c_rows, ada_w, ada_b):
    n = ada_w.shape[1]
    tn = ADA_TILE
    return pl.pallas_call(
        _ada_kernel,
        grid=(n // tn,),
        in_specs=[pl.BlockSpec(c_rows.shape, lambda j: (0, 0)),
                  pl.BlockSpec((D_MODEL, tn), lambda j: (0, j)),
                  pl.BlockSpec((1, tn), lambda j: (0, j))],
        out_specs=pl.BlockSpec((c_rows.shape[0], tn), lambda j: (0, j)),
        out_shape=jax.ShapeDtypeStruct((c_rows.shape[0], n), F32),
        compiler_params=_params(("parallel",)),
        name="ada",
    )(c_rows, ada_w, ada_b.reshape(1, n))


def _inproj_kernel(x_ref, mod_ref, nw_ref, w_ref, z_ref, xbc_ref, u_ref, dt_ref):
    xv = x_ref[0]
    h = xv * lax.rsqrt(jnp.mean(xv * xv, axis=-1, keepdims=True) + EPS) * nw_ref[...]
    h = h * (1.0 + mod_ref[0, 1:2, :]) + mod_ref[0, 0:1, :]
    hb = h.astype(BF16)

    def proj(lo, hi):
        return jnp.dot(hb, w_ref[:, lo:hi], preferred_element_type=F32)

    z_ref[0] = proj(0, D_SSM)
    xbc_ref[0] = proj(D_SSM, D_SSM + XBC_DIM)
    o = D_SSM + XBC_DIM
    u_ref[0] = proj(o, o + D_CONV) * jax.nn.sigmoid(proj(o + D_CONV, o + 2 * D_CONV))
    dt_ref[0] = proj(o + 2 * D_CONV, o + 2 * D_CONV + LANES)


def _inproj(x, mod, per_batch_mod, norm_w, w_r):
    b, l, d = x.shape
    tm = min(IN_TILE, l)
    mod_idx = (lambda i, j: (i, 0, 0)) if per_batch_mod else (lambda i, j: (0, 0, 0))
    tok = lambda w: pl.BlockSpec((1, tm, w), lambda i, j: (i, j, 0))
    return pl.pallas_call(
        _inproj_kernel,
        grid=(b, l // tm),
        in_specs=[tok(d),
                  pl.BlockSpec((1, 6, d), mod_idx),
                  pl.BlockSpec((1, d), lambda i, j: (0, 0)),
                  pl.BlockSpec(w_r.shape, lambda i, j: (0, 0))],
        out_specs=[tok(D_SSM), tok(XBC_DIM), tok(D_CONV), tok(LANES)],
        out_shape=[jax.ShapeDtypeStruct((b, l, w), F32) for w in (D_SSM, XBC_DIM, D_CONV, LANES)],
        compiler_params=_params(("parallel", "parallel")),
        name="inproj",
    )(x, mod, norm_w.reshape(1, d), w_r)


def _ssd_kernel(cur_ref, prev_ref, next_ref, dt_ref, h0_ref, cw_ref, cb_ref, dtb_ref, alog_ref,
                dskip_ref, expand_ref, y_ref, hfin_ref, *rest, reverse, nc, add_skip, activated):
    act_ref, ext_ref, state_ref = (None, *rest) if activated else rest
    q = SSD_CHUNK
    halo = SUBLANES
    pad = (SSM_CONV - 1) // 2
    gw = D_SSM // SSM_GROUPS
    hpg = SSM_HEADS // SSM_GROUPS
    col0 = SSM_HEADS if reverse else 0
    c = pl.program_id(1)
    cc = (nc - 1 - c) if reverse else c

    @pl.when(c == 0)
    def _():
        state_ref[...] = h0_ref[...]

    ri = lax.broadcasted_iota(jnp.int32, (q, q), 0)
    ci = lax.broadcasted_iota(jnp.int32, (q, q), 1)
    causal = (ci >= ri) if reverse else (ci <= ri)
    causal01 = jnp.where(causal, 1.0, 0.0).astype(BF16)
    a_row = -jnp.exp(alog_ref[...])

    for r in range(cur_ref.shape[0]):
        if activated:
            act = cur_ref[r]
        else:
            ext_ref[r, 0:halo, :] = jnp.where(cc == 0, 0.0, prev_ref[r])
            ext_ref[r, halo:halo + q, :] = cur_ref[r]
            ext_ref[r, halo + q:2 * halo + q, :] = jnp.where(cc == nc - 1, 0.0, next_ref[r])
            conv = cb_ref[...] + cw_ref[0:1, :] * ext_ref[r, halo - pad:halo - pad + q, :]
            for k in range(1, SSM_CONV):
                conv = conv + cw_ref[k:k + 1, :] * ext_ref[r, halo - pad + k:halo - pad + k + q, :]
            act = _silu(conv)
            act_ref[r] = act

        dtv = dt_ref[r] + dtb_ref[...]
        dt = jnp.maximum(dtv, 0.0) + jnp.log(1.0 + jnp.exp(-jnp.abs(dtv)))
        acum = _dot01_left(causal01, dt * a_row)
        acum_t = acum.T
        tot = acum[0:1, :] if reverse else acum[q - 1:q, :]
        stacked = jnp.concatenate([dt, acum, jnp.broadcast_to(tot, (SUBLANES, LANES))], axis=0)
        spread = _dot01_right(stacked, expand_ref[...])
        dt_e, acum_e, tot_e = spread[0:q], spread[q:2 * q], spread[2 * q:2 * q + 1]
        dte_e = jnp.exp(tot_e - acum_e)
        ea_e = jnp.exp(acum_e)
        etot_e = jnp.exp(tot_e)

        xs = act[:, :D_SSM]
        xdt = xs * dt_e
        xdt_b = xdt.astype(BF16)
        w_b = (xdt * dte_e).astype(BF16)
        for g in range(SSM_GROUPS):
            gs = slice(g * gw, (g + 1) * gw)
            b_g = act[:, D_SSM + g * SSM_STATE:D_SSM + (g + 1) * SSM_STATE]
            c_g = act[:, D_SSM + (SSM_GROUPS + g) * SSM_STATE:D_SSM + (SSM_GROUPS + g + 1) * SSM_STATE]
            c_b = c_g.astype(BF16)
            bt_b = b_g.T.astype(BF16)
            cb = jnp.dot(c_b, bt_b, preferred_element_type=F32)
            st = state_ref[r, :, gs]
            y_g = jnp.dot(c_b, st.astype(BF16), preferred_element_type=F32) * ea_e[:, gs]
            y_heads = []
            for hh in range(hpg):
                h = g * hpg + hh
                col = col0 + h
                diff = acum[:, col:col + 1] - acum_t[col:col + 1, :]
                decay = jnp.where(causal, jnp.exp(jnp.minimum(diff, 0.0)), 0.0)
                y_heads.append(jnp.dot((cb * decay).astype(BF16),
                                       xdt_b[:, h * SSM_HEADDIM:(h + 1) * SSM_HEADDIM],
                                       preferred_element_type=F32))
            y_g = y_g + jnp.concatenate(y_heads, axis=-1)
            if add_skip:
                y_g = y_g + dskip_ref[:, gs] * xs[:, gs]
            y_ref[r, :, gs] = y_g
            state_ref[r, :, gs] = st * etot_e[:, gs] + jnp.dot(bt_b, w_b[:, gs], preferred_element_type=F32)
    hfin_ref[...] = state_ref[...]


def _ssd(xbc, dtp, h0, conv_w, conv_b, dt_bias, a_log, d_skip, expand, reverse, add_skip, activated):
    b, l, _ = xbc.shape
    q = SSD_CHUNK
    nc = l // q
    nr = SSD_ROWS if b % SSD_ROWS == 0 else 1
    hb = q // SUBLANES
    pos = (lambda c: nc - 1 - c) if reverse else (lambda c: c)
    const = lambda shape: pl.BlockSpec(shape, lambda i, c: (0,) * len(shape))
    kern = functools.partial(_ssd_kernel, reverse=reverse, nc=nc, add_skip=add_skip, activated=activated)
    chunk = lambda w: pl.BlockSpec((nr, q, w), lambda i, c: (i, pos(c), 0))
    act_spec = [] if activated else [chunk(XBC_DIM)]
    act_shape = [] if activated else [jax.ShapeDtypeStruct((b, l, XBC_DIM), F32)]
    return pl.pallas_call(
        kern,
        grid=(b // nr, nc),
        in_specs=[pl.BlockSpec((nr, q, XBC_DIM), lambda i, c: (i, pos(c), 0)),
                  pl.BlockSpec((nr, SUBLANES, XBC_DIM), lambda i, c: (i, jnp.maximum(pos(c) * hb - 1, 0), 0)),
                  pl.BlockSpec((nr, SUBLANES, XBC_DIM),
                               lambda i, c: (i, jnp.minimum((pos(c) + 1) * hb, l // SUBLANES - 1), 0)),
                  pl.BlockSpec((nr, q, LANES), lambda i, c: (i, pos(c), 0)),
                  pl.BlockSpec((nr, SSM_STATE, D_SSM), lambda i, c: (i, 0, 0)),
                  const(conv_w.shape), const(conv_b.shape), const(dt_bias.shape), const(a_log.shape),
                  const(d_skip.shape), const(expand.shape)],
        out_specs=[chunk(D_SSM), pl.BlockSpec((nr, SSM_STATE, D_SSM), lambda i, c: (i, 0, 0))] + act_spec,
        out_shape=[jax.ShapeDtypeStruct((b, l, D_SSM), F32),
                   jax.ShapeDtypeStruct((b, SSM_STATE, D_SSM), F32)] + act_shape,
        scratch_shapes=[pltpu.VMEM((nr, q + 2 * SUBLANES, XBC_DIM), F32),
                        pltpu.VMEM((nr, SSM_STATE, D_SSM), F32)],
        compiler_params=_params(("parallel", "arbitrary")),
        name="ssd_bwd" if reverse else "ssd_fwd",
    )(xbc, xbc, xbc, dtp, h0, conv_w, conv_b, dt_bias, a_log, d_skip, expand)


def _cfm_kernel(cur_ref, prev_ref, next_ref, w_ref, b_ref, lnw_ref, lnb_ref, o_ref, hpad_ref, vbuf_ref,
                shift_ref, *, nb):
    rows = CFM_ROWS
    band = rows * GRID_W
    half = D_CONV // 2
    pad = (CONV_WIDTH - 1) // 2
    lead = 2 * SUBLANES
    r = pl.program_id(1)

    zeros = jnp.zeros((rows, lead, half), F32)
    hpad_ref[:, 0:lead, :] = zeros
    hpad_ref[:, lead + GRID_W:2 * lead + GRID_W, :] = zeros
    hpad_ref[:, lead:lead + GRID_W, :] = cur_ref[0, :, 0:half].reshape(rows, GRID_W, half)
    vbuf_ref[0:band, :] = jnp.where(r == 0, 0.0, prev_ref[0])
    vbuf_ref[band:2 * band, :] = cur_ref[0, :, half:D_CONV]
    vbuf_ref[2 * band:3 * band, :] = jnp.where(r == nb - 1, 0.0, next_ref[0])

    span = shift_ref.shape[1]

    def row_body(i, carry):
        for s in range(SUBLANES):
            shift_ref[s] = hpad_ref[i, s:s + span, :]
        acc_h = jnp.zeros((GRID_W, half), F32)
        acc_v = jnp.zeros((GRID_W, half), F32)
        for k in range(CONV_WIDTH):
            first = lead - pad + k
            tile0 = first // SUBLANES * SUBLANES
            acc_h = acc_h + w_ref[k:k + 1, 0:half] * shift_ref[first % SUBLANES, tile0:tile0 + GRID_W, :]
            start = pl.multiple_of(band + (i + k - pad) * GRID_W, GRID_W)
            acc_v = acc_v + w_ref[k:k + 1, half:D_CONV] * vbuf_ref[pl.ds(start, GRID_W), :]
        conv = jnp.concatenate([acc_h, acc_v], axis=-1) + b_ref[...]
        mu = jnp.mean(conv, axis=-1, keepdims=True)
        cen = conv - mu
        var = jnp.mean(cen * cen, axis=-1, keepdims=True)
        o_ref[0, pl.ds(pl.multiple_of(i * GRID_W, GRID_W), GRID_W), :] = _silu(
            cen * lax.rsqrt(var + EPS) * lnw_ref[...] + lnb_ref[...]).astype(BF16)
        return carry

    lax.fori_loop(0, rows, row_body, 0, unroll=4)


def _cfm(u, conv_w, conv_b, ln_w, ln_b):
    b, l, _ = u.shape
    band = CFM_ROWS * GRID_W
    nb = l // band
    half = D_CONV // 2
    const = lambda shape: pl.BlockSpec(shape, lambda i, r: (0,) * len(shape))
    return pl.pallas_call(
        functools.partial(_cfm_kernel, nb=nb),
        grid=(b, nb),
        in_specs=[pl.BlockSpec((1, band, D_CONV), lambda i, r: (i, r, 0)),
                  pl.BlockSpec((1, band, half), lambda i, r: (i, jnp.maximum(r - 1, 0), 1)),
                  pl.BlockSpec((1, band, half), lambda i, r: (i, jnp.minimum(r + 1, nb - 1), 1)),
                  const(conv_w.shape), const(conv_b.shape), const(ln_w.shape), const(ln_b.shape)],
        out_specs=pl.BlockSpec((1, band, D_CONV), lambda i, r: (i, r, 0)),
        out_shape=jax.ShapeDtypeStruct((b, l, D_CONV), BF16),
        scratch_shapes=[pltpu.VMEM((CFM_ROWS, GRID_W + 4 * SUBLANES, half), F32),
                        pltpu.VMEM((3 * band, half), F32),
                        pltpu.VMEM((SUBLANES, GRID_W + 3 * SUBLANES, half), F32)],
        compiler_params=_params(("parallel", "parallel")),
        name="cfm",
    )(u, u, u, conv_w, conv_b, ln_w, ln_b)


def _oddeven_merge(lo, hi, r):
    step = r * 2
    if step < hi - lo:
        yield from _oddeven_merge(lo, hi, step)
        yield from _oddeven_merge(lo + r, hi, step)
        yield from [(i, i + r) for i in range(lo + r, hi - r, step)]
    else:
        yield (lo, lo + r)


def _oddeven_sort(lo, hi):
    if hi > lo:
        mid = lo + (hi - lo) // 2
        yield from _oddeven_sort(lo, mid)
        yield from _oddeven_sort(mid + 1, hi)
        yield from _oddeven_merge(lo, hi, 1)


SORT_NET = tuple(_oddeven_sort(0, PEER_TOPK - 1))


def _cmp_exchange(v, i, j):
    v[i], v[j] = jnp.maximum(v[i], v[j]), jnp.minimum(v[i], v[j])


def _bitonic_sort(v):
    d = len(v) // 2
    while d:
        for i in range(len(v)):
            if not i & d:
                _cmp_exchange(v, i, i + d)
        d //= 2


def _merge_top(v, w):
    n = len(v)
    return [jnp.maximum(v[i], w[n - 1 - i]) for i in range(n)]


def _sublane_rolls():
    shift = SUBLANES // 2
    while shift:
        yield shift
        shift //= 2


def _top_sorted(v):
    for shift in _sublane_rolls():
        v = _merge_top(v, [pltpu.roll(x, shift, 0) for x in v])
        _bitonic_sort(v)
    return v


def _count_greater(b, t):
    n = len(b)
    count = jnp.zeros_like(t)
    above = []
    step = n // 2
    while step:
        cands = [b[base + step - 1] for base in range(0, n, 2 * step)]
        for hit in reversed(above):
            cands = [jnp.where(hit, cands[i + 1], cands[i]) for i in range(0, len(cands), 2)]
        hit = cands[0] > t
        count = count + jnp.where(hit, float(step), 0.0)
        above.append(hit)
        step //= 2
    return jnp.where(b[n - 1] > t, float(n), count)


def _score_tokens(q_ref, keys_ref, neg_ref, c1_ref, e1_ref, r2_ref, e2_ref):
    tm = q_ref.shape[0]
    nsub = tm // LANES
    kd = PEER_KEY_DIM // 2
    k = PEER_TOPK
    ntile = N_KEYS // SUBLANES

    def body(it, carry):
        h = it // nsub
        tok = pl.multiple_of((it % nsub) * LANES, LANES)
        nt = (((1,), (1,)), ((), ()))

        def score_tiles(p):
            qs = q_ref[pl.ds(tok, LANES), pl.ds(pl.multiple_of((2 * h + p) * kd, kd), kd)]
            s = lax.dot_general(keys_ref[2 * h + p], qs, nt, preferred_element_type=F32)
            return [s[i * SUBLANES:(i + 1) * SUBLANES, :] for i in range(ntile)]

        def top(tiles):
            v = list(tiles)
            for i, j in SORT_NET:
                _cmp_exchange(v, i, j)
            return _top_sorted(v)

        s1 = score_tiles(0)
        s2 = score_tiles(1)
        a = top(s1)
        b = top(s2)

        sub = lax.broadcasted_iota(jnp.int32, (SUBLANES, LANES), 0)
        pick = lambda lo, x, y: jnp.where(sub < lo, x, y)
        base = pick(4, pick(2, pick(1, a[0], a[1]), pick(3, a[2], a[3])), pick(5, b[0], pick(6, b[1], b[2])))
        cand = []
        for m in range(k):
            other = pick(4, b[m], a[4 + m]) if 4 + m < k else b[m]
            cand.append((other + base) + neg_ref[m])
        lists = list(cand)
        rolls = list(_sublane_rolls())
        for shift in rolls[:-1]:
            lists = _merge_top(lists, [pltpu.roll(x, shift, 0) for x in lists])
            _bitonic_sort(lists)
        lists = _merge_top(lists, [pltpu.roll(x, rolls[-1], 0) for x in lists])
        tau = functools.reduce(jnp.minimum, lists)

        best = a[0] + b[0]
        zsum = sum(jnp.where(c >= tau, jnp.exp(c - best), 0.0) for c in cand)
        for shift in rolls:
            zsum = zsum + pltpu.roll(zsum, shift, 0)
        scale = 0.5 / zsum
        top_extra = sum(jnp.where(a[0] + b[j] >= tau, 1.0, 0.0) for j in range(k // 2, k))

        c1_t, e1_t, r2_t, e2_t = [], [], [], []
        for t1, t2 in zip(s1, s2):
            c1 = jnp.zeros_like(t1)
            for j in range(k // 2):
                c1 = jnp.where(t1 + b[j] >= tau, float(j + 1), c1)
            c1_t.append(jnp.where(t1 == a[0], c1 + top_extra, c1))
            e1_t.append(jnp.exp(t1 - a[0]) * scale)
            r2_t.append(_count_greater(b, t2))
            e2_t.append(jnp.exp(t2 - b[0]))
        c1_ref[h, :, pl.ds(tok, LANES)] = jnp.concatenate(c1_t, axis=0)
        e1_ref[h, :, pl.ds(tok, LANES)] = jnp.concatenate(e1_t, axis=0)
        r2_ref[h, :, pl.ds(tok, LANES)] = jnp.concatenate(r2_t, axis=0).astype(BF16)
        e2_ref[h, :, pl.ds(tok, LANES)] = jnp.concatenate(e2_t, axis=0).astype(BF16)
        return carry

    lax.fori_loop(0, PEER_HEADS * nsub, body, 0, unroll=8)


def _candidate_mask():
    k = PEER_TOPK
    m = lax.broadcasted_iota(jnp.int32, (k, SUBLANES, LANES), 0)
    s = lax.broadcasted_iota(jnp.int32, (k, SUBLANES, LANES), 1)
    row_ok = (s < 4) & ((s + 1) * (m + 1) <= k)
    col_ok = (s >= 4) & (s < 7) & (4 + m < k) & ((5 + m) * (s - 3) <= k)
    return jnp.where(row_ok | col_ok, 0.0, -jnp.inf).astype(F32)


def _outscore_kernel(yf_ref, yb_ref, z_ref, cfm_ref, x_ref, mod_ref, gnw_ref, w_ref, n2w_ref,
                     wq_ref, keys_ref, neg_ref, u_ref, v_ref,
                     x1_ref, h2t_ref, c1_ref, e1_ref, r2_ref, e2_ref, ub_ref, vt_ref, q_ref):
    ub_ref[...] = u_ref[...].astype(BF16)
    vt_ref[...] = v_ref[...].T.astype(BF16)
    gw = D_SSM // SSM_GROUPS
    g = (yf_ref[0] + yb_ref[0]) * _silu(z_ref[0])
    parts = []
    for k in range(SSM_GROUPS):
        gk = g[:, k * gw:(k + 1) * gw]
        parts.append(gk * lax.rsqrt(jnp.mean(gk * gk, axis=-1, keepdims=True) + EPS))
    ssm = jnp.concatenate(parts, axis=-1) * gnw_ref[...]
    mixed = (jnp.dot(ssm.astype(BF16), w_ref[0:D_SSM, :], preferred_element_type=F32)
             + jnp.dot(cfm_ref[0], w_ref[D_SSM:, :], preferred_element_type=F32))
    x1 = x_ref[0] + mod_ref[0, 2:3, :] * mixed
    x1_ref[0] = x1
    h = x1 * lax.rsqrt(jnp.mean(x1 * x1, axis=-1, keepdims=True) + EPS) * n2w_ref[...]
    h2 = h * (1.0 + mod_ref[0, 4:5, :]) + mod_ref[0, 3:4, :]
    h2t_ref[...] = h2.T.astype(BF16)
    q_ref[...] = jnp.dot(h2.astype(BF16), wq_ref[...], preferred_element_type=F32).astype(BF16)
    _score_tokens(q_ref, keys_ref, neg_ref, c1_ref, e1_ref, r2_ref, e2_ref)


def _outscore(yf, yb, z, cfm, x, mod, gn_w, w_out_b, n2_w, wq_b, keys_b, u_tab, v_tab):
    b, l, d = x.shape
    t = b * l
    tm = min(OUT_TILE, l)
    neg = _candidate_mask()
    n_exp = u_tab.shape[0]
    rows = n_exp // (t // tm)
    assert rows * (t // tm) == n_exp and PEER_EBLK % rows == 0 and rows % LANES == 0
    per_slab = PEER_EBLK // rows
    tile = lambda i, j: i * (l // tm) + j
    tok = lambda w: pl.BlockSpec((1, tm, w), lambda i, j: (i, j, 0))
    const = lambda shape: pl.BlockSpec(shape, lambda i, j: (0,) * len(shape))
    big = lambda: pl.BlockSpec((PEER_HEADS, N_KEYS, tm), lambda i, j: (0, 0, tile(i, j)))
    big_shape = lambda dt: jax.ShapeDtypeStruct((PEER_HEADS, N_KEYS, t), dt)
    return pl.pallas_call(
        _outscore_kernel,
        grid=(b, l // tm),
        in_specs=[tok(D_SSM), tok(D_SSM), tok(D_SSM), tok(D_CONV), tok(d),
                  pl.BlockSpec((1, 6, d), lambda i, j: (i, 0, 0)),
                  const(gn_w.shape), const(w_out_b.shape), const(n2_w.shape),
                  const(wq_b.shape), const(keys_b.shape), const(neg.shape),
                  pl.BlockSpec((rows, d), lambda i, j: (tile(i, j), 0)),
                  pl.BlockSpec((rows, d), lambda i, j: (tile(i, j), 0))],
        out_specs=[tok(d), pl.BlockSpec((d, tm), lambda i, j: (0, tile(i, j))), big(), big(), big(), big(),
                   pl.BlockSpec((rows, d), lambda i, j: (tile(i, j), 0)),
                   pl.BlockSpec((None, d, rows), lambda i, j: (tile(i, j) // per_slab, 0, tile(i, j) % per_slab))],
        out_shape=[jax.ShapeDtypeStruct((b, l, d), F32), jax.ShapeDtypeStruct((d, t), BF16),
                   big_shape(F32), big_shape(F32), big_shape(BF16), big_shape(BF16),
                   jax.ShapeDtypeStruct((n_exp, d), BF16),
                   jax.ShapeDtypeStruct((n_exp // PEER_EBLK, d, PEER_EBLK), BF16)],
        scratch_shapes=[pltpu.VMEM((tm, PEER_HEADS * PEER_KEY_DIM), BF16)],
        compiler_params=_params(("parallel", "parallel")),
        name="outproj_score",
    )(yf, yb, z, cfm, x, mod, gn_w, w_out_b, n2_w, wq_b, keys_b, neg, u_tab, v_tab)


def _key_row_bf16(ref, h, key0, j):
    tile = ref[h, pl.ds(key0 + j // SUBLANES * SUBLANES, SUBLANES), :]
    row = tile[j % SUBLANES:j % SUBLANES + 1, :]
    rows = jnp.broadcast_to(row, (BF16_ROWS, ref.shape[-1])).astype(BF16)
    return jnp.tile(rows, (N_KEYS // BF16_ROWS, 1))


def _peer_kernel(ht_ref, u_ref, vt_ref, c1_ref, e1_ref, r2_ref, e2_ref, x1_ref, mod_ref, fnw_ref,
                 o_ref, acc_ref):
    eb = pl.program_id(1)
    keys_per_step = PEER_EBLK // N_KEYS
    key0 = pl.multiple_of(eb * keys_per_step, keys_per_step)
    starts = [sum(PEER_SUBS[:i]) for i in range(len(PEER_SUBS))]

    @pl.when(eb == 0)
    def _():
        acc_ref[...] = jnp.zeros_like(acc_ref)

    def first_matmul(sub):
        return jnp.dot(u_ref[starts[sub]:starts[sub] + PEER_SUBS[sub], :], ht_ref[...],
                       preferred_element_type=F32)

    def build_gates(sub):
        gates = []
        for kk in range(PEER_SUBS[sub] // N_KEYS):
            j = starts[sub] // N_KEYS + kk
            g = jnp.zeros((N_KEYS, ht_ref.shape[1]), BF16)
            for h in range(PEER_HEADS):
                c1 = _key_row_bf16(c1_ref, h, key0, j)
                e1 = _key_row_bf16(e1_ref, h, key0, j)
                g = g + jnp.where(r2_ref[h] < c1, e2_ref[h], jnp.zeros((), BF16)) * e1
            gates.append(g)
        return jnp.concatenate(gates, axis=0)

    nsub = len(PEER_SUBS)
    pre = first_matmul(0)
    weighted = []
    for sub in range(nsub):
        nxt = first_matmul(sub + 1) if sub + 1 < nsub else None
        act = pre * (1.0 + lax.erf(pre * (2.0 ** -0.5)))
        weighted.append(act.astype(BF16) * build_gates(sub))
        pre = nxt
    acc_ref[...] = jnp.dot(vt_ref[...], jnp.concatenate(weighted, axis=0),
                           preferred_element_type=F32) + acc_ref[...]

    @pl.when(eb == pl.num_programs(1) - 1)
    def _():
        x2 = x1_ref[...] + mod_ref[0, 5:6, :] * acc_ref[...].T
        o_ref[...] = x2 * lax.rsqrt(jnp.mean(x2 * x2, axis=-1, keepdims=True) + EPS) * fnw_ref[...]


def _peer(h2t, u_b, vt_b, c1, e1, r2, e2, x1, mod, fn_w, seq):
    d, t = h2t.shape
    tm = min(PEER_TILE, seq)
    n_exp = u_b.shape[0]
    big = lambda: pl.BlockSpec((PEER_HEADS, N_KEYS, tm), lambda j, e: (0, 0, j))
    return pl.pallas_call(
        _peer_kernel,
        grid=(t // tm, n_exp // PEER_EBLK),
        in_specs=[pl.BlockSpec((d, tm), lambda j, e: (0, j)),
                  pl.BlockSpec((PEER_EBLK, d), lambda j, e: (e, 0)),
                  pl.BlockSpec((None, d, PEER_EBLK), lambda j, e: (e, 0, 0)),
                  big(), big(), big(), big(),
                  pl.BlockSpec((tm, d), lambda j, e: (j, 0)),
                  pl.BlockSpec((1, 6, d), lambda j, e: ((j * tm) // seq, 0, 0)),
                  pl.BlockSpec((1, d), lambda j, e: (0, 0))],
        out_specs=pl.BlockSpec((tm, d), lambda j, e: (j, 0)),
        out_shape=jax.ShapeDtypeStruct((t, d), F32),
        scratch_shapes=[pltpu.VMEM((d, tm), F32)],
        compiler_params=_params(("parallel", "arbitrary")),
        name="peer_dense",
    )(h2t, u_b, vt_b, c1, e1, r2, e2, x1, mod, fn_w)


def _pad_lanes(v, width=LANES):
    flat = v.reshape(1, -1)
    return jnp.pad(flat, ((0, 0), (0, width - flat.shape[1])))


def _head_expand(col0):
    rows = lax.broadcasted_iota(jnp.int32, (LANES, D_SSM), 0)
    heads = lax.broadcasted_iota(jnp.int32, (LANES, D_SSM), 1) // SSM_HEADDIM
    return (rows == heads + col0).astype(BF16)


def kernel(x, c, ctx, c_ctx, ada_w, ada_b, norm1_w, norm2_w, w_in, ssm_conv_w, ssm_conv_b, ssm_dt_bias,
           ssm_a_log, ssm_d, ssm_norm_w, cfm_conv_w, cfm_conv_b, cfm_ln_w, cfm_ln_b, w_out, peer_wq,
           peer_subkeys, peer_u, peer_v, final_norm_w):
    depth = ada_w.shape[0]
    assert depth == 1, "single-layer configuration"
    b, l, d = x.shape
    i = 0

    c_rows = jnp.concatenate([c, c_ctx[None], jnp.zeros((2 * SUBLANES - b - 1, d), F32)], axis=0)
    mod_all = _ada(c_rows, ada_w[i], ada_b[i])
    mod = mod_all[:b].reshape(b, 6, d)
    mod_ctx = mod_all[b:b + 1].reshape(1, 6, d)

    wi = w_in[i]
    o_dt = D_SSM + XBC_DIM
    w_r = jnp.concatenate([wi[:, :o_dt], wi[:, o_dt + 2 * SSM_HEADS:], wi[:, o_dt:o_dt + 2 * SSM_HEADS],
                           jnp.zeros((d, LANES - 2 * SSM_HEADS), F32)], axis=1).astype(BF16)
    z_l, xbc_l, u_l, dt_l = _inproj(x, mod, True, norm1_w[i], w_r)
    _, xbc_c, _, dt_c = _inproj(ctx, mod_ctx, False, norm1_w[i], w_r)

    conv_w = jnp.pad(ssm_conv_w[i], ((0, SUBLANES - SSM_CONV), (0, 0)))
    conv_b = ssm_conv_b[i].reshape(1, XBC_DIM)
    dt_bias = _pad_lanes(ssm_dt_bias[i])
    a_log = _pad_lanes(ssm_a_log[i])
    d_skip = jnp.repeat(ssm_d[i], SSM_HEADDIM).reshape(1, D_SSM)
    zero_state = jnp.zeros((b, SSM_STATE, D_SSM), F32)
    y_dirs = []
    act_c = act_l = None
    for reverse in (False, True):
        expand = _head_expand(SSM_HEADS if reverse else 0)
        args = (conv_w, conv_b, dt_bias, a_log, d_skip, expand, reverse)
        if not reverse:
            _, h_ctx, act_c = _ssd(xbc_c, dt_c, zero_state, *args, add_skip=False, activated=False)
            y_dir, _, act_l = _ssd(xbc_l, dt_l, h_ctx, *args, add_skip=True, activated=False)
        else:
            _, h_ctx = _ssd(act_c, dt_c, zero_state, *args, add_skip=False, activated=True)
            y_dir, _ = _ssd(act_l, dt_l, h_ctx, *args, add_skip=False, activated=True)
        y_dirs.append(y_dir)

    cfm = _cfm(u_l, jnp.pad(cfm_conv_w[i], ((0, 1), (0, 0))), cfm_conv_b[i].reshape(1, D_CONV),
               cfm_ln_w[i].reshape(1, D_CONV), cfm_ln_b[i].reshape(1, D_CONV))

    t = b * l
    keys_b = peer_subkeys[i].reshape(PEER_HEADS * 2, N_KEYS, PEER_KEY_DIM // 2).astype(BF16)
    x1, h2t, c1, e1, r2, e2, u_b, vt_b = _outscore(
        y_dirs[0], y_dirs[1], z_l, cfm, x, mod, ssm_norm_w[i].reshape(1, D_SSM), w_out[i].astype(BF16),
        norm2_w[i].reshape(1, d), peer_wq[i].astype(BF16), keys_b, peer_u[i], peer_v[i])
    out = _peer(h2t, u_b, vt_b, c1, e1, r2, e2,
                x1.reshape(t, d), mod, final_norm_w.reshape(1, d), l)
    return out.reshape(b, l, d)
```

```python
import functools

import jax
import jax.numpy as jnp
from jax import lax
from jax.experimental import pallas as pl
from jax.experimental.pallas import tpu as pltpu

F32 = jnp.float32
BF16 = jnp.bfloat16

D_MODEL = 1024
GRID_W = 64
D_SSM = 512
SSM_HEADDIM = 64
SSM_HEADS = 8
SSM_STATE = 128
SSM_GROUPS = 2
SSM_CONV = 5
SSD_CHUNK = 128
XBC_DIM = D_SSM + 2 * SSM_GROUPS * SSM_STATE
D_CONV = 512
CONV_WIDTH = 31
PEER_HEADS = 8
PEER_KEY_DIM = 256
N_KEYS = 128
PEER_TOPK = 16
EPS = 1e-6

LANES = 128
SUBLANES = 8
VMEM_LIMIT = 56 * 1024 * 1024

ADA_TILE = 768
IN_TILE = 512
SSD_ROWS = 8
CFM_ROWS = 16
OUT_TILE = 512
PEER_TILE = 512
PEER_EBLK = 2048
PEER_SUBS = (128,) * 16
assert sum(PEER_SUBS) == PEER_EBLK and all(s % N_KEYS == 0 for s in PEER_SUBS)
BF16_ROWS = 2 * SUBLANES


def _params(sem):
    return pltpu.CompilerParams(dimension_semantics=sem, vmem_limit_bytes=VMEM_LIMIT)


def _silu(v):
    return v * jax.nn.sigmoid(v)


def _split3(v):
    hi = v.astype(BF16)
    r1 = v - hi.astype(F32)
    mid = r1.astype(BF16)
    lo = (r1 - mid.astype(F32)).astype(BF16)
    return hi, mid, lo


def _dot01_right(v, m01):
    return sum(jnp.dot(p, m01, preferred_element_type=F32) for p in _split3(v))


def _dot01_left(m01, v):
    return sum(jnp.dot(m01, p, preferred_element_type=F32) for p in _split3(v))


def _ada_kernel(c_ref, w_ref, b_ref, o_ref):
    sc = _silu(c_ref[...])
    o_ref[...] = jnp.dot(sc.astype(BF16), w_ref[...].astype(BF16),
                         preferred_element_type=F32) + b_ref[...]


def _ada(c_rows, ada_w, ada_b):
    n = ada_w.shape[1]
    tn = ADA_TILE
    return pl.pallas_call(
        _ada_kernel,
        grid=(n // tn,),
        in_specs=[pl.BlockSpec(c_rows.shape, lambda j: (0, 0)),
                  pl.BlockSpec((D_MODEL, tn), lambda j: (0, j)),
                  pl.BlockSpec((1, tn), lambda j: (0, j))],
        out_specs=pl.BlockSpec((c_rows.shape[0], tn), lambda j: (0, j)),
        out_shape=jax.ShapeDtypeStruct((c_rows.shape[0], n), F32),
        compiler_params=_params(("parallel",)),
        name="ada",
    )(c_rows, ada_w, ada_b.reshape(1, n))


def _inproj_kernel(x_ref, mod_ref, nw_ref, w_ref, z_ref, xbc_ref, u_ref, dt_ref):
    xv = x_ref[0]
    h = xv * lax.rsqrt(jnp.mean(xv * xv, axis=-1, keepdims=True) + EPS) * nw_ref[...]
    h = h * (1.0 + mod_ref[0, 1:2, :]) + mod_ref[0, 0:1, :]
    hb = h.astype(BF16)

    def proj(lo, hi):
        return jnp.dot(hb, w_ref[:, lo:hi], preferred_element_type=F32)

    z_ref[0] = proj(0, D_SSM)
    xbc_ref[0] = proj(D_SSM, D_SSM + XBC_DIM)
    o = D_SSM + XBC_DIM
    u_ref[0] = proj(o, o + D_CONV) * jax.nn.sigmoid(proj(o + D_CONV, o + 2 * D_CONV))
    dt_ref[0] = proj(o + 2 * D_CONV, o + 2 * D_CONV + LANES)


def _inproj(x, mod, per_batch_mod, norm_w, w_r):
    b, l, d = x.shape
    tm = min(IN_TILE, l)
    mod_idx = (lambda i, j: (i, 0, 0)) if per_batch_mod else (lambda i, j: (0, 0, 0))
    tok = lambda w: pl.BlockSpec((1, tm, w), lambda i, j: (i, j, 0))
    return pl.pallas_call(
        _inproj_kernel,
        grid=(b, l // tm),
        in_specs=[tok(d),
                  pl.BlockSpec((1, 6, d), mod_idx),
                  pl.BlockSpec((1, d), lambda i, j: (0, 0)),
                  pl.BlockSpec(w_r.shape, lambda i, j: (0, 0))],
        out_specs=[tok(D_SSM), tok(XBC_DIM), tok(D_CONV), tok(LANES)],
        out_shape=[jax.ShapeDtypeStruct((b, l, w), F32) for w in (D_SSM, XBC_DIM, D_CONV, LANES)],
        compiler_params=_params(("parallel", "parallel")),
        name="inproj",
    )(x, mod, norm_w.reshape(1, d), w_r)


def _ssd_kernel(cur_ref, prev_ref, next_ref, dt_ref, h0_ref, cw_ref, cb_ref, dtb_ref, alog_ref,
                dskip_ref, expand_ref, y_ref, hfin_ref, *rest, reverse, nc, add_skip, activated):
    act_ref, ext_ref, state_ref = (None, *rest) if activated else rest
    q = SSD_CHUNK
    halo = SUBLANES
    pad = (SSM_CONV - 1) // 2
    gw = D_SSM // SSM_GROUPS
    hpg = SSM_HEADS // SSM_GROUPS
    col0 = SSM_HEADS if reverse else 0
    c = pl.program_id(1)
    cc = (nc - 1 - c) if reverse else c

    @pl.when(c == 0)
    def _():
        state_ref[...] = h0_ref[...]

    ri = lax.broadcasted_iota(jnp.int32, (q, q), 0)
    ci = lax.broadcasted_iota(jnp.int32, (q, q), 1)
    causal = (ci >= ri) if reverse else (ci <= ri)
    causal01 = jnp.where(causal, 1.0, 0.0).astype(BF16)
    a_row = -jnp.exp(alog_ref[...])

    for r in range(cur_ref.shape[0]):
        if activated:
            act = cur_ref[r]
        else:
            ext_ref[r, 0:halo, :] = jnp.where(cc == 0, 0.0, prev_ref[r])
            ext_ref[r, halo:halo + q, :] = cur_ref[r]
            ext_ref[r, halo + q:2 * halo + q, :] = jnp.where(cc == nc - 1, 0.0, next_ref[r])
            conv = cb_ref[...] + cw_ref[0:1, :] * ext_ref[r, halo - pad:halo - pad + q, :]
            for k in range(1, SSM_CONV):
                conv = conv + cw_ref[k:k + 1, :] * ext_ref[r, halo - pad + k:halo - pad + k + q, :]
            act = _silu(conv)
            act_ref[r] = act

        dtv = dt_ref[r] + dtb_ref[...]
        dt = jnp.maximum(dtv, 0.0) + jnp.log(1.0 + jnp.exp(-jnp.abs(dtv)))
        acum = _dot01_left(causal01, dt * a_row)
        acum_t = acum.T
        tot = acum[0:1, :] if reverse else acum[q - 1:q, :]
        stacked = jnp.concatenate([dt, acum, jnp.broadcast_to(tot, (SUBLANES, LANES))], axis=0)
        spread = _dot01_right(stacked, expand_ref[...])
        dt_e, acum_e, tot_e = spread[0:q], spread[q:2 * q], spread[2 * q:2 * q + 1]
        dte_e = jnp.exp(tot_e - acum_e)
        ea_e = jnp.exp(acum_e)
        etot_e = jnp.exp(tot_e)

        xs = act[:, :D_SSM]
        xdt = xs * dt_e
        xdt_b = xdt.astype(BF16)
        w_b = (xdt * dte_e).astype(BF16)
        for g in range(SSM_GROUPS):
            gs = slice(g * gw, (g + 1) * gw)
            b_g = act[:, D_SSM + g * SSM_STATE:D_SSM + (g + 1) * SSM_STATE]
            c_g = act[:, D_SSM + (SSM_GROUPS + g) * SSM_STATE:D_SSM + (SSM_GROUPS + g + 1) * SSM_STATE]
            c_b = c_g.astype(BF16)
            bt_b = b_g.T.astype(BF16)
            cb = jnp.dot(c_b, bt_b, preferred_element_type=F32)
            st = state_ref[r, :, gs]
            y_g = jnp.dot(c_b, st.astype(BF16), preferred_element_type=F32) * ea_e[:, gs]
            y_heads = []
            for hh in range(hpg):
                h = g * hpg + hh
                col = col0 + h
                diff = acum[:, col:col + 1] - acum_t[col:col + 1, :]
                decay = jnp.where(causal, jnp.exp(jnp.minimum(diff, 0.0)), 0.0)
                y_heads.append(jnp.dot((cb * decay).astype(BF16),
                                       xdt_b[:, h * SSM_HEADDIM:(h + 1) * SSM_HEADDIM],
                                       preferred_element_type=F32))
            y_g = y_g + jnp.concatenate(y_heads, axis=-1)
            if add_skip:
                y_g = y_g + dskip_ref[:, gs] * xs[:, gs]
            y_ref[r, :, gs] = y_g
            state_ref[r, :, gs] = st * etot_e[:, gs] + jnp.dot(bt_b, w_b[:, gs], preferred_element_type=F32)
    hfin_ref[...] = state_ref[...]


def _ssd(xbc, dtp, h0, conv_w, conv_b, dt_bias, a_log, d_skip, expand, reverse, add_skip, activated):
    b, l, _ = xbc.shape
    q = SSD_CHUNK
    nc = l // q
    nr = SSD_ROWS if b % SSD_ROWS == 0 else 1
    hb = q // SUBLANES
    pos = (lambda c: nc - 1 - c) if reverse else (lambda c: c)
    const = lambda shape: pl.BlockSpec(shape, lambda i, c: (0,) * len(shape))
    kern = functools.partial(_ssd_kernel, reverse=reverse, nc=nc, add_skip=add_skip, activated=activated)
    chunk = lambda w: pl.BlockSpec((nr, q, w), lambda i, c: (i, pos(c), 0))
    act_spec = [] if activated else [chunk(XBC_DIM)]
    act_shape = [] if activated else [jax.ShapeDtypeStruct((b, l, XBC_DIM), F32)]
    return pl.pallas_call(
        kern,
        grid=(b // nr, nc),
        in_specs=[pl.BlockSpec((nr, q, XBC_DIM), lambda i, c: (i, pos(c), 0)),
                  pl.BlockSpec((nr, SUBLANES, XBC_DIM), lambda i, c: (i, jnp.maximum(pos(c) * hb - 1, 0), 0)),
                  pl.BlockSpec((nr, SUBLANES, XBC_DIM),
                               lambda i, c: (i, jnp.minimum((pos(c) + 1) * hb, l // SUBLANES - 1), 0)),
                  pl.BlockSpec((nr, q, LANES), lambda i, c: (i, pos(c), 0)),
                  pl.BlockSpec((nr, SSM_STATE, D_SSM), lambda i, c: (i, 0, 0)),
                  const(conv_w.shape), const(conv_b.shape), const(dt_bias.shape), const(a_log.shape),
                  const(d_skip.shape), const(expand.shape)],
        out_specs=[chunk(D_SSM), pl.BlockSpec((nr, SSM_STATE, D_SSM), lambda i, c: (i, 0, 0))] + act_spec,
        out_shape=[jax.ShapeDtypeStruct((b, l, D_SSM), F32),
                   jax.ShapeDtypeStruct((b, SSM_STATE, D_SSM), F32)] + act_shape,
        scratch_shapes=[pltpu.VMEM((nr, q + 2 * SUBLANES, XBC_DIM), F32),
                        pltpu.VMEM((nr, SSM_STATE, D_SSM), F32)],
        compiler_params=_params(("parallel", "arbitrary")),
        name="ssd_bwd" if reverse else "ssd_fwd",
    )(xbc, xbc, xbc, dtp, h0, conv_w, conv_b, dt_bias, a_log, d_skip, expand)


def _cfm_kernel(cur_ref, prev_ref, next_ref, w_ref, b_ref, lnw_ref, lnb_ref, o_ref, hpad_ref, vbuf_ref,
                shift_ref, *, nb):
    rows = CFM_ROWS
    band = rows * GRID_W
    half = D_CONV // 2
    pad = (CONV_WIDTH - 1) // 2
    lead = 2 * SUBLANES
    r = pl.program_id(1)

    zeros = jnp.zeros((rows, lead, half), F32)
    hpad_ref[:, 0:lead, :] = zeros
    hpad_ref[:, lead + GRID_W:2 * lead + GRID_W, :] = zeros
    hpad_ref[:, lead:lead + GRID_W, :] = cur_ref[0, :, 0:half].reshape(rows, GRID_W, half)
    vbuf_ref[0:band, :] = jnp.where(r == 0, 0.0, prev_ref[0])
    vbuf_ref[band:2 * band, :] = cur_ref[0, :, half:D_CONV]
    vbuf_ref[2 * band:3 * band, :] = jnp.where(r == nb - 1, 0.0, next_ref[0])

    span = shift_ref.shape[1]

    def row_body(i, carry):
        for s in range(SUBLANES):
            shift_ref[s] = hpad_ref[i, s:s + span, :]
        acc_h = jnp.zeros((GRID_W, half), F32)
        acc_v = jnp.zeros((GRID_W, half), F32)
        for k in range(CONV_WIDTH):
            first = lead - pad + k
            tile0 = first // SUBLANES * SUBLANES
            acc_h = acc_h + w_ref[k:k + 1, 0:half] * shift_ref[first % SUBLANES, tile0:tile0 + GRID_W, :]
            start = pl.multiple_of(band + (i + k - pad) * GRID_W, GRID_W)
            acc_v = acc_v + w_ref[k:k + 1, half:D_CONV] * vbuf_ref[pl.ds(start, GRID_W), :]
        conv = jnp.concatenate([acc_h, acc_v], axis=-1) + b_ref[...]
        mu = jnp.mean(conv, axis=-1, keepdims=True)
        cen = conv - mu
        var = jnp.mean(cen * cen, axis=-1, keepdims=True)
        o_ref[0, pl.ds(pl.multiple_of(i * GRID_W, GRID_W), GRID_W), :] = _silu(
            cen * lax.rsqrt(var + EPS) * lnw_ref[...] + lnb_ref[...]).astype(BF16)
        return carry

    lax.fori_loop(0, rows, row_body, 0, unroll=4)


def _cfm(u, conv_w, conv_b, ln_w, ln_b):
    b, l, _ = u.shape
    band = CFM_ROWS * GRID_W
    nb = l // band
    half = D_CONV // 2
    const = lambda shape: pl.BlockSpec(shape, lambda i, r: (0,) * len(shape))
    return pl.pallas_call(
        functools.partial(_cfm_kernel, nb=nb),
        grid=(b, nb),
        in_specs=[pl.BlockSpec((1, band, D_CONV), lambda i, r: (i, r, 0)),
                  pl.BlockSpec((1, band, half), lambda i, r: (i, jnp.maximum(r - 1, 0), 1)),
                  pl.BlockSpec((1, band, half), lambda i, r: (i, jnp.minimum(r + 1, nb - 1), 1)),
                  const(conv_w.shape), const(conv_b.shape), const(ln_w.shape), const(ln_b.shape)],
        out_specs=pl.BlockSpec((1, band, D_CONV), lambda i, r: (i, r, 0)),
        out_shape=jax.ShapeDtypeStruct((b, l, D_CONV), BF16),
        scratch_shapes=[pltpu.VMEM((CFM_ROWS, GRID_W + 4 * SUBLANES, half), F32),
                        pltpu.VMEM((3 * band, half), F32),
                        pltpu.VMEM((SUBLANES, GRID_W + 3 * SUBLANES, half), F32)],
        compiler_params=_params(("parallel", "parallel")),
        name="cfm",
    )(u, u, u, conv_w, conv_b, ln_w, ln_b)


def _oddeven_merge(lo, hi, r):
    step = r * 2
    if step < hi - lo:
        yield from _oddeven_merge(lo, hi, step)
        yield from _oddeven_merge(lo + r, hi, step)
        yield from [(i, i + r) for i in range(lo + r, hi - r, step)]
    else:
        yield (lo, lo + r)


def _oddeven_sort(lo, hi):
    if hi > lo:
        mid = lo + (hi - lo) // 2
        yield from _oddeven_sort(lo, mid)
        yield from _oddeven_sort(mid + 1, hi)
        yield from _oddeven_merge(lo, hi, 1)


SORT_NET = tuple(_oddeven_sort(0, PEER_TOPK - 1))


def _cmp_exchange(v, i, j):
    v[i], v[j] = jnp.maximum(v[i], v[j]), jnp.minimum(v[i], v[j])


def _bitonic_sort(v):
    d = len(v) // 2
    while d:
        for i in range(len(v)):
            if not i & d:
                _cmp_exchange(v, i, i + d)
        d //= 2


def _merge_top(v, w):
    n = len(v)
    return [jnp.maximum(v[i], w[n - 1 - i]) for i in range(n)]


def _sublane_rolls():
    shift = SUBLANES // 2
    while shift:
        yield shift
        shift //= 2


def _top_sorted(v):
    for shift in _sublane_rolls():
        v = _merge_top(v, [pltpu.roll(x, shift, 0) for x in v])
        _bitonic_sort(v)
    return v


def _count_prefix(b, test):
    n = len(b)
    count = jnp.zeros_like(b[0])
    above = []
    step = n // 2
    while step:
        cands = [b[base + step - 1] for base in range(0, n, 2 * step)]
        for hit in reversed(above):
            cands = [jnp.where(hit, cands[i + 1], cands[i]) for i in range(0, len(cands), 2)]
        hit = test(cands[0])
        count = count + jnp.where(hit, float(step), 0.0)
        above.append(hit)
        step //= 2
    return jnp.where(test(b[n - 1]), float(n), count)


def _score_tokens(q_ref, keys_ref, neg_ref, c1_ref, e1_ref, r2_ref, e2_ref):
    tm = q_ref.shape[0]
    nsub = tm // LANES
    kd = PEER_KEY_DIM // 2
    k = PEER_TOPK
    ntile = N_KEYS // SUBLANES

    def body(it, carry):
        h = it // nsub
        tok = pl.multiple_of((it % nsub) * LANES, LANES)
        nt = (((1,), (1,)), ((), ()))

        def score_tiles(p):
            qs = q_ref[pl.ds(tok, LANES), pl.ds(pl.multiple_of((2 * h + p) * kd, kd), kd)]
            s = lax.dot_general(keys_ref[2 * h + p], qs, nt, preferred_element_type=F32)
            return [s[i * SUBLANES:(i + 1) * SUBLANES, :] for i in range(ntile)]

        def top(tiles):
            v = list(tiles)
            for i, j in SORT_NET:
                _cmp_exchange(v, i, j)
            return _top_sorted(v)

        s1 = score_tiles(0)
        s2 = score_tiles(1)
        a = top(s1)
        b = top(s2)

        sub = lax.broadcasted_iota(jnp.int32, (SUBLANES, LANES), 0)
        pick = lambda lo, x, y: jnp.where(sub < lo, x, y)
        base = pick(4, pick(2, pick(1, a[0], a[1]), pick(3, a[2], a[3])), pick(5, b[0], pick(6, b[1], b[2])))
        cand = []
        for m in range(k):
            other = pick(4, b[m], a[4 + m]) if 4 + m < k else b[m]
            cand.append((other + base) + neg_ref[m])
        lists = list(cand)
        rolls = list(_sublane_rolls())
        for shift in rolls[:-1]:
            lists = _merge_top(lists, [pltpu.roll(x, shift, 0) for x in lists])
            _bitonic_sort(lists)
        lists = _merge_top(lists, [pltpu.roll(x, rolls[-1], 0) for x in lists])
        tau = functools.reduce(jnp.minimum, lists)

        best = a[0] + b[0]
        zsum = sum(jnp.where(c >= tau, jnp.exp(c - best), 0.0) for c in cand)
        for shift in rolls:
            zsum = zsum + pltpu.roll(zsum, shift, 0)
        scale = 0.5 / zsum
        top_extra = sum(jnp.where(a[0] + b[j] >= tau, 1.0, 0.0) for j in range(k // 2, k))

        c1_t, e1_t, r2_t, e2_t = [], [], [], []
        for t1, t2 in zip(s1, s2):
            c1 = _count_prefix(b[:k // 2], lambda p: t1 + p >= tau)
            c1_t.append(jnp.where(t1 == a[0], c1 + top_extra, c1))
            e1_t.append(jnp.exp(t1 - a[0]) * scale)
            r2_t.append(_count_prefix(b, lambda p: p > t2))
            e2_t.append(jnp.exp(t2 - b[0]))
        c1_ref[h, :, pl.ds(tok, LANES)] = jnp.concatenate(c1_t, axis=0)
        e1_ref[h, :, pl.ds(tok, LANES)] = jnp.concatenate(e1_t, axis=0)
        r2_ref[h, :, pl.ds(tok, LANES)] = jnp.concatenate(r2_t, axis=0).astype(BF16)
        e2_ref[h, :, pl.ds(tok, LANES)] = jnp.concatenate(e2_t, axis=0).astype(BF16)
        return carry

    lax.fori_loop(0, PEER_HEADS * nsub, body, 0, unroll=8)


def _candidate_mask():
    k = PEER_TOPK
    m = lax.broadcasted_iota(jnp.int32, (k, SUBLANES, LANES), 0)
    s = lax.broadcasted_iota(jnp.int32, (k, SUBLANES, LANES), 1)
    row_ok = (s < 4) & ((s + 1) * (m + 1) <= k)
    col_ok = (s >= 4) & (s < 7) & (4 + m < k) & ((5 + m) * (s - 3) <= k)
    return jnp.where(row_ok | col_ok, 0.0, -jnp.inf).astype(F32)


def _outscore_kernel(yf_ref, yb_ref, z_ref, cfm_ref, x_ref, mod_ref, gnw_ref, w_ref, n2w_ref,
                     wq_ref, keys_ref, neg_ref, u_ref, v_ref,
                     x1_ref, h2t_ref, c1_ref, e1_ref, r2_ref, e2_ref, ub_ref, vt_ref, q_ref):
    ub_ref[...] = u_ref[...].astype(BF16)
    vt_ref[...] = v_ref[...].T.astype(BF16)
    gw = D_SSM // SSM_GROUPS
    g = (yf_ref[0] + yb_ref[0]) * _silu(z_ref[0])
    parts = []
    for k in range(SSM_GROUPS):
        gk = g[:, k * gw:(k + 1) * gw]
        parts.append(gk * lax.rsqrt(jnp.mean(gk * gk, axis=-1, keepdims=True) + EPS))
    ssm = jnp.concatenate(parts, axis=-1) * gnw_ref[...]
    mixed = (jnp.dot(ssm.astype(BF16), w_ref[0:D_SSM, :], preferred_element_type=F32)
             + jnp.dot(cfm_ref[0], w_ref[D_SSM:, :], preferred_element_type=F32))
    x1 = x_ref[0] + mod_ref[0, 2:3, :] * mixed
    x1_ref[0] = x1
    h = x1 * lax.rsqrt(jnp.mean(x1 * x1, axis=-1, keepdims=True) + EPS) * n2w_ref[...]
    h2 = h * (1.0 + mod_ref[0, 4:5, :]) + mod_ref[0, 3:4, :]
    h2t_ref[...] = h2.T.astype(BF16)
    q_ref[...] = jnp.dot(h2.astype(BF16), wq_ref[...], preferred_element_type=F32).astype(BF16)
    _score_tokens(q_ref, keys_ref, neg_ref, c1_ref, e1_ref, r2_ref, e2_ref)


def _outscore(yf, yb, z, cfm, x, mod, gn_w, w_out_b, n2_w, wq_b, keys_b, u_tab, v_tab):
    b, l, d = x.shape
    t = b * l
    tm = min(OUT_TILE, l)
    neg = _candidate_mask()
    n_exp = u_tab.shape[0]
    rows = n_exp // (t // tm)
    assert rows * (t // tm) == n_exp and PEER_EBLK % rows == 0 and rows % LANES == 0
    per_slab = PEER_EBLK // rows
    tile = lambda i, j: i * (l // tm) + j
    tok = lambda w: pl.BlockSpec((1, tm, w), lambda i, j: (i, j, 0))
    const = lambda shape: pl.BlockSpec(shape, lambda i, j: (0,) * len(shape))
    big = lambda: pl.BlockSpec((PEER_HEADS, N_KEYS, tm), lambda i, j: (0, 0, tile(i, j)))
    big_shape = lambda dt: jax.ShapeDtypeStruct((PEER_HEADS, N_KEYS, t), dt)
    return pl.pallas_call(
        _outscore_kernel,
        grid=(b, l // tm),
        in_specs=[tok(D_SSM), tok(D_SSM), tok(D_SSM), tok(D_CONV), tok(d),
                  pl.BlockSpec((1, 6, d), lambda i, j: (i, 0, 0)),
                  const(gn_w.shape), const(w_out_b.shape), const(n2_w.shape),
                  const(wq_b.shape), const(keys_b.shape), const(neg.shape),
                  pl.BlockSpec((rows, d), lambda i, j: (tile(i, j), 0)),
                  pl.BlockSpec((rows, d), lambda i, j: (tile(i, j), 0))],
        out_specs=[tok(d), pl.BlockSpec((d, tm), lambda i, j: (0, tile(i, j))), big(), big(), big(), big(),
                   pl.BlockSpec((rows, d), lambda i, j: (tile(i, j), 0)),
                   pl.BlockSpec((None, d, rows), lambda i, j: (tile(i, j) // per_slab, 0, tile(i, j) % per_slab))],
        out_shape=[jax.ShapeDtypeStruct((b, l, d), F32), jax.ShapeDtypeStruct((d, t), BF16),
                   big_shape(F32), big_shape(F32), big_shape(BF16), big_shape(BF16),
                   jax.ShapeDtypeStruct((n_exp, d), BF16),
                   jax.ShapeDtypeStruct((n_exp // PEER_EBLK, d, PEER_EBLK), BF16)],
        scratch_shapes=[pltpu.VMEM((tm, PEER_HEADS * PEER_KEY_DIM), BF16)],
        compiler_params=_params(("parallel", "parallel")),
        name="outproj_score",
    )(yf, yb, z, cfm, x, mod, gn_w, w_out_b, n2_w, wq_b, keys_b, neg, u_tab, v_tab)


def _key_row_bf16(ref, h, key0, j):
    tile = ref[h, pl.ds(key0 + j // SUBLANES * SUBLANES, SUBLANES), :]
    row = tile[j % SUBLANES:j % SUBLANES + 1, :]
    rows = jnp.broadcast_to(row, (BF16_ROWS, ref.shape[-1])).astype(BF16)
    return jnp.tile(rows, (N_KEYS // BF16_ROWS, 1))


def _peer_kernel(ht_ref, u_ref, vt_ref, c1_ref, e1_ref, r2_ref, e2_ref, x1_ref, mod_ref, fnw_ref,
                 o_ref, acc_ref):
    eb = pl.program_id(1)
    keys_per_step = PEER_EBLK // N_KEYS
    key0 = pl.multiple_of(eb * keys_per_step, keys_per_step)
    starts = [sum(PEER_SUBS[:i]) for i in range(len(PEER_SUBS))]

    @pl.when(eb == 0)
    def _():
        acc_ref[...] = jnp.zeros_like(acc_ref)

    def first_matmul(sub):
        return jnp.dot(u_ref[starts[sub]:starts[sub] + PEER_SUBS[sub], :], ht_ref[...],
                       preferred_element_type=F32)

    def build_gates(sub):
        gates = []
        for kk in range(PEER_SUBS[sub] // N_KEYS):
            j = starts[sub] // N_KEYS + kk
            g = jnp.zeros((N_KEYS, ht_ref.shape[1]), BF16)
            for h in range(PEER_HEADS):
                c1 = _key_row_bf16(c1_ref, h, key0, j)
                e1 = _key_row_bf16(e1_ref, h, key0, j)
                g = g + jnp.where(r2_ref[h] < c1, e2_ref[h], jnp.zeros((), BF16)) * e1
            gates.append(g)
        return jnp.concatenate(gates, axis=0)

    nsub = len(PEER_SUBS)
    pre = first_matmul(0)
    weighted = []
    for sub in range(nsub):
        nxt = first_matmul(sub + 1) if sub + 1 < nsub else None
        act = pre * (1.0 + lax.erf(pre * (2.0 ** -0.5)))
        weighted.append(act.astype(BF16) * build_gates(sub))
        pre = nxt
    acc_ref[...] = jnp.dot(vt_ref[...], jnp.concatenate(weighted, axis=0),
                           preferred_element_type=F32) + acc_ref[...]

    @pl.when(eb == pl.num_programs(1) - 1)
    def _():
        x2 = x1_ref[...] + mod_ref[0, 5:6, :] * acc_ref[...].T
        o_ref[...] = x2 * lax.rsqrt(jnp.mean(x2 * x2, axis=-1, keepdims=True) + EPS) * fnw_ref[...]


def _peer(h2t, u_b, vt_b, c1, e1, r2, e2, x1, mod, fn_w, seq):
    d, t = h2t.shape
    tm = min(PEER_TILE, seq)
    n_exp = u_b.shape[0]
    big = lambda: pl.BlockSpec((PEER_HEADS, N_KEYS, tm), lambda j, e: (0, 0, j))
    return pl.pallas_call(
        _peer_kernel,
        grid=(t // tm, n_exp // PEER_EBLK),
        in_specs=[pl.BlockSpec((d, tm), lambda j, e: (0, j)),
                  pl.BlockSpec((PEER_EBLK, d), lambda j, e: (e, 0)),
                  pl.BlockSpec((None, d, PEER_EBLK), lambda j, e: (e, 0, 0)),
                  big(), big(), big(), big(),
                  pl.BlockSpec((tm, d), lambda j, e: (j, 0)),
                  pl.BlockSpec((1, 6, d), lambda j, e: ((j * tm) // seq, 0, 0)),
                  pl.BlockSpec((1, d), lambda j, e: (0, 0))],
        out_specs=pl.BlockSpec((tm, d), lambda j, e: (j, 0)),
        out_shape=jax.ShapeDtypeStruct((t, d), F32),
        scratch_shapes=[pltpu.VMEM((d, tm), F32)],
        compiler_params=_params(("parallel", "arbitrary")),
        name="peer_dense",
    )(h2t, u_b, vt_b, c1, e1, r2, e2, x1, mod, fn_w)


def _pad_lanes(v, width=LANES):
    flat = v.reshape(1, -1)
    return jnp.pad(flat, ((0, 0), (0, width - flat.shape[1])))


def _head_expand(col0):
    rows = lax.broadcasted_iota(jnp.int32, (LANES, D_SSM), 0)
    heads = lax.broadcasted_iota(jnp.int32, (LANES, D_SSM), 1) // SSM_HEADDIM
    return (rows == heads + col0).astype(BF16)


def kernel(x, c, ctx, c_ctx, ada_w, ada_b, norm1_w, norm2_w, w_in, ssm_conv_w, ssm_conv_b, ssm_dt_bias,
           ssm_a_log, ssm_d, ssm_norm_w, cfm_conv_w, cfm_conv_b, cfm_ln_w, cfm_ln_b, w_out, peer_wq,
           peer_subkeys, peer_u, peer_v, final_norm_w):
    depth = ada_w.shape[0]
    assert depth == 1, "single-layer configuration"
    b, l, d = x.shape
    i = 0

    c_rows = jnp.concatenate([c, c_ctx[None], jnp.zeros((2 * SUBLANES - b - 1, d), F32)], axis=0)
    mod_all = _ada(c_rows, ada_w[i], ada_b[i])
    mod = mod_all[:b].reshape(b, 6, d)
    mod_ctx = mod_all[b:b + 1].reshape(1, 6, d)

    wi = w_in[i]
    o_dt = D_SSM + XBC_DIM
    w_r = jnp.concatenate([wi[:, :o_dt], wi[:, o_dt + 2 * SSM_HEADS:], wi[:, o_dt:o_dt + 2 * SSM_HEADS],
                           jnp.zeros((d, LANES - 2 * SSM_HEADS), F32)], axis=1).astype(BF16)
    z_l, xbc_l, u_l, dt_l = _inproj(x, mod, True, norm1_w[i], w_r)
    _, xbc_c, _, dt_c = _inproj(ctx, mod_ctx, False, norm1_w[i], w_r)

    conv_w = jnp.pad(ssm_conv_w[i], ((0, SUBLANES - SSM_CONV), (0, 0)))
    conv_b = ssm_conv_b[i].reshape(1, XBC_DIM)
    dt_bias = _pad_lanes(ssm_dt_bias[i])
    a_log = _pad_lanes(ssm_a_log[i])
    d_skip = jnp.repeat(ssm_d[i], SSM_HEADDIM).reshape(1, D_SSM)
    zero_state = jnp.zeros((b, SSM_STATE, D_SSM), F32)
    y_dirs = []
    act_c = act_l = None
    for reverse in (False, True):
        expand = _head_expand(SSM_HEADS if reverse else 0)
        args = (conv_w, conv_b, dt_bias, a_log, d_skip, expand, reverse)
        if not reverse:
            _, h_ctx, act_c = _ssd(xbc_c, dt_c, zero_state, *args, add_skip=False, activated=False)
            y_dir, _, act_l = _ssd(xbc_l, dt_l, h_ctx, *args, add_skip=True, activated=False)
        else:
            _, h_ctx = _ssd(act_c, dt_c, zero_state, *args, add_skip=False, activated=True)
            y_dir, _ = _ssd(act_l, dt_l, h_ctx, *args, add_skip=False, activated=True)
        y_dirs.append(y_dir)

    cfm = _cfm(u_l, jnp.pad(cfm_conv_w[i], ((0, 1), (0, 0))), cfm_conv_b[i].reshape(1, D_CONV),
               cfm_ln_w[i].reshape(1, D_CONV), cfm_ln_b[i].reshape(1, D_CONV))

    t = b * l
    keys_b = peer_subkeys[i].reshape(PEER_HEADS * 2, N_KEYS, PEER_KEY_DIM // 2).astype(BF16)
    x1, h2t, c1, e1, r2, e2, u_b, vt_b = _outscore(
        y_dirs[0], y_dirs[1], z_l, cfm, x, mod, ssm_norm_w[i].reshape(1, D_SSM), w_out[i].astype(BF16),
        norm2_w[i].reshape(1, d), peer_wq[i].astype(BF16), keys_b, peer_u[i], peer_v[i])
    out = _peer(h2t, u_b, vt_b, c1, e1, r2, e2,
                x1.reshape(t, d), mod, final_norm_w.reshape(1, d), l)
    return out.reshape(b, l, d)
```
